```python
import math
import jax, jax.numpy as jnp
from jax import lax
import numpy as np

D_MODEL = 2048
BATCH = 16
SEQ = 256
DEPTH = 2
DEC_BATCH = 4
DEC_SEQ = 4096
PAST_LEN = 256

GRID_W = 64
N_EVEN = (DEPTH + 1) // 2
N_ODD = DEPTH // 2
NORM_EPS = 1e-6
H_A = 4
DK_A = 128
DV_A = 256
GATE_RANK = 16
GATE_TEMP = 16.0
H_B = 4
DK_B = 128
DV_B = 256
CHUNK = 64
H_C = 16
Q_RANK = 512
KV_RANK = 512
NOPE_DIM = 128
ROPE_DIM = 64
V_DIM = 128
ROPE_BASE = 10000.0
Q_BLOCK = 128
MLA_SCALE = (NOPE_DIM + ROPE_DIM) ** -0.5
MLA_IN = Q_RANK + KV_RANK + ROPE_DIM
D_FF = 5632
CONV_W = 3
AB_SPLITS = (H_A * DK_A, H_A * DK_A, H_A * DV_A, H_A * DV_A, 2 * GATE_RANK, H_B * DK_B, H_B * DK_B, H_B * DV_B, H_B * DV_B)
AB_IN = 2 * H_A * DK_A + 2 * H_A * DV_A + 2 * GATE_RANK + 2 * H_B * DK_B + 2 * H_B * DV_B
AB_MIX = H_A * DV_A + H_B * DV_B

kernel_name = 'bidir_gla_retnet_mla_convffn_ctxprefix_step'


def _split(z, sizes):
    idx = []
    s = 0
    for n in sizes[:-1]:
        s += n
        idx.append(s)
    return jnp.split(z, idx, axis=-1)


def rms_norm(x, g):
    x32 = x.astype(jnp.float32)
    y = x32 * lax.rsqrt(jnp.mean(x32 * x32, axis=-1, keepdims=True) + NORM_EPS)
    return (y * g.astype(jnp.float32)).astype(x.dtype)


def head_norm(o, g):
    H, dv = o.shape[-2], o.shape[-1]
    mu = jnp.mean(o, axis=-1, keepdims=True)
    var = jnp.mean(jnp.square(o - mu), axis=-1, keepdims=True)
    return (o - mu) * lax.rsqrt(var + NORM_EPS) * g.astype(jnp.float32).reshape(H, dv)


def modulation(cond, w, b):
    m = jax.nn.silu(cond) @ w + b
    return jnp.split(m[:, None, :], 6, axis=-1)


def _chunks(a, n):
    B, T, H, d = a.shape
    return jnp.moveaxis(a.astype(jnp.float32).reshape(B, n, CHUNK, H, d), 1, 0)


def gla_scan(q, k, v, log_a, s0):
    B, T, H, _ = q.shape
    dv = v.shape[-1]
    n = T // CHUNK
    causal = jnp.tril(jnp.ones((CHUNK, CHUNK), dtype=bool))

    def step(s, inp):
        qc, kc, vc, gc = inp
        b = jnp.cumsum(gc, axis=1)
        b_end = b[:, -1]
        q_dec = qc * jnp.exp(b)
        k_inv = kc * jnp.exp(-b)
        att = jnp.where(causal, jnp.einsum('bihd,bjhd->bhij', q_dec, k_inv), 0.0)
        o = jnp.einsum('bhij,bjhe->bihe', att, vc) + jnp.einsum('bihd,bhde->bihe', q_dec, s)
        k_end = kc * jnp.exp(b_end[:, None] - b)
        s = jnp.exp(b_end)[..., None] * s + jnp.einsum('bjhd,bjhe->bhde', k_end, vc)
        return s, o

    s_fin, o = lax.scan(step, s0.astype(jnp.float32), (_chunks(q, n), _chunks(k, n), _chunks(v, n), _chunks(log_a, n)))
    return jnp.moveaxis(o, 0, 1).reshape(B, T, H, dv), s_fin


def retention_scan(q, k, v, log_gamma, s0):
    B, T, H, _ = q.shape
    dv = v.shape[-1]
    n = T // CHUNK
    causal = jnp.tril(jnp.ones((CHUNK, CHUNK), dtype=bool))
    idx = jnp.arange(CHUNK, dtype=jnp.float32)
    lg = log_gamma.astype(jnp.float32)[:, None]
    diff = jnp.maximum(idx[:, None] - idx[None, :], 0.0)
    decay = jnp.where(causal, jnp.exp(lg[:, :, None] * diff), 0.0)
    q_dec = jnp.exp(lg * (idx + 1.0)).T[None, :, :, None]
    k_dec = jnp.exp(lg * (CHUNK - 1.0 - idx)).T[None, :, :, None]
    c_dec = jnp.exp(lg[:, 0] * CHUNK)[None, :, None, None]

    def step(s, inp):
        qc, kc, vc = inp
        att = jnp.einsum('bihd,bjhd->bhij', qc, kc) * decay
        o = jnp.einsum('bhij,bjhe->bihe', att, vc) + jnp.einsum('bihd,bhde->bihe', qc, s) * q_dec
        s = c_dec * s + jnp.einsum('bjhd,bjhe->bhde', kc * k_dec, vc)
        return s, o

    s_fin, o = lax.scan(step, s0.astype(jnp.float32), (_chunks(q, n), _chunks(k, n), _chunks(v, n)))
    return jnp.moveaxis(o, 0, 1).reshape(B, T, H, dv), s_fin


def _flip(a):
    return jnp.flip(a, axis=1)


def mixer_even(h, w_in, gate_w2, gate_b, ret_decay, gla_ng, ret_ng, w_out, s_gla0, s_ret0):
    B, T, _ = h.shape
    gq, gk, gv, gg, glr, rq, rk, rv, rg = _split(h @ w_in, AB_SPLITS)
    gq = gq.reshape(B, T, H_A, DK_A) * (DK_A ** -0.5)
    gk = gk.reshape(B, T, H_A, DK_A)
    gv = gv.reshape(B, T, H_A, DV_A)
    la_f = (jax.nn.log_sigmoid((glr[..., :GATE_RANK] @ gate_w2[0] + gate_b[0]).astype(jnp.float32)) / GATE_TEMP).reshape(B, T, H_A, DK_A)
    la_b = (jax.nn.log_sigmoid((glr[..., GATE_RANK:] @ gate_w2[1] + gate_b[1]).astype(jnp.float32)) / GATE_TEMP).reshape(B, T, H_A, DK_A)
    oa_f, sa_f = gla_scan(gq, gk, gv, la_f, s_gla0[:, 0])
    oa_b, sa_b = gla_scan(_flip(gq), _flip(gk), _flip(gv), _flip(la_b), s_gla0[:, 1])
    o_gla = oa_f + _flip(oa_b)
    rq = rq.reshape(B, T, H_B, DK_B)
    rk = rk.reshape(B, T, H_B, DK_B) * (DK_B ** -0.5)
    rv = rv.reshape(B, T, H_B, DV_B)
    lgam = jax.nn.log_sigmoid(ret_decay.astype(jnp.float32))
    ob_f, sb_f = retention_scan(rq, rk, rv, lgam[0], s_ret0[:, 0])
    ob_b, sb_b = retention_scan(_flip(rq), _flip(rk), _flip(rv), lgam[1], s_ret0[:, 1])
    o_ret = ob_f + _flip(ob_b)
    y_gla = jax.nn.silu(gg) * head_norm(o_gla, gla_ng).reshape(B, T, H_A * DV_A).astype(h.dtype)
    y_ret = jax.nn.silu(rg) * head_norm(o_ret, ret_ng).reshape(B, T, H_B * DV_B).astype(h.dtype)
    y = jnp.concatenate([y_gla, y_ret], axis=-1) @ w_out
    return y, jnp.stack([sa_f, sa_b], axis=1), jnp.stack([sb_f, sb_b], axis=1)


def axial_rope(x, rows, cols):
    half = ROPE_DIM // 2
    quarter = half // 2
    inv = ROPE_BASE ** (-jnp.arange(quarter, dtype=jnp.float32) * 2.0 / half)

    def rot(xa, pos):
        ang = pos[:, None] * inv[None, :]
        cos = jnp.cos(ang)[None, :, None, :]
        sin = jnp.sin(ang)[None, :, None, :]
        x1, x2 = xa[..., :quarter], xa[..., quarter:]
        return jnp.concatenate([x1 * cos - x2 * sin, x1 * sin + x2 * cos], axis=-1)

    x32 = x.astype(jnp.float32)
    return jnp.concatenate([rot(x32[..., :half], rows), rot(x32[..., half:], cols)], axis=-1).astype(x.dtype)


def mla_project(h, w_in, q_ng, w_uq, kv_ng):
    B, T, _ = h.shape
    cq, ckv, kpe = _split(h @ w_in, (Q_RANK, KV_RANK, ROPE_DIM))
    q = (rms_norm(cq, q_ng) @ w_uq).reshape(B, T, H_C, NOPE_DIM + ROPE_DIM)
    return q[..., :NOPE_DIM], q[..., NOPE_DIM:], rms_norm(ckv, kv_ng), kpe


def mla_expand(ckv, w_ukv):
    B, T, _ = ckv.shape
    kv = (ckv @ w_ukv).reshape(B, T, H_C, NOPE_DIM + V_DIM)
    return kv[..., :NOPE_DIM], kv[..., NOPE_DIM:]


def mla_attend(qn, qp, kn, kp, v):
    B, Tq, H, _ = qn.shape
    nb = Tq // Q_BLOCK
    qn_b = jnp.moveaxis(qn.reshape(B, nb, Q_BLOCK, H, NOPE_DIM), 1, 0)
    qp_b = jnp.moveaxis(qp.reshape(B, nb, Q_BLOCK, H, ROPE_DIM), 1, 0)

    def one_block(args):
        a, r = args
        s = jnp.einsum('bqhd,bkhd->bhqk', a, kn) + jnp.einsum('bqhr,bkr->bhqk', r, kp)
        p = jax.nn.softmax(s.astype(jnp.float32) * MLA_SCALE, axis=-1).astype(v.dtype)
        return jnp.einsum('bhqk,bkhd->bqhd', p, v)

    o = lax.map(one_block, (qn_b, qp_b))
    return jnp.moveaxis(o, 0, 1).reshape(B, Tq, H * V_DIM)


def mla_context(h, w_in, q_ng, w_uq, kv_ng, w_ukv, w_out):
    qn, qp, ckv, kpe = mla_project(h, w_in, q_ng, w_uq, kv_ng)
    kn, v = mla_expand(ckv, w_ukv)
    return mla_attend(qn, qp, kn, kpe, v) @ w_out, ckv, kpe


def mla_latent(h, w_in, q_ng, w_uq, kv_ng, w_ukv, w_out, ctx_ckv, ctx_kpe, rows, cols):
    qn, qp, ckv, kpe = mla_project(h, w_in, q_ng, w_uq, kv_ng)
    qp = axial_rope(qp, rows, cols)
    kpe = axial_rope(kpe[:, :, None, :], rows, cols)[:, :, 0, :]
    kn, v = mla_expand(jnp.concatenate([ctx_ckv.astype(ckv.dtype), ckv], axis=1), w_ukv)
    kp = jnp.concatenate([ctx_kpe.astype(kpe.dtype), kpe], axis=1)
    return mla_attend(qn, qp, kn, kp, v) @ w_out


def conv_ffn(h, w_in, conv_w, w_out):
    a, b = jnp.split(h @ w_in, 2, axis=-1)
    T = a.shape[1]
    ap = jnp.pad(a, ((0, 0), (1, 1), (0, 0)))
    a = conv_w[0] * ap[:, :T] + conv_w[1] * ap[:, 1:T + 1] + conv_w[2] * ap[:, 2:T + 2]
    return (jax.nn.silu(a) * b) @ w_out


def _normal(k, shape, scale):
    return scale * jax.random.normal(k, shape, jnp.float32)


def setup_inputs(seed: int = 0) -> dict:
    key = jax.random.key(seed)
    ks = jax.random.split(key, 32)
    D = D_MODEL
    ret_base = jnp.log(2.0 ** (5.0 + jnp.arange(H_B, dtype=jnp.float32)) - 1.0)
    return {
        'x_prompt': _normal(ks[0], (BATCH, SEQ, D), 1.0),
        'x_sample': _normal(ks[1], (DEC_BATCH, DEC_SEQ, D), 1.0),
        'state_gla': _normal(ks[2], (DEC_BATCH, N_EVEN, 2, H_A, DK_A, DV_A), 0.1),
        'state_ret': _normal(ks[3], (DEC_BATCH, N_EVEN, 2, H_B, DK_B, DV_B), 0.1),
        'cache_ckv': _normal(ks[4], (DEC_BATCH, N_ODD, PAST_LEN, KV_RANK), 1.0),
        'cache_kpe': _normal(ks[5], (DEC_BATCH, N_ODD, PAST_LEN, ROPE_DIM), 1.0),
        'c': _normal(ks[6], (DEC_BATCH, D), 1.0),
        'c_ctx': _normal(ks[7], (D,), 1.0),
        'mod_w': _normal(ks[8], (DEPTH, D, 6 * D), 0.5 * D ** -0.5),
        'mod_b': _normal(ks[9], (DEPTH, 6 * D), 0.02),
        'norm1_g': 1.0 + _normal(ks[10], (DEPTH, D), 0.05),
        'norm2_g': 1.0 + _normal(ks[11], (DEPTH, D), 0.05),
        'ab_w_in': _normal(ks[12], (N_EVEN, D, AB_IN), D ** -0.5),
        'gla_gate_w2': _normal(ks[13], (N_EVEN, 2, GATE_RANK, H_A * DK_A), GATE_RANK ** -0.5),
        'gla_gate_b': _normal(ks[14], (N_EVEN, 2, H_A * DK_A), 0.1),
        'ret_decay': ret_base + _normal(ks[15], (N_EVEN, 2, H_B), 0.1),
        'gla_norm_g': 1.0 + _normal(ks[16], (N_EVEN, H_A * DV_A), 0.05),
        'ret_norm_g': 1.0 + _normal(ks[17], (N_EVEN, H_B * DV_B), 0.05),
        'ab_w_out': _normal(ks[18], (N_EVEN, AB_MIX, D), AB_MIX ** -0.5),
        'mla_w_in': _normal(ks[19], (N_ODD, D, MLA_IN), D ** -0.5),
        'mla_q_norm_g': 1.0 + _normal(ks[20], (N_ODD, Q_RANK), 0.05),
        'mla_w_uq': _normal(ks[21], (N_ODD, Q_RANK, H_C * (NOPE_DIM + ROPE_DIM)), Q_RANK ** -0.5),
        'mla_kv_norm_g': 1.0 + _normal(ks[22], (N_ODD, KV_RANK), 0.05),
        'mla_w_ukv': _normal(ks[23], (N_ODD, KV_RANK, H_C * (NOPE_DIM + V_DIM)), KV_RANK ** -0.5),
        'mla_w_out': _normal(ks[24], (N_ODD, H_C * V_DIM, D), (H_C * V_DIM) ** -0.5),
        'ffn_w_in': _normal(ks[25], (DEPTH, D, 2 * D_FF), D ** -0.5),
        'ffn_conv': _normal(ks[26], (DEPTH, CONV_W, D_FF), CONV_W ** -0.5),
        'ffn_w_out': _normal(ks[27], (DEPTH, D_FF, D), D_FF ** -0.5),
        'final_norm_g': 1.0 + _normal(ks[28], (D,), 0.05),
    }


def reference(x_prompt, x_sample, state_gla, state_ret, cache_ckv, cache_kpe, c, c_ctx, mod_w, mod_b, norm1_g, norm2_g, ab_w_in, gla_gate_w2, gla_gate_b, ret_decay, gla_norm_g, ret_norm_g, ab_w_out, mla_w_in, mla_q_norm_g, mla_w_uq, mla_kv_norm_g, mla_w_ukv, mla_w_out, ffn_w_in, ffn_conv, ffn_w_out, final_norm_g):
    xc = x_prompt
    xl = x_sample
    bc = xc.shape[0]
    t_lat = xl.shape[1]
    ROWS = t_lat // GRID_W
    rows = jnp.repeat(jnp.arange(ROWS), GRID_W).astype(jnp.float32)
    cols = jnp.tile(jnp.arange(GRID_W), ROWS).astype(jnp.float32)
    gla_states, ret_states, ckv_list, kpe_list = [], [], [], []
    for l in range(DEPTH):
        mc = modulation(c_ctx[None, :], mod_w[l], mod_b[l])
        ml = modulation(c, mod_w[l], mod_b[l])
        hc = rms_norm(xc, norm1_g[l]) * (1.0 + mc[1]) + mc[0]
        hl = rms_norm(xl, norm1_g[l]) * (1.0 + ml[1]) + ml[0]
        if l % 2 == 0:
            e = l // 2
            zg = jnp.zeros((bc, 2, H_A, DK_A, DV_A), jnp.float32)
            zr = jnp.zeros((bc, 2, H_B, DK_B, DV_B), jnp.float32)
            oc, sg, sr = mixer_even(hc, ab_w_in[e], gla_gate_w2[e], gla_gate_b[e], ret_decay[e], gla_norm_g[e], ret_norm_g[e], ab_w_out[e], zg, zr)
            ol, _, _ = mixer_even(hl, ab_w_in[e], gla_gate_w2[e], gla_gate_b[e], ret_decay[e], gla_norm_g[e], ret_norm_g[e], ab_w_out[e], state_gla[:, e], state_ret[:, e])
            gla_states.append(sg)
            ret_states.append(sr)
        else:
            i = l // 2
            oc, ckv, kpe = mla_context(hc, mla_w_in[i], mla_q_norm_g[i], mla_w_uq[i], mla_kv_norm_g[i], mla_w_ukv[i], mla_w_out[i])
            ol = mla_latent(hl, mla_w_in[i], mla_q_norm_g[i], mla_w_uq[i], mla_kv_norm_g[i], mla_w_ukv[i], mla_w_out[i], cache_ckv[:, i], cache_kpe[:, i], rows, cols)
            ckv_list.append(ckv)
            kpe_list.append(kpe)
        xc = xc + mc[2] * oc
        xl = xl + ml[2] * ol
        hc = rms_norm(xc, norm2_g[l]) * (1.0 + mc[4]) + mc[3]
        hl = rms_norm(xl, norm2_g[l]) * (1.0 + ml[4]) + ml[3]
        xc = xc + mc[5] * conv_ffn(hc, ffn_w_in[l], ffn_conv[l], ffn_w_out[l])
        xl = xl + ml[5] * conv_ffn(hl, ffn_w_in[l], ffn_conv[l], ffn_w_out[l])
    y_prompt = rms_norm(xc, final_norm_g)
    y_sample = rms_norm(xl, final_norm_g)
    new_state_gla = jnp.stack(gla_states, axis=1)
    new_state_ret = jnp.stack(ret_states, axis=1)
    new_cache_ckv = jnp.stack(ckv_list, axis=1)
    new_cache_kpe = jnp.stack(kpe_list, axis=1)
    return (y_prompt, y_sample, new_state_gla, new_state_ret, new_cache_ckv, new_cache_kpe)
```

```python
import functools

import jax
import jax.numpy as jnp
from jax import lax
from jax.experimental import pallas as pl
from jax.experimental.pallas import tpu as pltpu

F32 = jnp.float32
BF16 = jnp.bfloat16

NORM_EPS = 1e-6
GATE_TEMP = 16.0
CHUNK = 64
GRID_W = 64
ROPE_BASE = 10000.0
ROPE_DIM = 64
NOPE_DIM = 128
V_DIM = 128
CONV_W = 3

LANE = 128
BF16_SUBLANE = 16
MOD_GROUPS = 8
VMEM_LIMIT = 56 * 1024 * 1024

NT_DIMS = (((1,), (1,)), ((), ()))
TN_DIMS = (((0,), (0,)), ((), ()))


def _dot(a, b):
    return jnp.dot(a, b, preferred_element_type=F32)


def _dot_nt(a, b):
    return lax.dot_general(a, b, NT_DIMS, preferred_element_type=F32)


def _dot_tn(a, b):
    return lax.dot_general(a, b, TN_DIMS, preferred_element_type=F32)


def _params(*sem):
    return pltpu.CompilerParams(dimension_semantics=sem, vmem_limit_bytes=VMEM_LIMIT)


def _rms(x, g):
    ms = jnp.mean(x * x, axis=-1, keepdims=True)
    return (x * lax.rsqrt(ms + NORM_EPS)) * g


def _norm_mod(x, g, scale, shift):
    return _rms(x, g) * (1.0 + scale) + shift


def _silu(x):
    return x * jax.nn.sigmoid(x)


def _log_sigmoid(x):
    return jnp.minimum(x, 0.0) - jnp.log1p(jnp.exp(-jnp.abs(x)))


def _largest_tile(n, cap):
    best = None
    for t in range(LANE, min(n, cap) + 1, LANE):
        if n % t == 0:
            best = t
    assert best is not None, (n, cap)
    return best


def _mod_kernel(c_ref, w_ref, b_ref, o_ref):
    s = _silu(c_ref[...]).astype(BF16)
    o_ref[0] = _dot(s, w_ref[0].astype(BF16)) + b_ref[0]


def _modulation(cond, mod_w, mod_b):
    L, D, N = mod_w.shape
    tn = _largest_tile(N, 1024)
    return pl.pallas_call(
        _mod_kernel,
        grid=(L, N // tn),
        in_specs=[
            pl.BlockSpec((MOD_GROUPS, D), lambda l, j: (0, 0)),
            pl.BlockSpec((1, D, tn), lambda l, j: (l, 0, j)),
            pl.BlockSpec((1, 1, tn), lambda l, j: (l, 0, j)),
        ],
        out_specs=pl.BlockSpec((1, MOD_GROUPS, tn), lambda l, j: (l, 0, j)),
        out_shape=jax.ShapeDtypeStruct((L, MOD_GROUPS, N), F32),
        compiler_params=_params("parallel", "parallel"),
        name="modulation",
    )(cond, mod_w, mod_b.reshape(L, 1, N))


class _Rows:
    def __init__(self, mc, seq, lat_len, n_lat):
        self.mc, self.seq, self.lat_len, self.n_lat = mc, seq, lat_len, n_lat
        self.m = mc + lat_len * n_lat

    def group(self, i, tm):
        r = i * tm
        return jnp.where(r < self.mc, 0, 1 + (r - self.mc) // self.lat_len)

    def mod_spec(self, which, tm, d):
        return pl.BlockSpec((1, 1, d), lambda i, j: (self.group(i, tm) * 6 + which, 0, 0))


def _inproj_kernel(x_ref, g_ref, sh_ref, sc_ref, w_ref, o_ref, h_ref):
    @pl.when(pl.program_id(1) == 0)
    def _():
        h_ref[...] = _norm_mod(x_ref[...], g_ref[...], sc_ref[0], sh_ref[0]).astype(BF16)

    o_ref[...] = _dot(h_ref[...], w_ref[...]).astype(o_ref.dtype)


def _inproj(rows, x, g, mod3, w, tm, tn_cap=1024):
    M, D = x.shape
    N = w.shape[1]
    tn = _largest_tile(N, tn_cap)
    return pl.pallas_call(
        _inproj_kernel,
        grid=(M // tm, N // tn),
        in_specs=[
            pl.BlockSpec((tm, D), lambda i, j: (i, 0)),
            pl.BlockSpec((1, D), lambda i, j: (0, 0)),
            rows.mod_spec(0, tm, D),
            rows.mod_spec(1, tm, D),
            pl.BlockSpec((D, tn), lambda i, j: (0, j)),
        ],
        out_specs=pl.BlockSpec((tm, tn), lambda i, j: (i, j)),
        out_shape=jax.ShapeDtypeStruct((M, N), F32),
        scratch_shapes=[pltpu.VMEM((tm, D), BF16)],
        compiler_params=_params("parallel", "arbitrary"),
        name="inproj",
    )(x, g.reshape(1, D), mod3, mod3, w)


def _outproj_kernel(y_ref, w_ref, x_ref, gate_ref, o_ref):
    o_ref[...] = x_ref[...] + gate_ref[0] * _dot(y_ref[...], w_ref[...])


def _outproj(rows, y, w, x, mod3, tm, tn_cap=1024):
    M, K = y.shape
    N = w.shape[1]
    tn = _largest_tile(N, tn_cap)
    return pl.pallas_call(
        _outproj_kernel,
        grid=(M // tm, N // tn),
        in_specs=[
            pl.BlockSpec((tm, K), lambda i, j: (i, 0)),
            pl.BlockSpec((K, tn), lambda i, j: (0, j)),
            pl.BlockSpec((tm, tn), lambda i, j: (i, j)),
            pl.BlockSpec((1, 1, tn), lambda i, j: (rows.group(i, tm) * 6 + 2, 0, j)),
        ],
        out_specs=pl.BlockSpec((tm, tn), lambda i, j: (i, j)),
        out_shape=jax.ShapeDtypeStruct((M, N), F32),
        compiler_params=_params("parallel", "arbitrary"),
        name="outproj",
    )(y, w, x, mod3)


HALO = BF16_SUBLANE


def _ffn_kernel(rows, tm, final_norm, xp_ref, x_ref, xn_ref, g_ref, sh_ref, sc_ref, gate_ref,
                wa_ref, wb_ref, cw_ref, wo_ref, fg_ref, o_ref, h_ref, a_ref, acc_ref):
    i = pl.program_id(0)
    f = pl.program_id(1)

    @pl.when(f == 0)
    def _():
        g, sc, sh = g_ref[...], sc_ref[0], sh_ref[0]
        h_ref[0:HALO, :] = _norm_mod(xp_ref[...], g, sc, sh).astype(BF16)
        h_ref[HALO:HALO + tm, :] = _norm_mod(x_ref[...], g, sc, sh).astype(BF16)
        h_ref[HALO + tm:, :] = _norm_mod(xn_ref[...], g, sc, sh).astype(BF16)
        acc_ref[...] = jnp.zeros_like(acc_ref)

    a_ref[...] = _dot(h_ref[...], wa_ref[...])
    b = _dot(h_ref[HALO:HALO + tm, :], wb_ref[...])
    row = i * tm + lax.broadcasted_iota(jnp.int32, (tm, 1), 0)
    pos = jnp.where(row < rows.mc, row & (rows.seq - 1), (row - rows.mc) & (rows.lat_len - 1))
    seq_len = jnp.where(row < rows.mc, rows.seq, rows.lat_len)
    a_prev = jnp.where(pos == 0, 0.0, a_ref[pl.ds(HALO - 1, tm), :])
    a_next = jnp.where(pos == seq_len - 1, 0.0, a_ref[pl.ds(HALO + 1, tm), :])
    a_mid = a_ref[pl.ds(HALO, tm), :]
    cw = cw_ref[...]
    a = cw[0:1] * a_prev + cw[1:2] * a_mid + cw[2:3] * a_next
    act = (_silu(a) * b).astype(BF16)
    acc_ref[...] += _dot(act, wo_ref[...])

    @pl.when(f == pl.num_programs(1) - 1)
    def _():
        out = x_ref[...] + gate_ref[0] * acc_ref[...]
        if final_norm:
            out = _rms(out, fg_ref[...])
        o_ref[...] = out


def _ffn(rows, x, g, mod3, w_in, conv_w, w_out, final_g, tm, tf, final_norm):
    M, D = x.shape
    F = w_out.shape[0]
    nf = F // tf
    nhalo = M // HALO
    kern = functools.partial(_ffn_kernel, rows, tm, final_norm)
    return pl.pallas_call(
        kern,
        grid=(M // tm, nf),
        in_specs=[
            pl.BlockSpec((HALO, D), lambda i, f: (jnp.maximum(i * (tm // HALO) - 1, 0), 0)),
            pl.BlockSpec((tm, D), lambda i, f: (i, 0)),
            pl.BlockSpec((HALO, D), lambda i, f: (jnp.minimum((i + 1) * (tm // HALO), nhalo - 1), 0)),
            pl.BlockSpec((1, D), lambda i, f: (0, 0)),
            rows.mod_spec(3, tm, D),
            rows.mod_spec(4, tm, D),
            rows.mod_spec(5, tm, D),
            pl.BlockSpec((D, tf), lambda i, f: (0, f)),
            pl.BlockSpec((D, tf), lambda i, f: (0, nf + f)),
            pl.BlockSpec((CONV_W, tf), lambda i, f: (0, f)),
            pl.BlockSpec((tf, D), lambda i, f: (f, 0)),
            pl.BlockSpec((1, D), lambda i, f: (0, 0)),
        ],
        out_specs=pl.BlockSpec((tm, D), lambda i, f: (i, 0)),
        out_shape=jax.ShapeDtypeStruct((M, D), F32),
        scratch_shapes=[
            pltpu.VMEM((tm + 2 * HALO, D), BF16),
            pltpu.VMEM((tm + 2 * HALO, tf), F32),
            pltpu.VMEM((tm, D), F32),
        ],
        compiler_params=_params("parallel", "arbitrary"),
        name="convffn",
    )(x, x, x, g.reshape(1, D), mod3, mod3, mod3, w_in, w_in, conv_w, w_out, final_g.reshape(1, D))


class _ScanCfg:
    def __init__(self, rows, batch, tb, h, dk, dv, backward):
        self.rows, self.batch, self.tb, self.h, self.dk, self.dv = rows, batch, tb, h, dk, dv
        self.backward = backward
        self.cps = rows.seq // tb
        self.lps = rows.lat_len // tb
        self.ctx_blocks = batch * self.cps
        self.nblk = rows.m // tb

    def block(self, i):
        return self.nblk - 1 - i if self.backward else i

    def is_ctx(self, r):
        return r < self.ctx_blocks

    def seq_pos(self, r):
        ctx = self.is_ctx(r)
        return (jnp.where(ctx, r % self.cps, (r - self.ctx_blocks) % self.lps),
                jnp.where(ctx, self.cps, self.lps))

    def lat_seq(self, r):
        return jnp.clip((r - self.ctx_blocks) // self.lps, 0, self.rows.n_lat - 1)

    def ctx_seq(self, r):
        return jnp.clip(r // self.cps, 0, self.batch - 1)


def _scan_kernel(cfg, gq_ref, gk_ref, gv_ref, rq_ref, rk_ref, rv_ref, glr_ref, gw_ref, gb_ref, rd_ref,
                 s0g_ref, s0r_ref, *rest):
    if cfg.backward:
        of_ref, gg_ref, rg_ref, gng_ref, rng_ref, y_ref, sgo_ref, sro_ref, sg_ref, sr_ref = rest
    else:
        o_ref, sgo_ref, sro_ref, sg_ref, sr_ref = rest
    H, DK, DV, C = cfg.h, cfg.dk, cfg.dv, CHUNK
    r = cfg.block(pl.program_id(0))
    blk, nblk_seq = cfg.seq_pos(r)
    first = blk == (nblk_seq - 1 if cfg.backward else 0)
    last = blk == (0 if cfg.backward else nblk_seq - 1)
    is_ctx = cfg.is_ctx(r)

    @pl.when(first & is_ctx)
    def _():
        sg_ref[...] = jnp.zeros_like(sg_ref)
        sr_ref[...] = jnp.zeros_like(sr_ref)

    @pl.when(first & jnp.logical_not(is_ctx))
    def _():
        sg_ref[...] = s0g_ref[0]
        sr_ref[...] = s0r_ref[0]

    ti = lax.broadcasted_iota(jnp.int32, (C, C), 0)
    tj = lax.broadcasted_iota(jnp.int32, (C, C), 1)
    sees = (tj >= ti) if cfg.backward else (tj <= ti)
    tri = jnp.where(sees, 1.0, 0.0).astype(BF16)
    dist = jnp.abs(ti - tj).astype(F32)
    rowi = lax.broadcasted_iota(jnp.int32, (C, LANE), 0)
    to_end = (rowi if cfg.backward else C - 1 - rowi).astype(F32)
    from_start = (C - rowi if cfg.backward else rowi + 1).astype(F32)

    ret_consts = []
    for h in range(H):
        lg = _log_sigmoid(rd_ref[h])
        lg_c = jnp.broadcast_to(lg[0:1, 0:C], (C, C))
        lg_l = jnp.broadcast_to(lg[0:1, :], (C, LANE))
        decay = jnp.where(sees, jnp.exp(lg_c * dist), 0.0)
        q_dec = jnp.exp(lg_l * from_start)
        k_dec = jnp.exp(lg_l * to_end)
        c_dec = jnp.exp(lg[0:1, :] * float(C))
        ret_consts.append((decay, q_dec, k_dec, c_dec))

    n_chunks = cfg.tb // C

    def chunk_body(ci, carry):
        c = n_chunks - 1 - ci if cfg.backward else ci
        rs = pl.ds(pl.multiple_of(c * C, C), C)
        pre = _dot(glr_ref[rs, :].astype(BF16), gw_ref[...]) + gb_ref[...]
        la = _log_sigmoid(pre) / GATE_TEMP
        for h in range(H):
            ks = slice(h * DK, (h + 1) * DK)
            vs = slice(h * DV, (h + 1) * DV)
            g = la[:, ks]
            g_hi = g.astype(BF16)
            g_lo = (g - g_hi.astype(F32)).astype(BF16)
            b = _dot(tri, g_hi) + _dot(tri, g_lo)
            b_end = b[0:1] if cfg.backward else b[C - 1:C]
            q = gq_ref[rs, ks] * (DK ** -0.5)
            k = gk_ref[rs, ks]
            v = gv_ref[rs, vs].astype(BF16)
            q_dec = (q * jnp.exp(b)).astype(BF16)
            k_inv = (k * jnp.exp(-b)).astype(BF16)
            k_end = (k * jnp.exp(b_end - b)).astype(BF16)
            att = jnp.where(sees, _dot_nt(q_dec, k_inv), 0.0)
            st = sg_ref[h]
            o_g = _dot(att.astype(BF16), v) + _dot_nt(q_dec, st.astype(BF16))
            sg_ref[h] = st * jnp.exp(b_end) + _dot_tn(v, k_end)
            decay, rq_dec, rk_dec, rc_dec = ret_consts[h]
            rq = rq_ref[rs, ks].astype(BF16)
            rk = rk_ref[rs, ks] * (DK ** -0.5)
            rv = rv_ref[rs, vs].astype(BF16)
            ratt = _dot_nt(rq, rk.astype(BF16)) * decay
            st = sr_ref[h]
            qs = _dot_nt(rq, st.astype(BF16))
            o_r = _dot(ratt.astype(BF16), rv) + qs * jnp.concatenate([rq_dec] * (DV // LANE), axis=1)
            sr_ref[h] = st * rc_dec + _dot_tn(rv, (rk * rk_dec).astype(BF16))
            gs = slice(h * DV, (h + 1) * DV)
            rs_cols = slice(H * DV + h * DV, H * DV + (h + 1) * DV)
            if cfg.backward:
                for o, cols, gate_ref, ng_ref in ((o_g, gs, gg_ref, gng_ref), (o_r, rs_cols, rg_ref, rng_ref)):
                    tot = o + of_ref[rs, cols]
                    mu = jnp.mean(tot, axis=-1, keepdims=True)
                    d = tot - mu
                    var = jnp.mean(d * d, axis=-1, keepdims=True)
                    yn = d * lax.rsqrt(var + NORM_EPS) * ng_ref[:, gs]
                    y_ref[rs, cols] = (_silu(gate_ref[rs, gs]) * yn).astype(y_ref.dtype)
            else:
                o_ref[rs, gs] = o_g
                o_ref[rs, rs_cols] = o_r
        return carry

    lax.fori_loop(0, n_chunks, chunk_body, 0)

    @pl.when(last & is_ctx)
    def _():
        sgo_ref[0] = sg_ref[...]
        sro_ref[0] = sr_ref[...]


def _scan(cfg, z, col, gate_w, gate_b, ret_decay_t, s0g, s0r, extra):
    H, DK, DV, tb = cfg.h, cfg.dk, cfg.dv, cfg.tb
    M = z.shape[0]

    def zspec(name):
        off, width = col[name]
        assert off % width == 0, (name, off, width)
        return pl.BlockSpec((tb, width), lambda i: (cfg.block(i), off // width))

    def full(shape):
        return pl.BlockSpec(shape, lambda i: (0,) * len(shape))

    state_in = pl.BlockSpec((1, H, DV, DK), lambda i: (cfg.lat_seq(cfg.block(i)), 0, 0, 0))
    state_out = pl.BlockSpec((1, H, DV, DK), lambda i: (cfg.ctx_seq(cfg.block(i)), 0, 0, 0))
    row_blk = pl.BlockSpec((tb, 2 * H * DV), lambda i: (cfg.block(i), 0))
    in_specs = [zspec("gq"), zspec("gk"), zspec("gv"), zspec("rq"), zspec("rk"), zspec("rv"), zspec("glr"),
                full(gate_w.shape), full(gate_b.shape), full(ret_decay_t.shape), state_in, state_in]
    args = [z, z, z, z, z, z, z, gate_w, gate_b, ret_decay_t, s0g, s0r]
    state_shape = jax.ShapeDtypeStruct((cfg.batch, H, DV, DK), F32)
    if cfg.backward:
        o_fwd, gla_ng, ret_ng = extra
        in_specs += [row_blk, zspec("gg"), zspec("rg"), full(gla_ng.shape), full(ret_ng.shape)]
        args += [o_fwd, z, z, gla_ng, ret_ng]
        out0 = jax.ShapeDtypeStruct((M, 2 * H * DV), BF16)
    else:
        out0 = jax.ShapeDtypeStruct((M, 2 * H * DV), F32)
    return pl.pallas_call(
        functools.partial(_scan_kernel, cfg),
        grid=(cfg.nblk,),
        in_specs=in_specs,
        out_specs=[row_blk, state_out, state_out],
        out_shape=[out0, state_shape, state_shape],
        scratch_shapes=[pltpu.VMEM((H, DV, DK), F32), pltpu.VMEM((H, DV, DK), F32)],
        compiler_params=_params("arbitrary"),
        name="scan_bwd" if cfg.backward else "scan_fwd",
    )(*args)


def _rope_keep(h, shape):
    lane = lax.broadcasted_iota(jnp.int32, shape, 1)
    return (lane < ROPE_DIM) if h % 2 == 0 else (lane >= ROPE_DIM)


def _mla_q_kernel(n_heads, scale, cq_ref, qng_ref, w_ref, cos_ref, sin_ref, o_ref):
    cqn = _rms(cq_ref[...], qng_ref[...]).astype(BF16)
    q = _dot(cqn, w_ref[...])
    nope_w = n_heads * NOPE_DIM
    rope_w = n_heads * ROPE_DIM
    cos, sin = cos_ref[...], sin_ref[...]
    for h in range(n_heads):
        p = h // 2
        tile = q[:, nope_w + p * LANE:nope_w + (p + 1) * LANE]
        partner = q[:, nope_w + rope_w + p * LANE:nope_w + rope_w + (p + 1) * LANE]
        rot = (tile * cos + partner * sin) * scale
        rot = jnp.where(_rope_keep(h, rot.shape), rot, 0.0)
        base = h * 2 * LANE
        o_ref[:, base:base + LANE] = (q[:, h * NOPE_DIM:(h + 1) * NOPE_DIM] * scale).astype(BF16)
        o_ref[:, base + LANE:base + 2 * LANE] = rot.astype(BF16)


def _mla_q(zc, q_ng, w_uq, cos_t, sin_t, n_heads, scale, tm):
    M = zc.shape[0]
    R = q_ng.shape[0]
    N = w_uq.shape[1]
    return pl.pallas_call(
        functools.partial(_mla_q_kernel, n_heads, scale),
        grid=(M // tm,),
        in_specs=[
            pl.BlockSpec((tm, R), lambda i: (i, 0)),
            pl.BlockSpec((1, R), lambda i: (0, 0)),
            pl.BlockSpec((R, N), lambda i: (0, 0)),
            pl.BlockSpec((tm, LANE), lambda i: (i, 0)),
            pl.BlockSpec((tm, LANE), lambda i: (i, 0)),
        ],
        out_specs=pl.BlockSpec((tm, n_heads * 2 * LANE), lambda i: (i, 0)),
        out_shape=jax.ShapeDtypeStruct((M, n_heads * 2 * LANE), BF16),
        compiler_params=_params("parallel"),
        name="mla_q",
    )(zc, q_ng.reshape(1, R), w_uq, cos_t, sin_t)


class _KeyRows:
    def __init__(self, rows, blk):
        self.rows, self.blk = rows, blk
        self.per_lat = 1 + rows.lat_len // blk
        self.lat_blocks = rows.n_lat * self.per_lat
        self.ctx_blocks = rows.mc // blk
        self.n = self.lat_blocks + self.ctx_blocks

    def is_cache(self, j):
        return (j < self.lat_blocks) & (j % self.per_lat == 0)

    def cache_idx(self, j):
        return jnp.clip(j // self.per_lat, 0, self.rows.n_lat - 1)

    def token_block(self, j):
        b = j // self.per_lat
        t = jnp.maximum(j % self.per_lat - 1, 0)
        lat = self.ctx_blocks + b * (self.per_lat - 1) + t
        return jnp.where(j < self.lat_blocks, lat, j - self.lat_blocks)


def _mla_kv_kernel(keys, n_heads, ckv_ref, kpe_ref, cckv_ref, ckpe_ref, kvng_ref, w_ref, cos_ref, sin_ref,
                   kcat_ref, v_ref, ckvn_ref):
    cached = keys.is_cache(pl.program_id(0))
    kpe = kpe_ref[...]
    rot = kpe[:, :LANE] * cos_ref[...] + kpe[:, LANE:] * sin_ref[...]
    ckvn = jnp.where(cached, cckv_ref[...], _rms(ckv_ref[...], kvng_ref[...]))
    kr = jnp.where(cached, ckpe_ref[...], rot)
    ckvn_ref[...] = ckvn
    kv = _dot(ckvn.astype(BF16), w_ref[...])
    nope_w = n_heads * NOPE_DIM
    v_ref[...] = kv[:, nope_w:].astype(BF16)
    for h in range(n_heads):
        base = h * 2 * LANE
        kcat_ref[:, base:base + LANE] = kv[:, h * NOPE_DIM:(h + 1) * NOPE_DIM].astype(BF16)
        kcat_ref[:, base + LANE:base + 2 * LANE] = jnp.where(_rope_keep(h, kr.shape), kr, 0.0).astype(BF16)


def _mla_kv(keys, zc, col, cache_ckv, cache_kpe2, kv_ng, w_ukv, cos_t, sin_t, n_heads):
    blk = keys.blk
    R = kv_ng.shape[0]
    N = w_ukv.shape[1]
    Mk = keys.n * blk
    ckv_off, ckv_w = col["ckv"]
    kpe_off, kpe_w = col["kpe"]
    assert ckv_off % ckv_w == 0 and kpe_off % kpe_w == 0
    tok = keys.token_block
    return pl.pallas_call(
        functools.partial(_mla_kv_kernel, keys, n_heads),
        grid=(keys.n,),
        in_specs=[
            pl.BlockSpec((blk, ckv_w), lambda j: (tok(j), ckv_off // ckv_w)),
            pl.BlockSpec((blk, kpe_w), lambda j: (tok(j), kpe_off // kpe_w)),
            pl.BlockSpec((blk, R), lambda j: (keys.cache_idx(j), 0)),
            pl.BlockSpec((blk, LANE), lambda j: (keys.cache_idx(j), 0)),
            pl.BlockSpec((1, R), lambda j: (0, 0)),
            pl.BlockSpec((R, N), lambda j: (0, 0)),
            pl.BlockSpec((blk, LANE), lambda j: (tok(j), 0)),
            pl.BlockSpec((blk, LANE), lambda j: (tok(j), 0)),
        ],
        out_specs=[
            pl.BlockSpec((blk, n_heads * 2 * LANE), lambda j: (j, 0)),
            pl.BlockSpec((blk, n_heads * V_DIM), lambda j: (j, 0)),
            pl.BlockSpec((blk, R), lambda j: (j, 0)),
        ],
        out_shape=[
            jax.ShapeDtypeStruct((Mk, n_heads * 2 * LANE), BF16),
            jax.ShapeDtypeStruct((Mk, n_heads * V_DIM), BF16),
            jax.ShapeDtypeStruct((Mk, R), F32),
        ],
        compiler_params=_params("parallel"),
        name="mla_kv",
    )(zc, zc, cache_ckv, cache_kpe2, kv_ng.reshape(1, R), w_ukv, cos_t, sin_t)


def _attn_kernel(q_ref, k_ref, v_ref, o_ref):
    s = _dot_nt(q_ref[...], k_ref[...])
    m = jnp.max(s, axis=-1, keepdims=True)
    p = jnp.exp(s - m)
    l = jnp.sum(p, axis=-1, keepdims=True)
    o_ref[...] = (_dot(p.astype(BF16), v_ref[...]) / l).astype(o_ref.dtype)


def _attention(qcat, kcat, v, out_rows, n_seq, n_heads, tq, q_blocks_per_seq, q_block0, tk, k_block0):
    return pl.pallas_call(
        _attn_kernel,
        grid=(n_seq, n_heads, q_blocks_per_seq),
        in_specs=[
            pl.BlockSpec((tq, 2 * LANE), lambda s, h, i: (q_block0 + s * q_blocks_per_seq + i, h)),
            pl.BlockSpec((tk, 2 * LANE), lambda s, h, i: (k_block0 + s, h)),
            pl.BlockSpec((tk, V_DIM), lambda s, h, i: (k_block0 + s, h)),
        ],
        out_specs=pl.BlockSpec((tq, V_DIM), lambda s, h, i: (s * q_blocks_per_seq + i, h)),
        out_shape=jax.ShapeDtypeStruct((out_rows, n_heads * V_DIM), BF16),
        compiler_params=_params("parallel", "parallel", "arbitrary"),
        name="mla_attn",
    )(qcat, kcat, v)


def _rope_partner_perm():
    q = ROPE_DIM // 4
    d = jnp.arange(ROPE_DIM)
    return jnp.where((d // q) % 2 == 0, d + q, d - q)


def _rope_tables(rows):
    half, quarter = ROPE_DIM // 2, ROPE_DIM // 4
    inv = ROPE_BASE ** (-jnp.arange(quarter, dtype=F32) * 2.0 / half)
    t = jnp.arange(rows.lat_len)
    r = (t // GRID_W).astype(F32)
    c = (t % GRID_W).astype(F32)
    ang_r = r[:, None] * inv[None, :]
    ang_c = c[:, None] * inv[None, :]
    cos = jnp.concatenate([jnp.cos(ang_r), jnp.cos(ang_r), jnp.cos(ang_c), jnp.cos(ang_c)], axis=1)
    sin = jnp.concatenate([-jnp.sin(ang_r), jnp.sin(ang_r), -jnp.sin(ang_c), jnp.sin(ang_c)], axis=1)
    cos = jnp.tile(jnp.concatenate([cos, cos], axis=1), (rows.n_lat, 1))
    sin = jnp.tile(jnp.concatenate([sin, sin], axis=1), (rows.n_lat, 1))
    cos = jnp.concatenate([jnp.ones((rows.mc, LANE), F32), cos], axis=0)
    sin = jnp.concatenate([jnp.zeros((rows.mc, LANE), F32), sin], axis=0)
    return cos, sin


def kernel(x_prompt, x_sample, state_gla, state_ret, cache_ckv, cache_kpe, c, c_ctx, mod_w, mod_b, norm1_g, norm2_g, ab_w_in, gla_gate_w2, gla_gate_b, ret_decay, gla_norm_g, ret_norm_g, ab_w_out, mla_w_in, mla_q_norm_g, mla_w_uq, mla_kv_norm_g, mla_w_ukv, mla_w_out, ffn_w_in, ffn_conv, ffn_w_out, final_norm_g):
    B, S, D = x_prompt.shape
    NB, T, _ = x_sample.shape
    depth = mod_w.shape[0]
    _, _, _, HA, DK, DV = state_gla.shape
    HB = state_ret.shape[3]
    assert HA == HB and state_ret.shape[4:] == (DK, DV)
    GR = gla_gate_w2.shape[2]
    past = cache_ckv.shape[2]
    q_rank = mla_q_norm_g.shape[1]
    kv_rank = mla_kv_norm_g.shape[1]
    HC = mla_w_uq.shape[2] // (NOPE_DIM + ROPE_DIM)
    F = ffn_w_out.shape[1]
    rows = _Rows(B * S, S, T, NB)
    M = rows.m
    tm = min(512, S * B, T)
    assert rows.mc % tm == 0 and T % tm == 0
    assert S & (S - 1) == 0 and T & (T - 1) == 0

    x = jnp.concatenate([x_prompt.reshape(B * S, D), x_sample.reshape(NB * T, D)], axis=0)
    cond = jnp.concatenate([c_ctx[None, :], c, jnp.zeros((MOD_GROUPS - 1 - NB, D), F32)], axis=0)
    mod = _modulation(cond, mod_w, mod_b)
    mod3 = mod.reshape(depth, MOD_GROUPS * 6, 1, D)

    new_gla, new_ret, new_ckv, new_kpe = [], [], [], []
    for l in range(depth):
        if l % 2 == 0:
            e = l // 2
            names = ("gq", "gk", "gv", "gg", "glr", "rq", "rk", "rv", "rg")
            sizes = (HA * DK, HA * DK, HA * DV, HA * DV, 2 * GR, HB * DK, HB * DK, HB * DV, HB * DV)
            src, o = {}, 0
            for n, sz in zip(names, sizes):
                src[n] = (o, sz)
                o += sz
            order = ("gq", "gk", "gv", "gg", "rq", "rk", "rv", "rg", "glr")
            col, o, parts = {}, 0, []
            for n in order:
                so, sz = src[n]
                parts.append(ab_w_in[e][:, so:so + sz])
                col[n] = (o, sz if n != "glr" else LANE)
                o += sz
            pad = (-o) % LANE
            w_in = jnp.concatenate(parts + [jnp.zeros((D, pad), F32)], axis=1).astype(BF16)
            z = _inproj(rows, x, norm1_g[l], mod3[l], w_in, tm)
            tb = min(256, S)
            o_fwd = None
            finals = []
            for dirn in (0, 1):
                cfg = _ScanCfg(rows, B, tb, HA, DK, DV, backward=bool(dirn))
                gate_w = jnp.zeros((LANE, HA * DK), F32).at[dirn * GR:(dirn + 1) * GR].set(gla_gate_w2[e, dirn])
                rd = jnp.broadcast_to(ret_decay[e, dirn][:, None, None], (HB, 8, LANE))
                s0g = jnp.swapaxes(state_gla[:, e, dirn], -1, -2)
                s0r = jnp.swapaxes(state_ret[:, e, dirn], -1, -2)
                extra = (o_fwd, gla_norm_g[e].reshape(1, -1), ret_norm_g[e].reshape(1, -1)) if dirn else None
                o_dir, sg, sr = _scan(cfg, z, col, gate_w.astype(BF16), gla_gate_b[e, dirn].reshape(1, -1), rd,
                                      s0g, s0r, extra)
                o_fwd = o_dir
                finals.append((jnp.swapaxes(sg, -1, -2), jnp.swapaxes(sr, -1, -2)))
            y = o_fwd
            new_gla.append(jnp.stack([finals[0][0], finals[1][0]], axis=1))
            new_ret.append(jnp.stack([finals[0][1], finals[1][1]], axis=1))
            x = _outproj(rows, y, ab_w_out[e].astype(BF16), x, mod3[l], tm)
        else:
            i = l // 2
            perm = _rope_partner_perm()
            w = mla_w_in[i]
            w_kpe = w[:, q_rank + kv_rank:]
            w_in = jnp.concatenate([w[:, :q_rank + kv_rank], w_kpe, w_kpe, w_kpe[:, perm], w_kpe[:, perm]],
                                   axis=1).astype(BF16)
            col = {"cq": (0, q_rank), "ckv": (q_rank, kv_rank), "kpe": (q_rank + kv_rank, 2 * LANE)}
            zc = _inproj(rows, x, norm1_g[l], mod3[l], w_in, tm, tn_cap=2048)
            cos_t, sin_t = _rope_tables(rows)
            wq = mla_w_uq[i].reshape(q_rank, HC, NOPE_DIM + ROPE_DIM)
            wq_rope = wq[:, :, NOPE_DIM:]
            w_uq = jnp.concatenate([wq[:, :, :NOPE_DIM].reshape(q_rank, -1), wq_rope.reshape(q_rank, -1),
                                    wq_rope[:, :, perm].reshape(q_rank, -1)], axis=1).astype(BF16)
            scale = (NOPE_DIM + ROPE_DIM) ** -0.5
            assert col["cq"][0] == 0
            qcat = _mla_q(zc, mla_q_norm_g[i], w_uq, cos_t, sin_t, HC, scale, min(256, tm))
            wkv = mla_w_ukv[i].reshape(kv_rank, HC, NOPE_DIM + V_DIM)
            w_ukv = jnp.concatenate([wkv[:, :, :NOPE_DIM].reshape(kv_rank, -1),
                                     wkv[:, :, NOPE_DIM:].reshape(kv_rank, -1)], axis=1).astype(BF16)
            assert S % past == 0 and T % past == 0
            keys = _KeyRows(rows, past)
            ckpe = cache_kpe[:, i].reshape(NB * past, ROPE_DIM)
            kcat, v, ckvn = _mla_kv(keys, zc, col, cache_ckv[:, i].reshape(NB * past, kv_rank),
                                    jnp.concatenate([ckpe, ckpe], axis=1), mla_kv_norm_g[i], w_ukv,
                                    cos_t, sin_t, HC)
            tk_lat = past + T
            assert (keys.lat_blocks * past) % S == 0
            tq = min(512, T)
            o_ctx = _attention(qcat, kcat, v, rows.mc, B, HC, S, 1, 0, S, keys.lat_blocks * past // S)
            o_lat = _attention(qcat, kcat, v, NB * T, NB, HC, tq, T // tq, rows.mc // tq, tk_lat, 0)
            o = jnp.concatenate([o_ctx, o_lat], axis=0)
            new_ckv.append(ckvn[keys.lat_blocks * past:].reshape(B, S, kv_rank))
            new_kpe.append(zc[:rows.mc, q_rank + kv_rank:q_rank + kv_rank + ROPE_DIM].reshape(B, S, ROPE_DIM))
            x = _outproj(rows, o, mla_w_out[i].astype(BF16), x, mod3[l], tm)
        tf = _largest_tile(F, 512)
        x = _ffn(rows, x, norm2_g[l], mod3[l], ffn_w_in[l].astype(BF16), ffn_conv[l], ffn_w_out[l].astype(BF16),
                 final_norm_g, tm, tf, final_norm=(l == depth - 1))

    y_prompt = x[:rows.mc].reshape(B, S, D)
    y_sample = x[rows.mc:].reshape(NB, T, D)
    return (y_prompt, y_sample, jnp.stack(new_gla, axis=1), jnp.stack(new_ret, axis=1),
            jnp.stack(new_ckv, axis=1), jnp.stack(new_kpe, axis=1))
```

```python
import functools

import jax
import jax.numpy as jnp
from jax import lax
from jax.experimental import pallas as pl
from jax.experimental.pallas import tpu as pltpu

F32 = jnp.float32
BF16 = jnp.bfloat16

NORM_EPS = 1e-6
GATE_TEMP = 16.0
CHUNK = 64
GRID_W = 64
ROPE_BASE = 10000.0
ROPE_DIM = 64
NOPE_DIM = 128
V_DIM = 128
CONV_W = 3
LOG2_E = 1.4426950408889634

LANE = 128
BF16_SUBLANE = 16
MOD_GROUPS = 8
VMEM_LIMIT = 56 * 1024 * 1024

NT_DIMS = (((1,), (1,)), ((), ()))
TN_DIMS = (((0,), (0,)), ((), ()))


def _dot(a, b):
    return jnp.dot(a, b, preferred_element_type=F32)


def _dot_nt(a, b):
    return lax.dot_general(a, b, NT_DIMS, preferred_element_type=F32)


def _dot_tn(a, b):
    return lax.dot_general(a, b, TN_DIMS, preferred_element_type=F32)


def _params(*sem):
    return pltpu.CompilerParams(dimension_semantics=sem, vmem_limit_bytes=VMEM_LIMIT)


def _rms(x, g):
    ms = jnp.mean(x * x, axis=-1, keepdims=True)
    return (x * lax.rsqrt(ms + NORM_EPS)) * g


def _norm_mod(x, g, scale, shift):
    return _rms(x, g) * (1.0 + scale) + shift


def _silu(x):
    return x * jax.nn.sigmoid(x)


def _log_sigmoid(x):
    return jnp.minimum(x, 0.0) - jnp.log1p(jnp.exp(-jnp.abs(x)))


def _largest_tile(n, cap):
    best = None
    for t in range(LANE, min(n, cap) + 1, LANE):
        if n % t == 0:
            best = t
    assert best is not None, (n, cap)
    return best


def _mod_kernel(c_ref, w_ref, b_ref, o_ref):
    s = _silu(c_ref[...]).astype(BF16)
    o_ref[0] = _dot(s, w_ref[0].astype(BF16)) + b_ref[0]


def _modulation(cond, mod_w, mod_b):
    L, D, N = mod_w.shape
    tn = _largest_tile(N, 1024)
    return pl.pallas_call(
        _mod_kernel,
        grid=(L, N // tn),
        in_specs=[
            pl.BlockSpec((MOD_GROUPS, D), lambda l, j: (0, 0)),
            pl.BlockSpec((1, D, tn), lambda l, j: (l, 0, j)),
            pl.BlockSpec((1, 1, tn), lambda l, j: (l, 0, j)),
        ],
        out_specs=pl.BlockSpec((1, MOD_GROUPS, tn), lambda l, j: (l, 0, j)),
        out_shape=jax.ShapeDtypeStruct((L, MOD_GROUPS, N), F32),
        compiler_params=_params("parallel", "parallel"),
        name="modulation",
    )(cond, mod_w, mod_b.reshape(L, 1, N))


class _Rows:
    def __init__(self, mc, seq, lat_len, n_lat):
        self.mc, self.seq, self.lat_len, self.n_lat = mc, seq, lat_len, n_lat
        self.m = mc + lat_len * n_lat

    def group(self, i, tm):
        r = i * tm
        return jnp.where(r < self.mc, 0, 1 + (r - self.mc) // self.lat_len)

    def mod_spec(self, which, tm, d):
        return pl.BlockSpec((1, 1, d), lambda i, j: (self.group(i, tm) * 6 + which, 0, 0))


def _inproj_kernel(x_ref, g_ref, sh_ref, sc_ref, w_ref, ws_ref, o_ref, os_ref, h_ref):
    @pl.when(pl.program_id(1) == 0)
    def _():
        h_ref[...] = _norm_mod(x_ref[...], g_ref[...], sc_ref[0], sh_ref[0]).astype(BF16)
        os_ref[...] = _dot(h_ref[...], ws_ref[...])

    o_ref[...] = _dot(h_ref[...], w_ref[...]).astype(o_ref.dtype)


def _inproj(rows, x, g, mod3, w, w_side, tm, tn_cap=1024):
    M, D = x.shape
    N = w.shape[1]
    NS = w_side.shape[1]
    tn = _largest_tile(N, tn_cap)
    return pl.pallas_call(
        _inproj_kernel,
        grid=(M // tm, N // tn),
        in_specs=[
            pl.BlockSpec((tm, D), lambda i, j: (i, 0)),
            pl.BlockSpec((1, D), lambda i, j: (0, 0)),
            rows.mod_spec(0, tm, D),
            rows.mod_spec(1, tm, D),
            pl.BlockSpec((D, tn), lambda i, j: (0, j)),
            pl.BlockSpec((D, NS), lambda i, j: (0, 0)),
        ],
        out_specs=[pl.BlockSpec((tm, tn), lambda i, j: (i, j)), pl.BlockSpec((tm, NS), lambda i, j: (i, 0))],
        out_shape=[jax.ShapeDtypeStruct((M, N), F32), jax.ShapeDtypeStruct((M, NS), F32)],
        scratch_shapes=[pltpu.VMEM((tm, D), BF16)],
        compiler_params=_params("parallel", "arbitrary"),
        name="inproj",
    )(x, g.reshape(1, D), mod3, mod3, w, w_side)


def _outproj_kernel(y_ref, w_ref, x_ref, gate_ref, o_ref):
    o_ref[...] = x_ref[...] + gate_ref[0] * _dot(y_ref[...], w_ref[...])


def _outproj(rows, y, w, x, mod3, tm, tn_cap=1024):
    M, K = y.shape
    N = w.shape[1]
    tn = _largest_tile(N, tn_cap)
    return pl.pallas_call(
        _outproj_kernel,
        grid=(M // tm, N // tn),
        in_specs=[
            pl.BlockSpec((tm, K), lambda i, j: (i, 0)),
            pl.BlockSpec((K, tn), lambda i, j: (0, j)),
            pl.BlockSpec((tm, tn), lambda i, j: (i, j)),
            pl.BlockSpec((1, 1, tn), lambda i, j: (rows.group(i, tm) * 6 + 2, 0, j)),
        ],
        out_specs=pl.BlockSpec((tm, tn), lambda i, j: (i, j)),
        out_shape=jax.ShapeDtypeStruct((M, N), F32),
        compiler_params=_params("parallel", "arbitrary"),
        name="outproj",
    )(y, w, x, mod3)


HALO = BF16_SUBLANE


def _ffn_kernel(rows, tm, final_norm, xp_ref, x_ref, xn_ref, g_ref, sh_ref, sc_ref, gate_ref,
                wa_ref, wb_ref, cw_ref, wo_ref, fg_ref, *rest):
    if final_norm:
        o_ctx_ref, o_lat_ref, h_ref, a_ref, acc_ref = rest
    else:
        o_ref, h_ref, a_ref, acc_ref = rest
    i = pl.program_id(0)
    f = pl.program_id(1)

    @pl.when(f == 0)
    def _():
        g, sc, sh = g_ref[...], sc_ref[0], sh_ref[0]
        h_ref[0:HALO, :] = _norm_mod(xp_ref[...], g, sc, sh).astype(BF16)
        h_ref[HALO:HALO + tm, :] = _norm_mod(x_ref[...], g, sc, sh).astype(BF16)
        h_ref[HALO + tm:, :] = _norm_mod(xn_ref[...], g, sc, sh).astype(BF16)
        acc_ref[...] = jnp.zeros_like(acc_ref)

    a_ref[...] = _dot(h_ref[...], wa_ref[...])
    b = _dot(h_ref[HALO:HALO + tm, :], wb_ref[...])
    row = i * tm + lax.broadcasted_iota(jnp.int32, (tm, 1), 0)
    pos = jnp.where(row < rows.mc, row & (rows.seq - 1), (row - rows.mc) & (rows.lat_len - 1))
    seq_len = jnp.where(row < rows.mc, rows.seq, rows.lat_len)
    a_prev = jnp.where(pos == 0, 0.0, a_ref[pl.ds(HALO - 1, tm), :])
    a_next = jnp.where(pos == seq_len - 1, 0.0, a_ref[pl.ds(HALO + 1, tm), :])
    a_mid = a_ref[pl.ds(HALO, tm), :]
    cw = cw_ref[...]
    a = cw[0:1] * a_prev + cw[1:2] * a_mid + cw[2:3] * a_next
    act = (_silu(a) * b).astype(BF16)
    acc_ref[...] += _dot(act, wo_ref[...])

    @pl.when(f == pl.num_programs(1) - 1)
    def _():
        out = x_ref[...] + gate_ref[0] * acc_ref[...]
        if final_norm:
            out = _rms(out, fg_ref[...])
            is_ctx = i * tm < rows.mc

            @pl.when(is_ctx)
            def _():
                o_ctx_ref[...] = out

            @pl.when(jnp.logical_not(is_ctx))
            def _():
                o_lat_ref[...] = out
        else:
            o_ref[...] = out


def _ffn(rows, x, g, mod3, w_in, conv_w, w_out, final_g, tm, tf, final_norm):
    M, D = x.shape
    F = w_out.shape[0]
    nf = F // tf
    nhalo = M // HALO
    kern = functools.partial(_ffn_kernel, rows, tm, final_norm)
    if final_norm:
        n_ctx = rows.mc // tm
        out_specs = [pl.BlockSpec((tm, D), lambda i, f: (jnp.minimum(i, n_ctx - 1), 0)),
                     pl.BlockSpec((tm, D), lambda i, f: (jnp.maximum(i - n_ctx, 0), 0))]
        out_shape = [jax.ShapeDtypeStruct((rows.mc, D), F32), jax.ShapeDtypeStruct((M - rows.mc, D), F32)]
        row_sem = "arbitrary"
    else:
        out_specs = pl.BlockSpec((tm, D), lambda i, f: (i, 0))
        out_shape = jax.ShapeDtypeStruct((M, D), F32)
        row_sem = "parallel"
    return pl.pallas_call(
        kern,
        grid=(M // tm, nf),
        in_specs=[
            pl.BlockSpec((HALO, D), lambda i, f: (jnp.maximum(i * (tm // HALO) - 1, 0), 0)),
            pl.BlockSpec((tm, D), lambda i, f: (i, 0)),
            pl.BlockSpec((HALO, D), lambda i, f: (jnp.minimum((i + 1) * (tm // HALO), nhalo - 1), 0)),
            pl.BlockSpec((1, D), lambda i, f: (0, 0)),
            rows.mod_spec(3, tm, D),
            rows.mod_spec(4, tm, D),
            rows.mod_spec(5, tm, D),
            pl.BlockSpec((D, tf), lambda i, f: (0, f)),
            pl.BlockSpec((D, tf), lambda i, f: (0, nf + f)),
            pl.BlockSpec((CONV_W, tf), lambda i, f: (0, f)),
            pl.BlockSpec((tf, D), lambda i, f: (f, 0)),
            pl.BlockSpec((1, D), lambda i, f: (0, 0)),
        ],
        out_specs=out_specs,
        out_shape=out_shape,
        scratch_shapes=[
            pltpu.VMEM((tm + 2 * HALO, D), BF16),
            pltpu.VMEM((tm + 2 * HALO, tf), F32),
            pltpu.VMEM((tm, D), F32),
        ],
        compiler_params=_params(row_sem, "arbitrary"),
        name="convffn",
    )(x, x, x, g.reshape(1, D), mod3, mod3, mod3, w_in, w_in, conv_w, w_out, final_g.reshape(1, D))


class _ScanCfg:
    def __init__(self, rows, batch, tb, h, dk, dv, backward):
        self.rows, self.batch, self.tb, self.h, self.dk, self.dv = rows, batch, tb, h, dk, dv
        self.backward = backward
        self.cps = rows.seq // tb
        self.lps = rows.lat_len // tb
        self.ctx_blocks = batch * self.cps
        self.nblk = rows.m // tb

    def block(self, i):
        return self.nblk - 1 - i if self.backward else i

    def is_ctx(self, r):
        return r < self.ctx_blocks

    def seq_pos(self, r):
        ctx = self.is_ctx(r)
        return (jnp.where(ctx, r % self.cps, (r - self.ctx_blocks) % self.lps),
                jnp.where(ctx, self.cps, self.lps))

    def lat_seq(self, r):
        return jnp.clip((r - self.ctx_blocks) // self.lps, 0, self.rows.n_lat - 1)

    def ctx_seq(self, r):
        return jnp.clip(r // self.cps, 0, self.batch - 1)


def _scan_kernel(cfg, gq_ref, gk_ref, gv_ref, rq_ref, rk_ref, rv_ref, glr_ref, gw_ref, gb_ref, rd_ref,
                 s0g_ref, s0r_ref, *rest):
    if cfg.backward:
        of_ref, gg_ref, rg_ref, gng_ref, rng_ref, y_ref, sgo_ref, sro_ref, sg_ref, sr_ref = rest
    else:
        o_ref, sgo_ref, sro_ref, sg_ref, sr_ref = rest
    H, DK, DV, C = cfg.h, cfg.dk, cfg.dv, CHUNK
    r = cfg.block(pl.program_id(0))
    blk, nblk_seq = cfg.seq_pos(r)
    first = blk == (nblk_seq - 1 if cfg.backward else 0)
    last = blk == (0 if cfg.backward else nblk_seq - 1)
    is_ctx = cfg.is_ctx(r)

    @pl.when(first & is_ctx)
    def _():
        sg_ref[...] = jnp.zeros_like(sg_ref)
        sr_ref[...] = jnp.zeros_like(sr_ref)

    @pl.when(first & jnp.logical_not(is_ctx))
    def _():
        sg_ref[...] = s0g_ref[0]
        sr_ref[...] = s0r_ref[0]

    ti = lax.broadcasted_iota(jnp.int32, (C, C), 0)
    tj = lax.broadcasted_iota(jnp.int32, (C, C), 1)
    sees = (tj >= ti) if cfg.backward else (tj <= ti)
    tri = jnp.where(sees, 1.0, 0.0).astype(BF16)
    dist = jnp.abs(ti - tj).astype(F32)
    rowi = lax.broadcasted_iota(jnp.int32, (C, LANE), 0)
    to_end = (rowi if cfg.backward else C - 1 - rowi).astype(F32)
    from_start = (C - rowi if cfg.backward else rowi + 1).astype(F32)

    ret_consts = []
    for h in range(H):
        lg = _log_sigmoid(rd_ref[h])
        lg_c = jnp.broadcast_to(lg[0:1, 0:C], (C, C))
        lg_l = jnp.broadcast_to(lg[0:1, :], (C, LANE))
        decay = jnp.where(sees, jnp.exp(lg_c * dist), 0.0)
        q_dec = jnp.exp(lg_l * from_start)
        k_dec = jnp.exp(lg_l * to_end)
        c_dec = jnp.exp(lg[0:1, :] * float(C))
        ret_consts.append((decay, q_dec, k_dec, c_dec))

    n_chunks = cfg.tb // C

    def chunk_body(ci, carry):
        c = n_chunks - 1 - ci if cfg.backward else ci
        rs = pl.ds(pl.multiple_of(c * C, C), C)
        pre = _dot(glr_ref[rs, :].astype(BF16), gw_ref[...]) + gb_ref[...]
        la = _log_sigmoid(pre) / GATE_TEMP
        for h in range(H):
            ks = slice(h * DK, (h + 1) * DK)
            vs = slice(h * DV, (h + 1) * DV)
            g = la[:, ks]
            g_hi = g.astype(BF16)
            g_lo = (g - g_hi.astype(F32)).astype(BF16)
            b = _dot(tri, g_hi) + _dot(tri, g_lo)
            b_end = b[0:1] if cfg.backward else b[C - 1:C]
            q = gq_ref[rs, ks] * (DK ** -0.5)
            k = gk_ref[rs, ks]
            v = gv_ref[rs, vs].astype(BF16)
            q_dec = (q * jnp.exp(b)).astype(BF16)
            k_inv = (k * jnp.exp(-b)).astype(BF16)
            k_end = (k * jnp.exp(b_end - b)).astype(BF16)
            att = jnp.where(sees, _dot_nt(q_dec, k_inv), 0.0)
            st = sg_ref[h]
            o_g = _dot(att.astype(BF16), v) + _dot_nt(q_dec, st.astype(BF16))
            sg_ref[h] = st * jnp.exp(b_end) + _dot_tn(v, k_end)
            decay, rq_dec, rk_dec, rc_dec = ret_consts[h]
            rq = rq_ref[rs, ks].astype(BF16)
            rk = rk_ref[rs, ks] * (DK ** -0.5)
            rv = rv_ref[rs, vs].astype(BF16)
            ratt = _dot_nt(rq, rk.astype(BF16)) * decay
            st = sr_ref[h]
            qs = _dot_nt(rq, st.astype(BF16))
            o_r = _dot(ratt.astype(BF16), rv) + qs * jnp.concatenate([rq_dec] * (DV // LANE), axis=1)
            sr_ref[h] = st * rc_dec + _dot_tn(rv, (rk * rk_dec).astype(BF16))
            gs = slice(h * DV, (h + 1) * DV)
            rs_cols = slice(H * DV + h * DV, H * DV + (h + 1) * DV)
            if cfg.backward:
                for o, cols, gate_ref, ng_ref in ((o_g, gs, gg_ref, gng_ref), (o_r, rs_cols, rg_ref, rng_ref)):
                    tot = o + of_ref[rs, cols]
                    mu = jnp.mean(tot, axis=-1, keepdims=True)
                    d = tot - mu
                    var = jnp.mean(d * d, axis=-1, keepdims=True)
                    yn = d * lax.rsqrt(var + NORM_EPS) * ng_ref[:, gs]
                    y_ref[rs, cols] = (_silu(gate_ref[rs, gs]) * yn).astype(y_ref.dtype)
            else:
                o_ref[rs, gs] = o_g
                o_ref[rs, rs_cols] = o_r
        return carry

    lax.fori_loop(0, n_chunks, chunk_body, 0)

    @pl.when(last & is_ctx)
    def _():
        sgo_ref[0] = sg_ref[...]
        sro_ref[0] = sr_ref[...]


def _scan(cfg, z, glr, col, gate_w, gate_b, ret_decay_t, s0g, s0r, extra):
    H, DK, DV, tb = cfg.h, cfg.dk, cfg.dv, cfg.tb
    M = z.shape[0]

    def zspec(name):
        off, width = col[name]
        assert off % width == 0, (name, off, width)
        return pl.BlockSpec((tb, width), lambda i: (cfg.block(i), off // width))

    def full(shape):
        return pl.BlockSpec(shape, lambda i: (0,) * len(shape))

    state_in = pl.BlockSpec((1, H, DV, DK), lambda i: (cfg.lat_seq(cfg.block(i)), 0, 0, 0))
    state_out = pl.BlockSpec((1, H, DV, DK), lambda i: (cfg.ctx_seq(cfg.block(i)), 0, 0, 0))
    row_blk = pl.BlockSpec((tb, 2 * H * DV), lambda i: (cfg.block(i), 0))
    glr_spec = pl.BlockSpec((tb, glr.shape[1]), lambda i: (cfg.block(i), 0))
    in_specs = [zspec("gq"), zspec("gk"), zspec("gv"), zspec("rq"), zspec("rk"), zspec("rv"), glr_spec,
                full(gate_w.shape), full(gate_b.shape), full(ret_decay_t.shape), state_in, state_in]
    args = [z, z, z, z, z, z, glr, gate_w, gate_b, ret_decay_t, s0g, s0r]
    state_shape = jax.ShapeDtypeStruct((cfg.batch, H, DV, DK), F32)
    if cfg.backward:
        o_fwd, gla_ng, ret_ng = extra
        in_specs += [row_blk, zspec("gg"), zspec("rg"), full(gla_ng.shape), full(ret_ng.shape)]
        args += [o_fwd, z, z, gla_ng, ret_ng]
        out0 = jax.ShapeDtypeStruct((M, 2 * H * DV), BF16)
    else:
        out0 = jax.ShapeDtypeStruct((M, 2 * H * DV), F32)
    return pl.pallas_call(
        functools.partial(_scan_kernel, cfg),
        grid=(cfg.nblk,),
        in_specs=in_specs,
        out_specs=[row_blk, state_out, state_out],
        out_shape=[out0, state_shape, state_shape],
        scratch_shapes=[pltpu.VMEM((H, DV, DK), F32), pltpu.VMEM((H, DV, DK), F32)],
        compiler_params=_params("arbitrary"),
        name="scan_bwd" if cfg.backward else "scan_fwd",
    )(*args)


def _rope_keep(h, shape):
    lane = lax.broadcasted_iota(jnp.int32, shape, 1)
    return (lane < ROPE_DIM) if h % 2 == 0 else (lane >= ROPE_DIM)


def _mla_q_kernel(n_heads, scale, cq_ref, qng_ref, w_ref, cos_ref, sin_ref, o_ref):
    cqn = _rms(cq_ref[...], qng_ref[...]).astype(BF16)
    q = _dot(cqn, w_ref[...])
    nope_w = n_heads * NOPE_DIM
    rope_w = n_heads * ROPE_DIM
    cos, sin = cos_ref[...], sin_ref[...]
    for h in range(n_heads):
        p = h // 2
        tile = q[:, nope_w + p * LANE:nope_w + (p + 1) * LANE]
        partner = q[:, nope_w + rope_w + p * LANE:nope_w + rope_w + (p + 1) * LANE]
        rot = (tile * cos + partner * sin) * scale
        rot = jnp.where(_rope_keep(h, rot.shape), rot, 0.0)
        base = h * 2 * LANE
        o_ref[:, base:base + LANE] = (q[:, h * NOPE_DIM:(h + 1) * NOPE_DIM] * scale).astype(BF16)
        o_ref[:, base + LANE:base + 2 * LANE] = rot.astype(BF16)


def _mla_q(zc, q_ng, w_uq, cos_t, sin_t, n_heads, scale, tm):
    M = zc.shape[0]
    R = q_ng.shape[0]
    N = w_uq.shape[1]
    return pl.pallas_call(
        functools.partial(_mla_q_kernel, n_heads, scale),
        grid=(M // tm,),
        in_specs=[
            pl.BlockSpec((tm, R), lambda i: (i, 0)),
            pl.BlockSpec((1, R), lambda i: (0, 0)),
            pl.BlockSpec((R, N), lambda i: (0, 0)),
            pl.BlockSpec((tm, LANE), lambda i: (i, 0)),
            pl.BlockSpec((tm, LANE), lambda i: (i, 0)),
        ],
        out_specs=pl.BlockSpec((tm, n_heads * 2 * LANE), lambda i: (i, 0)),
        out_shape=jax.ShapeDtypeStruct((M, n_heads * 2 * LANE), BF16),
        compiler_params=_params("parallel"),
        name="mla_q",
    )(zc, q_ng.reshape(1, R), w_uq, cos_t, sin_t)


class _KeyRows:
    def __init__(self, rows, blk):
        self.rows, self.blk = rows, blk
        self.per_lat = 1 + rows.lat_len // blk
        self.lat_blocks = rows.n_lat * self.per_lat
        self.ctx_blocks = rows.mc // blk
        self.n = self.lat_blocks + self.ctx_blocks

    def is_cache(self, j):
        return (j < self.lat_blocks) & (j % self.per_lat == 0)

    def cache_idx(self, j):
        return jnp.clip(j // self.per_lat, 0, self.rows.n_lat - 1)

    def token_block(self, j):
        b = j // self.per_lat
        t = jnp.maximum(j % self.per_lat - 1, 0)
        lat = self.ctx_blocks + b * (self.per_lat - 1) + t
        return jnp.where(j < self.lat_blocks, lat, j - self.lat_blocks)


def _mla_kv_kernel(keys, n_heads, ckv_ref, kpe_ref, cckv_ref, ckpe_ref, kvng_ref, wk_ref, wvt_ref, cos_ref, sin_ref,
                   kcat_ref, vt_ref, ckvn_ref):
    cached = keys.is_cache(pl.program_id(0))
    kpe = kpe_ref[...]
    rot = kpe[:, :LANE] * cos_ref[...] + kpe[:, LANE:] * sin_ref[...]
    ckvn = jnp.where(cached, cckv_ref[...], _rms(ckv_ref[...], kvng_ref[...]))
    kr = jnp.where(cached, ckpe_ref[...], rot)
    ckvn_ref[...] = ckvn
    ckvn_b = ckvn.astype(BF16)
    kn = _dot(ckvn_b, wk_ref[...])
    vt_ref[...] = _dot_nt(wvt_ref[...], ckvn_b).astype(BF16)
    for h in range(n_heads):
        base = h * 2 * LANE
        kcat_ref[:, base:base + LANE] = kn[:, h * NOPE_DIM:(h + 1) * NOPE_DIM].astype(BF16)
        kcat_ref[:, base + LANE:base + 2 * LANE] = jnp.where(_rope_keep(h, kr.shape), kr, 0.0).astype(BF16)


def _mla_kv(keys, zc, zk, col, cache_ckv, cache_kpe2, kv_ng, w_k, w_vt, cos_t, sin_t, n_heads):
    blk = keys.blk
    R = kv_ng.shape[0]
    Mk = keys.n * blk
    ckv_off, ckv_w = col["ckv"]
    assert ckv_off % ckv_w == 0
    tok = keys.token_block
    return pl.pallas_call(
        functools.partial(_mla_kv_kernel, keys, n_heads),
        grid=(keys.n,),
        in_specs=[
            pl.BlockSpec((blk, ckv_w), lambda j: (tok(j), ckv_off // ckv_w)),
            pl.BlockSpec((blk, 2 * LANE), lambda j: (tok(j), 0)),
            pl.BlockSpec((blk, R), lambda j: (keys.cache_idx(j), 0)),
            pl.BlockSpec((blk, LANE), lambda j: (keys.cache_idx(j), 0)),
            pl.BlockSpec((1, R), lambda j: (0, 0)),
            pl.BlockSpec(w_k.shape, lambda j: (0, 0)),
            pl.BlockSpec(w_vt.shape, lambda j: (0, 0)),
            pl.BlockSpec((blk, LANE), lambda j: (tok(j), 0)),
            pl.BlockSpec((blk, LANE), lambda j: (tok(j), 0)),
        ],
        out_specs=[
            pl.BlockSpec((blk, n_heads * 2 * LANE), lambda j: (j, 0)),
            pl.BlockSpec((n_heads * V_DIM, blk), lambda j: (0, j)),
            pl.BlockSpec((blk, R), lambda j: (j, 0)),
        ],
        out_shape=[
            jax.ShapeDtypeStruct((Mk, n_heads * 2 * LANE), BF16),
            jax.ShapeDtypeStruct((n_heads * V_DIM, Mk), BF16),
            jax.ShapeDtypeStruct((Mk, R), F32),
        ],
        compiler_params=_params("parallel"),
        name="mla_kv",
    )(zc, zk, cache_ckv, cache_kpe2, kv_ng.reshape(1, R), w_k, w_vt, cos_t, sin_t)


ATTN_TQ = 256
ATTN_TK = 512


def _attn_kernel(tq, chunks, q_ref, k_ref, vt_ref, o_ref, st0_ref, st1_ref):
    n_sub = q_ref.shape[0] // tq

    st_refs = (st0_ref, st1_ref)

    def scores(u, c0, c1):
        st = _dot_nt(k_ref[c0:c1, :], q_ref[u * tq:(u + 1) * tq, :])
        st_refs[u % 2][c0:c1, :] = st
        return jnp.max(st, axis=0, keepdims=True)

    def weights(u, c0, c1, m):
        p = jnp.exp2(st_refs[u % 2][c0:c1, :] - m)
        return jnp.sum(p, axis=0, keepdims=True), _dot(vt_ref[:, c0:c1], p.astype(BF16))

    m_prev = None
    for u in range(n_sub + 1):
        m_new = acc = l = None
        for c0, c1 in chunks:
            if u < n_sub:
                mc = scores(u, c0, c1)
                m_new = mc if m_new is None else jnp.maximum(m_new, mc)
            if u > 0:
                lc, pv = weights(u - 1, c0, c1, m_prev)
                l = lc if l is None else l + lc
                acc = pv if acc is None else acc + pv
        if u > 0:
            o_ref[(u - 1) * tq:u * tq, :] = (acc / l).T.astype(o_ref.dtype)
        m_prev = m_new


def _attn_kernel_aliased(tq, chunks, q_ref, k_ref, vt_ref, o_prev_ref, o_ref, st0_ref, st1_ref):
    del o_prev_ref
    _attn_kernel(tq, chunks, q_ref, k_ref, vt_ref, o_ref, st0_ref, st1_ref)


def _attention(qcat, kcat, vt, o_prev, n_seq, n_heads, tq, q_blocks_per_seq, q_block0, tk, k_block0):
    sub_q = ATTN_TQ if tq % ATTN_TQ == 0 else LANE
    assert tq % sub_q == 0
    q_spec = pl.BlockSpec((tq, 2 * LANE), lambda s, h, i: (q_block0 + s * q_blocks_per_seq + i, h))
    in_specs = [
        q_spec,
        pl.BlockSpec((tk, 2 * LANE), lambda s, h, i: (k_block0 + s, h)),
        pl.BlockSpec((V_DIM, tk), lambda s, h, i: (h, k_block0 + s)),
    ]
    args = [qcat, kcat, vt]
    chunks = tuple((c0, min(c0 + ATTN_TK, tk)) for c0 in range(0, tk, ATTN_TK))
    kern = functools.partial(_attn_kernel, sub_q, chunks)
    aliases = {}
    if o_prev is not None:
        in_specs.append(pl.BlockSpec(memory_space=pl.ANY))
        args.append(o_prev)
        aliases = {3: 0}
        kern = functools.partial(_attn_kernel_aliased, sub_q, chunks)
    return pl.pallas_call(
        kern,
        grid=(n_seq, n_heads, q_blocks_per_seq),
        in_specs=in_specs,
        out_specs=pl.BlockSpec((tq, V_DIM), lambda s, h, i: (q_block0 + s * q_blocks_per_seq + i, h)),
        out_shape=jax.ShapeDtypeStruct((qcat.shape[0], n_heads * V_DIM), BF16),
        input_output_aliases=aliases,
        scratch_shapes=[pltpu.VMEM((tk, sub_q), F32), pltpu.VMEM((tk, sub_q), F32)],
        compiler_params=_params("parallel", "parallel", "arbitrary"),
        name="mla_attn",
    )(*args)


def _rope_partner_perm():
    q = ROPE_DIM // 4
    d = jnp.arange(ROPE_DIM)
    return jnp.where((d // q) % 2 == 0, d + q, d - q)


def _rope_tables(rows):
    half, quarter = ROPE_DIM // 2, ROPE_DIM // 4
    inv = ROPE_BASE ** (-jnp.arange(quarter, dtype=F32) * 2.0 / half)
    t = jnp.arange(rows.lat_len)
    r = (t // GRID_W).astype(F32)
    c = (t % GRID_W).astype(F32)
    ang_r = r[:, None] * inv[None, :]
    ang_c = c[:, None] * inv[None, :]
    cos = jnp.concatenate([jnp.cos(ang_r), jnp.cos(ang_r), jnp.cos(ang_c), jnp.cos(ang_c)], axis=1)
    sin = jnp.concatenate([-jnp.sin(ang_r), jnp.sin(ang_r), -jnp.sin(ang_c), jnp.sin(ang_c)], axis=1)
    cos = jnp.tile(jnp.concatenate([cos, cos], axis=1), (rows.n_lat, 1))
    sin = jnp.tile(jnp.concatenate([sin, sin], axis=1), (rows.n_lat, 1))
    cos = jnp.concatenate([jnp.ones((rows.mc, LANE), F32), cos], axis=0)
    sin = jnp.concatenate([jnp.zeros((rows.mc, LANE), F32), sin], axis=0)
    return cos, sin


def kernel(x_prompt, x_sample, state_gla, state_ret, cache_ckv, cache_kpe, c, c_ctx, mod_w, mod_b, norm1_g, norm2_g, ab_w_in, gla_gate_w2, gla_gate_b, ret_decay, gla_norm_g, ret_norm_g, ab_w_out, mla_w_in, mla_q_norm_g, mla_w_uq, mla_kv_norm_g, mla_w_ukv, mla_w_out, ffn_w_in, ffn_conv, ffn_w_out, final_norm_g):
    B, S, D = x_prompt.shape
    NB, T, _ = x_sample.shape
    depth = mod_w.shape[0]
    _, _, _, HA, DK, DV = state_gla.shape
    HB = state_ret.shape[3]
    assert HA == HB and state_ret.shape[4:] == (DK, DV)
    GR = gla_gate_w2.shape[2]
    past = cache_ckv.shape[2]
    q_rank = mla_q_norm_g.shape[1]
    kv_rank = mla_kv_norm_g.shape[1]
    HC = mla_w_uq.shape[2] // (NOPE_DIM + ROPE_DIM)
    F = ffn_w_out.shape[1]
    rows = _Rows(B * S, S, T, NB)
    M = rows.m
    tm = min(512, S * B, T)
    assert rows.mc % tm == 0 and T % tm == 0
    assert S & (S - 1) == 0 and T & (T - 1) == 0

    x = jnp.concatenate([x_prompt.reshape(B * S, D), x_sample.reshape(NB * T, D)], axis=0)
    cond = jnp.concatenate([c_ctx[None, :], c, jnp.zeros((MOD_GROUPS - 1 - NB, D), F32)], axis=0)
    mod = _modulation(cond, mod_w, mod_b)
    mod3 = mod.reshape(depth, MOD_GROUPS * 6, 1, D)

    new_gla, new_ret, new_ckv, new_kpe = [], [], [], []
    for l in range(depth):
        if l % 2 == 0:
            e = l // 2
            names = ("gq", "gk", "gv", "gg", "glr", "rq", "rk", "rv", "rg")
            sizes = (HA * DK, HA * DK, HA * DV, HA * DV, 2 * GR, HB * DK, HB * DK, HB * DV, HB * DV)
            src, o = {}, 0
            for n, sz in zip(names, sizes):
                src[n] = (o, sz)
                o += sz
            order = ("gq", "gk", "gv", "gg", "rq", "rk", "rv", "rg")
            col, o, parts = {}, 0, []
            for n in order:
                so, sz = src[n]
                parts.append(ab_w_in[e][:, so:so + sz])
                col[n] = (o, sz)
                o += sz
            w_in = jnp.concatenate(parts, axis=1).astype(BF16)
            so, sz = src["glr"]
            w_glr = jnp.concatenate([ab_w_in[e][:, so:so + sz], jnp.zeros((D, LANE - sz), F32)], axis=1).astype(BF16)
            z, glr = _inproj(rows, x, norm1_g[l], mod3[l], w_in, w_glr, tm)
            tb = min(256, S)
            o_fwd = None
            finals = []
            for dirn in (0, 1):
                cfg = _ScanCfg(rows, B, tb, HA, DK, DV, backward=bool(dirn))
                gate_w = jnp.zeros((LANE, HA * DK), F32).at[dirn * GR:(dirn + 1) * GR].set(gla_gate_w2[e, dirn])
                rd = jnp.broadcast_to(ret_decay[e, dirn][:, None, None], (HB, 8, LANE))
                s0g = jnp.swapaxes(state_gla[:, e, dirn], -1, -2)
                s0r = jnp.swapaxes(state_ret[:, e, dirn], -1, -2)
                extra = (o_fwd, gla_norm_g[e].reshape(1, -1), ret_norm_g[e].reshape(1, -1)) if dirn else None
                o_dir, sg, sr = _scan(cfg, z, glr, col, gate_w.astype(BF16), gla_gate_b[e, dirn].reshape(1, -1), rd,
                                      s0g, s0r, extra)
                o_fwd = o_dir
                finals.append((jnp.swapaxes(sg, -1, -2), jnp.swapaxes(sr, -1, -2)))
            y = o_fwd
            new_gla.append(jnp.stack([finals[0][0], finals[1][0]], axis=1))
            new_ret.append(jnp.stack([finals[0][1], finals[1][1]], axis=1))
            x = _outproj(rows, y, ab_w_out[e].astype(BF16), x, mod3[l], tm)
        else:
            i = l // 2
            perm = _rope_partner_perm()
            w = mla_w_in[i]
            w_kpe = w[:, q_rank + kv_rank:]
            w_in = w[:, :q_rank + kv_rank].astype(BF16)
            w_side = jnp.concatenate([w_kpe, w_kpe, w_kpe[:, perm], w_kpe[:, perm]], axis=1).astype(BF16)
            col = {"cq": (0, q_rank), "ckv": (q_rank, kv_rank)}
            zc, zk = _inproj(rows, x, norm1_g[l], mod3[l], w_in, w_side, tm, tn_cap=2048)
            cos_t, sin_t = _rope_tables(rows)
            wq = mla_w_uq[i].reshape(q_rank, HC, NOPE_DIM + ROPE_DIM)
            wq_rope = wq[:, :, NOPE_DIM:]
            w_uq = jnp.concatenate([wq[:, :, :NOPE_DIM].reshape(q_rank, -1), wq_rope.reshape(q_rank, -1),
                                    wq_rope[:, :, perm].reshape(q_rank, -1)], axis=1).astype(BF16)
            scale = (NOPE_DIM + ROPE_DIM) ** -0.5 * LOG2_E
            assert col["cq"][0] == 0
            qcat = _mla_q(zc, mla_q_norm_g[i], w_uq, cos_t, sin_t, HC, scale, min(256, tm))
            wkv = mla_w_ukv[i].reshape(kv_rank, HC, NOPE_DIM + V_DIM)
            w_k = wkv[:, :, :NOPE_DIM].reshape(kv_rank, -1).astype(BF16)
            w_vt = wkv[:, :, NOPE_DIM:].reshape(kv_rank, -1).T.astype(BF16)
            assert S % past == 0 and T % past == 0
            keys = _KeyRows(rows, past)
            ckpe = cache_kpe[:, i].reshape(NB * past, ROPE_DIM)
            kcat, vt, ckvn = _mla_kv(keys, zc, zk, col, cache_ckv[:, i].reshape(NB * past, kv_rank),
                                     jnp.concatenate([ckpe, ckpe], axis=1), mla_kv_norm_g[i], w_k, w_vt,
                                     cos_t, sin_t, HC)
            tk_lat = past + T
            assert (keys.lat_blocks * past) % S == 0
            tq = min(2048, T)
            o = _attention(qcat, kcat, vt, None, B, HC, S, 1, 0, S, keys.lat_blocks * past // S)
            o = _attention(qcat, kcat, vt, o, NB, HC, tq, T // tq, rows.mc // tq, tk_lat, 0)
            new_ckv.append(ckvn[keys.lat_blocks * past:].reshape(B, S, kv_rank))
            new_kpe.append(zk[:rows.mc, :ROPE_DIM].reshape(B, S, ROPE_DIM))
            x = _outproj(rows, o, mla_w_out[i].astype(BF16), x, mod3[l], tm)
        tf = _largest_tile(F, 512)
        x = _ffn(rows, x, norm2_g[l], mod3[l], ffn_w_in[l].astype(BF16), ffn_conv[l], ffn_w_out[l].astype(BF16),
                 final_norm_g, tm, tf, final_norm=(l == depth - 1))

    y_ctx, y_lat = x
    y_prompt = y_ctx.reshape(B, S, D)
    y_sample = y_lat.reshape(NB, T, D)
    return (y_prompt, y_sample, jnp.stack(new_gla, axis=1), jnp.stack(new_ret, axis=1),
            jnp.stack(new_ckv, axis=1), jnp.stack(new_kpe, axis=1))
```

```python
import functools

import jax
import jax.numpy as jnp
from jax import lax
from jax.experimental import pallas as pl
from jax.experimental.pallas import tpu as pltpu

F32 = jnp.float32
BF16 = jnp.bfloat16

NORM_EPS = 1e-6
GATE_TEMP = 16.0
CHUNK = 64
GRID_W = 64
ROPE_BASE = 10000.0
ROPE_DIM = 64
NOPE_DIM = 128
V_DIM = 128
CONV_W = 3
LOG2_E = 1.4426950408889634

LANE = 128
BF16_SUBLANE = 16
MOD_GROUPS = 8
VMEM_LIMIT = 56 * 1024 * 1024

NT_DIMS = (((1,), (1,)), ((), ()))
TN_DIMS = (((0,), (0,)), ((), ()))


def _dot(a, b):
    return jnp.dot(a, b, preferred_element_type=F32)


def _dot_nt(a, b):
    return lax.dot_general(a, b, NT_DIMS, preferred_element_type=F32)


def _dot_tn(a, b):
    return lax.dot_general(a, b, TN_DIMS, preferred_element_type=F32)


def _params(*sem):
    return pltpu.CompilerParams(dimension_semantics=sem, vmem_limit_bytes=VMEM_LIMIT)


def _rms(x, g):
    ms = jnp.mean(x * x, axis=-1, keepdims=True)
    return (x * lax.rsqrt(ms + NORM_EPS)) * g


def _norm_mod(x, g, scale, shift):
    return _rms(x, g) * (1.0 + scale) + shift


def _silu(x):
    return x * jax.nn.sigmoid(x)


def _log_sigmoid(x):
    return jnp.minimum(x, 0.0) - jnp.log1p(jnp.exp(-jnp.abs(x)))


def _largest_tile(n, cap):
    best = None
    for t in range(LANE, min(n, cap) + 1, LANE):
        if n % t == 0:
            best = t
    assert best is not None, (n, cap)
    return best


def _mod_kernel(c_ref, w_ref, b_ref, o_ref):
    s = _silu(c_ref[...]).astype(BF16)
    o_ref[0] = _dot(s, w_ref[0].astype(BF16)) + b_ref[0]


def _modulation(cond, mod_w, mod_b):
    L, D, N = mod_w.shape
    tn = _largest_tile(N, 1024)
    return pl.pallas_call(
        _mod_kernel,
        grid=(L, N // tn),
        in_specs=[
            pl.BlockSpec((MOD_GROUPS, D), lambda l, j: (0, 0)),
            pl.BlockSpec((1, D, tn), lambda l, j: (l, 0, j)),
            pl.BlockSpec((1, 1, tn), lambda l, j: (l, 0, j)),
        ],
        out_specs=pl.BlockSpec((1, MOD_GROUPS, tn), lambda l, j: (l, 0, j)),
        out_shape=jax.ShapeDtypeStruct((L, MOD_GROUPS, N), F32),
        compiler_params=_params("parallel", "parallel"),
        name="modulation",
    )(cond, mod_w, mod_b.reshape(L, 1, N))


class _Rows:
    def __init__(self, mc, seq, lat_len, n_lat):
        self.mc, self.seq, self.lat_len, self.n_lat = mc, seq, lat_len, n_lat
        self.m = mc + lat_len * n_lat

    def group(self, i, tm):
        r = i * tm
        return jnp.where(r < self.mc, 0, 1 + (r - self.mc) // self.lat_len)

    def mod_spec(self, which, tm, d):
        return pl.BlockSpec((1, 1, d), lambda i, *_: (self.group(i, tm) * 6 + which, 0, 0))

    def split_specs(self, tm, d):
        n_ctx = self.mc // tm
        return [pl.BlockSpec((tm, d), lambda i, *_: (jnp.minimum(i, n_ctx - 1), 0)),
                pl.BlockSpec((tm, d), lambda i, *_: (jnp.maximum(i - n_ctx, 0), 0))]


ROW_CHUNK = 16


def _for_row_chunks(n_rows, body):
    def it(c, carry):
        body(pl.ds(pl.multiple_of(c * ROW_CHUNK, ROW_CHUNK), ROW_CHUNK))
        return carry
    lax.fori_loop(0, n_rows // ROW_CHUNK, it, 0, unroll=2)


def _fold_gain(gs_ref, g_ref, sc_ref):
    gs_ref[...] = g_ref[...] * (1.0 + sc_ref[0])


def _norm_mod_rows(x, gs_ref, sh_ref):
    ms = jnp.mean(x * x, axis=-1, keepdims=True)
    return (x * lax.rsqrt(ms + NORM_EPS)) * gs_ref[...] + sh_ref[0]


def _normmod_kernel(rows, tm, xa_ref, xb_ref, g_ref, sh_ref, sc_ref, h_ref, gs_ref):
    _fold_gain(gs_ref, g_ref, sc_ref)

    def run(x_ref):
        def body(r):
            h_ref[r, :] = _norm_mod_rows(x_ref[r, :], gs_ref, sh_ref).astype(h_ref.dtype)
        _for_row_chunks(tm, body)

    is_ctx = pl.program_id(0) * tm < rows.mc
    pl.when(is_ctx)(lambda: run(xa_ref))
    pl.when(jnp.logical_not(is_ctx))(lambda: run(xb_ref))


def _normmod(rows, x_pair, g, mod3, tm):
    D = x_pair[0].shape[1]
    return pl.pallas_call(
        functools.partial(_normmod_kernel, rows, tm),
        grid=(rows.m // tm,),
        in_specs=rows.split_specs(tm, D) + [
            pl.BlockSpec((1, D), lambda i: (0, 0)),
            rows.mod_spec(0, tm, D),
            rows.mod_spec(1, tm, D),
        ],
        out_specs=pl.BlockSpec((tm, D), lambda i: (i, 0)),
        out_shape=jax.ShapeDtypeStruct((rows.m, D), BF16),
        scratch_shapes=[pltpu.VMEM((1, D), F32)],
        compiler_params=_params("parallel"),
        name="normmod",
    )(*x_pair, g.reshape(1, D), mod3, mod3)


def _inproj_kernel(h_ref, w_ref, ws_ref, o_ref, os_ref):
    @pl.when(pl.program_id(1) == 0)
    def _():
        os_ref[...] = _dot(h_ref[...], ws_ref[...])

    o_ref[...] = _dot(h_ref[...], w_ref[...]).astype(o_ref.dtype)


def _inproj(h, w, w_side, out_dtype, tm, tn_cap=1024):
    M, D = h.shape
    N = w.shape[1]
    NS = w_side.shape[1]
    tn = _largest_tile(N, tn_cap)
    return pl.pallas_call(
        _inproj_kernel,
        grid=(M // tm, N // tn),
        in_specs=[
            pl.BlockSpec((tm, D), lambda i, j: (i, 0)),
            pl.BlockSpec((D, tn), lambda i, j: (0, j)),
            pl.BlockSpec((D, NS), lambda i, j: (0, 0)),
        ],
        out_specs=[pl.BlockSpec((tm, tn), lambda i, j: (i, j)), pl.BlockSpec((tm, NS), lambda i, j: (i, 0))],
        out_shape=[jax.ShapeDtypeStruct((M, N), out_dtype), jax.ShapeDtypeStruct((M, NS), F32)],
        compiler_params=_params("parallel", "arbitrary"),
        name="inproj",
    )(h, w, w_side)


OUTPROJ_PARTS = 4


def _outproj_kernel(rows, tm, split, y_ref, w_ref, *rest):
    if split:
        xa_ref, xb_ref, gate_ref, g_ref, sh_ref, sc_ref, o_ref, h_ref, acc_ref, gs_ref = rest
    else:
        xa_ref, gate_ref, g_ref, sh_ref, sc_ref, o_ref, h_ref, acc_ref, gs_ref = rest
        xb_ref = xa_ref
    _fold_gain(gs_ref, g_ref, sc_ref)
    is_ctx = pl.program_id(0) * tm < rows.mc
    part = tm // OUTPROJ_PARTS

    def finish(r):
        x = jnp.where(is_ctx, xa_ref[r, :], xb_ref[r, :]) if split else xa_ref[r, :]
        x1 = x + gate_ref[0] * acc_ref[r, :]
        o_ref[r, :] = x1
        h_ref[r, :] = _norm_mod_rows(x1, gs_ref, sh_ref).astype(h_ref.dtype)

    for p in range(OUTPROJ_PARTS + 1):
        if p < OUTPROJ_PARTS:
            rp = slice(p * part, (p + 1) * part)
            acc_ref[rp, :] = _dot(y_ref[rp, :], w_ref[...])
        if p > 0:
            for c in range(part // ROW_CHUNK):
                start = (p - 1) * part + c * ROW_CHUNK
                finish(slice(start, start + ROW_CHUNK))


def _outproj(rows, y, w, x, g, mod3, tm):
    M, K = y.shape
    N = w.shape[1]
    split = isinstance(x, (tuple, list))
    x_specs = rows.split_specs(tm, N) if split else [pl.BlockSpec((tm, N), lambda i: (i, 0))]
    x_args = list(x) if split else [x]
    return pl.pallas_call(
        functools.partial(_outproj_kernel, rows, tm, split),
        grid=(M // tm,),
        in_specs=[
            pl.BlockSpec((tm, K), lambda i: (i, 0)),
            pl.BlockSpec((K, N), lambda i: (0, 0)),
        ] + x_specs + [
            rows.mod_spec(2, tm, N),
            pl.BlockSpec((1, N), lambda i: (0, 0)),
            rows.mod_spec(3, tm, N),
            rows.mod_spec(4, tm, N),
        ],
        out_specs=[pl.BlockSpec((tm, N), lambda i: (i, 0)), pl.BlockSpec((tm, N), lambda i: (i, 0))],
        out_shape=[jax.ShapeDtypeStruct((M, N), F32), jax.ShapeDtypeStruct((M, N), BF16)],
        scratch_shapes=[pltpu.VMEM((tm, N), F32), pltpu.VMEM((1, N), F32)],
        compiler_params=_params("parallel"),
        name="outproj",
    )(y, w, *x_args, mod3, g.reshape(1, N), mod3, mod3)


HALO = BF16_SUBLANE


def _ffn_kernel(rows, tm, final_norm, hp_ref, h_ref, hn_ref, x_ref, gate_ref,
                wa_ref, wb_ref, cw_ref, wo_ref, *rest):
    if final_norm:
        fg_ref, o_ctx_ref, o_lat_ref, hs_ref, a_ref, acc_ref = rest
    else:
        g_ref, sh_ref, sc_ref, o_ref, hnext_ref, hs_ref, a_ref, acc_ref, gs_ref = rest
    i = pl.program_id(0)
    f = pl.program_id(1)

    @pl.when(f == 0)
    def _():
        hs_ref[0:HALO, :] = hp_ref[...]
        hs_ref[HALO:HALO + tm, :] = h_ref[...]
        hs_ref[HALO + tm:, :] = hn_ref[...]
        acc_ref[...] = jnp.zeros_like(acc_ref)

    a_ref[...] = _dot(hs_ref[...], wa_ref[...])
    b = _dot(h_ref[...], wb_ref[...])
    row = i * tm + lax.broadcasted_iota(jnp.int32, (tm, 1), 0)
    pos = jnp.where(row < rows.mc, row & (rows.seq - 1), (row - rows.mc) & (rows.lat_len - 1))
    seq_len = jnp.where(row < rows.mc, rows.seq, rows.lat_len)
    a_prev = jnp.where(pos == 0, 0.0, a_ref[pl.ds(HALO - 1, tm), :])
    a_next = jnp.where(pos == seq_len - 1, 0.0, a_ref[pl.ds(HALO + 1, tm), :])
    a_mid = a_ref[pl.ds(HALO, tm), :]
    cw = cw_ref[...]
    a = cw[0:1] * a_prev + cw[1:2] * a_mid + cw[2:3] * a_next
    act = (_silu(a) * b).astype(BF16)
    acc_ref[...] += _dot(act, wo_ref[...])

    @pl.when(f == pl.num_programs(1) - 1)
    def _():
        def residual(r):
            return x_ref[r, :] + gate_ref[0] * acc_ref[r, :]

        if final_norm:
            def run(out_ref):
                def body(r):
                    out_ref[r, :] = _rms(residual(r), fg_ref[...])
                _for_row_chunks(tm, body)

            is_ctx = i * tm < rows.mc
            pl.when(is_ctx)(lambda: run(o_ctx_ref))
            pl.when(jnp.logical_not(is_ctx))(lambda: run(o_lat_ref))
        else:
            _fold_gain(gs_ref, g_ref, sc_ref)

            def body(r):
                x2 = residual(r)
                o_ref[r, :] = x2
                hnext_ref[r, :] = _norm_mod_rows(x2, gs_ref, sh_ref).astype(hnext_ref.dtype)
            _for_row_chunks(tm, body)


def _ffn(rows, h, x, mod3, w_in, conv_w, w_out, tm, tf, next_norm=None, final_g=None):
    M, D = x.shape
    F = w_out.shape[0]
    nf = F // tf
    nhalo = M // HALO
    final_norm = final_g is not None
    kern = functools.partial(_ffn_kernel, rows, tm, final_norm)
    in_specs = [
        pl.BlockSpec((HALO, D), lambda i, f: (jnp.maximum(i * (tm // HALO) - 1, 0), 0)),
        pl.BlockSpec((tm, D), lambda i, f: (i, 0)),
        pl.BlockSpec((HALO, D), lambda i, f: (jnp.minimum((i + 1) * (tm // HALO), nhalo - 1), 0)),
        pl.BlockSpec((tm, D), lambda i, f: (i, 0)),
        rows.mod_spec(5, tm, D),
        pl.BlockSpec((D, tf), lambda i, f: (0, f)),
        pl.BlockSpec((D, tf), lambda i, f: (0, nf + f)),
        pl.BlockSpec((CONV_W, tf), lambda i, f: (0, f)),
        pl.BlockSpec((tf, D), lambda i, f: (f, 0)),
    ]
    args = [h, h, h, x, mod3, w_in, w_in, conv_w, w_out]
    vec = pl.BlockSpec((1, D), lambda i, f: (0, 0))
    if final_norm:
        n_ctx = rows.mc // tm
        in_specs += [vec]
        args += [final_g.reshape(1, D)]
        out_specs = [pl.BlockSpec((tm, D), lambda i, f: (jnp.minimum(i, n_ctx - 1), 0)),
                     pl.BlockSpec((tm, D), lambda i, f: (jnp.maximum(i - n_ctx, 0), 0))]
        out_shape = [jax.ShapeDtypeStruct((rows.mc, D), F32), jax.ShapeDtypeStruct((M - rows.mc, D), F32)]
        row_sem = "arbitrary"
    else:
        g_next, mod3_next = next_norm
        in_specs += [vec, rows.mod_spec(0, tm, D), rows.mod_spec(1, tm, D)]
        args += [g_next.reshape(1, D), mod3_next, mod3_next]
        out_specs = [pl.BlockSpec((tm, D), lambda i, f: (i, 0)), pl.BlockSpec((tm, D), lambda i, f: (i, 0))]
        out_shape = [jax.ShapeDtypeStruct((M, D), F32), jax.ShapeDtypeStruct((M, D), BF16)]
        row_sem = "parallel"
    return pl.pallas_call(
        kern,
        grid=(M // tm, nf),
        in_specs=in_specs,
        out_specs=out_specs,
        out_shape=out_shape,
        scratch_shapes=[
            pltpu.VMEM((tm + 2 * HALO, D), BF16),
            pltpu.VMEM((tm + 2 * HALO, tf), F32),
            pltpu.VMEM((tm, D), F32),
        ] + ([] if final_norm else [pltpu.VMEM((1, D), F32)]),
        compiler_params=_params(row_sem, "arbitrary"),
        name="convffn",
    )(*args)


class _ScanCfg:
    def __init__(self, rows, batch, tb, h, dk, dv, backward):
        self.rows, self.batch, self.tb, self.h, self.dk, self.dv = rows, batch, tb, h, dk, dv
        self.backward = backward
        self.cps = rows.seq // tb
        self.lps = rows.lat_len // tb
        self.ctx_blocks = batch * self.cps
        self.nblk = rows.m // tb

    def block(self, i):
        return self.nblk - 1 - i if self.backward else i

    def is_ctx(self, r):
        return r < self.ctx_blocks

    def seq_pos(self, r):
        ctx = self.is_ctx(r)
        return (jnp.where(ctx, r % self.cps, (r - self.ctx_blocks) % self.lps),
                jnp.where(ctx, self.cps, self.lps))

    def lat_seq(self, r):
        return jnp.clip((r - self.ctx_blocks) // self.lps, 0, self.rows.n_lat - 1)

    def ctx_seq(self, r):
        return jnp.clip(r // self.cps, 0, self.batch - 1)


def _scan_kernel(cfg, gq_ref, gk_ref, gv_ref, rq_ref, rk_ref, rv_ref, glr_ref, gw_ref, gb_ref, rd_ref,
                 s0g_ref, s0r_ref, *rest):
    if cfg.backward:
        of_ref, gg_ref, rg_ref, gng_ref, rng_ref, y_ref, sgo_ref, sro_ref, sg_ref, sr_ref = rest
    else:
        o_ref, sgo_ref, sro_ref, sg_ref, sr_ref = rest
    H, DK, DV, C = cfg.h, cfg.dk, cfg.dv, CHUNK
    r = cfg.block(pl.program_id(0))
    blk, nblk_seq = cfg.seq_pos(r)
    first = blk == (nblk_seq - 1 if cfg.backward else 0)
    last = blk == (0 if cfg.backward else nblk_seq - 1)
    is_ctx = cfg.is_ctx(r)

    @pl.when(first & is_ctx)
    def _():
        sg_ref[...] = jnp.zeros_like(sg_ref)
        sr_ref[...] = jnp.zeros_like(sr_ref)

    @pl.when(first & jnp.logical_not(is_ctx))
    def _():
        sg_ref[...] = s0g_ref[0]
        sr_ref[...] = s0r_ref[0]

    ti = lax.broadcasted_iota(jnp.int32, (C, C), 0)
    tj = lax.broadcasted_iota(jnp.int32, (C, C), 1)
    sees = (tj >= ti) if cfg.backward else (tj <= ti)
    tri = jnp.where(sees, 1.0, 0.0).astype(BF16)
    dist = jnp.abs(ti - tj).astype(F32)
    rowi = lax.broadcasted_iota(jnp.int32, (C, LANE), 0)
    to_end = (rowi if cfg.backward else C - 1 - rowi).astype(F32)
    from_start = (C - rowi if cfg.backward else rowi + 1).astype(F32)

    ret_consts = []
    for h in range(H):
        lg = _log_sigmoid(rd_ref[h])
        lg_c = jnp.broadcast_to(lg[0:1, 0:C], (C, C))
        lg_l = jnp.broadcast_to(lg[0:1, :], (C, LANE))
        decay = jnp.where(sees, jnp.exp(lg_c * dist), 0.0)
        q_dec = jnp.exp(lg_l * from_start)
        k_dec = jnp.exp(lg_l * to_end)
        c_dec = jnp.exp(lg[0:1, :] * float(C))
        ret_consts.append((decay, q_dec, k_dec, c_dec))

    n_chunks = cfg.tb // C

    def chunk_body(ci, carry):
        c = n_chunks - 1 - ci if cfg.backward else ci
        rs = pl.ds(pl.multiple_of(c * C, C), C)
        pre = _dot(glr_ref[rs, :].astype(BF16), gw_ref[...]) + gb_ref[...]
        la = _log_sigmoid(pre) / GATE_TEMP
        for h in range(H):
            ks = slice(h * DK, (h + 1) * DK)
            vs = slice(h * DV, (h + 1) * DV)
            g = la[:, ks]
            g_hi = g.astype(BF16)
            g_lo = (g - g_hi.astype(F32)).astype(BF16)
            b = _dot(tri, g_hi) + _dot(tri, g_lo)
            b_end = b[0:1] if cfg.backward else b[C - 1:C]
            q = gq_ref[rs, ks].astype(F32) * (DK ** -0.5)
            k = gk_ref[rs, ks].astype(F32)
            v = gv_ref[rs, vs].astype(BF16)
            q_dec = (q * jnp.exp(b)).astype(BF16)
            k_inv = (k * jnp.exp(-b)).astype(BF16)
            k_end = (k * jnp.exp(b_end - b)).astype(BF16)
            att = jnp.where(sees, _dot_nt(q_dec, k_inv), 0.0)
            st = sg_ref[h]
            o_g = _dot(att.astype(BF16), v) + _dot_nt(q_dec, st.astype(BF16))
            sg_ref[h] = st * jnp.exp(b_end) + _dot_tn(v, k_end)
            decay, rq_dec, rk_dec, rc_dec = ret_consts[h]
            rq = rq_ref[rs, ks].astype(BF16)
            rk = rk_ref[rs, ks].astype(F32) * (DK ** -0.5)
            rv = rv_ref[rs, vs].astype(BF16)
            ratt = _dot_nt(rq, rk.astype(BF16)) * decay
            st = sr_ref[h]
            qs = _dot_nt(rq, st.astype(BF16))
            o_r = _dot(ratt.astype(BF16), rv) + qs * jnp.concatenate([rq_dec] * (DV // LANE), axis=1)
            sr_ref[h] = st * rc_dec + _dot_tn(rv, (rk * rk_dec).astype(BF16))
            gs = slice(h * DV, (h + 1) * DV)
            rs_cols = slice(H * DV + h * DV, H * DV + (h + 1) * DV)
            if cfg.backward:
                for o, cols, gate_ref, ng_ref in ((o_g, gs, gg_ref, gng_ref), (o_r, rs_cols, rg_ref, rng_ref)):
                    tot = o + of_ref[rs, cols]
                    mu = jnp.mean(tot, axis=-1, keepdims=True)
                    d = tot - mu
                    var = jnp.mean(d * d, axis=-1, keepdims=True)
                    yn = d * lax.rsqrt(var + NORM_EPS) * ng_ref[:, gs]
                    y_ref[rs, cols] = (_silu(gate_ref[rs, gs].astype(F32)) * yn).astype(y_ref.dtype)
            else:
                o_ref[rs, gs] = o_g
                o_ref[rs, rs_cols] = o_r
        return carry

    lax.fori_loop(0, n_chunks, chunk_body, 0)

    @pl.when(last & is_ctx)
    def _():
        sgo_ref[0] = sg_ref[...]
        sro_ref[0] = sr_ref[...]


def _scan(cfg, z, glr, col, gate_w, gate_b, ret_decay_t, s0g, s0r, extra):
    H, DK, DV, tb = cfg.h, cfg.dk, cfg.dv, cfg.tb
    M = z.shape[0]

    def zspec(name):
        off, width = col[name]
        assert off % width == 0, (name, off, width)
        return pl.BlockSpec((tb, width), lambda i: (cfg.block(i), off // width))

    def full(shape):
        return pl.BlockSpec(shape, lambda i: (0,) * len(shape))

    state_in = pl.BlockSpec((1, H, DV, DK), lambda i: (cfg.lat_seq(cfg.block(i)), 0, 0, 0))
    state_out = pl.BlockSpec((1, H, DV, DK), lambda i: (cfg.ctx_seq(cfg.block(i)), 0, 0, 0))
    row_blk = pl.BlockSpec((tb, 2 * H * DV), lambda i: (cfg.block(i), 0))
    glr_spec = pl.BlockSpec((tb, glr.shape[1]), lambda i: (cfg.block(i), 0))
    in_specs = [zspec("gq"), zspec("gk"), zspec("gv"), zspec("rq"), zspec("rk"), zspec("rv"), glr_spec,
                full(gate_w.shape), full(gate_b.shape), full(ret_decay_t.shape), state_in, state_in]
    args = [z, z, z, z, z, z, glr, gate_w, gate_b, ret_decay_t, s0g, s0r]
    state_shape = jax.ShapeDtypeStruct((cfg.batch, H, DV, DK), F32)
    if cfg.backward:
        o_fwd, gla_ng, ret_ng = extra
        in_specs += [row_blk, zspec("gg"), zspec("rg"), full(gla_ng.shape), full(ret_ng.shape)]
        args += [o_fwd, z, z, gla_ng, ret_ng]
        out0 = jax.ShapeDtypeStruct((M, 2 * H * DV), BF16)
    else:
        out0 = jax.ShapeDtypeStruct((M, 2 * H * DV), F32)
    return pl.pallas_call(
        functools.partial(_scan_kernel, cfg),
        grid=(cfg.nblk,),
        in_specs=in_specs,
        out_specs=[row_blk, state_out, state_out],
        out_shape=[out0, state_shape, state_shape],
        scratch_shapes=[pltpu.VMEM((H, DV, DK), F32), pltpu.VMEM((H, DV, DK), F32)],
        compiler_params=_params("arbitrary"),
        name="scan_bwd" if cfg.backward else "scan_fwd",
    )(*args)


def _rope_keep(h, shape):
    lane = lax.broadcasted_iota(jnp.int32, shape, 1)
    return (lane < ROPE_DIM) if h % 2 == 0 else (lane >= ROPE_DIM)


def _mla_q_kernel(n_heads, scale, cq_ref, qng_ref, w_ref, cos_ref, sin_ref, o_ref):
    cqn = _rms(cq_ref[...], qng_ref[...]).astype(BF16)
    q = _dot(cqn, w_ref[...])
    nope_w = n_heads * NOPE_DIM
    rope_w = n_heads * ROPE_DIM
    cos, sin = cos_ref[...], sin_ref[...]
    for h in range(n_heads):
        p = h // 2
        tile = q[:, nope_w + p * LANE:nope_w + (p + 1) * LANE]
        partner = q[:, nope_w + rope_w + p * LANE:nope_w + rope_w + (p + 1) * LANE]
        rot = (tile * cos + partner * sin) * scale
        rot = jnp.where(_rope_keep(h, rot.shape), rot, 0.0)
        base = h * 2 * LANE
        o_ref[:, base:base + LANE] = (q[:, h * NOPE_DIM:(h + 1) * NOPE_DIM] * scale).astype(BF16)
        o_ref[:, base + LANE:base + 2 * LANE] = rot.astype(BF16)


def _mla_q(zc, q_ng, w_uq, cos_t, sin_t, n_heads, scale, tm):
    M = zc.shape[0]
    R = q_ng.shape[0]
    N = w_uq.shape[1]
    return pl.pallas_call(
        functools.partial(_mla_q_kernel, n_heads, scale),
        grid=(M // tm,),
        in_specs=[
            pl.BlockSpec((tm, R), lambda i: (i, 0)),
            pl.BlockSpec((1, R), lambda i: (0, 0)),
            pl.BlockSpec((R, N), lambda i: (0, 0)),
            pl.BlockSpec((tm, LANE), lambda i: (i, 0)),
            pl.BlockSpec((tm, LANE), lambda i: (i, 0)),
        ],
        out_specs=pl.BlockSpec((tm, n_heads * 2 * LANE), lambda i: (i, 0)),
        out_shape=jax.ShapeDtypeStruct((M, n_heads * 2 * LANE), BF16),
        compiler_params=_params("parallel"),
        name="mla_q",
    )(zc, q_ng.reshape(1, R), w_uq, cos_t, sin_t)


class _KeyRows:
    def __init__(self, rows, blk):
        self.rows, self.blk = rows, blk
        self.per_lat = 1 + rows.lat_len // blk
        self.lat_blocks = rows.n_lat * self.per_lat
        self.ctx_blocks = rows.mc // blk
        self.n = self.lat_blocks + self.ctx_blocks

    def is_cache(self, j):
        return (j < self.lat_blocks) & (j % self.per_lat == 0)

    def cache_idx(self, j):
        return jnp.clip(j // self.per_lat, 0, self.rows.n_lat - 1)

    def token_block(self, j):
        b = j // self.per_lat
        t = jnp.maximum(j % self.per_lat - 1, 0)
        lat = self.ctx_blocks + b * (self.per_lat - 1) + t
        return jnp.where(j < self.lat_blocks, lat, j - self.lat_blocks)


def _mla_kv_kernel(keys, n_heads, ckv_ref, kpe_ref, cckv_ref, ckpe_ref, kvng_ref, wk_ref, wvt_ref, cos_ref, sin_ref,
                   kcat_ref, vt_ref, ckvn_ref):
    cached = keys.is_cache(pl.program_id(0))
    kpe = kpe_ref[...]
    rot = kpe[:, :LANE] * cos_ref[...] + kpe[:, LANE:] * sin_ref[...]
    ckvn = jnp.where(cached, cckv_ref[...], _rms(ckv_ref[...], kvng_ref[...]))
    kr = jnp.where(cached, ckpe_ref[...], rot)
    ckvn_ref[...] = ckvn
    ckvn_b = ckvn.astype(BF16)
    kn = _dot(ckvn_b, wk_ref[...])
    vt_ref[...] = _dot_nt(wvt_ref[...], ckvn_b).astype(BF16)
    for h in range(n_heads):
        base = h * 2 * LANE
        kcat_ref[:, base:base + LANE] = kn[:, h * NOPE_DIM:(h + 1) * NOPE_DIM].astype(BF16)
        kcat_ref[:, base + LANE:base + 2 * LANE] = jnp.where(_rope_keep(h, kr.shape), kr, 0.0).astype(BF16)


def _mla_kv(keys, zc, zk, col, cache_ckv, cache_kpe2, kv_ng, w_k, w_vt, cos_t, sin_t, n_heads):
    blk = keys.blk
    R = kv_ng.shape[0]
    Mk = keys.n * blk
    ckv_off, ckv_w = col["ckv"]
    assert ckv_off % ckv_w == 0
    tok = keys.token_block
    return pl.pallas_call(
        functools.partial(_mla_kv_kernel, keys, n_heads),
        grid=(keys.n,),
        in_specs=[
            pl.BlockSpec((blk, ckv_w), lambda j: (tok(j), ckv_off // ckv_w)),
            pl.BlockSpec((blk, 2 * LANE), lambda j: (tok(j), 0)),
            pl.BlockSpec((blk, R), lambda j: (keys.cache_idx(j), 0)),
            pl.BlockSpec((blk, LANE), lambda j: (keys.cache_idx(j), 0)),
            pl.BlockSpec((1, R), lambda j: (0, 0)),
            pl.BlockSpec(w_k.shape, lambda j: (0, 0)),
            pl.BlockSpec(w_vt.shape, lambda j: (0, 0)),
            pl.BlockSpec((blk, LANE), lambda j: (tok(j), 0)),
            pl.BlockSpec((blk, LANE), lambda j: (tok(j), 0)),
        ],
        out_specs=[
            pl.BlockSpec((blk, n_heads * 2 * LANE), lambda j: (j, 0)),
            pl.BlockSpec((n_heads * V_DIM, blk), lambda j: (0, j)),
            pl.BlockSpec((blk, R), lambda j: (j, 0)),
        ],
        out_shape=[
            jax.ShapeDtypeStruct((Mk, n_heads * 2 * LANE), BF16),
            jax.ShapeDtypeStruct((n_heads * V_DIM, Mk), BF16),
            jax.ShapeDtypeStruct((Mk, R), F32),
        ],
        compiler_params=_params("parallel"),
        name="mla_kv",
    )(zc, zk, cache_ckv, cache_kpe2, kv_ng.reshape(1, R), w_k, w_vt, cos_t, sin_t)


ATTN_TQ = 256
ATTN_TK = 512


def _attn_kernel(tq, chunks, q_ref, k_ref, vt_ref, o_ref, st0_ref, st1_ref):
    n_heads = q_ref.shape[1] // (2 * LANE)
    units = [(h, u) for h in range(n_heads) for u in range(q_ref.shape[0] // tq)]
    st_refs = (st0_ref, st1_ref)

    def scores(n, c0, c1):
        h, u = units[n]
        hd = slice(h * 2 * LANE, (h + 1) * 2 * LANE)
        st = _dot_nt(k_ref[c0:c1, hd], q_ref[u * tq:(u + 1) * tq, hd])
        st_refs[n % 2][c0:c1, :] = st
        return jnp.max(st, axis=0, keepdims=True)

    def weights(n, c0, c1, m):
        h, _ = units[n]
        p = jnp.exp2(st_refs[n % 2][c0:c1, :] - m)
        return jnp.sum(p, axis=0, keepdims=True), _dot(vt_ref[h * V_DIM:(h + 1) * V_DIM, c0:c1], p.astype(BF16))

    m_prev = None
    for n in range(len(units) + 1):
        m_new = acc = l = None
        for c0, c1 in chunks:
            if n < len(units):
                mc = scores(n, c0, c1)
                m_new = mc if m_new is None else jnp.maximum(m_new, mc)
            if n > 0:
                lc, pv = weights(n - 1, c0, c1, m_prev)
                l = lc if l is None else l + lc
                acc = pv if acc is None else acc + pv
        if n > 0:
            h, u = units[n - 1]
            o_ref[u * tq:(u + 1) * tq, h * V_DIM:(h + 1) * V_DIM] = (acc / l).T.astype(o_ref.dtype)
        m_prev = m_new


def _attn_kernel_aliased(tq, chunks, q_ref, k_ref, vt_ref, o_prev_ref, o_ref, st0_ref, st1_ref):
    del o_prev_ref
    _attn_kernel(tq, chunks, q_ref, k_ref, vt_ref, o_ref, st0_ref, st1_ref)


def _attention(qcat, kcat, vt, o_prev, n_seq, n_heads, hps, tq, q_blocks_per_seq, q_block0, tk, k_block0):
    sub_q = ATTN_TQ if tq % ATTN_TQ == 0 else LANE
    assert tq % sub_q == 0 and n_heads % hps == 0
    q_spec = pl.BlockSpec((tq, hps * 2 * LANE), lambda s, h, i: (q_block0 + s * q_blocks_per_seq + i, h))
    in_specs = [
        q_spec,
        pl.BlockSpec((tk, hps * 2 * LANE), lambda s, h, i: (k_block0 + s, h)),
        pl.BlockSpec((hps * V_DIM, tk), lambda s, h, i: (h, k_block0 + s)),
    ]
    args = [qcat, kcat, vt]
    chunks = tuple((c0, min(c0 + ATTN_TK, tk)) for c0 in range(0, tk, ATTN_TK))
    kern = functools.partial(_attn_kernel, sub_q, chunks)
    aliases = {}
    if o_prev is not None:
        in_specs.append(pl.BlockSpec(memory_space=pl.ANY))
        args.append(o_prev)
        aliases = {3: 0}
        kern = functools.partial(_attn_kernel_aliased, sub_q, chunks)
    return pl.pallas_call(
        kern,
        grid=(n_seq, n_heads // hps, q_blocks_per_seq),
        in_specs=in_specs,
        out_specs=pl.BlockSpec((tq, hps * V_DIM), lambda s, h, i: (q_block0 + s * q_blocks_per_seq + i, h)),
        out_shape=jax.ShapeDtypeStruct((qcat.shape[0], n_heads * V_DIM), BF16),
        input_output_aliases=aliases,
        scratch_shapes=[pltpu.VMEM((tk, sub_q), F32), pltpu.VMEM((tk, sub_q), F32)],
        compiler_params=_params("parallel", "parallel", "arbitrary"),
        name="mla_attn",
    )(*args)


def _rope_partner_perm():
    q = ROPE_DIM // 4
    d = jnp.arange(ROPE_DIM)
    return jnp.where((d // q) % 2 == 0, d + q, d - q)


def _rope_tables(rows):
    half, quarter = ROPE_DIM // 2, ROPE_DIM // 4
    inv = ROPE_BASE ** (-jnp.arange(quarter, dtype=F32) * 2.0 / half)
    t = jnp.arange(rows.lat_len)
    r = (t // GRID_W).astype(F32)
    c = (t % GRID_W).astype(F32)
    ang_r = r[:, None] * inv[None, :]
    ang_c = c[:, None] * inv[None, :]
    cos = jnp.concatenate([jnp.cos(ang_r), jnp.cos(ang_r), jnp.cos(ang_c), jnp.cos(ang_c)], axis=1)
    sin = jnp.concatenate([-jnp.sin(ang_r), jnp.sin(ang_r), -jnp.sin(ang_c), jnp.sin(ang_c)], axis=1)
    cos = jnp.tile(jnp.concatenate([cos, cos], axis=1), (rows.n_lat, 1))
    sin = jnp.tile(jnp.concatenate([sin, sin], axis=1), (rows.n_lat, 1))
    cos = jnp.concatenate([jnp.ones((rows.mc, LANE), F32), cos], axis=0)
    sin = jnp.concatenate([jnp.zeros((rows.mc, LANE), F32), sin], axis=0)
    return cos, sin


def kernel(x_prompt, x_sample, state_gla, state_ret, cache_ckv, cache_kpe, c, c_ctx, mod_w, mod_b, norm1_g, norm2_g, ab_w_in, gla_gate_w2, gla_gate_b, ret_decay, gla_norm_g, ret_norm_g, ab_w_out, mla_w_in, mla_q_norm_g, mla_w_uq, mla_kv_norm_g, mla_w_ukv, mla_w_out, ffn_w_in, ffn_conv, ffn_w_out, final_norm_g):
    B, S, D = x_prompt.shape
    NB, T, _ = x_sample.shape
    depth = mod_w.shape[0]
    _, _, _, HA, DK, DV = state_gla.shape
    HB = state_ret.shape[3]
    assert HA == HB and state_ret.shape[4:] == (DK, DV)
    GR = gla_gate_w2.shape[2]
    past = cache_ckv.shape[2]
    q_rank = mla_q_norm_g.shape[1]
    kv_rank = mla_kv_norm_g.shape[1]
    HC = mla_w_uq.shape[2] // (NOPE_DIM + ROPE_DIM)
    F = ffn_w_out.shape[1]
    rows = _Rows(B * S, S, T, NB)
    M = rows.m
    tm = min(512, S * B, T)
    assert rows.mc % tm == 0 and T % tm == 0
    assert S & (S - 1) == 0 and T & (T - 1) == 0

    tm_proj = min(1024, S * B, T)
    assert rows.mc % tm_proj == 0 and T % tm_proj == 0
    cond = jnp.concatenate([c_ctx[None, :], c, jnp.zeros((MOD_GROUPS - 1 - NB, D), F32)], axis=0)
    mod = _modulation(cond, mod_w, mod_b)
    mod3 = mod.reshape(depth, MOD_GROUPS * 6, 1, D)

    x = (x_prompt.reshape(B * S, D), x_sample.reshape(NB * T, D))
    h = _normmod(rows, x, norm1_g[0], mod3[0], tm)
    new_gla, new_ret, new_ckv, new_kpe = [], [], [], []
    for l in range(depth):
        if l % 2 == 0:
            e = l // 2
            names = ("gq", "gk", "gv", "gg", "glr", "rq", "rk", "rv", "rg")
            sizes = (HA * DK, HA * DK, HA * DV, HA * DV, 2 * GR, HB * DK, HB * DK, HB * DV, HB * DV)
            src, o = {}, 0
            for n, sz in zip(names, sizes):
                src[n] = (o, sz)
                o += sz
            order = ("gq", "gk", "gv", "gg", "rq", "rk", "rv", "rg")
            col, o, parts = {}, 0, []
            for n in order:
                so, sz = src[n]
                parts.append(ab_w_in[e][:, so:so + sz])
                col[n] = (o, sz)
                o += sz
            w_in = jnp.concatenate(parts, axis=1).astype(BF16)
            so, sz = src["glr"]
            w_glr = jnp.concatenate([ab_w_in[e][:, so:so + sz], jnp.zeros((D, LANE - sz), F32)], axis=1).astype(BF16)
            z, glr = _inproj(h, w_in, w_glr, BF16, tm_proj)
            tb = min(256, S)
            o_fwd = None
            finals = []
            for dirn in (0, 1):
                cfg = _ScanCfg(rows, B, tb, HA, DK, DV, backward=bool(dirn))
                gate_w = jnp.zeros((LANE, HA * DK), F32).at[dirn * GR:(dirn + 1) * GR].set(gla_gate_w2[e, dirn])
                rd = jnp.broadcast_to(ret_decay[e, dirn][:, None, None], (HB, 8, LANE))
                s0g = jnp.swapaxes(state_gla[:, e, dirn], -1, -2)
                s0r = jnp.swapaxes(state_ret[:, e, dirn], -1, -2)
                extra = (o_fwd, gla_norm_g[e].reshape(1, -1), ret_norm_g[e].reshape(1, -1)) if dirn else None
                o_dir, sg, sr = _scan(cfg, z, glr, col, gate_w.astype(BF16), gla_gate_b[e, dirn].reshape(1, -1), rd,
                                      s0g, s0r, extra)
                o_fwd = o_dir
                finals.append((jnp.swapaxes(sg, -1, -2), jnp.swapaxes(sr, -1, -2)))
            y = o_fwd
            new_gla.append(jnp.stack([finals[0][0], finals[1][0]], axis=1))
            new_ret.append(jnp.stack([finals[0][1], finals[1][1]], axis=1))
            x, h = _outproj(rows, y, ab_w_out[e].astype(BF16), x, norm2_g[l], mod3[l], tm)
        else:
            i = l // 2
            perm = _rope_partner_perm()
            w = mla_w_in[i]
            w_kpe = w[:, q_rank + kv_rank:]
            w_in = w[:, :q_rank + kv_rank].astype(BF16)
            w_side = jnp.concatenate([w_kpe, w_kpe, w_kpe[:, perm], w_kpe[:, perm]], axis=1).astype(BF16)
            col = {"cq": (0, q_rank), "ckv": (q_rank, kv_rank)}
            zc, zk = _inproj(h, w_in, w_side, F32, tm_proj)
            cos_t, sin_t = _rope_tables(rows)
            wq = mla_w_uq[i].reshape(q_rank, HC, NOPE_DIM + ROPE_DIM)
            wq_rope = wq[:, :, NOPE_DIM:]
            w_uq = jnp.concatenate([wq[:, :, :NOPE_DIM].reshape(q_rank, -1), wq_rope.reshape(q_rank, -1),
                                    wq_rope[:, :, perm].reshape(q_rank, -1)], axis=1).astype(BF16)
            scale = (NOPE_DIM + ROPE_DIM) ** -0.5 * LOG2_E
            assert col["cq"][0] == 0
            qcat = _mla_q(zc, mla_q_norm_g[i], w_uq, cos_t, sin_t, HC, scale, min(256, tm))
            wkv = mla_w_ukv[i].reshape(kv_rank, HC, NOPE_DIM + V_DIM)
            w_k = wkv[:, :, :NOPE_DIM].reshape(kv_rank, -1).astype(BF16)
            w_vt = wkv[:, :, NOPE_DIM:].reshape(kv_rank, -1).T.astype(BF16)
            assert S % past == 0 and T % past == 0
            keys = _KeyRows(rows, past)
            ckpe = cache_kpe[:, i].reshape(NB * past, ROPE_DIM)
            kcat, vt, ckvn = _mla_kv(keys, zc, zk, col, cache_ckv[:, i].reshape(NB * past, kv_rank),
                                     jnp.concatenate([ckpe, ckpe], axis=1), mla_kv_norm_g[i], w_k, w_vt,
                                     cos_t, sin_t, HC)
            tk_lat = past + T
            assert (keys.lat_blocks * past) % S == 0
            tq = min(2048, T)
            o = _attention(qcat, kcat, vt, None, B, HC, HC, S, 1, 0, S, keys.lat_blocks * past // S)
            o = _attention(qcat, kcat, vt, o, NB, HC, 1, tq, T // tq, rows.mc // tq, tk_lat, 0)
            new_ckv.append(ckvn[keys.lat_blocks * past:].reshape(B, S, kv_rank))
            new_kpe.append(zk[:rows.mc, :ROPE_DIM].reshape(B, S, ROPE_DIM))
            x, h = _outproj(rows, o, mla_w_out[i].astype(BF16), x, norm2_g[l], mod3[l], tm)
        tf = _largest_tile(F, 512)
        ffn_w = (ffn_w_in[l].astype(BF16), ffn_conv[l], ffn_w_out[l].astype(BF16))
        if l == depth - 1:
            y_ctx, y_lat = _ffn(rows, h, x, mod3[l], *ffn_w, tm, tf, final_g=final_norm_g)
        else:
            x, h = _ffn(rows, h, x, mod3[l], *ffn_w, tm, tf, next_norm=(norm1_g[l + 1], mod3[l + 1]))

    y_prompt = y_ctx.reshape(B, S, D)
    y_sample = y_lat.reshape(NB, T, D)
    return (y_prompt, y_sample, jnp.stack(new_gla, axis=1), jnp.stack(new_ret, axis=1),
            jnp.stack(new_ckv, axis=1), jnp.stack(new_kpe, axis=1))
```

```python
import functools

import jax
import jax.numpy as jnp
from jax import lax
from jax.experimental import pallas as pl
from jax.experimental.pallas import tpu as pltpu

F32 = jnp.float32
BF16 = jnp.bfloat16

NORM_EPS = 1e-6
GATE_TEMP = 16.0
CHUNK = 64
GRID_W = 64
ROPE_BASE = 10000.0
ROPE_DIM = 64
NOPE_DIM = 128
V_DIM = 128
CONV_W = 3
LOG2_E = 1.4426950408889634

LANE = 128
BF16_SUBLANE = 16
MOD_GROUPS = 8
VMEM_LIMIT = 56 * 1024 * 1024

NT_DIMS = (((1,), (1,)), ((), ()))
TN_DIMS = (((0,), (0,)), ((), ()))


def _dot(a, b):
    return jnp.dot(a, b, preferred_element_type=F32)


def _dot_nt(a, b):
    return lax.dot_general(a, b, NT_DIMS, preferred_element_type=F32)


def _dot_tn(a, b):
    return lax.dot_general(a, b, TN_DIMS, preferred_element_type=F32)


def _params(*sem):
    return pltpu.CompilerParams(dimension_semantics=sem, vmem_limit_bytes=VMEM_LIMIT)


def _rms(x, g):
    ms = jnp.mean(x * x, axis=-1, keepdims=True)
    return (x * lax.rsqrt(ms + NORM_EPS)) * g


def _norm_mod(x, g, scale, shift):
    return _rms(x, g) * (1.0 + scale) + shift


def _silu(x):
    return x * jax.nn.sigmoid(x)


def _log_sigmoid(x):
    return jnp.minimum(x, 0.0) - jnp.log1p(jnp.exp(-jnp.abs(x)))


def _largest_tile(n, cap):
    best = None
    for t in range(LANE, min(n, cap) + 1, LANE):
        if n % t == 0:
            best = t
    assert best is not None, (n, cap)
    return best


def _mod_kernel(c_ref, w_ref, b_ref, o_ref):
    s = _silu(c_ref[...]).astype(BF16)
    o_ref[0] = _dot(s, w_ref[0].astype(BF16)) + b_ref[0]


def _modulation(cond, mod_w, mod_b):
    L, D, N = mod_w.shape
    tn = _largest_tile(N, 1024)
    return pl.pallas_call(
        _mod_kernel,
        grid=(L, N // tn),
        in_specs=[
            pl.BlockSpec((MOD_GROUPS, D), lambda l, j: (0, 0)),
            pl.BlockSpec((1, D, tn), lambda l, j: (l, 0, j)),
            pl.BlockSpec((1, 1, tn), lambda l, j: (l, 0, j)),
        ],
        out_specs=pl.BlockSpec((1, MOD_GROUPS, tn), lambda l, j: (l, 0, j)),
        out_shape=jax.ShapeDtypeStruct((L, MOD_GROUPS, N), F32),
        compiler_params=_params("parallel", "parallel"),
        name="modulation",
    )(cond, mod_w, mod_b.reshape(L, 1, N))


class _Rows:
    def __init__(self, mc, seq, lat_len, n_lat):
        self.mc, self.seq, self.lat_len, self.n_lat = mc, seq, lat_len, n_lat
        self.m = mc + lat_len * n_lat

    def group(self, i, tm):
        r = i * tm
        return jnp.where(r < self.mc, 0, 1 + (r - self.mc) // self.lat_len)

    def mod_spec(self, which, tm, d):
        return pl.BlockSpec((1, 1, d), lambda i, *_: (self.group(i, tm) * 6 + which, 0, 0))

    def split_specs(self, tm, d):
        n_ctx = self.mc // tm
        return [pl.BlockSpec((tm, d), lambda i, *_: (jnp.minimum(i, n_ctx - 1), 0)),
                pl.BlockSpec((tm, d), lambda i, *_: (jnp.maximum(i - n_ctx, 0), 0))]


ROW_CHUNK = 16


def _for_row_chunks(n_rows, body):
    def it(c, carry):
        body(pl.ds(pl.multiple_of(c * ROW_CHUNK, ROW_CHUNK), ROW_CHUNK))
        return carry
    lax.fori_loop(0, n_rows // ROW_CHUNK, it, 0, unroll=2)


def _fold_gain(gs_ref, g_ref, sc_ref):
    gs_ref[...] = g_ref[...] * (1.0 + sc_ref[0])


def _norm_mod_rows(x, gs_ref, sh_ref):
    ms = jnp.mean(x * x, axis=-1, keepdims=True)
    return (x * lax.rsqrt(ms + NORM_EPS)) * gs_ref[...] + sh_ref[0]


def _normmod_kernel(rows, tm, xa_ref, xb_ref, g_ref, sh_ref, sc_ref, h_ref, gs_ref):
    _fold_gain(gs_ref, g_ref, sc_ref)

    def run(x_ref):
        def body(r):
            h_ref[r, :] = _norm_mod_rows(x_ref[r, :], gs_ref, sh_ref).astype(h_ref.dtype)
        _for_row_chunks(tm, body)

    is_ctx = pl.program_id(0) * tm < rows.mc
    pl.when(is_ctx)(lambda: run(xa_ref))
    pl.when(jnp.logical_not(is_ctx))(lambda: run(xb_ref))


def _normmod(rows, x_pair, g, mod3, tm):
    D = x_pair[0].shape[1]
    return pl.pallas_call(
        functools.partial(_normmod_kernel, rows, tm),
        grid=(rows.m // tm,),
        in_specs=rows.split_specs(tm, D) + [
            pl.BlockSpec((1, D), lambda i: (0, 0)),
            rows.mod_spec(0, tm, D),
            rows.mod_spec(1, tm, D),
        ],
        out_specs=pl.BlockSpec((tm, D), lambda i: (i, 0)),
        out_shape=jax.ShapeDtypeStruct((rows.m, D), BF16),
        scratch_shapes=[pltpu.VMEM((1, D), F32)],
        compiler_params=_params("parallel"),
        name="normmod",
    )(*x_pair, g.reshape(1, D), mod3, mod3)


def _inproj_kernel(h_ref, w_ref, ws_ref, o_ref, os_ref):
    @pl.when(pl.program_id(1) == 0)
    def _():
        os_ref[...] = _dot(h_ref[...], ws_ref[...])

    o_ref[...] = _dot(h_ref[...], w_ref[...]).astype(o_ref.dtype)


def _inproj(h, w, w_side, out_dtype, tm, tn_cap=1024):
    M, D = h.shape
    N = w.shape[1]
    NS = w_side.shape[1]
    tn = _largest_tile(N, tn_cap)
    return pl.pallas_call(
        _inproj_kernel,
        grid=(M // tm, N // tn),
        in_specs=[
            pl.BlockSpec((tm, D), lambda i, j: (i, 0)),
            pl.BlockSpec((D, tn), lambda i, j: (0, j)),
            pl.BlockSpec((D, NS), lambda i, j: (0, 0)),
        ],
        out_specs=[pl.BlockSpec((tm, tn), lambda i, j: (i, j)), pl.BlockSpec((tm, NS), lambda i, j: (i, 0))],
        out_shape=[jax.ShapeDtypeStruct((M, N), out_dtype), jax.ShapeDtypeStruct((M, NS), F32)],
        compiler_params=_params("parallel", "arbitrary"),
        name="inproj",
    )(h, w, w_side)


OUTPROJ_PARTS = 4


def _outproj_kernel(rows, tm, split, y_ref, w_ref, *rest):
    if split:
        xa_ref, xb_ref, gate_ref, g_ref, sh_ref, sc_ref, o_ref, h_ref, acc_ref, gs_ref = rest
    else:
        xa_ref, gate_ref, g_ref, sh_ref, sc_ref, o_ref, h_ref, acc_ref, gs_ref = rest
        xb_ref = xa_ref
    _fold_gain(gs_ref, g_ref, sc_ref)
    is_ctx = pl.program_id(0) * tm < rows.mc
    part = tm // OUTPROJ_PARTS

    def finish(r):
        x = jnp.where(is_ctx, xa_ref[r, :], xb_ref[r, :]) if split else xa_ref[r, :]
        x1 = x + gate_ref[0] * acc_ref[r, :]
        o_ref[r, :] = x1
        h_ref[r, :] = _norm_mod_rows(x1, gs_ref, sh_ref).astype(h_ref.dtype)

    for p in range(OUTPROJ_PARTS + 1):
        if p < OUTPROJ_PARTS:
            rp = slice(p * part, (p + 1) * part)
            acc_ref[rp, :] = _dot(y_ref[rp, :], w_ref[...])
        if p > 0:
            for c in range(part // ROW_CHUNK):
                start = (p - 1) * part + c * ROW_CHUNK
                finish(slice(start, start + ROW_CHUNK))


def _outproj(rows, y, w, x, g, mod3, tm):
    M, K = y.shape
    N = w.shape[1]
    split = isinstance(x, (tuple, list))
    x_specs = rows.split_specs(tm, N) if split else [pl.BlockSpec((tm, N), lambda i: (i, 0))]
    x_args = list(x) if split else [x]
    return pl.pallas_call(
        functools.partial(_outproj_kernel, rows, tm, split),
        grid=(M // tm,),
        in_specs=[
            pl.BlockSpec((tm, K), lambda i: (i, 0)),
            pl.BlockSpec((K, N), lambda i: (0, 0)),
        ] + x_specs + [
            rows.mod_spec(2, tm, N),
            pl.BlockSpec((1, N), lambda i: (0, 0)),
            rows.mod_spec(3, tm, N),
            rows.mod_spec(4, tm, N),
        ],
        out_specs=[pl.BlockSpec((tm, N), lambda i: (i, 0)), pl.BlockSpec((tm, N), lambda i: (i, 0))],
        out_shape=[jax.ShapeDtypeStruct((M, N), F32), jax.ShapeDtypeStruct((M, N), BF16)],
        scratch_shapes=[pltpu.VMEM((tm, N), F32), pltpu.VMEM((1, N), F32)],
        compiler_params=_params("parallel"),
        name="outproj",
    )(y, w, *x_args, mod3, g.reshape(1, N), mod3, mod3)


HALO = BF16_SUBLANE


def _ffn_kernel(rows, tm, final_norm, hp_ref, h_ref, hn_ref, x_ref, gate_ref,
                wa_ref, wb_ref, cw_ref, wo_ref, *rest):
    if final_norm:
        fg_ref, o_ctx_ref, o_lat_ref, hs_ref, a_ref, acc_ref = rest
    else:
        g_ref, sh_ref, sc_ref, o_ref, hnext_ref, hs_ref, a_ref, acc_ref, gs_ref = rest
    i = pl.program_id(0)
    f = pl.program_id(1)

    @pl.when(f == 0)
    def _():
        hs_ref[0:HALO, :] = hp_ref[...]
        hs_ref[HALO:HALO + tm, :] = h_ref[...]
        hs_ref[HALO + tm:, :] = hn_ref[...]
        acc_ref[...] = jnp.zeros_like(acc_ref)

    a_ref[...] = _dot(hs_ref[...], wa_ref[0])
    b = _dot(h_ref[...], wb_ref[0])
    row = i * tm + lax.broadcasted_iota(jnp.int32, (tm, 1), 0)
    pos = jnp.where(row < rows.mc, row & (rows.seq - 1), (row - rows.mc) & (rows.lat_len - 1))
    seq_len = jnp.where(row < rows.mc, rows.seq, rows.lat_len)
    a_prev = jnp.where(pos == 0, 0.0, a_ref[pl.ds(HALO - 1, tm), :])
    a_next = jnp.where(pos == seq_len - 1, 0.0, a_ref[pl.ds(HALO + 1, tm), :])
    a_mid = a_ref[pl.ds(HALO, tm), :]
    cw = cw_ref[...]
    a = cw[0:1] * a_prev + cw[1:2] * a_mid + cw[2:3] * a_next
    act = (_silu(a) * b).astype(BF16)
    acc_ref[...] += _dot(act, wo_ref[...])

    @pl.when(f == pl.num_programs(1) - 1)
    def _():
        def residual(r):
            return x_ref[r, :] + gate_ref[0] * acc_ref[r, :]

        if final_norm:
            def run(out_ref):
                def body(r):
                    out_ref[r, :] = _rms(residual(r), fg_ref[...])
                _for_row_chunks(tm, body)

            is_ctx = i * tm < rows.mc
            pl.when(is_ctx)(lambda: run(o_ctx_ref))
            pl.when(jnp.logical_not(is_ctx))(lambda: run(o_lat_ref))
        else:
            _fold_gain(gs_ref, g_ref, sc_ref)

            def body(r):
                x2 = residual(r)
                o_ref[r, :] = x2
                hnext_ref[r, :] = _norm_mod_rows(x2, gs_ref, sh_ref).astype(hnext_ref.dtype)
            _for_row_chunks(tm, body)


def _ffn(rows, h, x, mod3, w_in, conv_w, w_out, tm, tf, next_norm=None, final_g=None):
    M, D = x.shape
    F = w_out.shape[0]
    nf = F // tf
    nhalo = M // HALO
    final_norm = final_g is not None
    kern = functools.partial(_ffn_kernel, rows, tm, final_norm)
    in_specs = [
        pl.BlockSpec((HALO, D), lambda i, f: (jnp.maximum(i * (tm // HALO) - 1, 0), 0)),
        pl.BlockSpec((tm, D), lambda i, f: (i, 0)),
        pl.BlockSpec((HALO, D), lambda i, f: (jnp.minimum((i + 1) * (tm // HALO), nhalo - 1), 0)),
        pl.BlockSpec((tm, D), lambda i, f: (i, 0)),
        rows.mod_spec(5, tm, D),
        pl.BlockSpec((1, D, tf), lambda i, f: (f, 0, 0)),
        pl.BlockSpec((1, D, tf), lambda i, f: (nf + f, 0, 0)),
        pl.BlockSpec((CONV_W, tf), lambda i, f: (0, f)),
        pl.BlockSpec((tf, D), lambda i, f: (f, 0)),
    ]
    args = [h, h, h, x, mod3, w_in, w_in, conv_w, w_out]
    vec = pl.BlockSpec((1, D), lambda i, f: (0, 0))
    if final_norm:
        n_ctx = rows.mc // tm
        in_specs += [vec]
        args += [final_g.reshape(1, D)]
        out_specs = [pl.BlockSpec((tm, D), lambda i, f: (jnp.minimum(i, n_ctx - 1), 0)),
                     pl.BlockSpec((tm, D), lambda i, f: (jnp.maximum(i - n_ctx, 0), 0))]
        out_shape = [jax.ShapeDtypeStruct((rows.mc, D), F32), jax.ShapeDtypeStruct((M - rows.mc, D), F32)]
        row_sem = "arbitrary"
    else:
        g_next, mod3_next = next_norm
        in_specs += [vec, rows.mod_spec(0, tm, D), rows.mod_spec(1, tm, D)]
        args += [g_next.reshape(1, D), mod3_next, mod3_next]
        out_specs = [pl.BlockSpec((tm, D), lambda i, f: (i, 0)), pl.BlockSpec((tm, D), lambda i, f: (i, 0))]
        out_shape = [jax.ShapeDtypeStruct((M, D), F32), jax.ShapeDtypeStruct((M, D), BF16)]
        row_sem = "parallel"
    return pl.pallas_call(
        kern,
        grid=(M // tm, nf),
        in_specs=in_specs,
        out_specs=out_specs,
        out_shape=out_shape,
        scratch_shapes=[
            pltpu.VMEM((tm + 2 * HALO, D), BF16),
            pltpu.VMEM((tm + 2 * HALO, tf), F32),
            pltpu.VMEM((tm, D), F32),
        ] + ([] if final_norm else [pltpu.VMEM((1, D), F32)]),
        compiler_params=_params(row_sem, "arbitrary"),
        name="convffn",
    )(*args)


class _ScanCfg:
    def __init__(self, rows, batch, tb, h, dk, dv, backward):
        self.rows, self.batch, self.tb, self.h, self.dk, self.dv = rows, batch, tb, h, dk, dv
        self.backward = backward
        self.cps = rows.seq // tb
        self.lps = rows.lat_len // tb
        self.ctx_blocks = batch * self.cps
        self.nblk = rows.m // tb

    def block(self, i):
        return self.nblk - 1 - i if self.backward else i

    def is_ctx(self, r):
        return r < self.ctx_blocks

    def seq_pos(self, r):
        ctx = self.is_ctx(r)
        return (jnp.where(ctx, r % self.cps, (r - self.ctx_blocks) % self.lps),
                jnp.where(ctx, self.cps, self.lps))

    def lat_seq(self, r):
        return jnp.clip((r - self.ctx_blocks) // self.lps, 0, self.rows.n_lat - 1)

    def ctx_seq(self, r):
        return jnp.clip(r // self.cps, 0, self.batch - 1)


def _scan_kernel(cfg, gq_ref, gk_ref, gv_ref, rq_ref, rk_ref, rv_ref, glr_ref, gw_ref, gb_ref, rd_ref,
                 s0g_ref, s0r_ref, *rest):
    scratch = rest[-8:]
    sg_ref, sr_ref, eb_ref, qd_ref, kinv_ref, kend_ref, rks_ref, rkd_ref = scratch
    if cfg.backward:
        of_ref, gg_ref, rg_ref, gng_ref, rng_ref, y_ref, sgo_ref, sro_ref = rest[:-8]
    else:
        o_ref, sgo_ref, sro_ref = rest[:-8]
    H, DK, DV, C = cfg.h, cfg.dk, cfg.dv, CHUNK
    r = cfg.block(pl.program_id(0))
    blk, nblk_seq = cfg.seq_pos(r)
    first = blk == (nblk_seq - 1 if cfg.backward else 0)
    last = blk == (0 if cfg.backward else nblk_seq - 1)
    is_ctx = cfg.is_ctx(r)

    @pl.when(first & is_ctx)
    def _():
        sg_ref[...] = jnp.zeros_like(sg_ref)
        sr_ref[...] = jnp.zeros_like(sr_ref)

    @pl.when(first & jnp.logical_not(is_ctx))
    def _():
        sg_ref[...] = s0g_ref[0]
        sr_ref[...] = s0r_ref[0]

    ti = lax.broadcasted_iota(jnp.int32, (C, C), 0)
    tj = lax.broadcasted_iota(jnp.int32, (C, C), 1)
    sees = (tj >= ti) if cfg.backward else (tj <= ti)
    dist = jnp.abs(ti - tj).astype(F32)
    rowi = lax.broadcasted_iota(jnp.int32, (C, LANE), 0)
    to_end = (rowi if cfg.backward else C - 1 - rowi).astype(F32)
    from_start = (C - rowi if cfg.backward else rowi + 1).astype(F32)

    n_chunks = cfg.tb // C
    HDK = H * DK
    scale = DK ** -0.5

    bi = lax.broadcasted_iota(jnp.int32, (cfg.tb, cfg.tb), 0)
    bj = lax.broadcasted_iota(jnp.int32, (cfg.tb, cfg.tb), 1)
    same_chunk = (bi & -C) == (bj & -C)
    blk_tri = jnp.where(same_chunk & ((bj >= bi) if cfg.backward else (bj <= bi)), 1.0, 0.0).astype(BF16)
    pre = _dot(glr_ref[...].astype(BF16), gw_ref[...]) + gb_ref[...]
    la = _log_sigmoid(pre) / GATE_TEMP
    la_hi = la.astype(BF16)
    la_lo = (la - la_hi.astype(F32)).astype(BF16)
    b = _dot(blk_tri, la_hi) + _dot(blk_tri, la_lo)
    b3 = b.reshape(n_chunks, C, HDK)
    b_end = b3[:, 0:1, :] if cfg.backward else b3[:, C - 1:C, :]
    eb = jnp.exp(b)
    eb_ref[...] = eb
    gk = gk_ref[...].astype(F32)
    qd_ref[...] = (gq_ref[...].astype(F32) * scale * eb).astype(BF16)
    kinv_ref[...] = (gk * jnp.exp(-b)).astype(BF16)
    kend_ref[...] = (gk * jnp.exp(b_end - b3).reshape(cfg.tb, HDK)).astype(BF16)

    decays, q_decs, c_decs, k_dec_cols = [], [], [], []
    for h in range(H):
        lg = _log_sigmoid(rd_ref[h])
        lg_c = jnp.broadcast_to(lg[0:1, 0:C], (C, C))
        lg_l = jnp.broadcast_to(lg[0:1, :], (C, LANE))
        decays.append(jnp.where(sees, jnp.exp(lg_c * dist), 0.0))
        q_decs.append(jnp.concatenate([jnp.exp(lg_l * from_start)] * (DV // LANE), axis=1))
        c_decs.append(jnp.exp(lg[0:1, :] * float(C)))
        k_dec_cols.append(jnp.concatenate([jnp.exp(lg_l * to_end)] * n_chunks, axis=0))
    rk = rk_ref[...].astype(F32) * scale
    rks_ref[...] = rk.astype(BF16)
    rkd_ref[...] = (rk * jnp.concatenate(k_dec_cols, axis=1)).astype(BF16)

    for ci in range(n_chunks):
        c = n_chunks - 1 - ci if cfg.backward else ci
        rs = slice(c * C, (c + 1) * C)
        end_row = c * C if cfg.backward else (c + 1) * C - 1
        heads = range(H)
        ksl = [slice(h * DK, (h + 1) * DK) for h in heads]
        vsl = [slice(h * DV, (h + 1) * DV) for h in heads]
        att_raw = [_dot_nt(qd_ref[rs, ksl[h]], kinv_ref[rs, ksl[h]]) for h in heads]
        ratt_raw = [_dot_nt(rq_ref[rs, ksl[h]], rks_ref[rs, ksl[h]]) for h in heads]
        qs_g = [_dot_nt(qd_ref[rs, ksl[h]], sg_ref[h].astype(BF16)) for h in heads]
        qs_r = [_dot_nt(rq_ref[rs, ksl[h]], sr_ref[h].astype(BF16)) for h in heads]
        for h in heads:
            att = jnp.where(sees, att_raw[h], 0.0).astype(BF16)
            o_g = _dot(att, gv_ref[rs, vsl[h]]) + qs_g[h]
            ratt = (ratt_raw[h] * decays[h]).astype(BF16)
            o_r = _dot(ratt, rv_ref[rs, vsl[h]]) + qs_r[h] * q_decs[h]
            rs_cols = slice(H * DV + h * DV, H * DV + (h + 1) * DV)
            if cfg.backward:
                for o, cols, gate_ref, ng_ref in ((o_g, vsl[h], gg_ref, gng_ref), (o_r, rs_cols, rg_ref, rng_ref)):
                    tot = o + of_ref[rs, cols]
                    mu = jnp.mean(tot, axis=-1, keepdims=True)
                    d = tot - mu
                    var = jnp.mean(d * d, axis=-1, keepdims=True)
                    yn = d * lax.rsqrt(var + NORM_EPS) * ng_ref[:, vsl[h]]
                    y_ref[rs, cols] = (_silu(gate_ref[rs, vsl[h]].astype(F32)) * yn).astype(y_ref.dtype)
            else:
                o_ref[rs, vsl[h]] = o_g
                o_ref[rs, rs_cols] = o_r
        for h in heads:
            sg_ref[h] = (sg_ref[h] * eb_ref[end_row:end_row + 1, ksl[h]]
                         + _dot_tn(gv_ref[rs, vsl[h]], kend_ref[rs, ksl[h]]))
            sr_ref[h] = sr_ref[h] * c_decs[h] + _dot_tn(rv_ref[rs, vsl[h]], rkd_ref[rs, ksl[h]])

    @pl.when(last & is_ctx)
    def _():
        sgo_ref[0] = sg_ref[...]
        sro_ref[0] = sr_ref[...]


def _scan(cfg, z, glr, col, gate_w, gate_b, ret_decay_t, s0g, s0r, extra):
    H, DK, DV, tb = cfg.h, cfg.dk, cfg.dv, cfg.tb
    M = z.shape[0]

    def zspec(name):
        off, width = col[name]
        assert off % width == 0, (name, off, width)
        return pl.BlockSpec((tb, width), lambda i: (cfg.block(i), off // width))

    def full(shape):
        return pl.BlockSpec(shape, lambda i: (0,) * len(shape))

    state_in = pl.BlockSpec((1, H, DV, DK), lambda i: (cfg.lat_seq(cfg.block(i)), 0, 0, 0))
    state_out = pl.BlockSpec((1, H, DV, DK), lambda i: (cfg.ctx_seq(cfg.block(i)), 0, 0, 0))
    row_blk = pl.BlockSpec((tb, 2 * H * DV), lambda i: (cfg.block(i), 0))
    glr_spec = pl.BlockSpec((tb, glr.shape[1]), lambda i: (cfg.block(i), 0))
    in_specs = [zspec("gq"), zspec("gk"), zspec("gv"), zspec("rq"), zspec("rk"), zspec("rv"), glr_spec,
                full(gate_w.shape), full(gate_b.shape), full(ret_decay_t.shape), state_in, state_in]
    args = [z, z, z, z, z, z, glr, gate_w, gate_b, ret_decay_t, s0g, s0r]
    state_shape = jax.ShapeDtypeStruct((cfg.batch, H, DV, DK), F32)
    if cfg.backward:
        o_fwd, gla_ng, ret_ng = extra
        in_specs += [row_blk, zspec("gg"), zspec("rg"), full(gla_ng.shape), full(ret_ng.shape)]
        args += [o_fwd, z, z, gla_ng, ret_ng]
        out0 = jax.ShapeDtypeStruct((M, 2 * H * DV), BF16)
    else:
        out0 = jax.ShapeDtypeStruct((M, 2 * H * DV), F32)
    return pl.pallas_call(
        functools.partial(_scan_kernel, cfg),
        grid=(cfg.nblk,),
        in_specs=in_specs,
        out_specs=[row_blk, state_out, state_out],
        out_shape=[out0, state_shape, state_shape],
        scratch_shapes=[pltpu.VMEM((H, DV, DK), F32), pltpu.VMEM((H, DV, DK), F32),
                        pltpu.VMEM((tb, H * DK), F32)] + [pltpu.VMEM((tb, H * DK), BF16)] * 5,
        compiler_params=_params("arbitrary"),
        name="scan_bwd" if cfg.backward else "scan_fwd",
    )(*args)


def _rope_keep(h, shape):
    lane = lax.broadcasted_iota(jnp.int32, shape, 1)
    return (lane < ROPE_DIM) if h % 2 == 0 else (lane >= ROPE_DIM)


def _mla_q_kernel(n_heads, scale, cq_ref, qng_ref, w_ref, cos_ref, sin_ref, o_ref):
    cqn = _rms(cq_ref[...], qng_ref[...]).astype(BF16)
    q = _dot(cqn, w_ref[...])
    nope_w = n_heads * NOPE_DIM
    rope_w = n_heads * ROPE_DIM
    cos, sin = cos_ref[...], sin_ref[...]
    for h in range(n_heads):
        p = h // 2
        tile = q[:, nope_w + p * LANE:nope_w + (p + 1) * LANE]
        partner = q[:, nope_w + rope_w + p * LANE:nope_w + rope_w + (p + 1) * LANE]
        rot = (tile * cos + partner * sin) * scale
        rot = jnp.where(_rope_keep(h, rot.shape), rot, 0.0)
        base = h * 2 * LANE
        o_ref[:, base:base + LANE] = (q[:, h * NOPE_DIM:(h + 1) * NOPE_DIM] * scale).astype(BF16)
        o_ref[:, base + LANE:base + 2 * LANE] = rot.astype(BF16)


def _mla_q(zc, q_ng, w_uq, cos_t, sin_t, n_heads, scale, tm):
    M = zc.shape[0]
    R = q_ng.shape[0]
    N = w_uq.shape[1]
    return pl.pallas_call(
        functools.partial(_mla_q_kernel, n_heads, scale),
        grid=(M // tm,),
        in_specs=[
            pl.BlockSpec((tm, R), lambda i: (i, 0)),
            pl.BlockSpec((1, R), lambda i: (0, 0)),
            pl.BlockSpec((R, N), lambda i: (0, 0)),
            pl.BlockSpec((tm, LANE), lambda i: (i, 0)),
            pl.BlockSpec((tm, LANE), lambda i: (i, 0)),
        ],
        out_specs=pl.BlockSpec((tm, n_heads * 2 * LANE), lambda i: (i, 0)),
        out_shape=jax.ShapeDtypeStruct((M, n_heads * 2 * LANE), BF16),
        compiler_params=_params("parallel"),
        name="mla_q",
    )(zc, q_ng.reshape(1, R), w_uq, cos_t, sin_t)


class _KeyRows:
    def __init__(self, rows, blk):
        self.rows, self.blk = rows, blk
        self.per_lat = 1 + rows.lat_len // blk
        self.lat_blocks = rows.n_lat * self.per_lat
        self.ctx_blocks = rows.mc // blk
        self.n = self.lat_blocks + self.ctx_blocks

    def is_cache(self, j):
        return (j < self.lat_blocks) & (j % self.per_lat == 0)

    def cache_idx(self, j):
        return jnp.clip(j // self.per_lat, 0, self.rows.n_lat - 1)

    def token_block(self, j):
        b = j // self.per_lat
        t = jnp.maximum(j % self.per_lat - 1, 0)
        lat = self.ctx_blocks + b * (self.per_lat - 1) + t
        return jnp.where(j < self.lat_blocks, lat, j - self.lat_blocks)


def _mla_kv_kernel(keys, n_heads, ckv_ref, kpe_ref, cckv_ref, ckpe_ref, kvng_ref, wk_ref, wvt_ref, cos_ref, sin_ref,
                   kcat_ref, vt_ref, ckvn_ref):
    cached = keys.is_cache(pl.program_id(0))
    kpe = kpe_ref[...]
    rot = kpe[:, :LANE] * cos_ref[...] + kpe[:, LANE:] * sin_ref[...]
    ckvn = jnp.where(cached, cckv_ref[...], _rms(ckv_ref[...], kvng_ref[...]))
    kr = jnp.where(cached, ckpe_ref[...], rot)
    ckvn_ref[...] = ckvn
    ckvn_b = ckvn.astype(BF16)
    kn = _dot(ckvn_b, wk_ref[...])
    vt_ref[...] = _dot_nt(wvt_ref[...], ckvn_b).astype(BF16)
    for h in range(n_heads):
        base = h * 2 * LANE
        kcat_ref[:, base:base + LANE] = kn[:, h * NOPE_DIM:(h + 1) * NOPE_DIM].astype(BF16)
        kcat_ref[:, base + LANE:base + 2 * LANE] = jnp.where(_rope_keep(h, kr.shape), kr, 0.0).astype(BF16)


def _mla_kv(keys, zc, zk, col, cache_ckv, cache_kpe2, kv_ng, w_k, w_vt, cos_t, sin_t, n_heads):
    blk = keys.blk
    R = kv_ng.shape[0]
    Mk = keys.n * blk
    ckv_off, ckv_w = col["ckv"]
    assert ckv_off % ckv_w == 0
    tok = keys.token_block
    return pl.pallas_call(
        functools.partial(_mla_kv_kernel, keys, n_heads),
        grid=(keys.n,),
        in_specs=[
            pl.BlockSpec((blk, ckv_w), lambda j: (tok(j), ckv_off // ckv_w)),
            pl.BlockSpec((blk, 2 * LANE), lambda j: (tok(j), 0)),
            pl.BlockSpec((blk, R), lambda j: (keys.cache_idx(j), 0)),
            pl.BlockSpec((blk, LANE), lambda j: (keys.cache_idx(j), 0)),
            pl.BlockSpec((1, R), lambda j: (0, 0)),
            pl.BlockSpec(w_k.shape, lambda j: (0, 0)),
            pl.BlockSpec(w_vt.shape, lambda j: (0, 0)),
            pl.BlockSpec((blk, LANE), lambda j: (tok(j), 0)),
            pl.BlockSpec((blk, LANE), lambda j: (tok(j), 0)),
        ],
        out_specs=[
            pl.BlockSpec((blk, n_heads * 2 * LANE), lambda j: (j, 0)),
            pl.BlockSpec((n_heads * V_DIM, blk), lambda j: (0, j)),
            pl.BlockSpec((blk, R), lambda j: (j, 0)),
        ],
        out_shape=[
            jax.ShapeDtypeStruct((Mk, n_heads * 2 * LANE), BF16),
            jax.ShapeDtypeStruct((n_heads * V_DIM, Mk), BF16),
            jax.ShapeDtypeStruct((Mk, R), F32),
        ],
        compiler_params=_params("parallel"),
        name="mla_kv",
    )(zc, zk, cache_ckv, cache_kpe2, kv_ng.reshape(1, R), w_k, w_vt, cos_t, sin_t)


ATTN_TQ = 256
ATTN_TK = 512


def _attn_kernel(tq, chunks, q_ref, k_ref, vt_ref, o_ref, st0_ref, st1_ref):
    n_heads = q_ref.shape[1] // (2 * LANE)
    units = [(h, u) for h in range(n_heads) for u in range(q_ref.shape[0] // tq)]
    st_refs = (st0_ref, st1_ref)

    def scores(n, c0, c1):
        h, u = units[n]
        hd = slice(h * 2 * LANE, (h + 1) * 2 * LANE)
        st = _dot_nt(k_ref[c0:c1, hd], q_ref[u * tq:(u + 1) * tq, hd])
        st_refs[n % 2][c0:c1, :] = st
        return jnp.max(st, axis=0, keepdims=True)

    def weights(n, c0, c1, m):
        h, _ = units[n]
        p = jnp.exp2(st_refs[n % 2][c0:c1, :] - m)
        return jnp.sum(p, axis=0, keepdims=True), _dot(vt_ref[h * V_DIM:(h + 1) * V_DIM, c0:c1], p.astype(BF16))

    m_prev = None
    for n in range(len(units) + 1):
        m_new = acc = l = None
        for c0, c1 in chunks:
            if n < len(units):
                mc = scores(n, c0, c1)
                m_new = mc if m_new is None else jnp.maximum(m_new, mc)
            if n > 0:
                lc, pv = weights(n - 1, c0, c1, m_prev)
                l = lc if l is None else l + lc
                acc = pv if acc is None else acc + pv
        if n > 0:
            h, u = units[n - 1]
            o_ref[u * tq:(u + 1) * tq, h * V_DIM:(h + 1) * V_DIM] = (acc / l).T.astype(o_ref.dtype)
        m_prev = m_new


def _attn_kernel_aliased(tq, chunks, q_ref, k_ref, vt_ref, o_prev_ref, o_ref, st0_ref, st1_ref):
    del o_prev_ref
    _attn_kernel(tq, chunks, q_ref, k_ref, vt_ref, o_ref, st0_ref, st1_ref)


def _attention(qcat, kcat, vt, o_prev, n_seq, n_heads, hps, tq, q_blocks_per_seq, q_block0, tk, k_block0):
    sub_q = ATTN_TQ if tq % ATTN_TQ == 0 else LANE
    assert tq % sub_q == 0 and n_heads % hps == 0
    q_spec = pl.BlockSpec((tq, hps * 2 * LANE), lambda s, h, i: (q_block0 + s * q_blocks_per_seq + i, h))
    in_specs = [
        q_spec,
        pl.BlockSpec((tk, hps * 2 * LANE), lambda s, h, i: (k_block0 + s, h)),
        pl.BlockSpec((hps * V_DIM, tk), lambda s, h, i: (h, k_block0 + s)),
    ]
    args = [qcat, kcat, vt]
    chunks = tuple((c0, min(c0 + ATTN_TK, tk)) for c0 in range(0, tk, ATTN_TK))
    kern = functools.partial(_attn_kernel, sub_q, chunks)
    aliases = {}
    if o_prev is not None:
        in_specs.append(pl.BlockSpec(memory_space=pl.ANY))
        args.append(o_prev)
        aliases = {3: 0}
        kern = functools.partial(_attn_kernel_aliased, sub_q, chunks)
    return pl.pallas_call(
        kern,
        grid=(n_seq, n_heads // hps, q_blocks_per_seq),
        in_specs=in_specs,
        out_specs=pl.BlockSpec((tq, hps * V_DIM), lambda s, h, i: (q_block0 + s * q_blocks_per_seq + i, h)),
        out_shape=jax.ShapeDtypeStruct((qcat.shape[0], n_heads * V_DIM), BF16),
        input_output_aliases=aliases,
        scratch_shapes=[pltpu.VMEM((tk, sub_q), F32), pltpu.VMEM((tk, sub_q), F32)],
        compiler_params=_params("parallel", "parallel", "arbitrary"),
        name="mla_attn",
    )(*args)


def _rope_partner_perm():
    q = ROPE_DIM // 4
    d = jnp.arange(ROPE_DIM)
    return jnp.where((d // q) % 2 == 0, d + q, d - q)


def _rope_tables(rows):
    half, quarter = ROPE_DIM // 2, ROPE_DIM // 4
    inv = ROPE_BASE ** (-jnp.arange(quarter, dtype=F32) * 2.0 / half)
    t = jnp.arange(rows.lat_len)
    r = (t // GRID_W).astype(F32)
    c = (t % GRID_W).astype(F32)
    ang_r = r[:, None] * inv[None, :]
    ang_c = c[:, None] * inv[None, :]
    cos = jnp.concatenate([jnp.cos(ang_r), jnp.cos(ang_r), jnp.cos(ang_c), jnp.cos(ang_c)], axis=1)
    sin = jnp.concatenate([-jnp.sin(ang_r), jnp.sin(ang_r), -jnp.sin(ang_c), jnp.sin(ang_c)], axis=1)
    cos = jnp.tile(jnp.concatenate([cos, cos], axis=1), (rows.n_lat, 1))
    sin = jnp.tile(jnp.concatenate([sin, sin], axis=1), (rows.n_lat, 1))
    cos = jnp.concatenate([jnp.ones((rows.mc, LANE), F32), cos], axis=0)
    sin = jnp.concatenate([jnp.zeros((rows.mc, LANE), F32), sin], axis=0)
    return cos, sin


def kernel(x_prompt, x_sample, state_gla, state_ret, cache_ckv, cache_kpe, c, c_ctx, mod_w, mod_b, norm1_g, norm2_g, ab_w_in, gla_gate_w2, gla_gate_b, ret_decay, gla_norm_g, ret_norm_g, ab_w_out, mla_w_in, mla_q_norm_g, mla_w_uq, mla_kv_norm_g, mla_w_ukv, mla_w_out, ffn_w_in, ffn_conv, ffn_w_out, final_norm_g):
    B, S, D = x_prompt.shape
    NB, T, _ = x_sample.shape
    depth = mod_w.shape[0]
    _, _, _, HA, DK, DV = state_gla.shape
    HB = state_ret.shape[3]
    assert HA == HB and state_ret.shape[4:] == (DK, DV)
    GR = gla_gate_w2.shape[2]
    past = cache_ckv.shape[2]
    q_rank = mla_q_norm_g.shape[1]
    kv_rank = mla_kv_norm_g.shape[1]
    HC = mla_w_uq.shape[2] // (NOPE_DIM + ROPE_DIM)
    F = ffn_w_out.shape[1]
    rows = _Rows(B * S, S, T, NB)
    M = rows.m
    tm = min(512, S * B, T)
    assert rows.mc % tm == 0 and T % tm == 0
    assert S & (S - 1) == 0 and T & (T - 1) == 0

    tm_proj = min(1024, S * B, T)
    assert rows.mc % tm_proj == 0 and T % tm_proj == 0
    cond = jnp.concatenate([c_ctx[None, :], c, jnp.zeros((MOD_GROUPS - 1 - NB, D), F32)], axis=0)
    mod = _modulation(cond, mod_w, mod_b)
    mod3 = mod.reshape(depth, MOD_GROUPS * 6, 1, D)

    x = (x_prompt.reshape(B * S, D), x_sample.reshape(NB * T, D))
    h = _normmod(rows, x, norm1_g[0], mod3[0], tm)
    new_gla, new_ret, new_ckv, new_kpe = [], [], [], []
    for l in range(depth):
        if l % 2 == 0:
            e = l // 2
            names = ("gq", "gk", "gv", "gg", "glr", "rq", "rk", "rv", "rg")
            sizes = (HA * DK, HA * DK, HA * DV, HA * DV, 2 * GR, HB * DK, HB * DK, HB * DV, HB * DV)
            src, o = {}, 0
            for n, sz in zip(names, sizes):
                src[n] = (o, sz)
                o += sz
            order = ("gq", "gk", "gv", "gg", "rq", "rk", "rv", "rg")
            col, o, parts = {}, 0, []
            for n in order:
                so, sz = src[n]
                parts.append(ab_w_in[e][:, so:so + sz])
                col[n] = (o, sz)
                o += sz
            w_in = jnp.concatenate(parts, axis=1).astype(BF16)
            so, sz = src["glr"]
            w_glr = jnp.concatenate([ab_w_in[e][:, so:so + sz], jnp.zeros((D, LANE - sz), F32)], axis=1).astype(BF16)
            z, glr = _inproj(h, w_in, w_glr, BF16, tm_proj)
            tb = min(256, S)
            o_fwd = None
            finals = []
            for dirn in (0, 1):
                cfg = _ScanCfg(rows, B, tb, HA, DK, DV, backward=bool(dirn))
                gate_w = jnp.zeros((LANE, HA * DK), F32).at[dirn * GR:(dirn + 1) * GR].set(gla_gate_w2[e, dirn])
                rd = jnp.broadcast_to(ret_decay[e, dirn][:, None, None], (HB, 8, LANE))
                s0g = jnp.swapaxes(state_gla[:, e, dirn], -1, -2)
                s0r = jnp.swapaxes(state_ret[:, e, dirn], -1, -2)
                extra = (o_fwd, gla_norm_g[e].reshape(1, -1), ret_norm_g[e].reshape(1, -1)) if dirn else None
                o_dir, sg, sr = _scan(cfg, z, glr, col, gate_w.astype(BF16), gla_gate_b[e, dirn].reshape(1, -1), rd,
                                      s0g, s0r, extra)
                o_fwd = o_dir
                finals.append((jnp.swapaxes(sg, -1, -2), jnp.swapaxes(sr, -1, -2)))
            y = o_fwd
            new_gla.append(jnp.stack([finals[0][0], finals[1][0]], axis=1))
            new_ret.append(jnp.stack([finals[0][1], finals[1][1]], axis=1))
            x, h = _outproj(rows, y, ab_w_out[e].astype(BF16), x, norm2_g[l], mod3[l], tm)
        else:
            i = l // 2
            perm = _rope_partner_perm()
            w = mla_w_in[i]
            w_kpe = w[:, q_rank + kv_rank:]
            w_in = w[:, :q_rank + kv_rank].astype(BF16)
            w_side = jnp.concatenate([w_kpe, w_kpe, w_kpe[:, perm], w_kpe[:, perm]], axis=1).astype(BF16)
            col = {"cq": (0, q_rank), "ckv": (q_rank, kv_rank)}
            zc, zk = _inproj(h, w_in, w_side, F32, tm_proj)
            cos_t, sin_t = _rope_tables(rows)
            wq = mla_w_uq[i].reshape(q_rank, HC, NOPE_DIM + ROPE_DIM)
            wq_rope = wq[:, :, NOPE_DIM:]
            w_uq = jnp.concatenate([wq[:, :, :NOPE_DIM].reshape(q_rank, -1), wq_rope.reshape(q_rank, -1),
                                    wq_rope[:, :, perm].reshape(q_rank, -1)], axis=1).astype(BF16)
            scale = (NOPE_DIM + ROPE_DIM) ** -0.5 * LOG2_E
            assert col["cq"][0] == 0
            qcat = _mla_q(zc, mla_q_norm_g[i], w_uq, cos_t, sin_t, HC, scale, min(256, tm))
            wkv = mla_w_ukv[i].reshape(kv_rank, HC, NOPE_DIM + V_DIM)
            w_k = wkv[:, :, :NOPE_DIM].reshape(kv_rank, -1).astype(BF16)
            w_vt = wkv[:, :, NOPE_DIM:].reshape(kv_rank, -1).T.astype(BF16)
            assert S % past == 0 and T % past == 0
            keys = _KeyRows(rows, past)
            ckpe = cache_kpe[:, i].reshape(NB * past, ROPE_DIM)
            kcat, vt, ckvn = _mla_kv(keys, zc, zk, col, cache_ckv[:, i].reshape(NB * past, kv_rank),
                                     jnp.concatenate([ckpe, ckpe], axis=1), mla_kv_norm_g[i], w_k, w_vt,
                                     cos_t, sin_t, HC)
            tk_lat = past + T
            assert (keys.lat_blocks * past) % S == 0
            tq = min(2048, T)
            o = _attention(qcat, kcat, vt, None, B, HC, HC, S, 1, 0, S, keys.lat_blocks * past // S)
            o = _attention(qcat, kcat, vt, o, NB, HC, 1, tq, T // tq, rows.mc // tq, tk_lat, 0)
            new_ckv.append(ckvn[keys.lat_blocks * past:].reshape(B, S, kv_rank))
            new_kpe.append(zk[:rows.mc, :ROPE_DIM].reshape(B, S, ROPE_DIM))
            x, h = _outproj(rows, o, mla_w_out[i].astype(BF16), x, norm2_g[l], mod3[l], tm)
        tf = _largest_tile(F, 512)
        w_in_tiles = ffn_w_in[l].reshape(D, 2 * F // tf, tf).transpose(1, 0, 2).astype(BF16)
        ffn_w = (w_in_tiles, ffn_conv[l], ffn_w_out[l].astype(BF16))
        if l == depth - 1:
            y_ctx, y_lat = _ffn(rows, h, x, mod3[l], *ffn_w, tm, tf, final_g=final_norm_g)
        else:
            x, h = _ffn(rows, h, x, mod3[l], *ffn_w, tm, tf, next_norm=(norm1_g[l + 1], mod3[l + 1]))

    y_prompt = y_ctx.reshape(B, S, D)
    y_sample = y_lat.reshape(NB, T, D)
    return (y_prompt, y_sample, jnp.stack(new_gla, axis=1), jnp.stack(new_ret, axis=1),
            jnp.stack(new_ckv, axis=1), jnp.stack(new_kpe, axis=1))
```

```python
import functools

import jax
import jax.numpy as jnp
from jax import lax
from jax.experimental import pallas as pl
from jax.experimental.pallas import tpu as pltpu

F32 = jnp.float32
BF16 = jnp.bfloat16

NORM_EPS = 1e-6
GATE_TEMP = 16.0
CHUNK = 64
GRID_W = 64
ROPE_BASE = 10000.0
ROPE_DIM = 64
NOPE_DIM = 128
V_DIM = 128
CONV_W = 3
LOG2_E = 1.4426950408889634

LANE = 128
BF16_SUBLANE = 16
MOD_GROUPS = 8
VMEM_LIMIT = 56 * 1024 * 1024

NT_DIMS = (((1,), (1,)), ((), ()))
TN_DIMS = (((0,), (0,)), ((), ()))


def _dot(a, b):
    return jnp.dot(a, b, preferred_element_type=F32)


def _dot_nt(a, b):
    return lax.dot_general(a, b, NT_DIMS, preferred_element_type=F32)


def _dot_tn(a, b):
    return lax.dot_general(a, b, TN_DIMS, preferred_element_type=F32)


def _params(*sem):
    return pltpu.CompilerParams(dimension_semantics=sem, vmem_limit_bytes=VMEM_LIMIT)


def _rms(x, g):
    ms = jnp.mean(x * x, axis=-1, keepdims=True)
    return (x * lax.rsqrt(ms + NORM_EPS)) * g


def _norm_mod(x, g, scale, shift):
    return _rms(x, g) * (1.0 + scale) + shift


def _silu(x):
    return x * jax.nn.sigmoid(x)


def _log_sigmoid(x):
    return jnp.minimum(x, 0.0) - jnp.log1p(jnp.exp(-jnp.abs(x)))


def _largest_tile(n, cap):
    best = None
    for t in range(LANE, min(n, cap) + 1, LANE):
        if n % t == 0:
            best = t
    assert best is not None, (n, cap)
    return best


def _mod_kernel(c_ref, w_ref, b_ref, o_ref):
    s = _silu(c_ref[...]).astype(BF16)
    m = _dot(s, w_ref[0].astype(BF16)) + b_ref[0]
    for g in range(MOD_GROUPS):
        o_ref[0, g, 0] = m[g:g + 1]


def _modulation(cond, mod_w, mod_b):
    L, D, N = mod_w.shape
    n_vec = N // D
    out = pl.pallas_call(
        _mod_kernel,
        grid=(L, n_vec),
        in_specs=[
            pl.BlockSpec((MOD_GROUPS, D), lambda l, k: (0, 0)),
            pl.BlockSpec((1, D, D), lambda l, k: (l, 0, k)),
            pl.BlockSpec((1, 1, D), lambda l, k: (l, 0, k)),
        ],
        out_specs=pl.BlockSpec((1, MOD_GROUPS, 1, 1, D), lambda l, k: (l, 0, k, 0, 0)),
        out_shape=jax.ShapeDtypeStruct((L, MOD_GROUPS, n_vec, 1, D), F32),
        compiler_params=_params("parallel", "parallel"),
        name="modulation",
    )(cond, mod_w, mod_b.reshape(L, 1, N))
    return out.reshape(L, MOD_GROUPS * n_vec, 1, D)


class _Rows:
    def __init__(self, mc, seq, lat_len, n_lat):
        self.mc, self.seq, self.lat_len, self.n_lat = mc, seq, lat_len, n_lat
        self.m = mc + lat_len * n_lat

    def group(self, i, tm):
        r = i * tm
        return jnp.where(r < self.mc, 0, 1 + (r - self.mc) // self.lat_len)

    def mod_spec(self, which, tm, d):
        return pl.BlockSpec((1, 1, d), lambda i, *_: (self.group(i, tm) * 6 + which, 0, 0))

    def split_specs(self, tm, d):
        n_ctx = self.mc // tm
        return [pl.BlockSpec((tm, d), lambda i, *_: (jnp.minimum(i, n_ctx - 1), 0)),
                pl.BlockSpec((tm, d), lambda i, *_: (jnp.maximum(i - n_ctx, 0), 0))]


ROW_CHUNK = 16


def _for_row_chunks(n_rows, body):
    def it(c, carry):
        body(pl.ds(pl.multiple_of(c * ROW_CHUNK, ROW_CHUNK), ROW_CHUNK))
        return carry
    lax.fori_loop(0, n_rows // ROW_CHUNK, it, 0, unroll=4)


def _fold_gain(gs_ref, g_ref, sc_ref):
    gs_ref[...] = g_ref[...] * (1.0 + sc_ref[0])


def _norm_mod_rows(x, gs_ref, sh_ref):
    ms = jnp.mean(x * x, axis=-1, keepdims=True)
    return (x * lax.rsqrt(ms + NORM_EPS)) * gs_ref[...] + sh_ref[0]


def _normmod_kernel(rows, tm, xa_ref, xb_ref, g_ref, sh_ref, sc_ref, h_ref, gs_ref):
    _fold_gain(gs_ref, g_ref, sc_ref)

    def run(x_ref):
        def body(r):
            h_ref[r, :] = _norm_mod_rows(x_ref[r, :], gs_ref, sh_ref).astype(h_ref.dtype)
        _for_row_chunks(tm, body)

    is_ctx = pl.program_id(0) * tm < rows.mc
    pl.when(is_ctx)(lambda: run(xa_ref))
    pl.when(jnp.logical_not(is_ctx))(lambda: run(xb_ref))


def _normmod(rows, x_pair, g, mod3, tm):
    D = x_pair[0].shape[1]
    return pl.pallas_call(
        functools.partial(_normmod_kernel, rows, tm),
        grid=(rows.m // tm,),
        in_specs=rows.split_specs(tm, D) + [
            pl.BlockSpec((1, D), lambda i: (0, 0)),
            rows.mod_spec(0, tm, D),
            rows.mod_spec(1, tm, D),
        ],
        out_specs=pl.BlockSpec((tm, D), lambda i: (i, 0)),
        out_shape=jax.ShapeDtypeStruct((rows.m, D), BF16),
        scratch_shapes=[pltpu.VMEM((1, D), F32)],
        compiler_params=_params("parallel"),
        name="normmod",
    )(*x_pair, g.reshape(1, D), mod3, mod3)


def _inproj_kernel(h_ref, w_ref, ws_ref, o_ref, os_ref):
    @pl.when(pl.program_id(1) == 0)
    def _():
        os_ref[...] = _dot(h_ref[...], ws_ref[...])

    o_ref[...] = _dot(h_ref[...], w_ref[...]).astype(o_ref.dtype)


def _inproj(h, w, w_side, out_dtype, tm, tn_cap=1024):
    M, D = h.shape
    N = w.shape[1]
    NS = w_side.shape[1]
    tn = _largest_tile(N, tn_cap)
    return pl.pallas_call(
        _inproj_kernel,
        grid=(M // tm, N // tn),
        in_specs=[
            pl.BlockSpec((tm, D), lambda i, j: (i, 0)),
            pl.BlockSpec((D, tn), lambda i, j: (0, j)),
            pl.BlockSpec((D, NS), lambda i, j: (0, 0)),
        ],
        out_specs=[pl.BlockSpec((tm, tn), lambda i, j: (i, j)), pl.BlockSpec((tm, NS), lambda i, j: (i, 0))],
        out_shape=[jax.ShapeDtypeStruct((M, N), out_dtype), jax.ShapeDtypeStruct((M, NS), F32)],
        compiler_params=_params("parallel", "arbitrary"),
        name="inproj",
    )(h, w, w_side)


OUTPROJ_PARTS = 4


def _outproj_kernel(rows, tm, split, y_ref, w_ref, *rest):
    if split:
        xa_ref, xb_ref, gate_ref, g_ref, sh_ref, sc_ref, o_ref, h_ref, acc_ref, gs_ref = rest
    else:
        xa_ref, gate_ref, g_ref, sh_ref, sc_ref, o_ref, h_ref, acc_ref, gs_ref = rest
        xb_ref = xa_ref
    _fold_gain(gs_ref, g_ref, sc_ref)
    is_ctx = pl.program_id(0) * tm < rows.mc
    part = tm // OUTPROJ_PARTS

    def finish(r):
        x = jnp.where(is_ctx, xa_ref[r, :], xb_ref[r, :]) if split else xa_ref[r, :]
        x1 = x + gate_ref[0] * acc_ref[r, :]
        o_ref[r, :] = x1
        h_ref[r, :] = _norm_mod_rows(x1, gs_ref, sh_ref).astype(h_ref.dtype)

    for p in range(OUTPROJ_PARTS + 1):
        if p < OUTPROJ_PARTS:
            rp = slice(p * part, (p + 1) * part)
            acc_ref[rp, :] = _dot(y_ref[rp, :], w_ref[...])
        if p > 0:
            for c in range(part // ROW_CHUNK):
                start = (p - 1) * part + c * ROW_CHUNK
                finish(slice(start, start + ROW_CHUNK))


def _outproj(rows, y, w, x, g, mod3, tm):
    M, K = y.shape
    N = w.shape[1]
    split = isinstance(x, (tuple, list))
    x_specs = rows.split_specs(tm, N) if split else [pl.BlockSpec((tm, N), lambda i: (i, 0))]
    x_args = list(x) if split else [x]
    return pl.pallas_call(
        functools.partial(_outproj_kernel, rows, tm, split),
        grid=(M // tm,),
        in_specs=[
            pl.BlockSpec((tm, K), lambda i: (i, 0)),
            pl.BlockSpec((K, N), lambda i: (0, 0)),
        ] + x_specs + [
            rows.mod_spec(2, tm, N),
            pl.BlockSpec((1, N), lambda i: (0, 0)),
            rows.mod_spec(3, tm, N),
            rows.mod_spec(4, tm, N),
        ],
        out_specs=[pl.BlockSpec((tm, N), lambda i: (i, 0)), pl.BlockSpec((tm, N), lambda i: (i, 0))],
        out_shape=[jax.ShapeDtypeStruct((M, N), F32), jax.ShapeDtypeStruct((M, N), BF16)],
        scratch_shapes=[pltpu.VMEM((tm, N), F32), pltpu.VMEM((1, N), F32)],
        compiler_params=_params("parallel"),
        name="outproj",
    )(y, w, *x_args, mod3, g.reshape(1, N), mod3, mod3)


HALO = BF16_SUBLANE


def _ffn_kernel(rows, tm, final_norm, hp_ref, h_ref, hn_ref, x_ref, gate_ref,
                wa_ref, wb_ref, cw_ref, wo_ref, *rest):
    if final_norm:
        fg_ref, o_ctx_ref, o_lat_ref, hs_ref, a_ref, acc_ref = rest
    else:
        g_ref, sh_ref, sc_ref, o_ref, hnext_ref, hs_ref, a_ref, acc_ref, gs_ref = rest
    i = pl.program_id(0)
    f = pl.program_id(1)

    @pl.when(f == 0)
    def _():
        hs_ref[0:HALO, :] = hp_ref[...]
        hs_ref[HALO:HALO + tm, :] = h_ref[...]
        hs_ref[HALO + tm:, :] = hn_ref[...]
        acc_ref[...] = jnp.zeros_like(acc_ref)

    a_ref[...] = _dot(hs_ref[...], wa_ref[...])
    b = _dot(h_ref[...], wb_ref[...])
    row = i * tm + lax.broadcasted_iota(jnp.int32, (tm, 1), 0)
    pos = jnp.where(row < rows.mc, row & (rows.seq - 1), (row - rows.mc) & (rows.lat_len - 1))
    seq_len = jnp.where(row < rows.mc, rows.seq, rows.lat_len)
    a_prev = jnp.where(pos == 0, 0.0, a_ref[pl.ds(HALO - 1, tm), :])
    a_next = jnp.where(pos == seq_len - 1, 0.0, a_ref[pl.ds(HALO + 1, tm), :])
    a_mid = a_ref[pl.ds(HALO, tm), :]
    cw = cw_ref[...]
    a = cw[0:1] * a_prev + cw[1:2] * a_mid + cw[2:3] * a_next
    act = (_silu(a) * b).astype(BF16)
    acc_ref[...] += _dot(act, wo_ref[...])

    @pl.when(f == pl.num_programs(1) - 1)
    def _():
        def residual(r):
            return x_ref[r, :] + gate_ref[0] * acc_ref[r, :]

        if final_norm:
            def run(out_ref):
                def body(r):
                    out_ref[r, :] = _rms(residual(r), fg_ref[...])
                _for_row_chunks(tm, body)

            is_ctx = i * tm < rows.mc
            pl.when(is_ctx)(lambda: run(o_ctx_ref))
            pl.when(jnp.logical_not(is_ctx))(lambda: run(o_lat_ref))
        else:
            _fold_gain(gs_ref, g_ref, sc_ref)

            def body(r):
                x2 = residual(r)
                o_ref[r, :] = x2
                hnext_ref[r, :] = _norm_mod_rows(x2, gs_ref, sh_ref).astype(hnext_ref.dtype)
            _for_row_chunks(tm, body)


def _ffn(rows, h, x, mod3, w_in, conv_w, w_out, tm, tf, next_norm=None, final_g=None):
    M, D = x.shape
    F = w_out.shape[0]
    nf = F // tf
    nhalo = M // HALO
    final_norm = final_g is not None
    kern = functools.partial(_ffn_kernel, rows, tm, final_norm)
    in_specs = [
        pl.BlockSpec((HALO, D), lambda i, f: (jnp.maximum(i * (tm // HALO) - 1, 0), 0)),
        pl.BlockSpec((tm, D), lambda i, f: (i, 0)),
        pl.BlockSpec((HALO, D), lambda i, f: (jnp.minimum((i + 1) * (tm // HALO), nhalo - 1), 0)),
        pl.BlockSpec((tm, D), lambda i, f: (i, 0)),
        rows.mod_spec(5, tm, D),
        pl.BlockSpec((D, tf), lambda i, f: (0, f)),
        pl.BlockSpec((D, tf), lambda i, f: (0, nf + f)),
        pl.BlockSpec((CONV_W, tf), lambda i, f: (0, f)),
        pl.BlockSpec((tf, D), lambda i, f: (f, 0)),
    ]
    args = [h, h, h, x, mod3, w_in, w_in, conv_w, w_out]
    vec = pl.BlockSpec((1, D), lambda i, f: (0, 0))
    if final_norm:
        n_ctx = rows.mc // tm
        in_specs += [vec]
        args += [final_g.reshape(1, D)]
        out_specs = [pl.BlockSpec((tm, D), lambda i, f: (jnp.minimum(i, n_ctx - 1), 0)),
                     pl.BlockSpec((tm, D), lambda i, f: (jnp.maximum(i - n_ctx, 0), 0))]
        out_shape = [jax.ShapeDtypeStruct((rows.mc, D), F32), jax.ShapeDtypeStruct((M - rows.mc, D), F32)]
        row_sem = "arbitrary"
    else:
        g_next, mod3_next = next_norm
        in_specs += [vec, rows.mod_spec(0, tm, D), rows.mod_spec(1, tm, D)]
        args += [g_next.reshape(1, D), mod3_next, mod3_next]
        out_specs = [pl.BlockSpec((tm, D), lambda i, f: (i, 0)), pl.BlockSpec((tm, D), lambda i, f: (i, 0))]
        out_shape = [jax.ShapeDtypeStruct((M, D), F32), jax.ShapeDtypeStruct((M, D), BF16)]
        row_sem = "parallel"
    return pl.pallas_call(
        kern,
        grid=(M // tm, nf),
        in_specs=in_specs,
        out_specs=out_specs,
        out_shape=out_shape,
        scratch_shapes=[
            pltpu.VMEM((tm + 2 * HALO, D), BF16),
            pltpu.VMEM((tm + 2 * HALO, tf), F32),
            pltpu.VMEM((tm, D), F32),
        ] + ([] if final_norm else [pltpu.VMEM((1, D), F32)]),
        compiler_params=_params(row_sem, "arbitrary"),
        name="convffn",
    )(*args)


class _ScanCfg:
    def __init__(self, rows, batch, tb, h, dk, dv, backward):
        self.rows, self.batch, self.tb, self.h, self.dk, self.dv = rows, batch, tb, h, dk, dv
        self.backward = backward
        self.cps = rows.seq // tb
        self.lps = rows.lat_len // tb
        self.ctx_blocks = batch * self.cps
        self.nblk = rows.m // tb

    def block(self, i):
        return self.nblk - 1 - i if self.backward else i

    def is_ctx(self, r):
        return r < self.ctx_blocks

    def seq_pos(self, r):
        ctx = self.is_ctx(r)
        return (jnp.where(ctx, r % self.cps, (r - self.ctx_blocks) % self.lps),
                jnp.where(ctx, self.cps, self.lps))

    def lat_seq(self, r):
        return jnp.clip((r - self.ctx_blocks) // self.lps, 0, self.rows.n_lat - 1)

    def ctx_seq(self, r):
        return jnp.clip(r // self.cps, 0, self.batch - 1)


def _scan_kernel(cfg, gq_ref, gk_ref, gv_ref, rq_ref, rk_ref, rv_ref, glr_ref, gw_ref, gb_ref, rd_ref,
                 s0g_ref, s0r_ref, *rest):
    scratch = rest[-8:]
    sg_ref, sr_ref, eb_ref, qd_ref, kinv_ref, kend_ref, rks_ref, rkd_ref = scratch
    if cfg.backward:
        of_ref, gg_ref, rg_ref, gng_ref, rng_ref, y_ref, sgo_ref, sro_ref = rest[:-8]
    else:
        o_ref, sgo_ref, sro_ref = rest[:-8]
    H, DK, DV, C = cfg.h, cfg.dk, cfg.dv, CHUNK
    r = cfg.block(pl.program_id(0))
    blk, nblk_seq = cfg.seq_pos(r)
    first = blk == (nblk_seq - 1 if cfg.backward else 0)
    last = blk == (0 if cfg.backward else nblk_seq - 1)
    is_ctx = cfg.is_ctx(r)

    @pl.when(first & is_ctx)
    def _():
        sg_ref[...] = jnp.zeros_like(sg_ref)
        sr_ref[...] = jnp.zeros_like(sr_ref)

    @pl.when(first & jnp.logical_not(is_ctx))
    def _():
        sg_ref[...] = s0g_ref[0]
        sr_ref[...] = s0r_ref[0]

    ti = lax.broadcasted_iota(jnp.int32, (C, C), 0)
    tj = lax.broadcasted_iota(jnp.int32, (C, C), 1)
    sees = (tj >= ti) if cfg.backward else (tj <= ti)
    dist = jnp.abs(ti - tj).astype(F32)
    rowi = lax.broadcasted_iota(jnp.int32, (C, LANE), 0)
    to_end = (rowi if cfg.backward else C - 1 - rowi).astype(F32)
    from_start = (C - rowi if cfg.backward else rowi + 1).astype(F32)

    n_chunks = cfg.tb // C
    HDK = H * DK
    scale = DK ** -0.5

    bi = lax.broadcasted_iota(jnp.int32, (cfg.tb, cfg.tb), 0)
    bj = lax.broadcasted_iota(jnp.int32, (cfg.tb, cfg.tb), 1)
    same_chunk = (bi & -C) == (bj & -C)
    blk_tri = jnp.where(same_chunk & ((bj >= bi) if cfg.backward else (bj <= bi)), 1.0, 0.0).astype(BF16)
    pre = _dot(glr_ref[...].astype(BF16), gw_ref[...]) + gb_ref[...]
    la = _log_sigmoid(pre) / GATE_TEMP
    la_hi = la.astype(BF16)
    la_lo = (la - la_hi.astype(F32)).astype(BF16)
    b = _dot(blk_tri, la_hi) + _dot(blk_tri, la_lo)
    b3 = b.reshape(n_chunks, C, HDK)
    b_end = b3[:, 0:1, :] if cfg.backward else b3[:, C - 1:C, :]
    eb = jnp.exp(b)
    eb_ref[...] = eb
    gk = gk_ref[...].astype(F32)
    qd_ref[...] = (gq_ref[...].astype(F32) * scale * eb).astype(BF16)
    kinv_ref[...] = (gk * jnp.exp(-b)).astype(BF16)
    kend_ref[...] = (gk * jnp.exp(b_end - b3).reshape(cfg.tb, HDK)).astype(BF16)

    decays, q_decs, c_decs, k_dec_cols = [], [], [], []
    for h in range(H):
        lg = _log_sigmoid(rd_ref[h])
        lg_c = jnp.broadcast_to(lg[0:1, 0:C], (C, C))
        lg_l = jnp.broadcast_to(lg[0:1, :], (C, LANE))
        decays.append(jnp.where(sees, jnp.exp(lg_c * dist), 0.0))
        q_decs.append(jnp.concatenate([jnp.exp(lg_l * from_start)] * (DV // LANE), axis=1))
        c_decs.append(jnp.exp(lg[0:1, :] * float(C)))
        k_dec_cols.append(jnp.concatenate([jnp.exp(lg_l * to_end)] * n_chunks, axis=0))
    rk = rk_ref[...].astype(F32) * scale
    rks_ref[...] = rk.astype(BF16)
    rkd_ref[...] = (rk * jnp.concatenate(k_dec_cols, axis=1)).astype(BF16)

    for ci in range(n_chunks):
        c = n_chunks - 1 - ci if cfg.backward else ci
        rs = slice(c * C, (c + 1) * C)
        end_row = c * C if cfg.backward else (c + 1) * C - 1
        heads = range(H)
        ksl = [slice(h * DK, (h + 1) * DK) for h in heads]
        vsl = [slice(h * DV, (h + 1) * DV) for h in heads]
        att_raw = [_dot_nt(qd_ref[rs, ksl[h]], kinv_ref[rs, ksl[h]]) for h in heads]
        ratt_raw = [_dot_nt(rq_ref[rs, ksl[h]], rks_ref[rs, ksl[h]]) for h in heads]
        qs_g = [_dot_nt(qd_ref[rs, ksl[h]], sg_ref[h].astype(BF16)) for h in heads]
        qs_r = [_dot_nt(rq_ref[rs, ksl[h]], sr_ref[h].astype(BF16)) for h in heads]
        for h in heads:
            att = jnp.where(sees, att_raw[h], 0.0).astype(BF16)
            o_g = _dot(att, gv_ref[rs, vsl[h]]) + qs_g[h]
            ratt = (ratt_raw[h] * decays[h]).astype(BF16)
            o_r = _dot(ratt, rv_ref[rs, vsl[h]]) + qs_r[h] * q_decs[h]
            rs_cols = slice(H * DV + h * DV, H * DV + (h + 1) * DV)
            if cfg.backward:
                for o, cols, gate_ref, ng_ref in ((o_g, vsl[h], gg_ref, gng_ref), (o_r, rs_cols, rg_ref, rng_ref)):
                    tot = o + of_ref[rs, cols]
                    mu = jnp.mean(tot, axis=-1, keepdims=True)
                    d = tot - mu
                    var = jnp.mean(d * d, axis=-1, keepdims=True)
                    yn = d * lax.rsqrt(var + NORM_EPS) * ng_ref[:, vsl[h]]
                    y_ref[rs, cols] = (_silu(gate_ref[rs, vsl[h]].astype(F32)) * yn).astype(y_ref.dtype)
            else:
                o_ref[rs, vsl[h]] = o_g
                o_ref[rs, rs_cols] = o_r
        for h in heads:
            sg_ref[h] = (sg_ref[h] * eb_ref[end_row:end_row + 1, ksl[h]]
                         + _dot_tn(gv_ref[rs, vsl[h]], kend_ref[rs, ksl[h]]))
            sr_ref[h] = sr_ref[h] * c_decs[h] + _dot_tn(rv_ref[rs, vsl[h]], rkd_ref[rs, ksl[h]])

    @pl.when(last & is_ctx)
    def _():
        sgo_ref[0] = sg_ref[...]
        sro_ref[0] = sr_ref[...]


def _scan(cfg, z, glr, col, gate_w, gate_b, ret_decay_t, s0g, s0r, extra):
    H, DK, DV, tb = cfg.h, cfg.dk, cfg.dv, cfg.tb
    M = z.shape[0]

    def zspec(name):
        off, width = col[name]
        assert off % width == 0, (name, off, width)
        return pl.BlockSpec((tb, width), lambda i: (cfg.block(i), off // width))

    def full(shape):
        return pl.BlockSpec(shape, lambda i: (0,) * len(shape))

    state_in = pl.BlockSpec((1, H, DV, DK), lambda i: (cfg.lat_seq(cfg.block(i)), 0, 0, 0))
    state_out = pl.BlockSpec((1, H, DV, DK), lambda i: (cfg.ctx_seq(cfg.block(i)), 0, 0, 0))
    row_blk = pl.BlockSpec((tb, 2 * H * DV), lambda i: (cfg.block(i), 0))
    glr_spec = pl.BlockSpec((tb, glr.shape[1]), lambda i: (cfg.block(i), 0))
    in_specs = [zspec("gq"), zspec("gk"), zspec("gv"), zspec("rq"), zspec("rk"), zspec("rv"), glr_spec,
                full(gate_w.shape), full(gate_b.shape), full(ret_decay_t.shape), state_in, state_in]
    args = [z, z, z, z, z, z, glr, gate_w, gate_b, ret_decay_t, s0g, s0r]
    state_shape = jax.ShapeDtypeStruct((cfg.batch, H, DV, DK), F32)
    if cfg.backward:
        o_fwd, gla_ng, ret_ng = extra
        in_specs += [row_blk, zspec("gg"), zspec("rg"), full(gla_ng.shape), full(ret_ng.shape)]
        args += [o_fwd, z, z, gla_ng, ret_ng]
        out0 = jax.ShapeDtypeStruct((M, 2 * H * DV), BF16)
    else:
        out0 = jax.ShapeDtypeStruct((M, 2 * H * DV), F32)
    return pl.pallas_call(
        functools.partial(_scan_kernel, cfg),
        grid=(cfg.nblk,),
        in_specs=in_specs,
        out_specs=[row_blk, state_out, state_out],
        out_shape=[out0, state_shape, state_shape],
        scratch_shapes=[pltpu.VMEM((H, DV, DK), F32), pltpu.VMEM((H, DV, DK), F32),
                        pltpu.VMEM((tb, H * DK), F32)] + [pltpu.VMEM((tb, H * DK), BF16)] * 5,
        compiler_params=_params("arbitrary"),
        name="scan_bwd" if cfg.backward else "scan_fwd",
    )(*args)


def _rope_keep(h, shape):
    lane = lax.broadcasted_iota(jnp.int32, shape, 1)
    return (lane < ROPE_DIM) if h % 2 == 0 else (lane >= ROPE_DIM)


def _mla_q_kernel(n_heads, scale, cq_ref, qng_ref, w_ref, cos_ref, sin_ref, o_ref):
    cqn = _rms(cq_ref[...], qng_ref[...]).astype(BF16)
    q = _dot(cqn, w_ref[...])
    nope_w = n_heads * NOPE_DIM
    rope_w = n_heads * ROPE_DIM
    cos, sin = cos_ref[...], sin_ref[...]
    for h in range(n_heads):
        p = h // 2
        tile = q[:, nope_w + p * LANE:nope_w + (p + 1) * LANE]
        partner = q[:, nope_w + rope_w + p * LANE:nope_w + rope_w + (p + 1) * LANE]
        rot = (tile * cos + partner * sin) * scale
        rot = jnp.where(_rope_keep(h, rot.shape), rot, 0.0)
        base = h * 2 * LANE
        o_ref[:, base:base + LANE] = (q[:, h * NOPE_DIM:(h + 1) * NOPE_DIM] * scale).astype(BF16)
        o_ref[:, base + LANE:base + 2 * LANE] = rot.astype(BF16)


def _mla_q(zc, q_ng, w_uq, cos_t, sin_t, n_heads, scale, tm):
    M = zc.shape[0]
    R = q_ng.shape[0]
    N = w_uq.shape[1]
    return pl.pallas_call(
        functools.partial(_mla_q_kernel, n_heads, scale),
        grid=(M // tm,),
        in_specs=[
            pl.BlockSpec((tm, R), lambda i: (i, 0)),
            pl.BlockSpec((1, R), lambda i: (0, 0)),
            pl.BlockSpec((R, N), lambda i: (0, 0)),
            pl.BlockSpec((tm, LANE), lambda i: (i, 0)),
            pl.BlockSpec((tm, LANE), lambda i: (i, 0)),
        ],
        out_specs=pl.BlockSpec((tm, n_heads * 2 * LANE), lambda i: (i, 0)),
        out_shape=jax.ShapeDtypeStruct((M, n_heads * 2 * LANE), BF16),
        compiler_params=_params("parallel"),
        name="mla_q",
    )(zc, q_ng.reshape(1, R), w_uq, cos_t, sin_t)


class _KeyRows:
    def __init__(self, rows, blk):
        self.rows, self.blk = rows, blk
        self.per_lat = 1 + rows.lat_len // blk
        self.lat_blocks = rows.n_lat * self.per_lat
        self.ctx_blocks = rows.mc // blk
        self.n = self.lat_blocks + self.ctx_blocks

    def is_cache(self, j):
        return (j < self.lat_blocks) & (j % self.per_lat == 0)

    def cache_idx(self, j):
        return jnp.clip(j // self.per_lat, 0, self.rows.n_lat - 1)

    def token_block(self, j):
        b = j // self.per_lat
        t = jnp.maximum(j % self.per_lat - 1, 0)
        lat = self.ctx_blocks + b * (self.per_lat - 1) + t
        return jnp.where(j < self.lat_blocks, lat, j - self.lat_blocks)


def _mla_kv_kernel(keys, n_heads, ckv_ref, kpe_ref, cckv_ref, ckpe_ref, kvng_ref, wk_ref, wvt_ref, cos_ref, sin_ref,
                   kcat_ref, vt_ref, ckvn_ref):
    cached = keys.is_cache(pl.program_id(0))
    kpe = kpe_ref[...]
    rot = kpe[:, :LANE] * cos_ref[...] + kpe[:, LANE:] * sin_ref[...]
    ckvn = jnp.where(cached, cckv_ref[...], _rms(ckv_ref[...], kvng_ref[...]))
    kr = jnp.where(cached, ckpe_ref[...], rot)
    ckvn_ref[...] = ckvn
    ckvn_b = ckvn.astype(BF16)
    kn = _dot(ckvn_b, wk_ref[...])
    vt_ref[...] = _dot_nt(wvt_ref[...], ckvn_b).astype(BF16)
    for h in range(n_heads):
        base = h * 2 * LANE
        kcat_ref[:, base:base + LANE] = kn[:, h * NOPE_DIM:(h + 1) * NOPE_DIM].astype(BF16)
        kcat_ref[:, base + LANE:base + 2 * LANE] = jnp.where(_rope_keep(h, kr.shape), kr, 0.0).astype(BF16)


def _mla_kv(keys, zc, zk, col, cache_ckv, cache_kpe2, kv_ng, w_k, w_vt, cos_t, sin_t, n_heads):
    blk = keys.blk
    R = kv_ng.shape[0]
    Mk = keys.n * blk
    ckv_off, ckv_w = col["ckv"]
    assert ckv_off % ckv_w == 0
    tok = keys.token_block
    return pl.pallas_call(
        functools.partial(_mla_kv_kernel, keys, n_heads),
        grid=(keys.n,),
        in_specs=[
            pl.BlockSpec((blk, ckv_w), lambda j: (tok(j), ckv_off // ckv_w)),
            pl.BlockSpec((blk, 2 * LANE), lambda j: (tok(j), 0)),
            pl.BlockSpec((blk, R), lambda j: (keys.cache_idx(j), 0)),
            pl.BlockSpec((blk, LANE), lambda j: (keys.cache_idx(j), 0)),
            pl.BlockSpec((1, R), lambda j: (0, 0)),
            pl.BlockSpec(w_k.shape, lambda j: (0, 0)),
            pl.BlockSpec(w_vt.shape, lambda j: (0, 0)),
            pl.BlockSpec((blk, LANE), lambda j: (tok(j), 0)),
            pl.BlockSpec((blk, LANE), lambda j: (tok(j), 0)),
        ],
        out_specs=[
            pl.BlockSpec((blk, n_heads * 2 * LANE), lambda j: (j, 0)),
            pl.BlockSpec((n_heads * V_DIM, blk), lambda j: (0, j)),
            pl.BlockSpec((blk, R), lambda j: (j, 0)),
        ],
        out_shape=[
            jax.ShapeDtypeStruct((Mk, n_heads * 2 * LANE), BF16),
            jax.ShapeDtypeStruct((n_heads * V_DIM, Mk), BF16),
            jax.ShapeDtypeStruct((Mk, R), F32),
        ],
        compiler_params=_params("parallel"),
        name="mla_kv",
    )(zc, zk, cache_ckv, cache_kpe2, kv_ng.reshape(1, R), w_k, w_vt, cos_t, sin_t)


ATTN_TQ = 256
ATTN_TK = 512


def _attn_kernel(tq, chunks, q_ref, k_ref, vt_ref, o_ref, st0_ref, st1_ref):
    n_heads = q_ref.shape[1] // (2 * LANE)
    units = [(h, u) for h in range(n_heads) for u in range(q_ref.shape[0] // tq)]
    st_refs = (st0_ref, st1_ref)

    def scores(n, c0, c1):
        h, u = units[n]
        hd = slice(h * 2 * LANE, (h + 1) * 2 * LANE)
        st = _dot_nt(k_ref[c0:c1, hd], q_ref[u * tq:(u + 1) * tq, hd])
        st_refs[n % 2][c0:c1, :] = st
        return jnp.max(st, axis=0, keepdims=True)

    def weights(n, c0, c1, m):
        h, _ = units[n]
        p = jnp.exp2(st_refs[n % 2][c0:c1, :] - m)
        return jnp.sum(p, axis=0, keepdims=True), _dot(vt_ref[h * V_DIM:(h + 1) * V_DIM, c0:c1], p.astype(BF16))

    m_prev = None
    for n in range(len(units) + 1):
        m_new = acc = l = None
        for c0, c1 in chunks:
            if n < len(units):
                mc = scores(n, c0, c1)
                m_new = mc if m_new is None else jnp.maximum(m_new, mc)
            if n > 0:
                lc, pv = weights(n - 1, c0, c1, m_prev)
                l = lc if l is None else l + lc
                acc = pv if acc is None else acc + pv
        if n > 0:
            h, u = units[n - 1]
            o_ref[u * tq:(u + 1) * tq, h * V_DIM:(h + 1) * V_DIM] = (acc / l).T.astype(o_ref.dtype)
        m_prev = m_new


def _attn_kernel_aliased(tq, chunks, q_ref, k_ref, vt_ref, o_prev_ref, o_ref, st0_ref, st1_ref):
    del o_prev_ref
    _attn_kernel(tq, chunks, q_ref, k_ref, vt_ref, o_ref, st0_ref, st1_ref)


def _attention(qcat, kcat, vt, o_prev, n_seq, n_heads, hps, tq, q_blocks_per_seq, q_block0, tk, k_block0):
    sub_q = ATTN_TQ if tq % ATTN_TQ == 0 else LANE
    assert tq % sub_q == 0 and n_heads % hps == 0
    q_spec = pl.BlockSpec((tq, hps * 2 * LANE), lambda s, h, i: (q_block0 + s * q_blocks_per_seq + i, h))
    in_specs = [
        q_spec,
        pl.BlockSpec((tk, hps * 2 * LANE), lambda s, h, i: (k_block0 + s, h)),
        pl.BlockSpec((hps * V_DIM, tk), lambda s, h, i: (h, k_block0 + s)),
    ]
    args = [qcat, kcat, vt]
    chunks = tuple((c0, min(c0 + ATTN_TK, tk)) for c0 in range(0, tk, ATTN_TK))
    kern = functools.partial(_attn_kernel, sub_q, chunks)
    aliases = {}
    if o_prev is not None:
        in_specs.append(pl.BlockSpec(memory_space=pl.ANY))
        args.append(o_prev)
        aliases = {3: 0}
        kern = functools.partial(_attn_kernel_aliased, sub_q, chunks)
    return pl.pallas_call(
        kern,
        grid=(n_seq, n_heads // hps, q_blocks_per_seq),
        in_specs=in_specs,
        out_specs=pl.BlockSpec((tq, hps * V_DIM), lambda s, h, i: (q_block0 + s * q_blocks_per_seq + i, h)),
        out_shape=jax.ShapeDtypeStruct((qcat.shape[0], n_heads * V_DIM), BF16),
        input_output_aliases=aliases,
        scratch_shapes=[pltpu.VMEM((tk, sub_q), F32), pltpu.VMEM((tk, sub_q), F32)],
        compiler_params=_params("parallel", "parallel", "arbitrary"),
        name="mla_attn",
    )(*args)


def _rope_partner_perm():
    q = ROPE_DIM // 4
    d = jnp.arange(ROPE_DIM)
    return jnp.where((d // q) % 2 == 0, d + q, d - q)


def _rope_tables(rows):
    half, quarter = ROPE_DIM // 2, ROPE_DIM // 4
    inv = ROPE_BASE ** (-jnp.arange(quarter, dtype=F32) * 2.0 / half)
    t = jnp.arange(rows.lat_len)
    r = (t // GRID_W).astype(F32)
    c = (t % GRID_W).astype(F32)
    ang_r = r[:, None] * inv[None, :]
    ang_c = c[:, None] * inv[None, :]
    cos = jnp.concatenate([jnp.cos(ang_r), jnp.cos(ang_r), jnp.cos(ang_c), jnp.cos(ang_c)], axis=1)
    sin = jnp.concatenate([-jnp.sin(ang_r), jnp.sin(ang_r), -jnp.sin(ang_c), jnp.sin(ang_c)], axis=1)
    cos = jnp.tile(jnp.concatenate([cos, cos], axis=1), (rows.n_lat, 1))
    sin = jnp.tile(jnp.concatenate([sin, sin], axis=1), (rows.n_lat, 1))
    cos = jnp.concatenate([jnp.ones((rows.mc, LANE), F32), cos], axis=0)
    sin = jnp.concatenate([jnp.zeros((rows.mc, LANE), F32), sin], axis=0)
    return cos, sin


def kernel(x_prompt, x_sample, state_gla, state_ret, cache_ckv, cache_kpe, c, c_ctx, mod_w, mod_b, norm1_g, norm2_g, ab_w_in, gla_gate_w2, gla_gate_b, ret_decay, gla_norm_g, ret_norm_g, ab_w_out, mla_w_in, mla_q_norm_g, mla_w_uq, mla_kv_norm_g, mla_w_ukv, mla_w_out, ffn_w_in, ffn_conv, ffn_w_out, final_norm_g):
    B, S, D = x_prompt.shape
    NB, T, _ = x_sample.shape
    depth = mod_w.shape[0]
    _, _, _, HA, DK, DV = state_gla.shape
    HB = state_ret.shape[3]
    assert HA == HB and state_ret.shape[4:] == (DK, DV)
    GR = gla_gate_w2.shape[2]
    past = cache_ckv.shape[2]
    q_rank = mla_q_norm_g.shape[1]
    kv_rank = mla_kv_norm_g.shape[1]
    HC = mla_w_uq.shape[2] // (NOPE_DIM + ROPE_DIM)
    F = ffn_w_out.shape[1]
    rows = _Rows(B * S, S, T, NB)
    M = rows.m
    tm = min(512, S * B, T)
    assert rows.mc % tm == 0 and T % tm == 0
    assert S & (S - 1) == 0 and T & (T - 1) == 0

    tm_proj = min(1024, S * B, T)
    assert rows.mc % tm_proj == 0 and T % tm_proj == 0
    cond = jnp.concatenate([c_ctx[None, :], c, jnp.zeros((MOD_GROUPS - 1 - NB, D), F32)], axis=0)
    mod3 = _modulation(cond, mod_w, mod_b)

    x = (x_prompt.reshape(B * S, D), x_sample.reshape(NB * T, D))
    h = _normmod(rows, x, norm1_g[0], mod3[0], tm)
    new_gla, new_ret, new_ckv, new_kpe = [], [], [], []
    for l in range(depth):
        if l % 2 == 0:
            e = l // 2
            names = ("gq", "gk", "gv", "gg", "glr", "rq", "rk", "rv", "rg")
            sizes = (HA * DK, HA * DK, HA * DV, HA * DV, 2 * GR, HB * DK, HB * DK, HB * DV, HB * DV)
            src, o = {}, 0
            for n, sz in zip(names, sizes):
                src[n] = (o, sz)
                o += sz
            order = ("gq", "gk", "gv", "gg", "rq", "rk", "rv", "rg")
            col, o, parts = {}, 0, []
            for n in order:
                so, sz = src[n]
                parts.append(ab_w_in[e][:, so:so + sz])
                col[n] = (o, sz)
                o += sz
            w_in = jnp.concatenate(parts, axis=1).astype(BF16)
            so, sz = src["glr"]
            w_glr = jnp.concatenate([ab_w_in[e][:, so:so + sz], jnp.zeros((D, LANE - sz), F32)], axis=1).astype(BF16)
            z, glr = _inproj(h, w_in, w_glr, BF16, tm_proj)
            tb = min(256, S)
            o_fwd = None
            finals = []
            for dirn in (0, 1):
                cfg = _ScanCfg(rows, B, tb, HA, DK, DV, backward=bool(dirn))
                gate_w = jnp.zeros((LANE, HA * DK), F32).at[dirn * GR:(dirn + 1) * GR].set(gla_gate_w2[e, dirn])
                rd = jnp.broadcast_to(ret_decay[e, dirn][:, None, None], (HB, 8, LANE))
                s0g = jnp.swapaxes(state_gla[:, e, dirn], -1, -2)
                s0r = jnp.swapaxes(state_ret[:, e, dirn], -1, -2)
                extra = (o_fwd, gla_norm_g[e].reshape(1, -1), ret_norm_g[e].reshape(1, -1)) if dirn else None
                o_dir, sg, sr = _scan(cfg, z, glr, col, gate_w.astype(BF16), gla_gate_b[e, dirn].reshape(1, -1), rd,
                                      s0g, s0r, extra)
                o_fwd = o_dir
                finals.append((jnp.swapaxes(sg, -1, -2), jnp.swapaxes(sr, -1, -2)))
            y = o_fwd
            new_gla.append(jnp.stack([finals[0][0], finals[1][0]], axis=1))
            new_ret.append(jnp.stack([finals[0][1], finals[1][1]], axis=1))
            x, h = _outproj(rows, y, ab_w_out[e].astype(BF16), x, norm2_g[l], mod3[l], tm)
        else:
            i = l // 2
            perm = _rope_partner_perm()
            w = mla_w_in[i]
            w_kpe = w[:, q_rank + kv_rank:]
            w_in = w[:, :q_rank + kv_rank].astype(BF16)
            w_side = jnp.concatenate([w_kpe, w_kpe, w_kpe[:, perm], w_kpe[:, perm]], axis=1).astype(BF16)
            col = {"cq": (0, q_rank), "ckv": (q_rank, kv_rank)}
            zc, zk = _inproj(h, w_in, w_side, F32, tm_proj)
            cos_t, sin_t = _rope_tables(rows)
            wq = mla_w_uq[i].reshape(q_rank, HC, NOPE_DIM + ROPE_DIM)
            wq_rope = wq[:, :, NOPE_DIM:]
            w_uq = jnp.concatenate([wq[:, :, :NOPE_DIM].reshape(q_rank, -1), wq_rope.reshape(q_rank, -1),
                                    wq_rope[:, :, perm].reshape(q_rank, -1)], axis=1).astype(BF16)
            scale = (NOPE_DIM + ROPE_DIM) ** -0.5 * LOG2_E
            assert col["cq"][0] == 0
            qcat = _mla_q(zc, mla_q_norm_g[i], w_uq, cos_t, sin_t, HC, scale, tm)
            wkv = mla_w_ukv[i].reshape(kv_rank, HC, NOPE_DIM + V_DIM)
            w_k = wkv[:, :, :NOPE_DIM].reshape(kv_rank, -1).astype(BF16)
            w_vt = wkv[:, :, NOPE_DIM:].reshape(kv_rank, -1).T.astype(BF16)
            assert S % past == 0 and T % past == 0
            keys = _KeyRows(rows, past)
            ckpe = cache_kpe[:, i].reshape(NB * past, ROPE_DIM)
            kcat, vt, ckvn = _mla_kv(keys, zc, zk, col, cache_ckv[:, i].reshape(NB * past, kv_rank),
                                     jnp.concatenate([ckpe, ckpe], axis=1), mla_kv_norm_g[i], w_k, w_vt,
                                     cos_t, sin_t, HC)
            tk_lat = past + T
            assert (keys.lat_blocks * past) % S == 0
            tq = min(2048, T)
            o = _attention(qcat, kcat, vt, None, B, HC, HC, S, 1, 0, S, keys.lat_blocks * past // S)
            o = _attention(qcat, kcat, vt, o, NB, HC, 1, tq, T // tq, rows.mc // tq, tk_lat, 0)
            new_ckv.append(ckvn[keys.lat_blocks * past:].reshape(B, S, kv_rank))
            new_kpe.append(zk[:rows.mc, :ROPE_DIM].reshape(B, S, ROPE_DIM))
            x, h = _outproj(rows, o, mla_w_out[i].astype(BF16), x, norm2_g[l], mod3[l], tm)
        tf = _largest_tile(F, 512)
        ffn_w = (ffn_w_in[l].astype(BF16), ffn_conv[l], ffn_w_out[l].astype(BF16))
        if l == depth - 1:
            y_ctx, y_lat = _ffn(rows, h, x, mod3[l], *ffn_w, tm, tf, final_g=final_norm_g)
        else:
            x, h = _ffn(rows, h, x, mod3[l], *ffn_w, tm, tf, next_norm=(norm1_g[l + 1], mod3[l + 1]))

    y_prompt = y_ctx.reshape(B, S, D)
    y_sample = y_lat.reshape(NB, T, D)
    return (y_prompt, y_sample, jnp.stack(new_gla, axis=1), jnp.stack(new_ret, axis=1),
            jnp.stack(new_ckv, axis=1), jnp.stack(new_kpe, axis=1))
```

```python
import functools

import jax
import jax.numpy as jnp
from jax import lax
from jax.experimental import pallas as pl
from jax.experimental.pallas import tpu as pltpu

F32 = jnp.float32
BF16 = jnp.bfloat16

NORM_EPS = 1e-6
GATE_TEMP = 16.0
CHUNK = 64
GRID_W = 64
ROPE_BASE = 10000.0
ROPE_DIM = 64
NOPE_DIM = 128
V_DIM = 128
CONV_W = 3
LOG2_E = 1.4426950408889634

LANE = 128
BF16_SUBLANE = 16
MOD_GROUPS = 8
VMEM_LIMIT = 56 * 1024 * 1024

NT_DIMS = (((1,), (1,)), ((), ()))
TN_DIMS = (((0,), (0,)), ((), ()))


def _dot(a, b):
    return jnp.dot(a, b, preferred_element_type=F32)


def _dot_nt(a, b):
    return lax.dot_general(a, b, NT_DIMS, preferred_element_type=F32)


def _dot_tn(a, b):
    return lax.dot_general(a, b, TN_DIMS, preferred_element_type=F32)


def _params(*sem):
    return pltpu.CompilerParams(dimension_semantics=sem, vmem_limit_bytes=VMEM_LIMIT)


def _rms(x, g):
    ms = jnp.mean(x * x, axis=-1, keepdims=True)
    return (x * lax.rsqrt(ms + NORM_EPS)) * g


def _silu(x):
    return x * jax.nn.sigmoid(x)


def _log_sigmoid(x):
    return jnp.minimum(x, 0.0) - jnp.log1p(jnp.exp(-jnp.abs(x)))


def _largest_tile(n, cap):
    best = None
    for t in range(LANE, min(n, cap) + 1, LANE):
        if n % t == 0:
            best = t
    assert best is not None, (n, cap)
    return best


def _mod_kernel(c_ref, w_ref, b_ref, o_ref):
    s = _silu(c_ref[...]).astype(BF16)
    m = _dot(s, w_ref[0].astype(BF16)) + b_ref[0]
    for g in range(MOD_GROUPS):
        o_ref[0, g, 0] = m[g:g + 1]


def _modulation(cond, mod_w, mod_b):
    L, D, N = mod_w.shape
    n_vec = N // D
    out = pl.pallas_call(
        _mod_kernel,
        grid=(L, n_vec),
        in_specs=[
            pl.BlockSpec((MOD_GROUPS, D), lambda l, k: (0, 0)),
            pl.BlockSpec((1, D, D), lambda l, k: (l, 0, k)),
            pl.BlockSpec((1, 1, D), lambda l, k: (l, 0, k)),
        ],
        out_specs=pl.BlockSpec((1, MOD_GROUPS, 1, 1, D), lambda l, k: (l, 0, k, 0, 0)),
        out_shape=jax.ShapeDtypeStruct((L, MOD_GROUPS, n_vec, 1, D), F32),
        compiler_params=_params("parallel", "parallel"),
        name="modulation",
    )(cond, mod_w, mod_b.reshape(L, 1, N))
    return out.reshape(L, MOD_GROUPS * n_vec, 1, D)


class _Rows:
    def __init__(self, mc, seq, lat_len, n_lat):
        self.mc, self.seq, self.lat_len, self.n_lat = mc, seq, lat_len, n_lat
        self.m = mc + lat_len * n_lat

    def group(self, i, tm):
        r = i * tm
        return jnp.where(r < self.mc, 0, 1 + (r - self.mc) // self.lat_len)

    def mod_spec(self, which, tm, d):
        return pl.BlockSpec((1, 1, d), lambda i, *_: (self.group(i, tm) * 6 + which, 0, 0))

    def split_specs(self, tm, d):
        n_ctx = self.mc // tm
        return [pl.BlockSpec((tm, d), lambda i, *_: (jnp.minimum(i, n_ctx - 1), 0)),
                pl.BlockSpec((tm, d), lambda i, *_: (jnp.maximum(i - n_ctx, 0), 0))]


ROW_CHUNK = 16
ROW_PARTS = 4


def _for_row_chunks(n_rows, body):
    def it(c, carry):
        body(pl.ds(pl.multiple_of(c * ROW_CHUNK, ROW_CHUNK), ROW_CHUNK))
        return carry
    lax.fori_loop(0, n_rows // ROW_CHUNK, it, 0, unroll=4)


def _fold_gain(gs_ref, g_ref, sc_ref):
    gs_ref[...] = g_ref[...] * (1.0 + sc_ref[0])


def _norm_mod_rows(x, gs_ref, sh_ref):
    ms = jnp.mean(x * x, axis=-1, keepdims=True)
    return (x * lax.rsqrt(ms + NORM_EPS)) * gs_ref[...] + sh_ref[0]


def _normmod_kernel(rows, tm, xa_ref, xb_ref, g_ref, sh_ref, sc_ref, h_ref, gs_ref):
    _fold_gain(gs_ref, g_ref, sc_ref)

    def run(x_ref):
        def body(r):
            h_ref[r, :] = _norm_mod_rows(x_ref[r, :], gs_ref, sh_ref).astype(h_ref.dtype)
        _for_row_chunks(tm, body)

    is_ctx = pl.program_id(0) * tm < rows.mc
    pl.when(is_ctx)(lambda: run(xa_ref))
    pl.when(jnp.logical_not(is_ctx))(lambda: run(xb_ref))


def _normmod(rows, x_pair, g, mod3, tm):
    D = x_pair[0].shape[1]
    return pl.pallas_call(
        functools.partial(_normmod_kernel, rows, tm),
        grid=(rows.m // tm,),
        in_specs=rows.split_specs(tm, D) + [
            pl.BlockSpec((1, D), lambda i: (0, 0)),
            rows.mod_spec(0, tm, D),
            rows.mod_spec(1, tm, D),
        ],
        out_specs=pl.BlockSpec((tm, D), lambda i: (i, 0)),
        out_shape=jax.ShapeDtypeStruct((rows.m, D), BF16),
        scratch_shapes=[pltpu.VMEM((1, D), F32)],
        compiler_params=_params("parallel"),
        name="normmod",
    )(*x_pair, g.reshape(1, D), mod3, mod3)


def _inproj_kernel(h_ref, w_ref, ws_ref, o_ref, os_ref):
    @pl.when(pl.program_id(1) == 0)
    def _():
        os_ref[...] = _dot(h_ref[...], ws_ref[...])

    o_ref[...] = _dot(h_ref[...], w_ref[...]).astype(o_ref.dtype)


def _inproj(h, w, w_side, out_dtype, tm, tn_cap=1024):
    M, D = h.shape
    N = w.shape[1]
    NS = w_side.shape[1]
    tn = _largest_tile(N, tn_cap)
    return pl.pallas_call(
        _inproj_kernel,
        grid=(M // tm, N // tn),
        in_specs=[
            pl.BlockSpec((tm, D), lambda i, j: (i, 0)),
            pl.BlockSpec((D, tn), lambda i, j: (0, j)),
            pl.BlockSpec((D, NS), lambda i, j: (0, 0)),
        ],
        out_specs=[pl.BlockSpec((tm, tn), lambda i, j: (i, j)), pl.BlockSpec((tm, NS), lambda i, j: (i, 0))],
        out_shape=[jax.ShapeDtypeStruct((M, N), out_dtype), jax.ShapeDtypeStruct((M, NS), F32)],
        compiler_params=_params("parallel", "arbitrary"),
        name="inproj",
    )(h, w, w_side)


def _outproj_kernel(rows, tm, split, y_ref, w_ref, *rest):
    if split:
        xa_ref, xb_ref, gate_ref, g_ref, sh_ref, sc_ref, o_ref, h_ref, acc_ref, gs_ref = rest
    else:
        xa_ref, gate_ref, g_ref, sh_ref, sc_ref, o_ref, h_ref, acc_ref, gs_ref = rest
        xb_ref = xa_ref
    _fold_gain(gs_ref, g_ref, sc_ref)
    is_ctx = pl.program_id(0) * tm < rows.mc
    part = tm // ROW_PARTS

    def finish(r):
        x = jnp.where(is_ctx, xa_ref[r, :], xb_ref[r, :]) if split else xa_ref[r, :]
        x1 = x + gate_ref[0] * acc_ref[r, :]
        o_ref[r, :] = x1
        h_ref[r, :] = _norm_mod_rows(x1, gs_ref, sh_ref).astype(h_ref.dtype)

    for p in range(ROW_PARTS + 1):
        if p < ROW_PARTS:
            rp = slice(p * part, (p + 1) * part)
            acc_ref[rp, :] = _dot(y_ref[rp, :], w_ref[...])
        if p > 0:
            for c in range(part // ROW_CHUNK):
                start = (p - 1) * part + c * ROW_CHUNK
                finish(slice(start, start + ROW_CHUNK))


def _outproj(rows, y, w, x, g, mod3, tm):
    M, K = y.shape
    N = w.shape[1]
    split = isinstance(x, (tuple, list))
    x_specs = rows.split_specs(tm, N) if split else [pl.BlockSpec((tm, N), lambda i: (i, 0))]
    x_args = list(x) if split else [x]
    return pl.pallas_call(
        functools.partial(_outproj_kernel, rows, tm, split),
        grid=(M // tm,),
        in_specs=[
            pl.BlockSpec((tm, K), lambda i: (i, 0)),
            pl.BlockSpec((K, N), lambda i: (0, 0)),
        ] + x_specs + [
            rows.mod_spec(2, tm, N),
            pl.BlockSpec((1, N), lambda i: (0, 0)),
            rows.mod_spec(3, tm, N),
            rows.mod_spec(4, tm, N),
        ],
        out_specs=[pl.BlockSpec((tm, N), lambda i: (i, 0)), pl.BlockSpec((tm, N), lambda i: (i, 0))],
        out_shape=[jax.ShapeDtypeStruct((M, N), F32), jax.ShapeDtypeStruct((M, N), BF16)],
        scratch_shapes=[pltpu.VMEM((tm, N), F32), pltpu.VMEM((1, N), F32)],
        compiler_params=_params("parallel"),
        name="outproj",
    )(y, w, *x_args, mod3, g.reshape(1, N), mod3, mod3)


HALO = BF16_SUBLANE


def _ffn_kernel(rows, tm, final_norm, hp_ref, h_ref, hn_ref, x_ref, gate_ref,
                wa_ref, wb_ref, cw_ref, wo_ref, *rest):
    if final_norm:
        fg_ref, o_ctx_ref, o_lat_ref, hs_ref, a_ref, acc_ref = rest
    else:
        g_ref, sh_ref, sc_ref, o_ref, hnext_ref, hs_ref, a_ref, acc_ref, gs_ref = rest
    i = pl.program_id(0)
    f = pl.program_id(1)

    @pl.when(f == 0)
    def _():
        hs_ref[0:HALO, :] = hp_ref[...]
        hs_ref[HALO:HALO + tm, :] = h_ref[...]
        hs_ref[HALO + tm:, :] = hn_ref[...]
        acc_ref[...] = jnp.zeros_like(acc_ref)

    def hidden_tile():
        a_ref[...] = _dot(hs_ref[...], wa_ref[...])
        b = _dot(h_ref[...], wb_ref[...])
        row = i * tm + lax.broadcasted_iota(jnp.int32, (tm, 1), 0)
        pos = jnp.where(row < rows.mc, row & (rows.seq - 1), (row - rows.mc) & (rows.lat_len - 1))
        seq_len = jnp.where(row < rows.mc, rows.seq, rows.lat_len)
        a_prev = jnp.where(pos == 0, 0.0, a_ref[pl.ds(HALO - 1, tm), :])
        a_next = jnp.where(pos == seq_len - 1, 0.0, a_ref[pl.ds(HALO + 1, tm), :])
        a_mid = a_ref[pl.ds(HALO, tm), :]
        cw = cw_ref[...]
        a = cw[0:1] * a_prev + cw[1:2] * a_mid + cw[2:3] * a_next
        return (_silu(a) * b).astype(BF16)

    def residual(r):
        return x_ref[r, :] + gate_ref[0] * acc_ref[r, :]

    def last_step(finish):
        act = hidden_tile()
        part = tm // ROW_PARTS
        for p in range(ROW_PARTS + 1):
            if p < ROW_PARTS:
                rp = slice(p * part, (p + 1) * part)
                acc_ref[rp, :] += _dot(act[rp, :], wo_ref[...])
            if p > 0:
                for c in range(part // ROW_CHUNK):
                    start = (p - 1) * part + c * ROW_CHUNK
                    finish(slice(start, start + ROW_CHUNK))

    last = pl.num_programs(1) - 1

    @pl.when(f < last)
    def _():
        acc_ref[...] += _dot(hidden_tile(), wo_ref[...])

    if final_norm:
        def finish_to(out_ref):
            def finish(r):
                out_ref[r, :] = _rms(residual(r), fg_ref[...])
            return finish

        is_ctx = i * tm < rows.mc
        pl.when((f == last) & is_ctx)(lambda: last_step(finish_to(o_ctx_ref)))
        pl.when((f == last) & jnp.logical_not(is_ctx))(lambda: last_step(finish_to(o_lat_ref)))
    else:
        def finish(r):
            x2 = residual(r)
            o_ref[r, :] = x2
            hnext_ref[r, :] = _norm_mod_rows(x2, gs_ref, sh_ref).astype(hnext_ref.dtype)

        @pl.when(f == last)
        def _():
            _fold_gain(gs_ref, g_ref, sc_ref)
            last_step(finish)


def _ffn(rows, h, x, mod3, w_in, conv_w, w_out, tm, tf, next_norm=None, final_g=None):
    M, D = x.shape
    F = w_out.shape[0]
    nf = F // tf
    nhalo = M // HALO
    final_norm = final_g is not None
    kern = functools.partial(_ffn_kernel, rows, tm, final_norm)
    in_specs = [
        pl.BlockSpec((HALO, D), lambda i, f: (jnp.maximum(i * (tm // HALO) - 1, 0), 0)),
        pl.BlockSpec((tm, D), lambda i, f: (i, 0)),
        pl.BlockSpec((HALO, D), lambda i, f: (jnp.minimum((i + 1) * (tm // HALO), nhalo - 1), 0)),
        pl.BlockSpec((tm, D), lambda i, f: (i, 0)),
        rows.mod_spec(5, tm, D),
        pl.BlockSpec((D, tf), lambda i, f: (0, f)),
        pl.BlockSpec((D, tf), lambda i, f: (0, nf + f)),
        pl.BlockSpec((CONV_W, tf), lambda i, f: (0, f)),
        pl.BlockSpec((tf, D), lambda i, f: (f, 0)),
    ]
    args = [h, h, h, x, mod3, w_in, w_in, conv_w, w_out]
    vec = pl.BlockSpec((1, D), lambda i, f: (0, 0))
    if final_norm:
        n_ctx = rows.mc // tm
        in_specs += [vec]
        args += [final_g.reshape(1, D)]
        out_specs = [pl.BlockSpec((tm, D), lambda i, f: (jnp.minimum(i, n_ctx - 1), 0)),
                     pl.BlockSpec((tm, D), lambda i, f: (jnp.maximum(i - n_ctx, 0), 0))]
        out_shape = [jax.ShapeDtypeStruct((rows.mc, D), F32), jax.ShapeDtypeStruct((M - rows.mc, D), F32)]
        row_sem = "arbitrary"
    else:
        g_next, mod3_next = next_norm
        in_specs += [vec, rows.mod_spec(0, tm, D), rows.mod_spec(1, tm, D)]
        args += [g_next.reshape(1, D), mod3_next, mod3_next]
        out_specs = [pl.BlockSpec((tm, D), lambda i, f: (i, 0)), pl.BlockSpec((tm, D), lambda i, f: (i, 0))]
        out_shape = [jax.ShapeDtypeStruct((M, D), F32), jax.ShapeDtypeStruct((M, D), BF16)]
        row_sem = "parallel"
    return pl.pallas_call(
        kern,
        grid=(M // tm, nf),
        in_specs=in_specs,
        out_specs=out_specs,
        out_shape=out_shape,
        scratch_shapes=[
            pltpu.VMEM((tm + 2 * HALO, D), BF16),
            pltpu.VMEM((tm + 2 * HALO, tf), F32),
            pltpu.VMEM((tm, D), F32),
        ] + ([] if final_norm else [pltpu.VMEM((1, D), F32)]),
        compiler_params=_params(row_sem, "arbitrary"),
        name="convffn",
    )(*args)


class _ScanCfg:
    def __init__(self, rows, batch, tb, h, dk, dv, backward):
        self.rows, self.batch, self.tb, self.h, self.dk, self.dv = rows, batch, tb, h, dk, dv
        self.backward = backward
        self.cps = rows.seq // tb
        self.lps = rows.lat_len // tb
        self.ctx_blocks = batch * self.cps
        self.nblk = rows.m // tb

    def block(self, i):
        return self.nblk - 1 - i if self.backward else i

    def is_ctx(self, r):
        return r < self.ctx_blocks

    def seq_pos(self, r):
        ctx = self.is_ctx(r)
        return (jnp.where(ctx, r % self.cps, (r - self.ctx_blocks) % self.lps),
                jnp.where(ctx, self.cps, self.lps))

    def lat_seq(self, r):
        return jnp.clip((r - self.ctx_blocks) // self.lps, 0, self.rows.n_lat - 1)

    def ctx_seq(self, r):
        return jnp.clip(r // self.cps, 0, self.batch - 1)


def _scan_kernel(cfg, gq_ref, gk_ref, gv_ref, rq_ref, rk_ref, rv_ref, glr_ref, gw_ref, gb_ref, rd_ref,
                 s0g_ref, s0r_ref, *rest):
    scratch = rest[-8:]
    sg_ref, sr_ref, eb_ref, qd_ref, kinv_ref, kend_ref, rks_ref, rkd_ref = scratch
    if cfg.backward:
        of_ref, gg_ref, rg_ref, gng_ref, rng_ref, y_ref, sgo_ref, sro_ref = rest[:-8]
    else:
        o_ref, sgo_ref, sro_ref = rest[:-8]
    H, DK, DV, C = cfg.h, cfg.dk, cfg.dv, CHUNK
    r = cfg.block(pl.program_id(0))
    blk, nblk_seq = cfg.seq_pos(r)
    first = blk == (nblk_seq - 1 if cfg.backward else 0)
    last = blk == (0 if cfg.backward else nblk_seq - 1)
    is_ctx = cfg.is_ctx(r)

    @pl.when(first & is_ctx)
    def _():
        sg_ref[...] = jnp.zeros_like(sg_ref)
        sr_ref[...] = jnp.zeros_like(sr_ref)

    @pl.when(first & jnp.logical_not(is_ctx))
    def _():
        sg_ref[...] = s0g_ref[0]
        sr_ref[...] = s0r_ref[0]

    ti = lax.broadcasted_iota(jnp.int32, (C, C), 0)
    tj = lax.broadcasted_iota(jnp.int32, (C, C), 1)
    sees = (tj >= ti) if cfg.backward else (tj <= ti)
    dist = jnp.abs(ti - tj).astype(F32)
    rowi = lax.broadcasted_iota(jnp.int32, (C, LANE), 0)
    to_end = (rowi if cfg.backward else C - 1 - rowi).astype(F32)
    from_start = (C - rowi if cfg.backward else rowi + 1).astype(F32)

    n_chunks = cfg.tb // C
    HDK = H * DK
    scale = DK ** -0.5

    bi = lax.broadcasted_iota(jnp.int32, (cfg.tb, cfg.tb), 0)
    bj = lax.broadcasted_iota(jnp.int32, (cfg.tb, cfg.tb), 1)
    same_chunk = (bi & -C) == (bj & -C)
    blk_tri = jnp.where(same_chunk & ((bj >= bi) if cfg.backward else (bj <= bi)), 1.0, 0.0).astype(BF16)
    pre = _dot(glr_ref[...].astype(BF16), gw_ref[...]) + gb_ref[...]
    la = _log_sigmoid(pre) / GATE_TEMP
    la_hi = la.astype(BF16)
    la_lo = (la - la_hi.astype(F32)).astype(BF16)
    b = _dot(blk_tri, la_hi) + _dot(blk_tri, la_lo)
    b3 = b.reshape(n_chunks, C, HDK)
    b_end = b3[:, 0:1, :] if cfg.backward else b3[:, C - 1:C, :]
    eb = jnp.exp(b)
    eb_ref[...] = eb
    gk = gk_ref[...].astype(F32)
    qd_ref[...] = (gq_ref[...].astype(F32) * scale * eb).astype(BF16)
    kinv_ref[...] = (gk * jnp.exp(-b)).astype(BF16)
    kend_ref[...] = (gk * jnp.exp(b_end - b3).reshape(cfg.tb, HDK)).astype(BF16)

    decays, q_decs, c_decs, k_dec_cols = [], [], [], []
    for h in range(H):
        lg = _log_sigmoid(rd_ref[h])
        lg_c = jnp.broadcast_to(lg[0:1, 0:C], (C, C))
        lg_l = jnp.broadcast_to(lg[0:1, :], (C, LANE))
        decays.append(jnp.where(sees, jnp.exp(lg_c * dist), 0.0))
        q_decs.append(jnp.concatenate([jnp.exp(lg_l * from_start)] * (DV // LANE), axis=1))
        c_decs.append(jnp.exp(lg[0:1, :] * float(C)))
        k_dec_cols.append(jnp.concatenate([jnp.exp(lg_l * to_end)] * n_chunks, axis=0))
    rk = rk_ref[...].astype(F32) * scale
    rks_ref[...] = rk.astype(BF16)
    rkd_ref[...] = (rk * jnp.concatenate(k_dec_cols, axis=1)).astype(BF16)

    for ci in range(n_chunks):
        c = n_chunks - 1 - ci if cfg.backward else ci
        rs = slice(c * C, (c + 1) * C)
        end_row = c * C if cfg.backward else (c + 1) * C - 1
        heads = range(H)
        ksl = [slice(h * DK, (h + 1) * DK) for h in heads]
        vsl = [slice(h * DV, (h + 1) * DV) for h in heads]
        att_raw = [_dot_nt(qd_ref[rs, ksl[h]], kinv_ref[rs, ksl[h]]) for h in heads]
        ratt_raw = [_dot_nt(rq_ref[rs, ksl[h]], rks_ref[rs, ksl[h]]) for h in heads]
        qs_g = [_dot_nt(qd_ref[rs, ksl[h]], sg_ref[h].astype(BF16)) for h in heads]
        qs_r = [_dot_nt(rq_ref[rs, ksl[h]], sr_ref[h].astype(BF16)) for h in heads]
        for h in heads:
            att = jnp.where(sees, att_raw[h], 0.0).astype(BF16)
            o_g = _dot(att, gv_ref[rs, vsl[h]]) + qs_g[h]
            ratt = (ratt_raw[h] * decays[h]).astype(BF16)
            o_r = _dot(ratt, rv_ref[rs, vsl[h]]) + qs_r[h] * q_decs[h]
            rs_cols = slice(H * DV + h * DV, H * DV + (h + 1) * DV)
            if cfg.backward:
                for o, cols, gate_ref, ng_ref in ((o_g, vsl[h], gg_ref, gng_ref), (o_r, rs_cols, rg_ref, rng_ref)):
                    tot = o + of_ref[rs, cols]
                    mu = jnp.mean(tot, axis=-1, keepdims=True)
                    d = tot - mu
                    var = jnp.mean(d * d, axis=-1, keepdims=True)
                    yn = d * lax.rsqrt(var + NORM_EPS) * ng_ref[:, vsl[h]]
                    y_ref[rs, cols] = (_silu(gate_ref[rs, vsl[h]].astype(F32)) * yn).astype(y_ref.dtype)
            else:
                o_ref[rs, vsl[h]] = o_g
                o_ref[rs, rs_cols] = o_r
        for h in heads:
            sg_ref[h] = (sg_ref[h] * eb_ref[end_row:end_row + 1, ksl[h]]
                         + _dot_tn(gv_ref[rs, vsl[h]], kend_ref[rs, ksl[h]]))
            sr_ref[h] = sr_ref[h] * c_decs[h] + _dot_tn(rv_ref[rs, vsl[h]], rkd_ref[rs, ksl[h]])

    @pl.when(last & is_ctx)
    def _():
        sgo_ref[0] = sg_ref[...]
        sro_ref[0] = sr_ref[...]


def _scan(cfg, z, glr, col, gate_w, gate_b, ret_decay_t, s0g, s0r, extra):
    H, DK, DV, tb = cfg.h, cfg.dk, cfg.dv, cfg.tb
    M = z.shape[0]

    def zspec(name):
        off, width = col[name]
        assert off % width == 0, (name, off, width)
        return pl.BlockSpec((tb, width), lambda i: (cfg.block(i), off // width))

    def full(shape):
        return pl.BlockSpec(shape, lambda i: (0,) * len(shape))

    state_in = pl.BlockSpec((1, H, DV, DK), lambda i: (cfg.lat_seq(cfg.block(i)), 0, 0, 0))
    state_out = pl.BlockSpec((1, H, DV, DK), lambda i: (cfg.ctx_seq(cfg.block(i)), 0, 0, 0))
    row_blk = pl.BlockSpec((tb, 2 * H * DV), lambda i: (cfg.block(i), 0))
    glr_spec = pl.BlockSpec((tb, glr.shape[1]), lambda i: (cfg.block(i), 0))
    in_specs = [zspec("gq"), zspec("gk"), zspec("gv"), zspec("rq"), zspec("rk"), zspec("rv"), glr_spec,
                full(gate_w.shape), full(gate_b.shape), full(ret_decay_t.shape), state_in, state_in]
    args = [z, z, z, z, z, z, glr, gate_w, gate_b, ret_decay_t, s0g, s0r]
    state_shape = jax.ShapeDtypeStruct((cfg.batch, H, DV, DK), F32)
    if cfg.backward:
        o_fwd, gla_ng, ret_ng = extra
        in_specs += [row_blk, zspec("gg"), zspec("rg"), full(gla_ng.shape), full(ret_ng.shape)]
        args += [o_fwd, z, z, gla_ng, ret_ng]
        out0 = jax.ShapeDtypeStruct((M, 2 * H * DV), BF16)
    else:
        out0 = jax.ShapeDtypeStruct((M, 2 * H * DV), F32)
    return pl.pallas_call(
        functools.partial(_scan_kernel, cfg),
        grid=(cfg.nblk,),
        in_specs=in_specs,
        out_specs=[row_blk, state_out, state_out],
        out_shape=[out0, state_shape, state_shape],
        scratch_shapes=[pltpu.VMEM((H, DV, DK), F32), pltpu.VMEM((H, DV, DK), F32),
                        pltpu.VMEM((tb, H * DK), F32)] + [pltpu.VMEM((tb, H * DK), BF16)] * 5,
        compiler_params=_params("arbitrary"),
        name="scan_bwd" if cfg.backward else "scan_fwd",
    )(*args)


def _rope_keep(h, shape):
    lane = lax.broadcasted_iota(jnp.int32, shape, 1)
    return (lane < ROPE_DIM) if h % 2 == 0 else (lane >= ROPE_DIM)


def _mla_q_kernel(n_heads, scale, cq_ref, qng_ref, w_ref, cos_ref, sin_ref, o_ref):
    cqn = _rms(cq_ref[...], qng_ref[...]).astype(BF16)
    q = _dot(cqn, w_ref[...])
    nope_w = n_heads * NOPE_DIM
    rope_w = n_heads * ROPE_DIM
    cos, sin = cos_ref[...], sin_ref[...]
    for h in range(n_heads):
        p = h // 2
        tile = q[:, nope_w + p * LANE:nope_w + (p + 1) * LANE]
        partner = q[:, nope_w + rope_w + p * LANE:nope_w + rope_w + (p + 1) * LANE]
        rot = (tile * cos + partner * sin) * scale
        rot = jnp.where(_rope_keep(h, rot.shape), rot, 0.0)
        base = h * 2 * LANE
        o_ref[:, base:base + LANE] = (q[:, h * NOPE_DIM:(h + 1) * NOPE_DIM] * scale).astype(BF16)
        o_ref[:, base + LANE:base + 2 * LANE] = rot.astype(BF16)


def _mla_q(zc, q_ng, w_uq, cos_t, sin_t, n_heads, scale, tm):
    M = zc.shape[0]
    R = q_ng.shape[0]
    N = w_uq.shape[1]
    return pl.pallas_call(
        functools.partial(_mla_q_kernel, n_heads, scale),
        grid=(M // tm,),
        in_specs=[
            pl.BlockSpec((tm, R), lambda i: (i, 0)),
            pl.BlockSpec((1, R), lambda i: (0, 0)),
            pl.BlockSpec((R, N), lambda i: (0, 0)),
            pl.BlockSpec((tm, LANE), lambda i: (i, 0)),
            pl.BlockSpec((tm, LANE), lambda i: (i, 0)),
        ],
        out_specs=pl.BlockSpec((tm, n_heads * 2 * LANE), lambda i: (i, 0)),
        out_shape=jax.ShapeDtypeStruct((M, n_heads * 2 * LANE), BF16),
        compiler_params=_params("parallel"),
        name="mla_q",
    )(zc, q_ng.reshape(1, R), w_uq, cos_t, sin_t)


class _KeyRows:
    def __init__(self, rows, blk):
        self.rows, self.blk = rows, blk
        self.per_lat = 1 + rows.lat_len // blk
        self.lat_blocks = rows.n_lat * self.per_lat
        self.ctx_blocks = rows.mc // blk
        self.n = self.lat_blocks + self.ctx_blocks

    def is_cache(self, j):
        return (j < self.lat_blocks) & (j % self.per_lat == 0)

    def cache_idx(self, j):
        return jnp.clip(j // self.per_lat, 0, self.rows.n_lat - 1)

    def token_block(self, j):
        b = j // self.per_lat
        t = jnp.maximum(j % self.per_lat - 1, 0)
        lat = self.ctx_blocks + b * (self.per_lat - 1) + t
        return jnp.where(j < self.lat_blocks, lat, j - self.lat_blocks)


def _mla_kv_kernel(keys, n_heads, ckv_ref, kpe_ref, cckv_ref, ckpe_ref, kvng_ref, wk_ref, wvt_ref, cos_ref, sin_ref,
                   kcat_ref, vt_ref, ckvn_ref):
    cached = keys.is_cache(pl.program_id(0))
    kpe = kpe_ref[...]
    rot = kpe[:, :LANE] * cos_ref[...] + kpe[:, LANE:] * sin_ref[...]
    ckvn = jnp.where(cached, cckv_ref[...], _rms(ckv_ref[...], kvng_ref[...]))
    kr = jnp.where(cached, ckpe_ref[...], rot)
    ckvn_ref[...] = ckvn
    ckvn_b = ckvn.astype(BF16)
    kn = _dot(ckvn_b, wk_ref[...])
    vt_ref[...] = _dot_nt(wvt_ref[...], ckvn_b).astype(BF16)
    for h in range(n_heads):
        base = h * 2 * LANE
        kcat_ref[:, base:base + LANE] = kn[:, h * NOPE_DIM:(h + 1) * NOPE_DIM].astype(BF16)
        kcat_ref[:, base + LANE:base + 2 * LANE] = jnp.where(_rope_keep(h, kr.shape), kr, 0.0).astype(BF16)


def _mla_kv(keys, zc, zk, col, cache_ckv, cache_kpe2, kv_ng, w_k, w_vt, cos_t, sin_t, n_heads):
    blk = keys.blk
    R = kv_ng.shape[0]
    Mk = keys.n * blk
    ckv_off, ckv_w = col["ckv"]
    assert ckv_off % ckv_w == 0
    tok = keys.token_block
    return pl.pallas_call(
        functools.partial(_mla_kv_kernel, keys, n_heads),
        grid=(keys.n,),
        in_specs=[
            pl.BlockSpec((blk, ckv_w), lambda j: (tok(j), ckv_off // ckv_w)),
            pl.BlockSpec((blk, 2 * LANE), lambda j: (tok(j), 0)),
            pl.BlockSpec((blk, R), lambda j: (keys.cache_idx(j), 0)),
            pl.BlockSpec((blk, LANE), lambda j: (keys.cache_idx(j), 0)),
            pl.BlockSpec((1, R), lambda j: (0, 0)),
            pl.BlockSpec(w_k.shape, lambda j: (0, 0)),
            pl.BlockSpec(w_vt.shape, lambda j: (0, 0)),
            pl.BlockSpec((blk, LANE), lambda j: (tok(j), 0)),
            pl.BlockSpec((blk, LANE), lambda j: (tok(j), 0)),
        ],
        out_specs=[
            pl.BlockSpec((blk, n_heads * 2 * LANE), lambda j: (j, 0)),
            pl.BlockSpec((n_heads * V_DIM, blk), lambda j: (0, j)),
            pl.BlockSpec((blk, R), lambda j: (j, 0)),
        ],
        out_shape=[
            jax.ShapeDtypeStruct((Mk, n_heads * 2 * LANE), BF16),
            jax.ShapeDtypeStruct((n_heads * V_DIM, Mk), BF16),
            jax.ShapeDtypeStruct((Mk, R), F32),
        ],
        compiler_params=_params("parallel"),
        name="mla_kv",
    )(zc, zk, cache_ckv, cache_kpe2, kv_ng.reshape(1, R), w_k, w_vt, cos_t, sin_t)


ATTN_TQ = 256
ATTN_TK = 512


def _attn_kernel(tq, chunks, q_ref, k_ref, vt_ref, o_ref, st0_ref, st1_ref):
    n_heads = q_ref.shape[1] // (2 * LANE)
    units = [(h, u) for h in range(n_heads) for u in range(q_ref.shape[0] // tq)]
    st_refs = (st0_ref, st1_ref)

    def scores(n, c0, c1):
        h, u = units[n]
        hd = slice(h * 2 * LANE, (h + 1) * 2 * LANE)
        st = _dot_nt(k_ref[c0:c1, hd], q_ref[u * tq:(u + 1) * tq, hd])
        st_refs[n % 2][c0:c1, :] = st
        return jnp.max(st, axis=0, keepdims=True)

    def weights(n, c0, c1, m):
        h, _ = units[n]
        p = jnp.exp2(st_refs[n % 2][c0:c1, :] - m)
        return jnp.sum(p, axis=0, keepdims=True), _dot(vt_ref[h * V_DIM:(h + 1) * V_DIM, c0:c1], p.astype(BF16))

    m_prev = None
    for n in range(len(units) + 1):
        m_new = acc = l = None
        for c0, c1 in chunks:
            if n < len(units):
                mc = scores(n, c0, c1)
                m_new = mc if m_new is None else jnp.maximum(m_new, mc)
            if n > 0:
                lc, pv = weights(n - 1, c0, c1, m_prev)
                l = lc if l is None else l + lc
                acc = pv if acc is None else acc + pv
        if n > 0:
            h, u = units[n - 1]
            o_ref[u * tq:(u + 1) * tq, h * V_DIM:(h + 1) * V_DIM] = (acc / l).T.astype(o_ref.dtype)
        m_prev = m_new


def _attn_kernel_aliased(tq, chunks, q_ref, k_ref, vt_ref, o_prev_ref, o_ref, st0_ref, st1_ref):
    del o_prev_ref
    _attn_kernel(tq, chunks, q_ref, k_ref, vt_ref, o_ref, st0_ref, st1_ref)


def _attention(qcat, kcat, vt, o_prev, n_seq, n_heads, hps, tq, q_blocks_per_seq, q_block0, tk, k_block0):
    sub_q = ATTN_TQ if tq % ATTN_TQ == 0 else LANE
    assert tq % sub_q == 0 and n_heads % hps == 0
    q_spec = pl.BlockSpec((tq, hps * 2 * LANE), lambda s, h, i: (q_block0 + s * q_blocks_per_seq + i, h))
    in_specs = [
        q_spec,
        pl.BlockSpec((tk, hps * 2 * LANE), lambda s, h, i: (k_block0 + s, h)),
        pl.BlockSpec((hps * V_DIM, tk), lambda s, h, i: (h, k_block0 + s)),
    ]
    args = [qcat, kcat, vt]
    chunks = tuple((c0, min(c0 + ATTN_TK, tk)) for c0 in range(0, tk, ATTN_TK))
    kern = functools.partial(_attn_kernel, sub_q, chunks)
    aliases = {}
    if o_prev is not None:
        in_specs.append(pl.BlockSpec(memory_space=pl.ANY))
        args.append(o_prev)
        aliases = {3: 0}
        kern = functools.partial(_attn_kernel_aliased, sub_q, chunks)
    return pl.pallas_call(
        kern,
        grid=(n_seq, n_heads // hps, q_blocks_per_seq),
        in_specs=in_specs,
        out_specs=pl.BlockSpec((tq, hps * V_DIM), lambda s, h, i: (q_block0 + s * q_blocks_per_seq + i, h)),
        out_shape=jax.ShapeDtypeStruct((qcat.shape[0], n_heads * V_DIM), BF16),
        input_output_aliases=aliases,
        scratch_shapes=[pltpu.VMEM((tk, sub_q), F32), pltpu.VMEM((tk, sub_q), F32)],
        compiler_params=_params("parallel", "parallel", "arbitrary"),
        name="mla_attn",
    )(*args)


def _rope_partner(w):
    q = ROPE_DIM // 4
    return jnp.flip(w.reshape(w.shape[:-1] + (2, 2, q)), axis=-2).reshape(w.shape)


def _rope_tables(rows):
    half, quarter = ROPE_DIM // 2, ROPE_DIM // 4
    inv = ROPE_BASE ** (-jnp.arange(quarter, dtype=F32) * 2.0 / half)
    t = jnp.arange(rows.lat_len)
    r = (t // GRID_W).astype(F32)
    c = (t % GRID_W).astype(F32)
    ang_r = r[:, None] * inv[None, :]
    ang_c = c[:, None] * inv[None, :]
    cos = jnp.concatenate([jnp.cos(ang_r), jnp.cos(ang_r), jnp.cos(ang_c), jnp.cos(ang_c)], axis=1)
    sin = jnp.concatenate([-jnp.sin(ang_r), jnp.sin(ang_r), -jnp.sin(ang_c), jnp.sin(ang_c)], axis=1)
    cos = jnp.tile(jnp.concatenate([cos, cos], axis=1), (rows.n_lat, 1))
    sin = jnp.tile(jnp.concatenate([sin, sin], axis=1), (rows.n_lat, 1))
    cos = jnp.concatenate([jnp.ones((rows.mc, LANE), F32), cos], axis=0)
    sin = jnp.concatenate([jnp.zeros((rows.mc, LANE), F32), sin], axis=0)
    return cos, sin


def kernel(x_prompt, x_sample, state_gla, state_ret, cache_ckv, cache_kpe, c, c_ctx, mod_w, mod_b, norm1_g, norm2_g, ab_w_in, gla_gate_w2, gla_gate_b, ret_decay, gla_norm_g, ret_norm_g, ab_w_out, mla_w_in, mla_q_norm_g, mla_w_uq, mla_kv_norm_g, mla_w_ukv, mla_w_out, ffn_w_in, ffn_conv, ffn_w_out, final_norm_g):
    B, S, D = x_prompt.shape
    NB, T, _ = x_sample.shape
    depth = mod_w.shape[0]
    _, _, _, HA, DK, DV = state_gla.shape
    HB = state_ret.shape[3]
    assert HA == HB and state_ret.shape[4:] == (DK, DV)
    GR = gla_gate_w2.shape[2]
    past = cache_ckv.shape[2]
    q_rank = mla_q_norm_g.shape[1]
    kv_rank = mla_kv_norm_g.shape[1]
    HC = mla_w_uq.shape[2] // (NOPE_DIM + ROPE_DIM)
    F = ffn_w_out.shape[1]
    rows = _Rows(B * S, S, T, NB)
    M = rows.m
    tm = min(512, S * B, T)
    assert rows.mc % tm == 0 and T % tm == 0
    assert S & (S - 1) == 0 and T & (T - 1) == 0

    tm_proj = min(1024, S * B, T)
    assert rows.mc % tm_proj == 0 and T % tm_proj == 0
    cond = jnp.concatenate([c_ctx[None, :], c, jnp.zeros((MOD_GROUPS - 1 - NB, D), F32)], axis=0)
    mod3 = _modulation(cond, mod_w, mod_b)

    x = (x_prompt.reshape(B * S, D), x_sample.reshape(NB * T, D))
    h = _normmod(rows, x, norm1_g[0], mod3[0], tm)
    new_gla, new_ret, new_ckv, new_kpe = [], [], [], []
    for l in range(depth):
        if l % 2 == 0:
            e = l // 2
            names = ("gq", "gk", "gv", "gg", "glr", "rq", "rk", "rv", "rg")
            sizes = (HA * DK, HA * DK, HA * DV, HA * DV, 2 * GR, HB * DK, HB * DK, HB * DV, HB * DV)
            src, o = {}, 0
            for n, sz in zip(names, sizes):
                src[n] = (o, sz)
                o += sz
            order = ("gq", "gk", "gv", "gg", "rq", "rk", "rv", "rg")
            col, o = {}, 0
            for n in order:
                col[n] = (o, src[n][1])
                o += src[n][1]
            so, sz = src["glr"]
            w_in = jnp.concatenate([ab_w_in[e][:, :so], ab_w_in[e][:, so + sz:]], axis=1).astype(BF16)
            w_glr = jnp.concatenate([ab_w_in[e][:, so:so + sz], jnp.zeros((D, LANE - sz), F32)], axis=1).astype(BF16)
            z, glr = _inproj(h, w_in, w_glr, BF16, tm_proj)
            tb = min(256, S)
            o_fwd = None
            finals = []
            for dirn in (0, 1):
                cfg = _ScanCfg(rows, B, tb, HA, DK, DV, backward=bool(dirn))
                gate_w = jnp.zeros((LANE, HA * DK), F32).at[dirn * GR:(dirn + 1) * GR].set(gla_gate_w2[e, dirn])
                rd = jnp.broadcast_to(ret_decay[e, dirn][:, None, None], (HB, 8, LANE))
                s0g = jnp.swapaxes(state_gla[:, e, dirn], -1, -2)
                s0r = jnp.swapaxes(state_ret[:, e, dirn], -1, -2)
                extra = (o_fwd, gla_norm_g[e].reshape(1, -1), ret_norm_g[e].reshape(1, -1)) if dirn else None
                o_dir, sg, sr = _scan(cfg, z, glr, col, gate_w.astype(BF16), gla_gate_b[e, dirn].reshape(1, -1), rd,
                                      s0g, s0r, extra)
                o_fwd = o_dir
                finals.append((jnp.swapaxes(sg, -1, -2), jnp.swapaxes(sr, -1, -2)))
            y = o_fwd
            new_gla.append(jnp.stack([finals[0][0], finals[1][0]], axis=1))
            new_ret.append(jnp.stack([finals[0][1], finals[1][1]], axis=1))
            x, h = _outproj(rows, y, ab_w_out[e].astype(BF16), x, norm2_g[l], mod3[l], tm)
        else:
            i = l // 2
            w = mla_w_in[i]
            w_kpe = w[:, q_rank + kv_rank:]
            w_in = w[:, :q_rank + kv_rank].astype(BF16)
            w_kpe_p = _rope_partner(w_kpe)
            w_side = jnp.concatenate([w_kpe, w_kpe, w_kpe_p, w_kpe_p], axis=1).astype(BF16)
            col = {"cq": (0, q_rank), "ckv": (q_rank, kv_rank)}
            zc, zk = _inproj(h, w_in, w_side, F32, tm_proj)
            cos_t, sin_t = _rope_tables(rows)
            wq = mla_w_uq[i].reshape(q_rank, HC, NOPE_DIM + ROPE_DIM)
            wq_rope = wq[:, :, NOPE_DIM:]
            w_uq = jnp.concatenate([wq[:, :, :NOPE_DIM].reshape(q_rank, -1), wq_rope.reshape(q_rank, -1),
                                    _rope_partner(wq_rope).reshape(q_rank, -1)], axis=1).astype(BF16)
            scale = (NOPE_DIM + ROPE_DIM) ** -0.5 * LOG2_E
            assert col["cq"][0] == 0
            qcat = _mla_q(zc, mla_q_norm_g[i], w_uq, cos_t, sin_t, HC, scale, tm)
            wkv = mla_w_ukv[i].reshape(kv_rank, HC, NOPE_DIM + V_DIM)
            w_k = wkv[:, :, :NOPE_DIM].reshape(kv_rank, -1).astype(BF16)
            w_vt = wkv[:, :, NOPE_DIM:].reshape(kv_rank, -1).T.astype(BF16)
            assert S % past == 0 and T % past == 0
            keys = _KeyRows(rows, past)
            ckpe = cache_kpe[:, i].reshape(NB * past, ROPE_DIM)
            kcat, vt, ckvn = _mla_kv(keys, zc, zk, col, cache_ckv[:, i].reshape(NB * past, kv_rank),
                                     jnp.concatenate([ckpe, ckpe], axis=1), mla_kv_norm_g[i], w_k, w_vt,
                                     cos_t, sin_t, HC)
            tk_lat = past + T
            assert (keys.lat_blocks * past) % S == 0
            tq = min(2048, T)
            o = _attention(qcat, kcat, vt, None, B, HC, HC, S, 1, 0, S, keys.lat_blocks * past // S)
            o = _attention(qcat, kcat, vt, o, NB, HC, 1, tq, T // tq, rows.mc // tq, tk_lat, 0)
            new_ckv.append(ckvn[keys.lat_blocks * past:].reshape(B, S, kv_rank))
            new_kpe.append(zk[:rows.mc, :ROPE_DIM].reshape(B, S, ROPE_DIM))
            x, h = _outproj(rows, o, mla_w_out[i].astype(BF16), x, norm2_g[l], mod3[l], tm)
        tf = _largest_tile(F, 512)
        ffn_w = (ffn_w_in[l].astype(BF16), ffn_conv[l], ffn_w_out[l].astype(BF16))
        if l == depth - 1:
            y_ctx, y_lat = _ffn(rows, h, x, mod3[l], *ffn_w, tm, tf, final_g=final_norm_g)
        else:
            x, h = _ffn(rows, h, x, mod3[l], *ffn_w, tm, tf, next_norm=(norm1_g[l + 1], mod3[l + 1]))

    y_prompt = y_ctx.reshape(B, S, D)
    y_sample = y_lat.reshape(NB, T, D)
    return (y_prompt, y_sample, jnp.stack(new_gla, axis=1), jnp.stack(new_ret, axis=1),
            jnp.stack(new_ckv, axis=1), jnp.stack(new_kpe, axis=1))
```

```python
import functools

import jax
import jax.numpy as jnp
from jax import lax
from jax.experimental import pallas as pl
from jax.experimental.pallas import tpu as pltpu

F32 = jnp.float32
BF16 = jnp.bfloat16

NORM_EPS = 1e-6
GATE_TEMP = 16.0
CHUNK = 64
GRID_W = 64
ROPE_BASE = 10000.0
ROPE_DIM = 64
NOPE_DIM = 128
V_DIM = 128
CONV_W = 3
LOG2_E = 1.4426950408889634

LANE = 128
BF16_SUBLANE = 16
MOD_GROUPS = 8
VMEM_LIMIT = 56 * 1024 * 1024

NT_DIMS = (((1,), (1,)), ((), ()))
TN_DIMS = (((0,), (0,)), ((), ()))


def _dot(a, b):
    return jnp.dot(a, b, preferred_element_type=F32)


def _dot_nt(a, b):
    return lax.dot_general(a, b, NT_DIMS, preferred_element_type=F32)


def _dot_tn(a, b):
    return lax.dot_general(a, b, TN_DIMS, preferred_element_type=F32)


def _params(*sem):
    return pltpu.CompilerParams(dimension_semantics=sem, vmem_limit_bytes=VMEM_LIMIT)


def _rms(x, g):
    ms = jnp.mean(x * x, axis=-1, keepdims=True)
    return (x * lax.rsqrt(ms + NORM_EPS)) * g


def _silu(x):
    return x * jax.nn.sigmoid(x)


def _log_sigmoid(x):
    return jnp.minimum(x, 0.0) - jnp.log1p(jnp.exp(-jnp.abs(x)))


def _largest_tile(n, cap):
    best = None
    for t in range(LANE, min(n, cap) + 1, LANE):
        if n % t == 0:
            best = t
    assert best is not None, (n, cap)
    return best


def _mod_kernel(c_ref, w_ref, b_ref, o_ref):
    s = _silu(c_ref[...]).astype(BF16)
    m = _dot(s, w_ref[0].astype(BF16)) + b_ref[0]
    for g in range(MOD_GROUPS):
        o_ref[0, g, 0] = m[g:g + 1]


def _modulation(cond, mod_w, mod_b):
    L, D, N = mod_w.shape
    n_vec = N // D
    out = pl.pallas_call(
        _mod_kernel,
        grid=(L, n_vec),
        in_specs=[
            pl.BlockSpec((MOD_GROUPS, D), lambda l, k: (0, 0)),
            pl.BlockSpec((1, D, D), lambda l, k: (l, 0, k)),
            pl.BlockSpec((1, 1, D), lambda l, k: (l, 0, k)),
        ],
        out_specs=pl.BlockSpec((1, MOD_GROUPS, 1, 1, D), lambda l, k: (l, 0, k, 0, 0)),
        out_shape=jax.ShapeDtypeStruct((L, MOD_GROUPS, n_vec, 1, D), F32),
        compiler_params=_params("parallel", "parallel"),
        name="modulation",
    )(cond, mod_w, mod_b.reshape(L, 1, N))
    return out.reshape(L, MOD_GROUPS * n_vec, 1, D)


class _Rows:
    def __init__(self, mc, seq, lat_len, n_lat):
        self.mc, self.seq, self.lat_len, self.n_lat = mc, seq, lat_len, n_lat
        self.m = mc + lat_len * n_lat

    def group(self, i, tm):
        r = i * tm
        return jnp.where(r < self.mc, 0, 1 + (r - self.mc) // self.lat_len)

    def mod_spec(self, which, tm, d):
        return pl.BlockSpec((1, 1, d), lambda i, *_: (self.group(i, tm) * 6 + which, 0, 0))

    def split_specs(self, tm, d):
        n_ctx = self.mc // tm
        return [pl.BlockSpec((tm, d), lambda i, *_: (jnp.minimum(i, n_ctx - 1), 0)),
                pl.BlockSpec((tm, d), lambda i, *_: (jnp.maximum(i - n_ctx, 0), 0))]


ROW_CHUNK = 16
ROW_PARTS = 4


def _for_row_chunks(n_rows, body):
    def it(c, carry):
        body(pl.ds(pl.multiple_of(c * ROW_CHUNK, ROW_CHUNK), ROW_CHUNK))
        return carry
    lax.fori_loop(0, n_rows // ROW_CHUNK, it, 0, unroll=4)


def _fold_gain(gs_ref, g_ref, sc_ref):
    gs_ref[...] = g_ref[...] * (1.0 + sc_ref[0])


def _norm_mod_rows(x, gs_ref, sh_ref):
    ms = jnp.mean(x * x, axis=-1, keepdims=True)
    return (x * lax.rsqrt(ms + NORM_EPS)) * gs_ref[...] + sh_ref[0]


def _normmod_kernel(rows, tm, xa_ref, xb_ref, g_ref, sh_ref, sc_ref, h_ref, gs_ref):
    _fold_gain(gs_ref, g_ref, sc_ref)

    def run(x_ref):
        def body(r):
            h_ref[r, :] = _norm_mod_rows(x_ref[r, :], gs_ref, sh_ref).astype(h_ref.dtype)
        _for_row_chunks(tm, body)

    is_ctx = pl.program_id(0) * tm < rows.mc
    pl.when(is_ctx)(lambda: run(xa_ref))
    pl.when(jnp.logical_not(is_ctx))(lambda: run(xb_ref))


def _normmod(rows, x_pair, g, mod3, tm):
    D = x_pair[0].shape[1]
    return pl.pallas_call(
        functools.partial(_normmod_kernel, rows, tm),
        grid=(rows.m // tm,),
        in_specs=rows.split_specs(tm, D) + [
            pl.BlockSpec((1, D), lambda i: (0, 0)),
            rows.mod_spec(0, tm, D),
            rows.mod_spec(1, tm, D),
        ],
        out_specs=pl.BlockSpec((tm, D), lambda i: (i, 0)),
        out_shape=jax.ShapeDtypeStruct((rows.m, D), BF16),
        scratch_shapes=[pltpu.VMEM((1, D), F32)],
        compiler_params=_params("parallel"),
        name="normmod",
    )(*x_pair, g.reshape(1, D), mod3, mod3)


def _inproj_kernel(h_ref, w_ref, ws_ref, o_ref, os_ref):
    @pl.when(pl.program_id(1) == 0)
    def _():
        os_ref[...] = _dot(h_ref[...], ws_ref[...])

    o_ref[...] = _dot(h_ref[...], w_ref[...]).astype(o_ref.dtype)


def _inproj(h, w, w_side, out_dtype, tm, tn_cap=1024):
    M, D = h.shape
    N = w.shape[1]
    NS = w_side.shape[1]
    tn = _largest_tile(N, tn_cap)
    return pl.pallas_call(
        _inproj_kernel,
        grid=(M // tm, N // tn),
        in_specs=[
            pl.BlockSpec((tm, D), lambda i, j: (i, 0)),
            pl.BlockSpec((D, tn), lambda i, j: (0, j)),
            pl.BlockSpec((D, NS), lambda i, j: (0, 0)),
        ],
        out_specs=[pl.BlockSpec((tm, tn), lambda i, j: (i, j)), pl.BlockSpec((tm, NS), lambda i, j: (i, 0))],
        out_shape=[jax.ShapeDtypeStruct((M, N), out_dtype), jax.ShapeDtypeStruct((M, NS), F32)],
        compiler_params=_params("parallel", "arbitrary"),
        name="inproj",
    )(h, w, w_side)


def _outproj_kernel(rows, tm, split, y_ref, w_ref, *rest):
    if split:
        xa_ref, xb_ref, gate_ref, g_ref, sh_ref, sc_ref, o_ref, h_ref, acc_ref, gs_ref = rest
    else:
        xa_ref, gate_ref, g_ref, sh_ref, sc_ref, o_ref, h_ref, acc_ref, gs_ref = rest
        xb_ref = xa_ref
    _fold_gain(gs_ref, g_ref, sc_ref)
    is_ctx = pl.program_id(0) * tm < rows.mc
    part = tm // ROW_PARTS

    def finish(r):
        x = jnp.where(is_ctx, xa_ref[r, :], xb_ref[r, :]) if split else xa_ref[r, :]
        x1 = x + gate_ref[0] * acc_ref[r, :]
        o_ref[r, :] = x1
        h_ref[r, :] = _norm_mod_rows(x1, gs_ref, sh_ref).astype(h_ref.dtype)

    for p in range(ROW_PARTS + 1):
        if p < ROW_PARTS:
            rp = slice(p * part, (p + 1) * part)
            acc_ref[rp, :] = _dot(y_ref[rp, :], w_ref[...])
        if p > 0:
            for c in range(part // ROW_CHUNK):
                start = (p - 1) * part + c * ROW_CHUNK
                finish(slice(start, start + ROW_CHUNK))


def _outproj(rows, y, w, x, g, mod3, tm):
    M, K = y.shape
    N = w.shape[1]
    split = isinstance(x, (tuple, list))
    x_specs = rows.split_specs(tm, N) if split else [pl.BlockSpec((tm, N), lambda i: (i, 0))]
    x_args = list(x) if split else [x]
    return pl.pallas_call(
        functools.partial(_outproj_kernel, rows, tm, split),
        grid=(M // tm,),
        in_specs=[
            pl.BlockSpec((tm, K), lambda i: (i, 0)),
            pl.BlockSpec((K, N), lambda i: (0, 0)),
        ] + x_specs + [
            rows.mod_spec(2, tm, N),
            pl.BlockSpec((1, N), lambda i: (0, 0)),
            rows.mod_spec(3, tm, N),
            rows.mod_spec(4, tm, N),
        ],
        out_specs=[pl.BlockSpec((tm, N), lambda i: (i, 0)), pl.BlockSpec((tm, N), lambda i: (i, 0))],
        out_shape=[jax.ShapeDtypeStruct((M, N), F32), jax.ShapeDtypeStruct((M, N), BF16)],
        scratch_shapes=[pltpu.VMEM((tm, N), F32), pltpu.VMEM((1, N), F32)],
        compiler_params=_params("parallel"),
        name="outproj",
    )(y, w, *x_args, mod3, g.reshape(1, N), mod3, mod3)


HALO = BF16_SUBLANE


def _ffn_kernel(rows, tm, final_norm, hp_ref, h_ref, hn_ref, x_ref, gate_ref,
                wa_ref, wb_ref, cw_ref, wo_ref, *rest):
    if final_norm:
        fg_ref, o_ctx_ref, o_lat_ref, hs_ref, a_ref, acc_ref = rest
    else:
        g_ref, sh_ref, sc_ref, o_ref, hnext_ref, hs_ref, a_ref, acc_ref, gs_ref = rest
    i = pl.program_id(0)
    f = pl.program_id(1)

    @pl.when(f == 0)
    def _():
        hs_ref[0:HALO, :] = hp_ref[...]
        hs_ref[HALO:HALO + tm, :] = h_ref[...]
        hs_ref[HALO + tm:, :] = hn_ref[...]
        acc_ref[...] = jnp.zeros_like(acc_ref)

    def hidden_tile():
        a_ref[...] = _dot(hs_ref[...], wa_ref[0])
        b = _dot(h_ref[...], wb_ref[0])
        row = i * tm + lax.broadcasted_iota(jnp.int32, (tm, 1), 0)
        pos = jnp.where(row < rows.mc, row & (rows.seq - 1), (row - rows.mc) & (rows.lat_len - 1))
        seq_len = jnp.where(row < rows.mc, rows.seq, rows.lat_len)
        a_prev = jnp.where(pos == 0, 0.0, a_ref[pl.ds(HALO - 1, tm), :])
        a_next = jnp.where(pos == seq_len - 1, 0.0, a_ref[pl.ds(HALO + 1, tm), :])
        a_mid = a_ref[pl.ds(HALO, tm), :]
        cw = cw_ref[0]
        a = cw[0:1] * a_prev + cw[1:2] * a_mid + cw[2:3] * a_next
        return (_silu(a) * b).astype(BF16)

    def residual(r):
        return x_ref[r, :] + gate_ref[0] * acc_ref[r, :]

    def last_step(finish):
        act = hidden_tile()
        part = tm // ROW_PARTS
        for p in range(ROW_PARTS + 1):
            if p < ROW_PARTS:
                rp = slice(p * part, (p + 1) * part)
                acc_ref[rp, :] += _dot(act[rp, :], wo_ref[0])
            if p > 0:
                for c in range(part // ROW_CHUNK):
                    start = (p - 1) * part + c * ROW_CHUNK
                    finish(slice(start, start + ROW_CHUNK))

    last = pl.num_programs(1) - 1

    @pl.when(f < last)
    def _():
        acc_ref[...] += _dot(hidden_tile(), wo_ref[0])

    if final_norm:
        def finish_to(out_ref):
            def finish(r):
                out_ref[r, :] = _rms(residual(r), fg_ref[...])
            return finish

        is_ctx = i * tm < rows.mc
        pl.when((f == last) & is_ctx)(lambda: last_step(finish_to(o_ctx_ref)))
        pl.when((f == last) & jnp.logical_not(is_ctx))(lambda: last_step(finish_to(o_lat_ref)))
    else:
        def finish(r):
            x2 = residual(r)
            o_ref[r, :] = x2
            hnext_ref[r, :] = _norm_mod_rows(x2, gs_ref, sh_ref).astype(hnext_ref.dtype)

        @pl.when(f == last)
        def _():
            _fold_gain(gs_ref, g_ref, sc_ref)
            last_step(finish)


def _ffn(rows, h, x, mod3, layer, w_in, conv_w, w_out, tm, tf, next_norm=None, final_g=None):
    M, D = x.shape
    F = w_out.shape[1]
    nf = F // tf
    nhalo = M // HALO
    final_norm = final_g is not None
    kern = functools.partial(_ffn_kernel, rows, tm, final_norm)
    in_specs = [
        pl.BlockSpec((HALO, D), lambda i, f: (jnp.maximum(i * (tm // HALO) - 1, 0), 0)),
        pl.BlockSpec((tm, D), lambda i, f: (i, 0)),
        pl.BlockSpec((HALO, D), lambda i, f: (jnp.minimum((i + 1) * (tm // HALO), nhalo - 1), 0)),
        pl.BlockSpec((tm, D), lambda i, f: (i, 0)),
        rows.mod_spec(5, tm, D),
        pl.BlockSpec((1, D, tf), lambda i, f: (layer, 0, f)),
        pl.BlockSpec((1, D, tf), lambda i, f: (layer, 0, nf + f)),
        pl.BlockSpec((1, CONV_W, tf), lambda i, f: (layer, 0, f)),
        pl.BlockSpec((1, tf, D), lambda i, f: (layer, f, 0)),
    ]
    args = [h, h, h, x, mod3, w_in, w_in, conv_w, w_out]
    vec = pl.BlockSpec((1, D), lambda i, f: (0, 0))
    if final_norm:
        n_ctx = rows.mc // tm
        in_specs += [vec]
        args += [final_g.reshape(1, D)]
        out_specs = [pl.BlockSpec((tm, D), lambda i, f: (jnp.minimum(i, n_ctx - 1), 0)),
                     pl.BlockSpec((tm, D), lambda i, f: (jnp.maximum(i - n_ctx, 0), 0))]
        out_shape = [jax.ShapeDtypeStruct((rows.mc, D), F32), jax.ShapeDtypeStruct((M - rows.mc, D), F32)]
        row_sem = "arbitrary"
    else:
        g_next, mod3_next = next_norm
        in_specs += [vec, rows.mod_spec(0, tm, D), rows.mod_spec(1, tm, D)]
        args += [g_next.reshape(1, D), mod3_next, mod3_next]
        out_specs = [pl.BlockSpec((tm, D), lambda i, f: (i, 0)), pl.BlockSpec((tm, D), lambda i, f: (i, 0))]
        out_shape = [jax.ShapeDtypeStruct((M, D), F32), jax.ShapeDtypeStruct((M, D), BF16)]
        row_sem = "parallel"
    return pl.pallas_call(
        kern,
        grid=(M // tm, nf),
        in_specs=in_specs,
        out_specs=out_specs,
        out_shape=out_shape,
        scratch_shapes=[
            pltpu.VMEM((tm + 2 * HALO, D), BF16),
            pltpu.VMEM((tm + 2 * HALO, tf), F32),
            pltpu.VMEM((tm, D), F32),
        ] + ([] if final_norm else [pltpu.VMEM((1, D), F32)]),
        compiler_params=_params(row_sem, "arbitrary"),
        name="convffn",
    )(*args)


class _ScanCfg:
    def __init__(self, rows, batch, tb, h, dk, dv, backward):
        self.rows, self.batch, self.tb, self.h, self.dk, self.dv = rows, batch, tb, h, dk, dv
        self.backward = backward
        self.cps = rows.seq // tb
        self.lps = rows.lat_len // tb
        self.ctx_blocks = batch * self.cps
        self.nblk = rows.m // tb

    def block(self, i):
        return self.nblk - 1 - i if self.backward else i

    def is_ctx(self, r):
        return r < self.ctx_blocks

    def seq_pos(self, r):
        ctx = self.is_ctx(r)
        return (jnp.where(ctx, r % self.cps, (r - self.ctx_blocks) % self.lps),
                jnp.where(ctx, self.cps, self.lps))

    def lat_seq(self, r):
        return jnp.clip((r - self.ctx_blocks) // self.lps, 0, self.rows.n_lat - 1)

    def ctx_seq(self, r):
        return jnp.clip(r // self.cps, 0, self.batch - 1)


def _scan_kernel(cfg, gq_ref, gk_ref, gv_ref, rq_ref, rk_ref, rv_ref, glr_ref, gw_ref, gb_ref, rd_ref,
                 s0g_ref, s0r_ref, *rest):
    scratch = rest[-8:]
    sg_ref, sr_ref, dend_ref, qd_ref, kinv_ref, kend_ref, rks_ref, rkd_ref = scratch
    if cfg.backward:
        of_ref, gg_ref, rg_ref, gng_ref, rng_ref, y_ref, sgo_ref, sro_ref = rest[:-8]
    else:
        o_ref, sgo_ref, sro_ref = rest[:-8]
    H, DK, DV, C = cfg.h, cfg.dk, cfg.dv, CHUNK
    r = cfg.block(pl.program_id(0))
    blk, nblk_seq = cfg.seq_pos(r)
    first = blk == (nblk_seq - 1 if cfg.backward else 0)
    last = blk == (0 if cfg.backward else nblk_seq - 1)
    is_ctx = cfg.is_ctx(r)

    @pl.when(first & is_ctx)
    def _():
        sg_ref[...] = jnp.zeros_like(sg_ref)
        sr_ref[...] = jnp.zeros_like(sr_ref)

    @pl.when(first & jnp.logical_not(is_ctx))
    def _():
        sg_ref[...] = s0g_ref[0]
        sr_ref[...] = s0r_ref[0]

    ti = lax.broadcasted_iota(jnp.int32, (C, C), 0)
    tj = lax.broadcasted_iota(jnp.int32, (C, C), 1)
    sees = (tj >= ti) if cfg.backward else (tj <= ti)
    dist = jnp.abs(ti - tj).astype(F32)
    rowi = lax.broadcasted_iota(jnp.int32, (C, LANE), 0)
    to_end = (rowi if cfg.backward else C - 1 - rowi).astype(F32)
    from_start = (C - rowi if cfg.backward else rowi + 1).astype(F32)

    n_chunks = cfg.tb // C
    HDK = H * DK
    scale = DK ** -0.5

    bi = lax.broadcasted_iota(jnp.int32, (cfg.tb, cfg.tb), 0)
    bj = lax.broadcasted_iota(jnp.int32, (cfg.tb, cfg.tb), 1)
    same_chunk = (bi & -C) == (bj & -C)
    blk_tri = jnp.where(same_chunk & ((bj >= bi) if cfg.backward else (bj <= bi)), 1.0, 0.0).astype(BF16)
    pre = _dot(glr_ref[...].astype(BF16), gw_ref[...]) + gb_ref[...]
    la = _log_sigmoid(pre) / GATE_TEMP
    la_hi = la.astype(BF16)
    la_lo = (la - la_hi.astype(F32)).astype(BF16)
    b = _dot(blk_tri, la_hi) + _dot(blk_tri, la_lo)
    b3 = b.reshape(n_chunks, C, HDK)
    b_end = b3[:, 0:1, :] if cfg.backward else b3[:, C - 1:C, :]
    eb = jnp.exp(b)
    chunk_of_row = jnp.right_shift(lax.broadcasted_iota(jnp.int32, (cfg.tb, n_chunks * LANE), 0), C.bit_length() - 1)
    chunk_of_col = jnp.right_shift(lax.broadcasted_iota(jnp.int32, (cfg.tb, n_chunks * LANE), 1),
                                   LANE.bit_length() - 1)
    in_chunk = jnp.where(chunk_of_row == chunk_of_col, 1.0, 0.0).astype(BF16)
    dend_ref[...] = jnp.exp(_dot_tn(la_hi, in_chunk) + _dot_tn(la_lo, in_chunk))
    gk = gk_ref[...].astype(F32)
    qd_ref[...] = (gq_ref[...].astype(F32) * scale * eb).astype(BF16)
    kinv_ref[...] = (gk * jnp.exp(-b)).astype(BF16)
    kend_ref[...] = (gk * jnp.exp(b_end - b3).reshape(cfg.tb, HDK)).astype(BF16)

    decays, q_decs, c_decs, k_dec_cols = [], [], [], []
    for h in range(H):
        lg = _log_sigmoid(rd_ref[h])
        lg_c = jnp.broadcast_to(lg[0:1, 0:C], (C, C))
        lg_l = jnp.broadcast_to(lg[0:1, :], (C, LANE))
        decays.append(jnp.where(sees, jnp.exp(lg_c * dist), 0.0))
        q_decs.append(jnp.concatenate([jnp.exp(lg_l * from_start)] * (DV // LANE), axis=1))
        c_decs.append(jnp.concatenate([jnp.exp(lg[0:1, :] * float(C))] * (DV // LANE), axis=1))
        k_dec_cols.append(jnp.concatenate([jnp.exp(lg_l * to_end)] * n_chunks, axis=0))
    rk = rk_ref[...].astype(F32) * scale
    rks_ref[...] = rk.astype(BF16)
    rkd_ref[...] = (rk * jnp.concatenate(k_dec_cols, axis=1)).astype(BF16)

    for ci in range(n_chunks):
        c = n_chunks - 1 - ci if cfg.backward else ci
        rs = slice(c * C, (c + 1) * C)
        heads = range(H)
        ksl = [slice(h * DK, (h + 1) * DK) for h in heads]
        vsl = [slice(h * DV, (h + 1) * DV) for h in heads]
        att_raw = [_dot_nt(qd_ref[rs, ksl[h]], kinv_ref[rs, ksl[h]]) for h in heads]
        ratt_raw = [_dot_nt(rq_ref[rs, ksl[h]], rks_ref[rs, ksl[h]]) for h in heads]
        qs_g = [_dot(qd_ref[rs, ksl[h]], sg_ref[h].astype(BF16)) for h in heads]
        qs_r = [_dot(rq_ref[rs, ksl[h]], sr_ref[h].astype(BF16)) for h in heads]
        for h in heads:
            att = jnp.where(sees, att_raw[h], 0.0).astype(BF16)
            o_g = _dot(att, gv_ref[rs, vsl[h]]) + qs_g[h]
            ratt = (ratt_raw[h] * decays[h]).astype(BF16)
            o_r = _dot(ratt, rv_ref[rs, vsl[h]]) + qs_r[h] * q_decs[h]
            rs_cols = slice(H * DV + h * DV, H * DV + (h + 1) * DV)
            if cfg.backward:
                for o, cols, gate_ref, ng_ref in ((o_g, vsl[h], gg_ref, gng_ref), (o_r, rs_cols, rg_ref, rng_ref)):
                    tot = o + of_ref[rs, cols]
                    mu = jnp.mean(tot, axis=-1, keepdims=True)
                    d = tot - mu
                    var = jnp.mean(d * d, axis=-1, keepdims=True)
                    yn = d * lax.rsqrt(var + NORM_EPS) * ng_ref[:, vsl[h]]
                    y_ref[rs, cols] = (_silu(gate_ref[rs, vsl[h]].astype(F32)) * yn).astype(y_ref.dtype)
            else:
                o_ref[rs, vsl[h]] = o_g
                o_ref[rs, rs_cols] = o_r
        for h in heads:
            dend = dend_ref[ksl[h], c * LANE:(c + 1) * LANE]
            sg_ref[h] = (sg_ref[h] * jnp.concatenate([dend] * (DV // LANE), axis=1)
                         + _dot_tn(kend_ref[rs, ksl[h]], gv_ref[rs, vsl[h]]))
            sr_ref[h] = sr_ref[h] * c_decs[h] + _dot_tn(rkd_ref[rs, ksl[h]], rv_ref[rs, vsl[h]])

    @pl.when(last & is_ctx)
    def _():
        sgo_ref[0] = sg_ref[...]
        sro_ref[0] = sr_ref[...]


def _scan(cfg, z, glr, col, gate_w, gate_b, ret_decay_t, s0g, s0r, extra):
    H, DK, DV, tb = cfg.h, cfg.dk, cfg.dv, cfg.tb
    M = z.shape[0]

    def zspec(name):
        off, width = col[name]
        assert off % width == 0, (name, off, width)
        return pl.BlockSpec((tb, width), lambda i: (cfg.block(i), off // width))

    def full(shape):
        return pl.BlockSpec(shape, lambda i: (0,) * len(shape))

    state_in = pl.BlockSpec((1, H, DK, DV), lambda i: (cfg.lat_seq(cfg.block(i)), 0, 0, 0))
    state_out = pl.BlockSpec((1, H, DK, DV), lambda i: (cfg.ctx_seq(cfg.block(i)), 0, 0, 0))
    row_blk = pl.BlockSpec((tb, 2 * H * DV), lambda i: (cfg.block(i), 0))
    glr_spec = pl.BlockSpec((tb, glr.shape[1]), lambda i: (cfg.block(i), 0))
    in_specs = [zspec("gq"), zspec("gk"), zspec("gv"), zspec("rq"), zspec("rk"), zspec("rv"), glr_spec,
                full(gate_w.shape), full(gate_b.shape), full(ret_decay_t.shape), state_in, state_in]
    args = [z, z, z, z, z, z, glr, gate_w, gate_b, ret_decay_t, s0g, s0r]
    state_shape = jax.ShapeDtypeStruct((cfg.batch, H, DK, DV), F32)
    if cfg.backward:
        o_fwd, gla_ng, ret_ng = extra
        in_specs += [row_blk, zspec("gg"), zspec("rg"), full(gla_ng.shape), full(ret_ng.shape)]
        args += [o_fwd, z, z, gla_ng, ret_ng]
        out0 = jax.ShapeDtypeStruct((M, 2 * H * DV), BF16)
    else:
        out0 = jax.ShapeDtypeStruct((M, 2 * H * DV), F32)
    return pl.pallas_call(
        functools.partial(_scan_kernel, cfg),
        grid=(cfg.nblk,),
        in_specs=in_specs,
        out_specs=[row_blk, state_out, state_out],
        out_shape=[out0, state_shape, state_shape],
        scratch_shapes=[pltpu.VMEM((H, DK, DV), F32), pltpu.VMEM((H, DK, DV), F32),
                        pltpu.VMEM((H * DK, (tb // CHUNK) * LANE), F32)] + [pltpu.VMEM((tb, H * DK), BF16)] * 5,
        compiler_params=_params("arbitrary"),
        name="scan_bwd" if cfg.backward else "scan_fwd",
    )(*args)


def _rope_keep(h, shape):
    lane = lax.broadcasted_iota(jnp.int32, shape, 1)
    return (lane < ROPE_DIM) if h % 2 == 0 else (lane >= ROPE_DIM)


def _mla_q_kernel(n_heads, scale, cq_ref, qng_ref, w_ref, cos_ref, sin_ref, o_ref):
    cqn = _rms(cq_ref[...], qng_ref[...]).astype(BF16)
    q = _dot(cqn, w_ref[...])
    nope_w = n_heads * NOPE_DIM
    rope_w = n_heads * ROPE_DIM
    cos, sin = cos_ref[...], sin_ref[...]
    for h in range(n_heads):
        p = h // 2
        tile = q[:, nope_w + p * LANE:nope_w + (p + 1) * LANE]
        partner = q[:, nope_w + rope_w + p * LANE:nope_w + rope_w + (p + 1) * LANE]
        rot = (tile * cos + partner * sin) * scale
        rot = jnp.where(_rope_keep(h, rot.shape), rot, 0.0)
        base = h * 2 * LANE
        o_ref[:, base:base + LANE] = (q[:, h * NOPE_DIM:(h + 1) * NOPE_DIM] * scale).astype(BF16)
        o_ref[:, base + LANE:base + 2 * LANE] = rot.astype(BF16)


def _mla_q(zc, q_ng, w_uq, cos_t, sin_t, n_heads, scale, tm):
    M = zc.shape[0]
    R = q_ng.shape[0]
    N = w_uq.shape[1]
    return pl.pallas_call(
        functools.partial(_mla_q_kernel, n_heads, scale),
        grid=(M // tm,),
        in_specs=[
            pl.BlockSpec((tm, R), lambda i: (i, 0)),
            pl.BlockSpec((1, R), lambda i: (0, 0)),
            pl.BlockSpec((R, N), lambda i: (0, 0)),
            pl.BlockSpec((tm, LANE), lambda i: (i, 0)),
            pl.BlockSpec((tm, LANE), lambda i: (i, 0)),
        ],
        out_specs=pl.BlockSpec((tm, n_heads * 2 * LANE), lambda i: (i, 0)),
        out_shape=jax.ShapeDtypeStruct((M, n_heads * 2 * LANE), BF16),
        compiler_params=_params("parallel"),
        name="mla_q",
    )(zc, q_ng.reshape(1, R), w_uq, cos_t, sin_t)


class _KeyRows:
    def __init__(self, rows, blk):
        self.rows, self.blk = rows, blk
        self.per_lat = 1 + rows.lat_len // blk
        self.lat_blocks = rows.n_lat * self.per_lat
        self.ctx_blocks = rows.mc // blk
        self.n = self.lat_blocks + self.ctx_blocks

    def is_cache(self, j):
        return (j < self.lat_blocks) & (j % self.per_lat == 0)

    def cache_idx(self, j):
        return jnp.clip(j // self.per_lat, 0, self.rows.n_lat - 1)

    def token_block(self, j):
        b = j // self.per_lat
        t = jnp.maximum(j % self.per_lat - 1, 0)
        lat = self.ctx_blocks + b * (self.per_lat - 1) + t
        return jnp.where(j < self.lat_blocks, lat, j - self.lat_blocks)


def _mla_kv_kernel(keys, n_heads, ckv_ref, kpe_ref, cckv_ref, ckpe_ref, kvng_ref, wk_ref, wvt_ref, cos_ref, sin_ref,
                   kcat_ref, vt_ref, ckvn_ref):
    cached = keys.is_cache(pl.program_id(0))
    kpe = kpe_ref[...]
    rot = kpe[:, :LANE] * cos_ref[...] + kpe[:, LANE:] * sin_ref[...]
    ckvn = jnp.where(cached, cckv_ref[...], _rms(ckv_ref[...], kvng_ref[...]))
    kr = jnp.where(cached, ckpe_ref[...], rot)
    ckvn_ref[...] = ckvn
    ckvn_b = ckvn.astype(BF16)
    kn = _dot(ckvn_b, wk_ref[...])
    vt_ref[...] = _dot_nt(wvt_ref[...], ckvn_b).astype(BF16)
    for h in range(n_heads):
        base = h * 2 * LANE
        kcat_ref[:, base:base + LANE] = kn[:, h * NOPE_DIM:(h + 1) * NOPE_DIM].astype(BF16)
        kcat_ref[:, base + LANE:base + 2 * LANE] = jnp.where(_rope_keep(h, kr.shape), kr, 0.0).astype(BF16)


def _mla_kv(keys, zc, zk, col, cache_ckv, cache_kpe2, kv_ng, w_k, w_vt, cos_t, sin_t, n_heads):
    blk = keys.blk
    R = kv_ng.shape[0]
    Mk = keys.n * blk
    ckv_off, ckv_w = col["ckv"]
    assert ckv_off % ckv_w == 0
    tok = keys.token_block
    return pl.pallas_call(
        functools.partial(_mla_kv_kernel, keys, n_heads),
        grid=(keys.n,),
        in_specs=[
            pl.BlockSpec((blk, ckv_w), lambda j: (tok(j), ckv_off // ckv_w)),
            pl.BlockSpec((blk, 2 * LANE), lambda j: (tok(j), 0)),
            pl.BlockSpec((blk, R), lambda j: (keys.cache_idx(j), 0)),
            pl.BlockSpec((blk, LANE), lambda j: (keys.cache_idx(j), 0)),
            pl.BlockSpec((1, R), lambda j: (0, 0)),
            pl.BlockSpec(w_k.shape, lambda j: (0, 0)),
            pl.BlockSpec(w_vt.shape, lambda j: (0, 0)),
            pl.BlockSpec((blk, LANE), lambda j: (tok(j), 0)),
            pl.BlockSpec((blk, LANE), lambda j: (tok(j), 0)),
        ],
        out_specs=[
            pl.BlockSpec((blk, n_heads * 2 * LANE), lambda j: (j, 0)),
            pl.BlockSpec((n_heads * V_DIM, blk), lambda j: (0, j)),
            pl.BlockSpec((blk, R), lambda j: (j, 0)),
        ],
        out_shape=[
            jax.ShapeDtypeStruct((Mk, n_heads * 2 * LANE), BF16),
            jax.ShapeDtypeStruct((n_heads * V_DIM, Mk), BF16),
            jax.ShapeDtypeStruct((Mk, R), F32),
        ],
        compiler_params=_params("parallel"),
        name="mla_kv",
    )(zc, zk, cache_ckv, cache_kpe2, kv_ng.reshape(1, R), w_k, w_vt, cos_t, sin_t)


ATTN_TQ = 256
ATTN_TK = 512


def _attn_kernel(tq, chunks, q_ref, k_ref, vt_ref, o_ref, st0_ref, st1_ref):
    n_heads = q_ref.shape[1] // (2 * LANE)
    units = [(h, u) for h in range(n_heads) for u in range(q_ref.shape[0] // tq)]
    st_refs = (st0_ref, st1_ref)

    def scores(n, c0, c1):
        h, u = units[n]
        hd = slice(h * 2 * LANE, (h + 1) * 2 * LANE)
        st = _dot_nt(k_ref[c0:c1, hd], q_ref[u * tq:(u + 1) * tq, hd])
        st_refs[n % 2][c0:c1, :] = st
        return jnp.max(st, axis=0, keepdims=True)

    def weights(n, c0, c1, m):
        h, _ = units[n]
        p = jnp.exp2(st_refs[n % 2][c0:c1, :] - m)
        return jnp.sum(p, axis=0, keepdims=True), _dot(vt_ref[h * V_DIM:(h + 1) * V_DIM, c0:c1], p.astype(BF16))

    m_prev = None
    for n in range(len(units) + 1):
        m_new = acc = l = None
        for c0, c1 in chunks:
            if n < len(units):
                mc = scores(n, c0, c1)
                m_new = mc if m_new is None else jnp.maximum(m_new, mc)
            if n > 0:
                lc, pv = weights(n - 1, c0, c1, m_prev)
                l = lc if l is None else l + lc
                acc = pv if acc is None else acc + pv
        if n > 0:
            h, u = units[n - 1]
            o_ref[u * tq:(u + 1) * tq, h * V_DIM:(h + 1) * V_DIM] = (acc / l).T.astype(o_ref.dtype)
        m_prev = m_new


def _attn_kernel_aliased(tq, chunks, q_ref, k_ref, vt_ref, o_prev_ref, o_ref, st0_ref, st1_ref):
    del o_prev_ref
    _attn_kernel(tq, chunks, q_ref, k_ref, vt_ref, o_ref, st0_ref, st1_ref)


def _attention(qcat, kcat, vt, o_prev, n_seq, n_heads, hps, tq, q_blocks_per_seq, q_block0, tk, k_block0):
    sub_q = ATTN_TQ if tq % ATTN_TQ == 0 else LANE
    assert tq % sub_q == 0 and n_heads % hps == 0
    q_spec = pl.BlockSpec((tq, hps * 2 * LANE), lambda s, h, i: (q_block0 + s * q_blocks_per_seq + i, h))
    in_specs = [
        q_spec,
        pl.BlockSpec((tk, hps * 2 * LANE), lambda s, h, i: (k_block0 + s, h)),
        pl.BlockSpec((hps * V_DIM, tk), lambda s, h, i: (h, k_block0 + s)),
    ]
    args = [qcat, kcat, vt]
    chunks = tuple((c0, min(c0 + ATTN_TK, tk)) for c0 in range(0, tk, ATTN_TK))
    kern = functools.partial(_attn_kernel, sub_q, chunks)
    aliases = {}
    if o_prev is not None:
        in_specs.append(pl.BlockSpec(memory_space=pl.ANY))
        args.append(o_prev)
        aliases = {3: 0}
        kern = functools.partial(_attn_kernel_aliased, sub_q, chunks)
    return pl.pallas_call(
        kern,
        grid=(n_seq, n_heads // hps, q_blocks_per_seq),
        in_specs=in_specs,
        out_specs=pl.BlockSpec((tq, hps * V_DIM), lambda s, h, i: (q_block0 + s * q_blocks_per_seq + i, h)),
        out_shape=jax.ShapeDtypeStruct((qcat.shape[0], n_heads * V_DIM), BF16),
        input_output_aliases=aliases,
        scratch_shapes=[pltpu.VMEM((tk, sub_q), F32), pltpu.VMEM((tk, sub_q), F32)],
        compiler_params=_params("parallel", "parallel", "arbitrary"),
        name="mla_attn",
    )(*args)


def _rope_partner(w):
    q = ROPE_DIM // 4
    return jnp.flip(w.reshape(w.shape[:-1] + (2, 2, q)), axis=-2).reshape(w.shape)


def _rope_tables(rows):
    half, quarter = ROPE_DIM // 2, ROPE_DIM // 4
    inv = ROPE_BASE ** (-jnp.arange(quarter, dtype=F32) * 2.0 / half)
    t = jnp.arange(rows.lat_len)
    r = (t // GRID_W).astype(F32)
    c = (t % GRID_W).astype(F32)
    ang_r = r[:, None] * inv[None, :]
    ang_c = c[:, None] * inv[None, :]
    cos = jnp.concatenate([jnp.cos(ang_r), jnp.cos(ang_r), jnp.cos(ang_c), jnp.cos(ang_c)], axis=1)
    sin = jnp.concatenate([-jnp.sin(ang_r), jnp.sin(ang_r), -jnp.sin(ang_c), jnp.sin(ang_c)], axis=1)
    cos = jnp.tile(jnp.concatenate([cos, cos], axis=1), (rows.n_lat, 1))
    sin = jnp.tile(jnp.concatenate([sin, sin], axis=1), (rows.n_lat, 1))
    cos = jnp.concatenate([jnp.ones((rows.mc, LANE), F32), cos], axis=0)
    sin = jnp.concatenate([jnp.zeros((rows.mc, LANE), F32), sin], axis=0)
    return cos, sin


def kernel(x_prompt, x_sample, state_gla, state_ret, cache_ckv, cache_kpe, c, c_ctx, mod_w, mod_b, norm1_g, norm2_g, ab_w_in, gla_gate_w2, gla_gate_b, ret_decay, gla_norm_g, ret_norm_g, ab_w_out, mla_w_in, mla_q_norm_g, mla_w_uq, mla_kv_norm_g, mla_w_ukv, mla_w_out, ffn_w_in, ffn_conv, ffn_w_out, final_norm_g):
    B, S, D = x_prompt.shape
    NB, T, _ = x_sample.shape
    depth = mod_w.shape[0]
    _, _, _, HA, DK, DV = state_gla.shape
    HB = state_ret.shape[3]
    assert HA == HB and state_ret.shape[4:] == (DK, DV)
    GR = gla_gate_w2.shape[2]
    past = cache_ckv.shape[2]
    q_rank = mla_q_norm_g.shape[1]
    kv_rank = mla_kv_norm_g.shape[1]
    HC = mla_w_uq.shape[2] // (NOPE_DIM + ROPE_DIM)
    F = ffn_w_out.shape[1]
    rows = _Rows(B * S, S, T, NB)
    M = rows.m
    tm = min(512, S * B, T)
    assert rows.mc % tm == 0 and T % tm == 0
    assert S & (S - 1) == 0 and T & (T - 1) == 0

    tm_proj = min(1024, S * B, T)
    assert rows.mc % tm_proj == 0 and T % tm_proj == 0
    cond = jnp.concatenate([c_ctx[None, :], c, jnp.zeros((MOD_GROUPS - 1 - NB, D), F32)], axis=0)
    mod3 = _modulation(cond, mod_w, mod_b)

    x = (x_prompt.reshape(B * S, D), x_sample.reshape(NB * T, D))
    h = _normmod(rows, x, norm1_g[0], mod3[0], tm)
    ffn_w = (ffn_w_in.astype(BF16), ffn_conv, ffn_w_out.astype(BF16))
    new_gla, new_ret, new_ckv, new_kpe = [], [], [], []
    for l in range(depth):
        if l % 2 == 0:
            e = l // 2
            names = ("gq", "gk", "gv", "gg", "glr", "rq", "rk", "rv", "rg")
            sizes = (HA * DK, HA * DK, HA * DV, HA * DV, 2 * GR, HB * DK, HB * DK, HB * DV, HB * DV)
            src, o = {}, 0
            for n, sz in zip(names, sizes):
                src[n] = (o, sz)
                o += sz
            order = ("gq", "gk", "gv", "gg", "rq", "rk", "rv", "rg")
            col, o = {}, 0
            for n in order:
                col[n] = (o, src[n][1])
                o += src[n][1]
            so, sz = src["glr"]
            w_in = jnp.concatenate([ab_w_in[e][:, :so], ab_w_in[e][:, so + sz:]], axis=1).astype(BF16)
            w_glr = jnp.concatenate([ab_w_in[e][:, so:so + sz], jnp.zeros((D, LANE - sz), F32)], axis=1).astype(BF16)
            z, glr = _inproj(h, w_in, w_glr, BF16, tm_proj)
            tb = min(256, S)
            o_fwd = None
            finals = []
            for dirn in (0, 1):
                cfg = _ScanCfg(rows, B, tb, HA, DK, DV, backward=bool(dirn))
                gate_w = jnp.zeros((LANE, HA * DK), F32).at[dirn * GR:(dirn + 1) * GR].set(gla_gate_w2[e, dirn])
                rd = jnp.broadcast_to(ret_decay[e, dirn][:, None, None], (HB, 8, LANE))
                extra = (o_fwd, gla_norm_g[e].reshape(1, -1), ret_norm_g[e].reshape(1, -1)) if dirn else None
                o_dir, sg, sr = _scan(cfg, z, glr, col, gate_w.astype(BF16), gla_gate_b[e, dirn].reshape(1, -1), rd,
                                      state_gla[:, e, dirn], state_ret[:, e, dirn], extra)
                o_fwd = o_dir
                finals.append((sg, sr))
            y = o_fwd
            new_gla.append(jnp.stack([finals[0][0], finals[1][0]], axis=1))
            new_ret.append(jnp.stack([finals[0][1], finals[1][1]], axis=1))
            x, h = _outproj(rows, y, ab_w_out[e].astype(BF16), x, norm2_g[l], mod3[l], tm)
        else:
            i = l // 2
            w = mla_w_in[i]
            w_kpe = w[:, q_rank + kv_rank:]
            w_in = w[:, :q_rank + kv_rank].astype(BF16)
            w_kpe_p = _rope_partner(w_kpe)
            w_side = jnp.concatenate([w_kpe, w_kpe, w_kpe_p, w_kpe_p], axis=1).astype(BF16)
            col = {"cq": (0, q_rank), "ckv": (q_rank, kv_rank)}
            zc, zk = _inproj(h, w_in, w_side, F32, tm_proj)
            cos_t, sin_t = _rope_tables(rows)
            wq = mla_w_uq[i].reshape(q_rank, HC, NOPE_DIM + ROPE_DIM)
            wq_rope = wq[:, :, NOPE_DIM:]
            w_uq = jnp.concatenate([wq[:, :, :NOPE_DIM].reshape(q_rank, -1), wq_rope.reshape(q_rank, -1),
                                    _rope_partner(wq_rope).reshape(q_rank, -1)], axis=1).astype(BF16)
            scale = (NOPE_DIM + ROPE_DIM) ** -0.5 * LOG2_E
            assert col["cq"][0] == 0
            qcat = _mla_q(zc, mla_q_norm_g[i], w_uq, cos_t, sin_t, HC, scale, tm)
            wkv = mla_w_ukv[i].reshape(kv_rank, HC, NOPE_DIM + V_DIM)
            w_k = wkv[:, :, :NOPE_DIM].reshape(kv_rank, -1).astype(BF16)
            w_vt = wkv[:, :, NOPE_DIM:].reshape(kv_rank, -1).T.astype(BF16)
            assert S % past == 0 and T % past == 0
            keys = _KeyRows(rows, past)
            ckpe = cache_kpe[:, i].reshape(NB * past, ROPE_DIM)
            kcat, vt, ckvn = _mla_kv(keys, zc, zk, col, cache_ckv[:, i].reshape(NB * past, kv_rank),
                                     jnp.concatenate([ckpe, ckpe], axis=1), mla_kv_norm_g[i], w_k, w_vt,
                                     cos_t, sin_t, HC)
            tk_lat = past + T
            assert (keys.lat_blocks * past) % S == 0
            tq = min(2048, T)
            o = _attention(qcat, kcat, vt, None, B, HC, HC, S, 1, 0, S, keys.lat_blocks * past // S)
            o = _attention(qcat, kcat, vt, o, NB, HC, 1, tq, T // tq, rows.mc // tq, tk_lat, 0)
            new_ckv.append(ckvn[keys.lat_blocks * past:].reshape(B, S, kv_rank))
            new_kpe.append(zk[:rows.mc, :ROPE_DIM].reshape(B, S, ROPE_DIM))
            x, h = _outproj(rows, o, mla_w_out[i].astype(BF16), x, norm2_g[l], mod3[l], tm)
        tf = _largest_tile(F, 512)
        if l == depth - 1:
            y_ctx, y_lat = _ffn(rows, h, x, mod3[l], l, *ffn_w, tm, tf, final_g=final_norm_g)
        else:
            x, h = _ffn(rows, h, x, mod3[l], l, *ffn_w, tm, tf, next_norm=(norm1_g[l + 1], mod3[l + 1]))

    y_prompt = y_ctx.reshape(B, S, D)
    y_sample = y_lat.reshape(NB, T, D)
    return (y_prompt, y_sample, jnp.stack(new_gla, axis=1), jnp.stack(new_ret, axis=1),
            jnp.stack(new_ckv, axis=1), jnp.stack(new_kpe, axis=1))
```

```python
import functools

import jax
import jax.numpy as jnp
from jax import lax
from jax.experimental import pallas as pl
from jax.experimental.pallas import tpu as pltpu

F32 = jnp.float32
BF16 = jnp.bfloat16

NORM_EPS = 1e-6
GATE_TEMP = 16.0
CHUNK = 64
GRID_W = 64
ROPE_BASE = 10000.0
ROPE_DIM = 64
NOPE_DIM = 128
V_DIM = 128
CONV_W = 3
LOG2_E = 1.4426950408889634

LANE = 128
BF16_SUBLANE = 16
MOD_GROUPS = 8
VMEM_LIMIT = 56 * 1024 * 1024

NT_DIMS = (((1,), (1,)), ((), ()))
TN_DIMS = (((0,), (0,)), ((), ()))


def _dot(a, b):
    return jnp.dot(a, b, preferred_element_type=F32)


def _dot_nt(a, b):
    return lax.dot_general(a, b, NT_DIMS, preferred_element_type=F32)


def _dot_tn(a, b):
    return lax.dot_general(a, b, TN_DIMS, preferred_element_type=F32)


def _params(*sem):
    return pltpu.CompilerParams(dimension_semantics=sem, vmem_limit_bytes=VMEM_LIMIT)


def _rms(x, g):
    ms = jnp.mean(x * x, axis=-1, keepdims=True)
    return (x * lax.rsqrt(ms + NORM_EPS)) * g


def _silu(x):
    return x * jax.nn.sigmoid(x)


def _log_sigmoid(x):
    return jnp.minimum(x, 0.0) - jnp.log1p(jnp.exp(-jnp.abs(x)))


def _largest_tile(n, cap):
    best = None
    for t in range(LANE, min(n, cap) + 1, LANE):
        if n % t == 0:
            best = t
    assert best is not None, (n, cap)
    return best


def _mod_kernel(c_ref, w_ref, b_ref, o_ref):
    s = _silu(c_ref[...]).astype(BF16)
    m = _dot(s, w_ref[0].astype(BF16)) + b_ref[0]
    for g in range(MOD_GROUPS):
        o_ref[0, g, 0] = m[g:g + 1]


def _modulation(cond, mod_w, mod_b):
    L, D, N = mod_w.shape
    n_vec = N // D
    out = pl.pallas_call(
        _mod_kernel,
        grid=(L, n_vec),
        in_specs=[
            pl.BlockSpec((MOD_GROUPS, D), lambda l, k: (0, 0)),
            pl.BlockSpec((1, D, D), lambda l, k: (l, 0, k)),
            pl.BlockSpec((1, 1, D), lambda l, k: (l, 0, k)),
        ],
        out_specs=pl.BlockSpec((1, MOD_GROUPS, 1, 1, D), lambda l, k: (l, 0, k, 0, 0)),
        out_shape=jax.ShapeDtypeStruct((L, MOD_GROUPS, n_vec, 1, D), F32),
        compiler_params=_params("parallel", "parallel"),
        name="modulation",
    )(cond, mod_w, mod_b.reshape(L, 1, N))
    return out.reshape(L, MOD_GROUPS * n_vec, 1, D)


class _Rows:
    def __init__(self, mc, seq, lat_len, n_lat):
        self.mc, self.seq, self.lat_len, self.n_lat = mc, seq, lat_len, n_lat
        self.m = mc + lat_len * n_lat

    def group(self, i, tm):
        r = i * tm
        return jnp.where(r < self.mc, 0, 1 + (r - self.mc) // self.lat_len)

    def mod_spec(self, which, tm, d):
        return pl.BlockSpec((1, 1, d), lambda i, *_: (self.group(i, tm) * 6 + which, 0, 0))

    def split_specs(self, tm, d):
        n_ctx = self.mc // tm
        return [pl.BlockSpec((tm, d), lambda i, *_: (jnp.minimum(i, n_ctx - 1), 0)),
                pl.BlockSpec((tm, d), lambda i, *_: (jnp.maximum(i - n_ctx, 0), 0))]


ROW_CHUNK = 16
ROW_PARTS = 4


def _for_row_chunks(n_rows, body):
    def it(c, carry):
        body(pl.ds(pl.multiple_of(c * ROW_CHUNK, ROW_CHUNK), ROW_CHUNK))
        return carry
    lax.fori_loop(0, n_rows // ROW_CHUNK, it, 0, unroll=4)


def _fold_gain(gs_ref, g_ref, sc_ref):
    gs_ref[...] = g_ref[...] * (1.0 + sc_ref[0])


def _norm_mod_rows(x, gs_ref, sh_ref):
    ms = jnp.mean(x * x, axis=-1, keepdims=True)
    return (x * lax.rsqrt(ms + NORM_EPS)) * gs_ref[...] + sh_ref[0]


def _normmod_kernel(rows, tm, xa_ref, xb_ref, g_ref, sh_ref, sc_ref, h_ref, gs_ref):
    _fold_gain(gs_ref, g_ref, sc_ref)

    def run(x_ref):
        def body(r):
            h_ref[r, :] = _norm_mod_rows(x_ref[r, :], gs_ref, sh_ref).astype(h_ref.dtype)
        _for_row_chunks(tm, body)

    is_ctx = pl.program_id(0) * tm < rows.mc
    pl.when(is_ctx)(lambda: run(xa_ref))
    pl.when(jnp.logical_not(is_ctx))(lambda: run(xb_ref))


def _normmod(rows, x_pair, g, mod3, tm):
    D = x_pair[0].shape[1]
    return pl.pallas_call(
        functools.partial(_normmod_kernel, rows, tm),
        grid=(rows.m // tm,),
        in_specs=rows.split_specs(tm, D) + [
            pl.BlockSpec((1, D), lambda i: (0, 0)),
            rows.mod_spec(0, tm, D),
            rows.mod_spec(1, tm, D),
        ],
        out_specs=pl.BlockSpec((tm, D), lambda i: (i, 0)),
        out_shape=jax.ShapeDtypeStruct((rows.m, D), BF16),
        scratch_shapes=[pltpu.VMEM((1, D), F32)],
        compiler_params=_params("parallel"),
        name="normmod",
    )(*x_pair, g.reshape(1, D), mod3, mod3)


def _inproj_kernel(h_ref, w_ref, ws_ref, o_ref, os_ref):
    @pl.when(pl.program_id(1) == 0)
    def _():
        os_ref[...] = _dot(h_ref[...], ws_ref[...])

    o_ref[...] = _dot(h_ref[...], w_ref[...]).astype(o_ref.dtype)


def _inproj(h, w, w_side, out_dtype, tm, tn_cap=1024):
    M, D = h.shape
    N = w.shape[1]
    NS = w_side.shape[1]
    tn = _largest_tile(N, tn_cap)
    return pl.pallas_call(
        _inproj_kernel,
        grid=(M // tm, N // tn),
        in_specs=[
            pl.BlockSpec((tm, D), lambda i, j: (i, 0)),
            pl.BlockSpec((D, tn), lambda i, j: (0, j)),
            pl.BlockSpec((D, NS), lambda i, j: (0, 0)),
        ],
        out_specs=[pl.BlockSpec((tm, tn), lambda i, j: (i, j)), pl.BlockSpec((tm, NS), lambda i, j: (i, 0))],
        out_shape=[jax.ShapeDtypeStruct((M, N), out_dtype), jax.ShapeDtypeStruct((M, NS), F32)],
        compiler_params=_params("parallel", "arbitrary"),
        name="inproj",
    )(h, w, w_side)


def _outproj_kernel(rows, tm, split, y_ref, w_ref, *rest):
    if split:
        xa_ref, xb_ref, gate_ref, g_ref, sh_ref, sc_ref, o_ref, h_ref, acc_ref, gs_ref = rest
    else:
        xa_ref, gate_ref, g_ref, sh_ref, sc_ref, o_ref, h_ref, acc_ref, gs_ref = rest
        xb_ref = xa_ref
    _fold_gain(gs_ref, g_ref, sc_ref)
    is_ctx = pl.program_id(0) * tm < rows.mc
    part = tm // ROW_PARTS

    def finish(r):
        x = jnp.where(is_ctx, xa_ref[r, :], xb_ref[r, :]) if split else xa_ref[r, :]
        x1 = x + gate_ref[0] * acc_ref[r, :]
        o_ref[r, :] = x1
        h_ref[r, :] = _norm_mod_rows(x1, gs_ref, sh_ref).astype(h_ref.dtype)

    for p in range(ROW_PARTS + 1):
        if p < ROW_PARTS:
            rp = slice(p * part, (p + 1) * part)
            acc_ref[rp, :] = _dot(y_ref[rp, :], w_ref[...])
        if p > 0:
            for c in range(part // ROW_CHUNK):
                start = (p - 1) * part + c * ROW_CHUNK
                finish(slice(start, start + ROW_CHUNK))


def _outproj(rows, y, w, x, g, mod3, tm):
    M, K = y.shape
    N = w.shape[1]
    split = isinstance(x, (tuple, list))
    x_specs = rows.split_specs(tm, N) if split else [pl.BlockSpec((tm, N), lambda i: (i, 0))]
    x_args = list(x) if split else [x]
    return pl.pallas_call(
        functools.partial(_outproj_kernel, rows, tm, split),
        grid=(M // tm,),
        in_specs=[
            pl.BlockSpec((tm, K), lambda i: (i, 0)),
            pl.BlockSpec((K, N), lambda i: (0, 0)),
        ] + x_specs + [
            rows.mod_spec(2, tm, N),
            pl.BlockSpec((1, N), lambda i: (0, 0)),
            rows.mod_spec(3, tm, N),
            rows.mod_spec(4, tm, N),
        ],
        out_specs=[pl.BlockSpec((tm, N), lambda i: (i, 0)), pl.BlockSpec((tm, N), lambda i: (i, 0))],
        out_shape=[jax.ShapeDtypeStruct((M, N), F32), jax.ShapeDtypeStruct((M, N), BF16)],
        scratch_shapes=[pltpu.VMEM((tm, N), F32), pltpu.VMEM((1, N), F32)],
        compiler_params=_params("parallel"),
        name="outproj",
    )(y, w, *x_args, mod3, g.reshape(1, N), mod3, mod3)


HALO = BF16_SUBLANE


def _ffn_kernel(rows, tm, final_norm, hp_ref, h_ref, hn_ref, x_ref, gate_ref,
                wa_ref, wb_ref, cw_ref, wo_ref, *rest):
    if final_norm:
        fg_ref, o_ctx_ref, o_lat_ref, hs_ref, a_ref, acc_ref = rest
    else:
        g_ref, sh_ref, sc_ref, o_ref, hnext_ref, hs_ref, a_ref, acc_ref, gs_ref = rest
    i = pl.program_id(0)
    f = pl.program_id(1)

    @pl.when(f == 0)
    def _():
        hs_ref[0:tm, :] = h_ref[...]
        last_row = lax.broadcasted_iota(jnp.int32, (HALO, 1), 0) == HALO - 1
        hs_ref[tm:, :] = jnp.where(last_row, hp_ref[...].astype(F32), hn_ref[...].astype(F32)).astype(BF16)
        acc_ref[...] = jnp.zeros_like(acc_ref)

    def hidden_tile():
        a = _dot(hs_ref[...], wa_ref[0])
        a_ref[HALO:, :] = a
        a_ref[HALO - 8:HALO, :] = a[tm + HALO - 8:, :]
        b = _dot(h_ref[...], wb_ref[0])
        row = i * tm + lax.broadcasted_iota(jnp.int32, (tm, 1), 0)
        pos = jnp.where(row < rows.mc, row & (rows.seq - 1), (row - rows.mc) & (rows.lat_len - 1))
        seq_len = jnp.where(row < rows.mc, rows.seq, rows.lat_len)
        a_prev = jnp.where(pos == 0, 0.0, a_ref[pl.ds(HALO - 1, tm), :])
        a_next = jnp.where(pos == seq_len - 1, 0.0, a_ref[pl.ds(HALO + 1, tm), :])
        a_mid = a_ref[pl.ds(HALO, tm), :]
        cw = cw_ref[0]
        a = cw[0:1] * a_prev + cw[1:2] * a_mid + cw[2:3] * a_next
        return (_silu(a) * b).astype(BF16)

    def residual(r):
        return x_ref[r, :] + gate_ref[0] * acc_ref[r, :]

    def last_step(finish):
        act = hidden_tile()
        part = tm // ROW_PARTS
        for p in range(ROW_PARTS + 1):
            if p < ROW_PARTS:
                rp = slice(p * part, (p + 1) * part)
                acc_ref[rp, :] += _dot(act[rp, :], wo_ref[0])
            if p > 0:
                for c in range(part // ROW_CHUNK):
                    start = (p - 1) * part + c * ROW_CHUNK
                    finish(slice(start, start + ROW_CHUNK))

    last = pl.num_programs(1) - 1

    @pl.when(f < last)
    def _():
        acc_ref[...] += _dot(hidden_tile(), wo_ref[0])

    if final_norm:
        def finish_to(out_ref):
            def finish(r):
                out_ref[r, :] = _rms(residual(r), fg_ref[...])
            return finish

        is_ctx = i * tm < rows.mc
        pl.when((f == last) & is_ctx)(lambda: last_step(finish_to(o_ctx_ref)))
        pl.when((f == last) & jnp.logical_not(is_ctx))(lambda: last_step(finish_to(o_lat_ref)))
    else:
        def finish(r):
            x2 = residual(r)
            o_ref[r, :] = x2
            hnext_ref[r, :] = _norm_mod_rows(x2, gs_ref, sh_ref).astype(hnext_ref.dtype)

        @pl.when(f == last)
        def _():
            _fold_gain(gs_ref, g_ref, sc_ref)
            last_step(finish)


def _ffn(rows, h, x, mod3, layer, w_in, conv_w, w_out, tm, tf, next_norm=None, final_g=None):
    M, D = x.shape
    F = w_out.shape[1]
    nf = F // tf
    nhalo = M // HALO
    final_norm = final_g is not None
    kern = functools.partial(_ffn_kernel, rows, tm, final_norm)
    in_specs = [
        pl.BlockSpec((HALO, D), lambda i, f: (jnp.maximum(i * (tm // HALO) - 1, 0), 0)),
        pl.BlockSpec((tm, D), lambda i, f: (i, 0)),
        pl.BlockSpec((HALO, D), lambda i, f: (jnp.minimum((i + 1) * (tm // HALO), nhalo - 1), 0)),
        pl.BlockSpec((tm, D), lambda i, f: (i, 0)),
        rows.mod_spec(5, tm, D),
        pl.BlockSpec((1, D, tf), lambda i, f: (layer, 0, f)),
        pl.BlockSpec((1, D, tf), lambda i, f: (layer, 0, nf + f)),
        pl.BlockSpec((1, CONV_W, tf), lambda i, f: (layer, 0, f)),
        pl.BlockSpec((1, tf, D), lambda i, f: (layer, f, 0)),
    ]
    args = [h, h, h, x, mod3, w_in, w_in, conv_w, w_out]
    vec = pl.BlockSpec((1, D), lambda i, f: (0, 0))
    if final_norm:
        n_ctx = rows.mc // tm
        in_specs += [vec]
        args += [final_g.reshape(1, D)]
        out_specs = [pl.BlockSpec((tm, D), lambda i, f: (jnp.minimum(i, n_ctx - 1), 0)),
                     pl.BlockSpec((tm, D), lambda i, f: (jnp.maximum(i - n_ctx, 0), 0))]
        out_shape = [jax.ShapeDtypeStruct((rows.mc, D), F32), jax.ShapeDtypeStruct((M - rows.mc, D), F32)]
        row_sem = "arbitrary"
    else:
        g_next, mod3_next = next_norm
        in_specs += [vec, rows.mod_spec(0, tm, D), rows.mod_spec(1, tm, D)]
        args += [g_next.reshape(1, D), mod3_next, mod3_next]
        out_specs = [pl.BlockSpec((tm, D), lambda i, f: (i, 0)), pl.BlockSpec((tm, D), lambda i, f: (i, 0))]
        out_shape = [jax.ShapeDtypeStruct((M, D), F32), jax.ShapeDtypeStruct((M, D), BF16)]
        row_sem = "parallel"
    return pl.pallas_call(
        kern,
        grid=(M // tm, nf),
        in_specs=in_specs,
        out_specs=out_specs,
        out_shape=out_shape,
        scratch_shapes=[
            pltpu.VMEM((tm + HALO, D), BF16),
            pltpu.VMEM((tm + 2 * HALO, tf), F32),
            pltpu.VMEM((tm, D), F32),
        ] + ([] if final_norm else [pltpu.VMEM((1, D), F32)]),
        compiler_params=_params(row_sem, "arbitrary"),
        name="convffn",
    )(*args)


class _ScanCfg:
    def __init__(self, rows, batch, tb, h, dk, dv, backward):
        self.rows, self.batch, self.tb, self.h, self.dk, self.dv = rows, batch, tb, h, dk, dv
        self.backward = backward
        self.cps = rows.seq // tb
        self.lps = rows.lat_len // tb
        self.ctx_blocks = batch * self.cps
        self.nblk = rows.m // tb

    def block(self, i):
        return self.nblk - 1 - i if self.backward else i

    def is_ctx(self, r):
        return r < self.ctx_blocks

    def seq_pos(self, r):
        ctx = self.is_ctx(r)
        return (jnp.where(ctx, r % self.cps, (r - self.ctx_blocks) % self.lps),
                jnp.where(ctx, self.cps, self.lps))

    def lat_seq(self, r):
        return jnp.clip((r - self.ctx_blocks) // self.lps, 0, self.rows.n_lat - 1)

    def ctx_seq(self, r):
        return jnp.clip(r // self.cps, 0, self.batch - 1)


def _scan_kernel(cfg, gq_ref, gk_ref, gv_ref, rq_ref, rk_ref, rv_ref, glr_ref, gw_ref, gb_ref, rd_ref,
                 s0g_ref, s0r_ref, *rest):
    scratch = rest[-8:]
    sg_ref, sr_ref, dend_ref, qd_ref, kinv_ref, kend_ref, rks_ref, rkd_ref = scratch
    if cfg.backward:
        of_ref, gg_ref, rg_ref, gng_ref, rng_ref, _, _, y_ref, sgo_ref, sro_ref = rest[:-8]
    else:
        o_ref, sgo_ref, sro_ref = rest[:-8]
    H, DK, DV, C = cfg.h, cfg.dk, cfg.dv, CHUNK
    r = cfg.block(pl.program_id(0))
    blk, nblk_seq = cfg.seq_pos(r)
    first = blk == (nblk_seq - 1 if cfg.backward else 0)
    last = blk == (0 if cfg.backward else nblk_seq - 1)
    is_ctx = cfg.is_ctx(r)

    @pl.when(first & is_ctx)
    def _():
        sg_ref[...] = jnp.zeros_like(sg_ref)
        sr_ref[...] = jnp.zeros_like(sr_ref)

    @pl.when(first & jnp.logical_not(is_ctx))
    def _():
        sg_ref[...] = s0g_ref[0, 0, 0]
        sr_ref[...] = s0r_ref[0, 0, 0]

    ti = lax.broadcasted_iota(jnp.int32, (C, C), 0)
    tj = lax.broadcasted_iota(jnp.int32, (C, C), 1)
    sees = (tj >= ti) if cfg.backward else (tj <= ti)
    dist = jnp.abs(ti - tj).astype(F32)
    rowi = lax.broadcasted_iota(jnp.int32, (C, LANE), 0)
    to_end = (rowi if cfg.backward else C - 1 - rowi).astype(F32)
    from_start = (C - rowi if cfg.backward else rowi + 1).astype(F32)

    n_chunks = cfg.tb // C
    HDK = H * DK
    scale = DK ** -0.5

    bi = lax.broadcasted_iota(jnp.int32, (cfg.tb, cfg.tb), 0)
    bj = lax.broadcasted_iota(jnp.int32, (cfg.tb, cfg.tb), 1)
    same_chunk = (bi & -C) == (bj & -C)
    blk_tri = jnp.where(same_chunk & ((bj >= bi) if cfg.backward else (bj <= bi)), 1.0, 0.0).astype(BF16)
    pre = _dot(glr_ref[...].astype(BF16), gw_ref[...]) + gb_ref[...]
    la = _log_sigmoid(pre) / GATE_TEMP
    la_hi = la.astype(BF16)
    la_lo = (la - la_hi.astype(F32)).astype(BF16)
    b = _dot(blk_tri, la_hi) + _dot(blk_tri, la_lo)
    b3 = b.reshape(n_chunks, C, HDK)
    b_end = b3[:, 0:1, :] if cfg.backward else b3[:, C - 1:C, :]
    eb = jnp.exp(b)
    chunk_of_row = jnp.right_shift(lax.broadcasted_iota(jnp.int32, (cfg.tb, n_chunks * LANE), 0), C.bit_length() - 1)
    chunk_of_col = jnp.right_shift(lax.broadcasted_iota(jnp.int32, (cfg.tb, n_chunks * LANE), 1),
                                   LANE.bit_length() - 1)
    in_chunk = jnp.where(chunk_of_row == chunk_of_col, 1.0, 0.0).astype(BF16)
    dend_ref[...] = jnp.exp(_dot_tn(la_hi, in_chunk) + _dot_tn(la_lo, in_chunk))
    gk = gk_ref[...].astype(F32)
    qd_ref[...] = (gq_ref[...].astype(F32) * scale * eb).astype(BF16)
    kinv_ref[...] = (gk * jnp.exp(-b)).astype(BF16)
    kend_ref[...] = (gk * jnp.exp(b_end - b3).reshape(cfg.tb, HDK)).astype(BF16)

    decays, q_decs, c_decs, k_dec_cols = [], [], [], []
    for h in range(H):
        lg = _log_sigmoid(rd_ref[h])
        lg_c = jnp.broadcast_to(lg[0:1, 0:C], (C, C))
        lg_l = jnp.broadcast_to(lg[0:1, :], (C, LANE))
        decays.append(jnp.where(sees, jnp.exp(lg_c * dist), 0.0))
        q_decs.append(jnp.concatenate([jnp.exp(lg_l * from_start)] * (DV // LANE), axis=1))
        c_decs.append(jnp.concatenate([jnp.exp(lg[0:1, :] * float(C))] * (DV // LANE), axis=1))
        k_dec_cols.append(jnp.concatenate([jnp.exp(lg_l * to_end)] * n_chunks, axis=0))
    rk = rk_ref[...].astype(F32) * scale
    rks_ref[...] = rk.astype(BF16)
    rkd_ref[...] = (rk * jnp.concatenate(k_dec_cols, axis=1)).astype(BF16)

    for ci in range(n_chunks):
        c = n_chunks - 1 - ci if cfg.backward else ci
        rs = slice(c * C, (c + 1) * C)
        heads = range(H)
        ksl = [slice(h * DK, (h + 1) * DK) for h in heads]
        vsl = [slice(h * DV, (h + 1) * DV) for h in heads]
        att_raw = [_dot_nt(qd_ref[rs, ksl[h]], kinv_ref[rs, ksl[h]]) for h in heads]
        ratt_raw = [_dot_nt(rq_ref[rs, ksl[h]], rks_ref[rs, ksl[h]]) for h in heads]
        qs_g = [_dot(qd_ref[rs, ksl[h]], sg_ref[h].astype(BF16)) for h in heads]
        qs_r = [_dot(rq_ref[rs, ksl[h]], sr_ref[h].astype(BF16)) for h in heads]
        for h in heads:
            att = jnp.where(sees, att_raw[h], 0.0).astype(BF16)
            o_g = _dot(att, gv_ref[rs, vsl[h]]) + qs_g[h]
            ratt = (ratt_raw[h] * decays[h]).astype(BF16)
            o_r = _dot(ratt, rv_ref[rs, vsl[h]]) + qs_r[h] * q_decs[h]
            rs_cols = slice(H * DV + h * DV, H * DV + (h + 1) * DV)
            if cfg.backward:
                for o, cols, gate_ref, ng_ref in ((o_g, vsl[h], gg_ref, gng_ref), (o_r, rs_cols, rg_ref, rng_ref)):
                    tot = o + of_ref[rs, cols]
                    mu = jnp.mean(tot, axis=-1, keepdims=True)
                    d = tot - mu
                    var = jnp.mean(d * d, axis=-1, keepdims=True)
                    yn = d * lax.rsqrt(var + NORM_EPS) * ng_ref[:, vsl[h]]
                    y_ref[rs, cols] = (_silu(gate_ref[rs, vsl[h]].astype(F32)) * yn).astype(y_ref.dtype)
            else:
                o_ref[rs, vsl[h]] = o_g
                o_ref[rs, rs_cols] = o_r
        for h in heads:
            dend = dend_ref[ksl[h], c * LANE:(c + 1) * LANE]
            sg_ref[h] = (sg_ref[h] * jnp.concatenate([dend] * (DV // LANE), axis=1)
                         + _dot_tn(kend_ref[rs, ksl[h]], gv_ref[rs, vsl[h]]))
            sr_ref[h] = sr_ref[h] * c_decs[h] + _dot_tn(rkd_ref[rs, ksl[h]], rv_ref[rs, vsl[h]])

    @pl.when(last & is_ctx)
    def _():
        sgo_ref[0, 0] = sg_ref[...]
        sro_ref[0, 0] = sr_ref[...]


def _scan(cfg, z, glr, col, gate_w, gate_b, ret_decay_t, s0g, s0r, layer, extra):
    H, DK, DV, tb = cfg.h, cfg.dk, cfg.dv, cfg.tb
    M = z.shape[0]
    dirn = int(cfg.backward)

    def zspec(name):
        off, width = col[name]
        assert off % width == 0, (name, off, width)
        return pl.BlockSpec((tb, width), lambda i: (cfg.block(i), off // width))

    def full(shape):
        return pl.BlockSpec(shape, lambda i: (0,) * len(shape))

    state_in = pl.BlockSpec((1, 1, 1, H, DK, DV),
                            lambda i: (cfg.lat_seq(cfg.block(i)), layer, dirn, 0, 0, 0))
    state_out = pl.BlockSpec((1, 1, H, DK, DV), lambda i: (cfg.ctx_seq(cfg.block(i)), dirn, 0, 0, 0))
    row_blk = pl.BlockSpec((tb, 2 * H * DV), lambda i: (cfg.block(i), 0))
    glr_spec = pl.BlockSpec((tb, glr.shape[1]), lambda i: (cfg.block(i), 0))
    in_specs = [zspec("gq"), zspec("gk"), zspec("gv"), zspec("rq"), zspec("rk"), zspec("rv"), glr_spec,
                full(gate_w.shape), full(gate_b.shape), full(ret_decay_t.shape), state_in, state_in]
    args = [z, z, z, z, z, z, glr, gate_w, gate_b, ret_decay_t, s0g, s0r]
    state_shape = jax.ShapeDtypeStruct((cfg.batch, 2, H, DK, DV), F32)
    aliases = {}
    if cfg.backward:
        o_fwd, sg_fwd, sr_fwd, gla_ng, ret_ng = extra
        in_specs += [row_blk, zspec("gg"), zspec("rg"), full(gla_ng.shape), full(ret_ng.shape),
                     pl.BlockSpec(memory_space=pl.ANY), pl.BlockSpec(memory_space=pl.ANY)]
        args += [o_fwd, z, z, gla_ng, ret_ng, sg_fwd, sr_fwd]
        aliases = {len(args) - 2: 1, len(args) - 1: 2}
        out0 = jax.ShapeDtypeStruct((M, 2 * H * DV), BF16)
    else:
        out0 = jax.ShapeDtypeStruct((M, 2 * H * DV), F32)
    return pl.pallas_call(
        functools.partial(_scan_kernel, cfg),
        grid=(cfg.nblk,),
        in_specs=in_specs,
        out_specs=[row_blk, state_out, state_out],
        out_shape=[out0, state_shape, state_shape],
        input_output_aliases=aliases,
        scratch_shapes=[pltpu.VMEM((H, DK, DV), F32), pltpu.VMEM((H, DK, DV), F32),
                        pltpu.VMEM((H * DK, (tb // CHUNK) * LANE), F32)] + [pltpu.VMEM((tb, H * DK), BF16)] * 5,
        compiler_params=_params("arbitrary"),
        name="scan_bwd" if cfg.backward else "scan_fwd",
    )(*args)


def _rope_keep(h, shape):
    lane = lax.broadcasted_iota(jnp.int32, shape, 1)
    return (lane < ROPE_DIM) if h % 2 == 0 else (lane >= ROPE_DIM)


def _mla_q_kernel(n_heads, scale, cq_ref, qng_ref, w_ref, cos_ref, sin_ref, o_ref):
    cqn = _rms(cq_ref[...], qng_ref[...]).astype(BF16)
    q = _dot(cqn, w_ref[...])
    nope_w = n_heads * NOPE_DIM
    rope_w = n_heads * ROPE_DIM
    cos, sin = cos_ref[...], sin_ref[...]
    for h in range(n_heads):
        p = h // 2
        tile = q[:, nope_w + p * LANE:nope_w + (p + 1) * LANE]
        partner = q[:, nope_w + rope_w + p * LANE:nope_w + rope_w + (p + 1) * LANE]
        rot = (tile * cos + partner * sin) * scale
        rot = jnp.where(_rope_keep(h, rot.shape), rot, 0.0)
        base = h * 2 * LANE
        o_ref[:, base:base + LANE] = (q[:, h * NOPE_DIM:(h + 1) * NOPE_DIM] * scale).astype(BF16)
        o_ref[:, base + LANE:base + 2 * LANE] = rot.astype(BF16)


def _mla_q(zc, q_ng, w_uq, cos_t, sin_t, n_heads, scale, tm):
    M = zc.shape[0]
    R = q_ng.shape[0]
    N = w_uq.shape[1]
    return pl.pallas_call(
        functools.partial(_mla_q_kernel, n_heads, scale),
        grid=(M // tm,),
        in_specs=[
            pl.BlockSpec((tm, R), lambda i: (i, 0)),
            pl.BlockSpec((1, R), lambda i: (0, 0)),
            pl.BlockSpec((R, N), lambda i: (0, 0)),
            pl.BlockSpec((tm, LANE), lambda i: (i, 0)),
            pl.BlockSpec((tm, LANE), lambda i: (i, 0)),
        ],
        out_specs=pl.BlockSpec((tm, n_heads * 2 * LANE), lambda i: (i, 0)),
        out_shape=jax.ShapeDtypeStruct((M, n_heads * 2 * LANE), BF16),
        compiler_params=_params("parallel"),
        name="mla_q",
    )(zc, q_ng.reshape(1, R), w_uq, cos_t, sin_t)


class _KeyRows:
    def __init__(self, rows, blk):
        self.rows, self.blk = rows, blk
        self.per_lat = 1 + rows.lat_len // blk
        self.lat_blocks = rows.n_lat * self.per_lat
        self.ctx_blocks = rows.mc // blk
        self.n = self.lat_blocks + self.ctx_blocks

    def is_cache(self, j):
        return (j < self.lat_blocks) & (j % self.per_lat == 0)

    def cache_idx(self, j):
        return jnp.clip(j // self.per_lat, 0, self.rows.n_lat - 1)

    def token_block(self, j):
        b = j // self.per_lat
        t = jnp.maximum(j % self.per_lat - 1, 0)
        lat = self.ctx_blocks + b * (self.per_lat - 1) + t
        return jnp.where(j < self.lat_blocks, lat, j - self.lat_blocks)


def _mla_kv_kernel(keys, n_heads, ckv_ref, kpe_ref, cckv_ref, ckpe_ref, kvng_ref, wk_ref, wvt_ref, cos_ref, sin_ref,
                   kcat_ref, vt_ref, ckvn_ref):
    cached = keys.is_cache(pl.program_id(0))
    kpe = kpe_ref[...]
    rot = kpe[:, :LANE] * cos_ref[...] + kpe[:, LANE:] * sin_ref[...]
    ckvn = jnp.where(cached, cckv_ref[...], _rms(ckv_ref[...], kvng_ref[...]))
    kr = jnp.where(cached, ckpe_ref[...], rot)
    ckvn_ref[...] = ckvn
    ckvn_b = ckvn.astype(BF16)
    kn = _dot(ckvn_b, wk_ref[...])
    vt_ref[...] = _dot_nt(wvt_ref[...], ckvn_b).astype(BF16)
    for h in range(n_heads):
        base = h * 2 * LANE
        kcat_ref[:, base:base + LANE] = kn[:, h * NOPE_DIM:(h + 1) * NOPE_DIM].astype(BF16)
        kcat_ref[:, base + LANE:base + 2 * LANE] = jnp.where(_rope_keep(h, kr.shape), kr, 0.0).astype(BF16)


def _mla_kv(keys, zc, zk, col, cache_ckv, cache_kpe2, kv_ng, w_k, w_vt, cos_t, sin_t, n_heads):
    blk = keys.blk
    R = kv_ng.shape[0]
    Mk = keys.n * blk
    ckv_off, ckv_w = col["ckv"]
    assert ckv_off % ckv_w == 0
    tok = keys.token_block
    return pl.pallas_call(
        functools.partial(_mla_kv_kernel, keys, n_heads),
        grid=(keys.n,),
        in_specs=[
            pl.BlockSpec((blk, ckv_w), lambda j: (tok(j), ckv_off // ckv_w)),
            pl.BlockSpec((blk, 2 * LANE), lambda j: (tok(j), 0)),
            pl.BlockSpec((blk, R), lambda j: (keys.cache_idx(j), 0)),
            pl.BlockSpec((blk, LANE), lambda j: (keys.cache_idx(j), 0)),
            pl.BlockSpec((1, R), lambda j: (0, 0)),
            pl.BlockSpec(w_k.shape, lambda j: (0, 0)),
            pl.BlockSpec(w_vt.shape, lambda j: (0, 0)),
            pl.BlockSpec((blk, LANE), lambda j: (tok(j), 0)),
            pl.BlockSpec((blk, LANE), lambda j: (tok(j), 0)),
        ],
        out_specs=[
            pl.BlockSpec((blk, n_heads * 2 * LANE), lambda j: (j, 0)),
            pl.BlockSpec((n_heads * V_DIM, blk), lambda j: (0, j)),
            pl.BlockSpec((blk, R), lambda j: (j, 0)),
        ],
        out_shape=[
            jax.ShapeDtypeStruct((Mk, n_heads * 2 * LANE), BF16),
            jax.ShapeDtypeStruct((n_heads * V_DIM, Mk), BF16),
            jax.ShapeDtypeStruct((Mk, R), F32),
        ],
        compiler_params=_params("parallel"),
        name="mla_kv",
    )(zc, zk, cache_ckv, cache_kpe2, kv_ng.reshape(1, R), w_k, w_vt, cos_t, sin_t)


ATTN_TQ = 256
ATTN_TK = 512


def _attn_kernel(tq, chunks, q_ref, k_ref, vt_ref, o_ref, st0_ref, st1_ref):
    n_heads = q_ref.shape[1] // (2 * LANE)
    units = [(h, u) for h in range(n_heads) for u in range(q_ref.shape[0] // tq)]
    st_refs = (st0_ref, st1_ref)

    def scores(n, c0, c1):
        h, u = units[n]
        hd = slice(h * 2 * LANE, (h + 1) * 2 * LANE)
        st = _dot_nt(k_ref[c0:c1, hd], q_ref[u * tq:(u + 1) * tq, hd])
        st_refs[n % 2][c0:c1, :] = st
        return jnp.max(st, axis=0, keepdims=True)

    def weights(n, c0, c1, m):
        h, _ = units[n]
        p = jnp.exp2(st_refs[n % 2][c0:c1, :] - m)
        return jnp.sum(p, axis=0, keepdims=True), _dot(vt_ref[h * V_DIM:(h + 1) * V_DIM, c0:c1], p.astype(BF16))

    m_prev = None
    for n in range(len(units) + 1):
        m_new = acc = l = None
        for c0, c1 in chunks:
            if n < len(units):
                mc = scores(n, c0, c1)
                m_new = mc if m_new is None else jnp.maximum(m_new, mc)
            if n > 0:
                lc, pv = weights(n - 1, c0, c1, m_prev)
                l = lc if l is None else l + lc
                acc = pv if acc is None else acc + pv
        if n > 0:
            h, u = units[n - 1]
            o_ref[u * tq:(u + 1) * tq, h * V_DIM:(h + 1) * V_DIM] = (acc / l).T.astype(o_ref.dtype)
        m_prev = m_new


def _attn_kernel_aliased(tq, chunks, q_ref, k_ref, vt_ref, o_prev_ref, o_ref, st0_ref, st1_ref):
    del o_prev_ref
    _attn_kernel(tq, chunks, q_ref, k_ref, vt_ref, o_ref, st0_ref, st1_ref)


def _attention(qcat, kcat, vt, o_prev, n_seq, n_heads, hps, tq, q_blocks_per_seq, q_block0, tk, k_block0):
    sub_q = ATTN_TQ if tq % ATTN_TQ == 0 else LANE
    assert tq % sub_q == 0 and n_heads % hps == 0
    q_spec = pl.BlockSpec((tq, hps * 2 * LANE), lambda s, h, i: (q_block0 + s * q_blocks_per_seq + i, h))
    in_specs = [
        q_spec,
        pl.BlockSpec((tk, hps * 2 * LANE), lambda s, h, i: (k_block0 + s, h)),
        pl.BlockSpec((hps * V_DIM, tk), lambda s, h, i: (h, k_block0 + s)),
    ]
    args = [qcat, kcat, vt]
    chunks = tuple((c0, min(c0 + ATTN_TK, tk)) for c0 in range(0, tk, ATTN_TK))
    kern = functools.partial(_attn_kernel, sub_q, chunks)
    aliases = {}
    if o_prev is not None:
        in_specs.append(pl.BlockSpec(memory_space=pl.ANY))
        args.append(o_prev)
        aliases = {3: 0}
        kern = functools.partial(_attn_kernel_aliased, sub_q, chunks)
    return pl.pallas_call(
        kern,
        grid=(n_seq, n_heads // hps, q_blocks_per_seq),
        in_specs=in_specs,
        out_specs=pl.BlockSpec((tq, hps * V_DIM), lambda s, h, i: (q_block0 + s * q_blocks_per_seq + i, h)),
        out_shape=jax.ShapeDtypeStruct((qcat.shape[0], n_heads * V_DIM), BF16),
        input_output_aliases=aliases,
        scratch_shapes=[pltpu.VMEM((tk, sub_q), F32), pltpu.VMEM((tk, sub_q), F32)],
        compiler_params=_params("parallel", "parallel", "arbitrary"),
        name="mla_attn",
    )(*args)


def _rope_partner(w):
    q = ROPE_DIM // 4
    return jnp.flip(w.reshape(w.shape[:-1] + (2, 2, q)), axis=-2).reshape(w.shape)


def _rope_tables(rows):
    half, quarter = ROPE_DIM // 2, ROPE_DIM // 4
    inv = ROPE_BASE ** (-jnp.arange(quarter, dtype=F32) * 2.0 / half)
    t = jnp.arange(rows.lat_len)
    r = (t // GRID_W).astype(F32)
    c = (t % GRID_W).astype(F32)
    ang_r = r[:, None] * inv[None, :]
    ang_c = c[:, None] * inv[None, :]
    cos = jnp.concatenate([jnp.cos(ang_r), jnp.cos(ang_r), jnp.cos(ang_c), jnp.cos(ang_c)], axis=1)
    sin = jnp.concatenate([-jnp.sin(ang_r), jnp.sin(ang_r), -jnp.sin(ang_c), jnp.sin(ang_c)], axis=1)
    cos = jnp.tile(jnp.concatenate([cos, cos], axis=1), (rows.n_lat, 1))
    sin = jnp.tile(jnp.concatenate([sin, sin], axis=1), (rows.n_lat, 1))
    cos = jnp.concatenate([jnp.ones((rows.mc, LANE), F32), cos], axis=0)
    sin = jnp.concatenate([jnp.zeros((rows.mc, LANE), F32), sin], axis=0)
    return cos, sin


def kernel(x_prompt, x_sample, state_gla, state_ret, cache_ckv, cache_kpe, c, c_ctx, mod_w, mod_b, norm1_g, norm2_g, ab_w_in, gla_gate_w2, gla_gate_b, ret_decay, gla_norm_g, ret_norm_g, ab_w_out, mla_w_in, mla_q_norm_g, mla_w_uq, mla_kv_norm_g, mla_w_ukv, mla_w_out, ffn_w_in, ffn_conv, ffn_w_out, final_norm_g):
    B, S, D = x_prompt.shape
    NB, T, _ = x_sample.shape
    depth = mod_w.shape[0]
    _, _, _, HA, DK, DV = state_gla.shape
    HB = state_ret.shape[3]
    assert HA == HB and state_ret.shape[4:] == (DK, DV)
    GR = gla_gate_w2.shape[2]
    past = cache_ckv.shape[2]
    q_rank = mla_q_norm_g.shape[1]
    kv_rank = mla_kv_norm_g.shape[1]
    HC = mla_w_uq.shape[2] // (NOPE_DIM + ROPE_DIM)
    F = ffn_w_out.shape[1]
    rows = _Rows(B * S, S, T, NB)
    M = rows.m
    tm = min(512, S * B, T)
    assert rows.mc % tm == 0 and T % tm == 0
    assert S & (S - 1) == 0 and T & (T - 1) == 0

    tm_proj = min(1024, S * B, T)
    assert rows.mc % tm_proj == 0 and T % tm_proj == 0
    cond = jnp.concatenate([c_ctx[None, :], c, jnp.zeros((MOD_GROUPS - 1 - NB, D), F32)], axis=0)
    mod3 = _modulation(cond, mod_w, mod_b)

    x = (x_prompt.reshape(B * S, D), x_sample.reshape(NB * T, D))
    h = _normmod(rows, x, norm1_g[0], mod3[0], tm)
    ffn_w = (ffn_w_in.astype(BF16), ffn_conv, ffn_w_out.astype(BF16))
    new_gla, new_ret, new_ckv, new_kpe = [], [], [], []
    for l in range(depth):
        if l % 2 == 0:
            e = l // 2
            names = ("gq", "gk", "gv", "gg", "glr", "rq", "rk", "rv", "rg")
            sizes = (HA * DK, HA * DK, HA * DV, HA * DV, 2 * GR, HB * DK, HB * DK, HB * DV, HB * DV)
            src, o = {}, 0
            for n, sz in zip(names, sizes):
                src[n] = (o, sz)
                o += sz
            order = ("gq", "gk", "gv", "gg", "rq", "rk", "rv", "rg")
            col, o = {}, 0
            for n in order:
                col[n] = (o, src[n][1])
                o += src[n][1]
            so, sz = src["glr"]
            w_all = ab_w_in[e].astype(BF16)
            w_in = jnp.concatenate([w_all[:, :so], w_all[:, so + sz:]], axis=1)
            w_glr = jnp.concatenate([ab_w_in[e][:, so:so + sz], jnp.zeros((D, LANE - sz), F32)], axis=1).astype(BF16)
            z, glr = _inproj(h, w_in, w_glr, BF16, tm_proj)
            tb = min(256, S)
            extra = None
            for dirn in (0, 1):
                cfg = _ScanCfg(rows, B, tb, HA, DK, DV, backward=bool(dirn))
                gate_w = jnp.zeros((LANE, HA * DK), F32).at[dirn * GR:(dirn + 1) * GR].set(gla_gate_w2[e, dirn])
                rd = jnp.broadcast_to(ret_decay[e, dirn][:, None, None], (HB, 8, LANE))
                y, sg, sr = _scan(cfg, z, glr, col, gate_w.astype(BF16), gla_gate_b[e, dirn].reshape(1, -1), rd,
                                  state_gla, state_ret, e, extra)
                extra = (y, sg, sr, gla_norm_g[e].reshape(1, -1), ret_norm_g[e].reshape(1, -1))
            new_gla.append(sg)
            new_ret.append(sr)
            x, h = _outproj(rows, y, ab_w_out[e].astype(BF16), x, norm2_g[l], mod3[l], tm)
        else:
            i = l // 2
            w = mla_w_in[i]
            w_kpe = w[:, q_rank + kv_rank:]
            w_in = w[:, :q_rank + kv_rank].astype(BF16)
            w_kpe_p = _rope_partner(w_kpe)
            w_side = jnp.concatenate([w_kpe, w_kpe, w_kpe_p, w_kpe_p], axis=1).astype(BF16)
            col = {"cq": (0, q_rank), "ckv": (q_rank, kv_rank)}
            zc, zk = _inproj(h, w_in, w_side, F32, tm_proj)
            cos_t, sin_t = _rope_tables(rows)
            wq = mla_w_uq[i].reshape(q_rank, HC, NOPE_DIM + ROPE_DIM)
            wq_rope = wq[:, :, NOPE_DIM:]
            w_uq = jnp.concatenate([wq[:, :, :NOPE_DIM].reshape(q_rank, -1), wq_rope.reshape(q_rank, -1),
                                    _rope_partner(wq_rope).reshape(q_rank, -1)], axis=1).astype(BF16)
            scale = (NOPE_DIM + ROPE_DIM) ** -0.5 * LOG2_E
            assert col["cq"][0] == 0
            qcat = _mla_q(zc, mla_q_norm_g[i], w_uq, cos_t, sin_t, HC, scale, tm)
            wkv = mla_w_ukv[i].reshape(kv_rank, HC, NOPE_DIM + V_DIM)
            w_k = wkv[:, :, :NOPE_DIM].reshape(kv_rank, -1).astype(BF16)
            w_vt = wkv[:, :, NOPE_DIM:].reshape(kv_rank, -1).T.astype(BF16)
            assert S % past == 0 and T % past == 0
            keys = _KeyRows(rows, past)
            ckpe = cache_kpe[:, i].reshape(NB * past, ROPE_DIM)
            kcat, vt, ckvn = _mla_kv(keys, zc, zk, col, cache_ckv[:, i].reshape(NB * past, kv_rank),
                                     jnp.concatenate([ckpe, ckpe], axis=1), mla_kv_norm_g[i], w_k, w_vt,
                                     cos_t, sin_t, HC)
            tk_lat = past + T
            assert (keys.lat_blocks * past) % S == 0
            tq = min(2048, T)
            assert rows.mc % tq == 0
            o = _attention(qcat, kcat, vt, None, B, HC, HC, S, 1, 0, S, keys.lat_blocks * past // S)
            o = _attention(qcat, kcat, vt, o, NB, HC, 1, tq, T // tq, rows.mc // tq, tk_lat, 0)
            new_ckv.append(ckvn[keys.lat_blocks * past:].reshape(B, S, kv_rank))
            new_kpe.append(zk[:rows.mc, :ROPE_DIM].reshape(B, S, ROPE_DIM))
            x, h = _outproj(rows, o, mla_w_out[i].astype(BF16), x, norm2_g[l], mod3[l], tm)
        tf = _largest_tile(F, 512)
        if l == depth - 1:
            y_ctx, y_lat = _ffn(rows, h, x, mod3[l], l, *ffn_w, tm, tf, final_g=final_norm_g)
        else:
            x, h = _ffn(rows, h, x, mod3[l], l, *ffn_w, tm, tf, next_norm=(norm1_g[l + 1], mod3[l + 1]))

    y_prompt = y_ctx.reshape(B, S, D)
    y_sample = y_lat.reshape(NB, T, D)
    return (y_prompt, y_sample, jnp.stack(new_gla, axis=1), jnp.stack(new_ret, axis=1),
            jnp.stack(new_ckv, axis=1), jnp.stack(new_kpe, axis=1))
```

```python
import functools

import jax
import jax.numpy as jnp
from jax import lax
from jax.experimental import pallas as pl
from jax.experimental.pallas import tpu as pltpu

F32 = jnp.float32
BF16 = jnp.bfloat16

NORM_EPS = 1e-6
GATE_TEMP = 16.0
CHUNK = 64
GRID_W = 64
ROPE_BASE = 10000.0
ROPE_DIM = 64
NOPE_DIM = 128
V_DIM = 128
CONV_W = 3
LOG2_E = 1.4426950408889634

LANE = 128
BF16_SUBLANE = 16
MOD_GROUPS = 8
VMEM_LIMIT = 56 * 1024 * 1024

NT_DIMS = (((1,), (1,)), ((), ()))
TN_DIMS = (((0,), (0,)), ((), ()))


def _dot(a, b):
    return jnp.dot(a, b, preferred_element_type=F32)


def _dot_nt(a, b):
    return lax.dot_general(a, b, NT_DIMS, preferred_element_type=F32)


def _dot_tn(a, b):
    return lax.dot_general(a, b, TN_DIMS, preferred_element_type=F32)


def _params(*sem):
    return pltpu.CompilerParams(dimension_semantics=sem, vmem_limit_bytes=VMEM_LIMIT)


def _rms(x, g):
    ms = jnp.mean(x * x, axis=-1, keepdims=True)
    return (x * lax.rsqrt(ms + NORM_EPS)) * g


def _silu(x):
    return x * jax.nn.sigmoid(x)


def _log_sigmoid(x):
    return jnp.minimum(x, 0.0) - jnp.log1p(jnp.exp(-jnp.abs(x)))


def _largest_tile(n, cap):
    best = None
    for t in range(LANE, min(n, cap) + 1, LANE):
        if n % t == 0:
            best = t
    assert best is not None, (n, cap)
    return best


def _mod_kernel(c_ref, w_ref, b_ref, o_ref):
    s = _silu(c_ref[...]).astype(BF16)
    m = _dot(s, w_ref[0].astype(BF16)) + b_ref[0]
    for g in range(MOD_GROUPS):
        o_ref[0, g, 0] = m[g:g + 1]


def _modulation(cond, mod_w, mod_b):
    L, D, N = mod_w.shape
    n_vec = N // D
    out = pl.pallas_call(
        _mod_kernel,
        grid=(L, n_vec),
        in_specs=[
            pl.BlockSpec((MOD_GROUPS, D), lambda l, k: (0, 0)),
            pl.BlockSpec((1, D, D), lambda l, k: (l, 0, k)),
            pl.BlockSpec((1, 1, D), lambda l, k: (l, 0, k)),
        ],
        out_specs=pl.BlockSpec((1, MOD_GROUPS, 1, 1, D), lambda l, k: (l, 0, k, 0, 0)),
        out_shape=jax.ShapeDtypeStruct((L, MOD_GROUPS, n_vec, 1, D), F32),
        compiler_params=_params("parallel", "parallel"),
        name="modulation",
    )(cond, mod_w, mod_b.reshape(L, 1, N))
    return out.reshape(L, MOD_GROUPS * n_vec, 1, D)


class _Rows:
    def __init__(self, mc, seq, lat_len, n_lat):
        self.mc, self.seq, self.lat_len, self.n_lat = mc, seq, lat_len, n_lat
        self.m = mc + lat_len * n_lat

    def group(self, i, tm):
        r = i * tm
        return jnp.where(r < self.mc, 0, 1 + (r - self.mc) // self.lat_len)

    def mod_spec(self, which, tm, d):
        return pl.BlockSpec((1, 1, d), lambda i, *_: (self.group(i, tm) * 6 + which, 0, 0))

    def split_specs(self, tm, d):
        n_ctx = self.mc // tm
        return [pl.BlockSpec((tm, d), lambda i, *_: (jnp.minimum(i, n_ctx - 1), 0)),
                pl.BlockSpec((tm, d), lambda i, *_: (jnp.maximum(i - n_ctx, 0), 0))]


ROW_CHUNK = 16
ROW_PARTS = 4


def _for_row_chunks(n_rows, body):
    def it(c, carry):
        body(pl.ds(pl.multiple_of(c * ROW_CHUNK, ROW_CHUNK), ROW_CHUNK))
        return carry
    lax.fori_loop(0, n_rows // ROW_CHUNK, it, 0, unroll=4)


def _fold_gain(gs_ref, g_ref, sc_ref):
    gs_ref[...] = g_ref[...] * (1.0 + sc_ref[0])


def _norm_mod_rows(x, gs_ref, sh_ref):
    ms = jnp.mean(x * x, axis=-1, keepdims=True)
    return (x * lax.rsqrt(ms + NORM_EPS)) * gs_ref[...] + sh_ref[0]


def _normmod_kernel(rows, tm, xa_ref, xb_ref, g_ref, sh_ref, sc_ref, h_ref, gs_ref):
    _fold_gain(gs_ref, g_ref, sc_ref)

    def run(x_ref):
        def body(r):
            h_ref[r, :] = _norm_mod_rows(x_ref[r, :], gs_ref, sh_ref).astype(h_ref.dtype)
        _for_row_chunks(tm, body)

    is_ctx = pl.program_id(0) * tm < rows.mc
    pl.when(is_ctx)(lambda: run(xa_ref))
    pl.when(jnp.logical_not(is_ctx))(lambda: run(xb_ref))


def _normmod(rows, x_pair, g, mod3, tm):
    D = x_pair[0].shape[1]
    return pl.pallas_call(
        functools.partial(_normmod_kernel, rows, tm),
        grid=(rows.m // tm,),
        in_specs=rows.split_specs(tm, D) + [
            pl.BlockSpec((1, D), lambda i: (0, 0)),
            rows.mod_spec(0, tm, D),
            rows.mod_spec(1, tm, D),
        ],
        out_specs=pl.BlockSpec((tm, D), lambda i: (i, 0)),
        out_shape=jax.ShapeDtypeStruct((rows.m, D), BF16),
        scratch_shapes=[pltpu.VMEM((1, D), F32)],
        compiler_params=_params("parallel"),
        name="normmod",
    )(*x_pair, g.reshape(1, D), mod3, mod3)


def _inproj_kernel(h_ref, w_ref, ws_ref, o_ref, os_ref):
    @pl.when(pl.program_id(1) == 0)
    def _():
        os_ref[...] = _dot(h_ref[...], ws_ref[...])

    o_ref[...] = _dot(h_ref[...], w_ref[...]).astype(o_ref.dtype)


def _inproj(h, w, w_side, out_dtype, tm, tn_cap=1024):
    M, D = h.shape
    N = w.shape[1]
    NS = w_side.shape[1]
    tn = _largest_tile(N, tn_cap)
    return pl.pallas_call(
        _inproj_kernel,
        grid=(M // tm, N // tn),
        in_specs=[
            pl.BlockSpec((tm, D), lambda i, j: (i, 0)),
            pl.BlockSpec((D, tn), lambda i, j: (0, j)),
            pl.BlockSpec((D, NS), lambda i, j: (0, 0)),
        ],
        out_specs=[pl.BlockSpec((tm, tn), lambda i, j: (i, j)), pl.BlockSpec((tm, NS), lambda i, j: (i, 0))],
        out_shape=[jax.ShapeDtypeStruct((M, N), out_dtype), jax.ShapeDtypeStruct((M, NS), F32)],
        compiler_params=_params("parallel", "arbitrary"),
        name="inproj",
    )(h, w, w_side)


def _outproj_kernel(rows, tm, split, y_ref, w_ref, *rest):
    if split:
        xa_ref, xb_ref, gate_ref, g_ref, sh_ref, sc_ref, o_ref, h_ref, acc_ref, gs_ref = rest
    else:
        xa_ref, gate_ref, g_ref, sh_ref, sc_ref, o_ref, h_ref, acc_ref, gs_ref = rest
        xb_ref = xa_ref
    _fold_gain(gs_ref, g_ref, sc_ref)
    is_ctx = pl.program_id(0) * tm < rows.mc
    part = tm // ROW_PARTS

    def finish(r):
        x = jnp.where(is_ctx, xa_ref[r, :], xb_ref[r, :]) if split else xa_ref[r, :]
        x1 = x + gate_ref[0] * acc_ref[r, :]
        o_ref[r, :] = x1
        h_ref[r, :] = _norm_mod_rows(x1, gs_ref, sh_ref).astype(h_ref.dtype)

    for p in range(ROW_PARTS + 1):
        if p < ROW_PARTS:
            rp = slice(p * part, (p + 1) * part)
            acc_ref[rp, :] = _dot(y_ref[rp, :], w_ref[...])
        if p > 0:
            for c in range(part // ROW_CHUNK):
                start = (p - 1) * part + c * ROW_CHUNK
                finish(slice(start, start + ROW_CHUNK))


def _outproj(rows, y, w, x, g, mod3, tm):
    M, K = y.shape
    N = w.shape[1]
    split = isinstance(x, (tuple, list))
    x_specs = rows.split_specs(tm, N) if split else [pl.BlockSpec((tm, N), lambda i: (i, 0))]
    x_args = list(x) if split else [x]
    return pl.pallas_call(
        functools.partial(_outproj_kernel, rows, tm, split),
        grid=(M // tm,),
        in_specs=[
            pl.BlockSpec((tm, K), lambda i: (i, 0)),
            pl.BlockSpec((K, N), lambda i: (0, 0)),
        ] + x_specs + [
            rows.mod_spec(2, tm, N),
            pl.BlockSpec((1, N), lambda i: (0, 0)),
            rows.mod_spec(3, tm, N),
            rows.mod_spec(4, tm, N),
        ],
        out_specs=[pl.BlockSpec((tm, N), lambda i: (i, 0)), pl.BlockSpec((tm, N), lambda i: (i, 0))],
        out_shape=[jax.ShapeDtypeStruct((M, N), F32), jax.ShapeDtypeStruct((M, N), BF16)],
        scratch_shapes=[pltpu.VMEM((tm, N), F32), pltpu.VMEM((1, N), F32)],
        compiler_params=_params("parallel"),
        name="outproj",
    )(y, w, *x_args, mod3, g.reshape(1, N), mod3, mod3)


HALO = BF16_SUBLANE


def _ffn_kernel(rows, tm, final_norm, hp_ref, h_ref, hn_ref, x_ref, gate_ref,
                wa_ref, wb_ref, cw_ref, wo_ref, *rest):
    if final_norm:
        fg_ref, o_ctx_ref, o_lat_ref, hs_ref, a_ref, acc_ref = rest
    else:
        g_ref, sh_ref, sc_ref, o_ref, hnext_ref, hs_ref, a_ref, acc_ref, gs_ref = rest
    i = pl.program_id(0)
    f = pl.program_id(1)

    @pl.when(f == 0)
    def _():
        hs_ref[0:tm, :] = h_ref[...]
        last_row = lax.broadcasted_iota(jnp.int32, (HALO, 1), 0) == HALO - 1
        hs_ref[tm:, :] = jnp.where(last_row, hp_ref[...].astype(F32), hn_ref[...].astype(F32)).astype(BF16)
        acc_ref[...] = jnp.zeros_like(acc_ref)

    def hidden_tile():
        a = _dot(hs_ref[...], wa_ref[0])
        a_ref[HALO:, :] = a
        a_ref[HALO - 8:HALO, :] = a[tm + HALO - 8:, :]
        b = _dot(h_ref[...], wb_ref[0])
        row = i * tm + lax.broadcasted_iota(jnp.int32, (tm, 1), 0)
        pos = jnp.where(row < rows.mc, row & (rows.seq - 1), (row - rows.mc) & (rows.lat_len - 1))
        seq_len = jnp.where(row < rows.mc, rows.seq, rows.lat_len)
        a_prev = jnp.where(pos == 0, 0.0, a_ref[pl.ds(HALO - 1, tm), :])
        a_next = jnp.where(pos == seq_len - 1, 0.0, a_ref[pl.ds(HALO + 1, tm), :])
        a_mid = a_ref[pl.ds(HALO, tm), :]
        cw = cw_ref[0]
        a = cw[0:1] * a_prev + cw[1:2] * a_mid + cw[2:3] * a_next
        return (_silu(a) * b).astype(BF16)

    def residual(r):
        return x_ref[r, :] + gate_ref[0] * acc_ref[r, :]

    def last_step(finish):
        act = hidden_tile()
        part = tm // ROW_PARTS
        for p in range(ROW_PARTS + 1):
            if p < ROW_PARTS:
                rp = slice(p * part, (p + 1) * part)
                acc_ref[rp, :] += _dot(act[rp, :], wo_ref[0])
            if p > 0:
                for c in range(part // ROW_CHUNK):
                    start = (p - 1) * part + c * ROW_CHUNK
                    finish(slice(start, start + ROW_CHUNK))

    last = pl.num_programs(1) - 1

    @pl.when(f < last)
    def _():
        acc_ref[...] += _dot(hidden_tile(), wo_ref[0])

    if final_norm:
        def finish_to(out_ref):
            def finish(r):
                out_ref[r, :] = _rms(residual(r), fg_ref[...])
            return finish

        is_ctx = i * tm < rows.mc
        pl.when((f == last) & is_ctx)(lambda: last_step(finish_to(o_ctx_ref)))
        pl.when((f == last) & jnp.logical_not(is_ctx))(lambda: last_step(finish_to(o_lat_ref)))
    else:
        def finish(r):
            x2 = residual(r)
            o_ref[r, :] = x2
            hnext_ref[r, :] = _norm_mod_rows(x2, gs_ref, sh_ref).astype(hnext_ref.dtype)

        @pl.when(f == last)
        def _():
            _fold_gain(gs_ref, g_ref, sc_ref)
            last_step(finish)


def _ffn(rows, h, x, mod3, layer, w_in, conv_w, w_out, tm, tf, next_norm=None, final_g=None):
    M, D = x.shape
    F = w_out.shape[1]
    nf = F // tf
    nhalo = M // HALO
    final_norm = final_g is not None
    kern = functools.partial(_ffn_kernel, rows, tm, final_norm)
    in_specs = [
        pl.BlockSpec((HALO, D), lambda i, f: (jnp.maximum(i * (tm // HALO) - 1, 0), 0)),
        pl.BlockSpec((tm, D), lambda i, f: (i, 0)),
        pl.BlockSpec((HALO, D), lambda i, f: (jnp.minimum((i + 1) * (tm // HALO), nhalo - 1), 0)),
        pl.BlockSpec((tm, D), lambda i, f: (i, 0)),
        rows.mod_spec(5, tm, D),
        pl.BlockSpec((1, D, tf), lambda i, f: (layer, 0, f)),
        pl.BlockSpec((1, D, tf), lambda i, f: (layer, 0, nf + f)),
        pl.BlockSpec((1, CONV_W, tf), lambda i, f: (layer, 0, f)),
        pl.BlockSpec((1, tf, D), lambda i, f: (layer, f, 0)),
    ]
    args = [h, h, h, x, mod3, w_in, w_in, conv_w, w_out]
    vec = pl.BlockSpec((1, D), lambda i, f: (0, 0))
    if final_norm:
        n_ctx = rows.mc // tm
        in_specs += [vec]
        args += [final_g.reshape(1, D)]
        out_specs = [pl.BlockSpec((tm, D), lambda i, f: (jnp.minimum(i, n_ctx - 1), 0)),
                     pl.BlockSpec((tm, D), lambda i, f: (jnp.maximum(i - n_ctx, 0), 0))]
        out_shape = [jax.ShapeDtypeStruct((rows.mc, D), F32), jax.ShapeDtypeStruct((M - rows.mc, D), F32)]
        row_sem = "arbitrary"
    else:
        g_next, mod3_next = next_norm
        in_specs += [vec, rows.mod_spec(0, tm, D), rows.mod_spec(1, tm, D)]
        args += [g_next.reshape(1, D), mod3_next, mod3_next]
        out_specs = [pl.BlockSpec((tm, D), lambda i, f: (i, 0)), pl.BlockSpec((tm, D), lambda i, f: (i, 0))]
        out_shape = [jax.ShapeDtypeStruct((M, D), F32), jax.ShapeDtypeStruct((M, D), BF16)]
        row_sem = "parallel"
    return pl.pallas_call(
        kern,
        grid=(M // tm, nf),
        in_specs=in_specs,
        out_specs=out_specs,
        out_shape=out_shape,
        scratch_shapes=[
            pltpu.VMEM((tm + HALO, D), BF16),
            pltpu.VMEM((tm + 2 * HALO, tf), F32),
            pltpu.VMEM((tm, D), F32),
        ] + ([] if final_norm else [pltpu.VMEM((1, D), F32)]),
        compiler_params=_params(row_sem, "arbitrary"),
        name="convffn",
    )(*args)


class _ScanCfg:
    def __init__(self, rows, batch, tb, h, dk, dv, backward):
        self.rows, self.batch, self.tb, self.h, self.dk, self.dv = rows, batch, tb, h, dk, dv
        self.backward = backward
        self.cps = rows.seq // tb
        self.lps = rows.lat_len // tb
        self.ctx_blocks = batch * self.cps
        self.nblk = rows.m // tb

    def block(self, i):
        return self.nblk - 1 - i if self.backward else i

    def is_ctx(self, r):
        return r < self.ctx_blocks

    def seq_pos(self, r):
        ctx = self.is_ctx(r)
        return (jnp.where(ctx, r % self.cps, (r - self.ctx_blocks) % self.lps),
                jnp.where(ctx, self.cps, self.lps))

    def lat_seq(self, r):
        return jnp.clip((r - self.ctx_blocks) // self.lps, 0, self.rows.n_lat - 1)

    def ctx_seq(self, r):
        return jnp.clip(r // self.cps, 0, self.batch - 1)


def _scan_kernel(cfg, gq_ref, gk_ref, gv_ref, rq_ref, rk_ref, rv_ref, glr_ref, gw_ref, gb_ref, rd_ref,
                 s0g_ref, s0r_ref, *rest):
    scratch = rest[-8:]
    sg_ref, sr_ref, dend_ref, qd_ref, kinv_ref, kend_ref, rks_ref, rkd_ref = scratch
    if cfg.backward:
        of_ref, gg_ref, rg_ref, gng_ref, rng_ref, _, _, y_ref, sgo_ref, sro_ref = rest[:-8]
    else:
        o_ref, sgo_ref, sro_ref = rest[:-8]
    H, DK, DV, C = cfg.h, cfg.dk, cfg.dv, CHUNK
    r = cfg.block(pl.program_id(0))
    blk, nblk_seq = cfg.seq_pos(r)
    first = blk == (nblk_seq - 1 if cfg.backward else 0)
    last = blk == (0 if cfg.backward else nblk_seq - 1)
    is_ctx = cfg.is_ctx(r)

    @pl.when(first & is_ctx)
    def _():
        sg_ref[...] = jnp.zeros_like(sg_ref)
        sr_ref[...] = jnp.zeros_like(sr_ref)

    @pl.when(first & jnp.logical_not(is_ctx))
    def _():
        sg_ref[...] = s0g_ref[0, 0, 0]
        sr_ref[...] = s0r_ref[0, 0, 0]

    ti = lax.broadcasted_iota(jnp.int32, (C, C), 0)
    tj = lax.broadcasted_iota(jnp.int32, (C, C), 1)
    sees = (tj >= ti) if cfg.backward else (tj <= ti)
    dist = jnp.abs(ti - tj).astype(F32)
    rowi = lax.broadcasted_iota(jnp.int32, (C, LANE), 0)
    to_end = (rowi if cfg.backward else C - 1 - rowi).astype(F32)
    from_start = (C - rowi if cfg.backward else rowi + 1).astype(F32)

    n_chunks = cfg.tb // C
    HDK = H * DK
    scale = DK ** -0.5

    bi = lax.broadcasted_iota(jnp.int32, (cfg.tb, cfg.tb), 0)
    bj = lax.broadcasted_iota(jnp.int32, (cfg.tb, cfg.tb), 1)
    same_chunk = (bi & -C) == (bj & -C)
    blk_tri = jnp.where(same_chunk & ((bj >= bi) if cfg.backward else (bj <= bi)), 1.0, 0.0).astype(BF16)
    pre = _dot(glr_ref[...].astype(BF16), gw_ref[...]) + gb_ref[...]
    la = _log_sigmoid(pre) / GATE_TEMP
    la_hi = la.astype(BF16)
    la_lo = (la - la_hi.astype(F32)).astype(BF16)
    b = _dot(blk_tri, la_hi) + _dot(blk_tri, la_lo)
    b3 = b.reshape(n_chunks, C, HDK)
    b_end = b3[:, 0:1, :] if cfg.backward else b3[:, C - 1:C, :]
    eb = jnp.exp(b)
    chunk_of_row = jnp.right_shift(lax.broadcasted_iota(jnp.int32, (cfg.tb, LANE), 0), C.bit_length() - 1)
    lane = lax.broadcasted_iota(jnp.int32, (cfg.tb, LANE), 1)
    in_chunk = jnp.where(chunk_of_row == lane, 1.0, 0.0).astype(BF16)
    dend_ref[...] = jnp.exp(_dot_tn(la_hi, in_chunk) + _dot_tn(la_lo, in_chunk))
    gk = gk_ref[...].astype(F32)
    qd_ref[...] = (gq_ref[...].astype(F32) * scale * eb).astype(BF16)
    kinv_ref[...] = (gk * jnp.exp(-b)).astype(BF16)
    kend_ref[...] = (gk * jnp.exp(b_end - b3).reshape(cfg.tb, HDK)).astype(BF16)

    decays, q_decs, c_decs, k_dec_cols = [], [], [], []
    for h in range(H):
        lg = _log_sigmoid(rd_ref[h])
        lg_c = jnp.broadcast_to(lg[0:1, 0:C], (C, C))
        lg_l = jnp.broadcast_to(lg[0:1, :], (C, LANE))
        decays.append(jnp.where(sees, jnp.exp(lg_c * dist), 0.0))
        q_decs.append(jnp.concatenate([jnp.exp(lg_l * from_start)] * (DV // LANE), axis=1))
        c_decs.append(jnp.concatenate([jnp.exp(lg[0:1, :] * float(C))] * (DV // LANE), axis=1))
        k_dec_cols.append(jnp.concatenate([jnp.exp(lg_l * to_end)] * n_chunks, axis=0))
    rk = rk_ref[...].astype(F32) * scale
    rks_ref[...] = rk.astype(BF16)
    rkd_ref[...] = (rk * jnp.concatenate(k_dec_cols, axis=1)).astype(BF16)

    for ci in range(n_chunks):
        c = n_chunks - 1 - ci if cfg.backward else ci
        rs = slice(c * C, (c + 1) * C)
        heads = range(H)
        ksl = [slice(h * DK, (h + 1) * DK) for h in heads]
        vsl = [slice(h * DV, (h + 1) * DV) for h in heads]
        att_raw = [_dot_nt(qd_ref[rs, ksl[h]], kinv_ref[rs, ksl[h]]) for h in heads]
        ratt_raw = [_dot_nt(rq_ref[rs, ksl[h]], rks_ref[rs, ksl[h]]) for h in heads]
        qs_g = [_dot(qd_ref[rs, ksl[h]], sg_ref[h].astype(BF16)) for h in heads]
        qs_r = [_dot(rq_ref[rs, ksl[h]], sr_ref[h].astype(BF16)) for h in heads]
        for h in heads:
            att = jnp.where(sees, att_raw[h], 0.0).astype(BF16)
            o_g = _dot(att, gv_ref[rs, vsl[h]]) + qs_g[h]
            ratt = (ratt_raw[h] * decays[h]).astype(BF16)
            o_r = _dot(ratt, rv_ref[rs, vsl[h]]) + qs_r[h] * q_decs[h]
            rs_cols = slice(H * DV + h * DV, H * DV + (h + 1) * DV)
            if cfg.backward:
                for o, cols, gate_ref, ng_ref in ((o_g, vsl[h], gg_ref, gng_ref), (o_r, rs_cols, rg_ref, rng_ref)):
                    tot = o + of_ref[rs, cols]
                    mu = jnp.mean(tot, axis=-1, keepdims=True)
                    d = tot - mu
                    var = jnp.mean(d * d, axis=-1, keepdims=True)
                    yn = d * lax.rsqrt(var + NORM_EPS) * ng_ref[:, vsl[h]]
                    y_ref[rs, cols] = (_silu(gate_ref[rs, vsl[h]].astype(F32)) * yn).astype(y_ref.dtype)
            else:
                o_ref[rs, vsl[h]] = o_g
                o_ref[rs, rs_cols] = o_r
        for h in heads:
            dend = dend_ref[ksl[h], c:c + 1]
            sg_ref[h] = sg_ref[h] * dend + _dot_tn(kend_ref[rs, ksl[h]], gv_ref[rs, vsl[h]])
            sr_ref[h] = sr_ref[h] * c_decs[h] + _dot_tn(rkd_ref[rs, ksl[h]], rv_ref[rs, vsl[h]])

    @pl.when(last & is_ctx)
    def _():
        sgo_ref[0, 0] = sg_ref[...]
        sro_ref[0, 0] = sr_ref[...]


def _scan(cfg, z, glr, col, gate_w, gate_b, ret_decay_t, s0g, s0r, layer, extra):
    H, DK, DV, tb = cfg.h, cfg.dk, cfg.dv, cfg.tb
    M = z.shape[0]
    dirn = int(cfg.backward)

    def zspec(name):
        off, width = col[name]
        assert off % width == 0, (name, off, width)
        return pl.BlockSpec((tb, width), lambda i: (cfg.block(i), off // width))

    def full(shape):
        return pl.BlockSpec(shape, lambda i: (0,) * len(shape))

    state_in = pl.BlockSpec((1, 1, 1, H, DK, DV),
                            lambda i: (cfg.lat_seq(cfg.block(i)), layer, dirn, 0, 0, 0))
    state_out = pl.BlockSpec((1, 1, H, DK, DV), lambda i: (cfg.ctx_seq(cfg.block(i)), dirn, 0, 0, 0))
    row_blk = pl.BlockSpec((tb, 2 * H * DV), lambda i: (cfg.block(i), 0))
    glr_spec = pl.BlockSpec((tb, glr.shape[1]), lambda i: (cfg.block(i), 0))
    in_specs = [zspec("gq"), zspec("gk"), zspec("gv"), zspec("rq"), zspec("rk"), zspec("rv"), glr_spec,
                full(gate_w.shape), full(gate_b.shape), full(ret_decay_t.shape), state_in, state_in]
    args = [z, z, z, z, z, z, glr, gate_w, gate_b, ret_decay_t, s0g, s0r]
    state_shape = jax.ShapeDtypeStruct((cfg.batch, 2, H, DK, DV), F32)
    aliases = {}
    if cfg.backward:
        o_fwd, sg_fwd, sr_fwd, gla_ng, ret_ng = extra
        in_specs += [row_blk, zspec("gg"), zspec("rg"), full(gla_ng.shape), full(ret_ng.shape),
                     pl.BlockSpec(memory_space=pl.ANY), pl.BlockSpec(memory_space=pl.ANY)]
        args += [o_fwd, z, z, gla_ng, ret_ng, sg_fwd, sr_fwd]
        aliases = {len(args) - 2: 1, len(args) - 1: 2}
        out0 = jax.ShapeDtypeStruct((M, 2 * H * DV), BF16)
    else:
        out0 = jax.ShapeDtypeStruct((M, 2 * H * DV), F32)
    return pl.pallas_call(
        functools.partial(_scan_kernel, cfg),
        grid=(cfg.nblk,),
        in_specs=in_specs,
        out_specs=[row_blk, state_out, state_out],
        out_shape=[out0, state_shape, state_shape],
        input_output_aliases=aliases,
        scratch_shapes=[pltpu.VMEM((H, DK, DV), F32), pltpu.VMEM((H, DK, DV), F32),
                        pltpu.VMEM((H * DK, LANE), F32)] + [pltpu.VMEM((tb, H * DK), BF16)] * 5,
        compiler_params=_params("arbitrary"),
        name="scan_bwd" if cfg.backward else "scan_fwd",
    )(*args)


def _rope_keep(h, shape):
    lane = lax.broadcasted_iota(jnp.int32, shape, 1)
    return (lane < ROPE_DIM) if h % 2 == 0 else (lane >= ROPE_DIM)


def _mla_q_kernel(n_heads, scale, cq_ref, qng_ref, w_ref, cos_ref, sin_ref, o_ref):
    cqn = _rms(cq_ref[...], qng_ref[...]).astype(BF16)
    q = _dot(cqn, w_ref[...])
    nope_w = n_heads * NOPE_DIM
    rope_w = n_heads * ROPE_DIM
    cos, sin = cos_ref[...], sin_ref[...]
    for h in range(n_heads):
        p = h // 2
        tile = q[:, nope_w + p * LANE:nope_w + (p + 1) * LANE]
        partner = q[:, nope_w + rope_w + p * LANE:nope_w + rope_w + (p + 1) * LANE]
        rot = (tile * cos + partner * sin) * scale
        rot = jnp.where(_rope_keep(h, rot.shape), rot, 0.0)
        base = h * 2 * LANE
        o_ref[:, base:base + LANE] = (q[:, h * NOPE_DIM:(h + 1) * NOPE_DIM] * scale).astype(BF16)
        o_ref[:, base + LANE:base + 2 * LANE] = rot.astype(BF16)


def _mla_q(zc, q_ng, w_uq, cos_t, sin_t, n_heads, scale, tm):
    M = zc.shape[0]
    R = q_ng.shape[0]
    N = w_uq.shape[1]
    return pl.pallas_call(
        functools.partial(_mla_q_kernel, n_heads, scale),
        grid=(M // tm,),
        in_specs=[
            pl.BlockSpec((tm, R), lambda i: (i, 0)),
            pl.BlockSpec((1, R), lambda i: (0, 0)),
            pl.BlockSpec((R, N), lambda i: (0, 0)),
            pl.BlockSpec((tm, LANE), lambda i: (i, 0)),
            pl.BlockSpec((tm, LANE), lambda i: (i, 0)),
        ],
        out_specs=pl.BlockSpec((tm, n_heads * 2 * LANE), lambda i: (i, 0)),
        out_shape=jax.ShapeDtypeStruct((M, n_heads * 2 * LANE), BF16),
        compiler_params=_params("parallel"),
        name="mla_q",
    )(zc, q_ng.reshape(1, R), w_uq, cos_t, sin_t)


class _KeyRows:
    def __init__(self, rows, blk):
        self.rows, self.blk = rows, blk
        self.per_lat = 1 + rows.lat_len // blk
        self.lat_blocks = rows.n_lat * self.per_lat
        self.ctx_blocks = rows.mc // blk
        self.n = self.lat_blocks + self.ctx_blocks

    def is_cache(self, j):
        return (j < self.lat_blocks) & (j % self.per_lat == 0)

    def cache_idx(self, j):
        return jnp.clip(j // self.per_lat, 0, self.rows.n_lat - 1)

    def token_block(self, j):
        b = j // self.per_lat
        t = jnp.maximum(j % self.per_lat - 1, 0)
        lat = self.ctx_blocks + b * (self.per_lat - 1) + t
        return jnp.where(j < self.lat_blocks, lat, j - self.lat_blocks)


def _mla_kv_kernel(keys, n_heads, ckv_ref, kpe_ref, cckv_ref, ckpe_ref, kvng_ref, wk_ref, wvt_ref, cos_ref, sin_ref,
                   kcat_ref, vt_ref, ckvn_ref):
    cached = keys.is_cache(pl.program_id(0))
    kpe = kpe_ref[...]
    rot = kpe[:, :LANE] * cos_ref[...] + kpe[:, LANE:] * sin_ref[...]
    ckvn = jnp.where(cached, cckv_ref[...], _rms(ckv_ref[...], kvng_ref[...]))
    kr = jnp.where(cached, ckpe_ref[...], rot)
    ckvn_ref[...] = ckvn
    ckvn_b = ckvn.astype(BF16)
    kn = _dot(ckvn_b, wk_ref[...])
    vt_ref[...] = _dot_nt(wvt_ref[...], ckvn_b).astype(BF16)
    for h in range(n_heads):
        base = h * 2 * LANE
        kcat_ref[:, base:base + LANE] = kn[:, h * NOPE_DIM:(h + 1) * NOPE_DIM].astype(BF16)
        kcat_ref[:, base + LANE:base + 2 * LANE] = jnp.where(_rope_keep(h, kr.shape), kr, 0.0).astype(BF16)


def _mla_kv(keys, zc, zk, col, cache_ckv, cache_kpe2, kv_ng, w_k, w_vt, cos_t, sin_t, n_heads):
    blk = keys.blk
    R = kv_ng.shape[0]
    Mk = keys.n * blk
    ckv_off, ckv_w = col["ckv"]
    assert ckv_off % ckv_w == 0
    tok = keys.token_block
    return pl.pallas_call(
        functools.partial(_mla_kv_kernel, keys, n_heads),
        grid=(keys.n,),
        in_specs=[
            pl.BlockSpec((blk, ckv_w), lambda j: (tok(j), ckv_off // ckv_w)),
            pl.BlockSpec((blk, 2 * LANE), lambda j: (tok(j), 0)),
            pl.BlockSpec((blk, R), lambda j: (keys.cache_idx(j), 0)),
            pl.BlockSpec((blk, LANE), lambda j: (keys.cache_idx(j), 0)),
            pl.BlockSpec((1, R), lambda j: (0, 0)),
            pl.BlockSpec(w_k.shape, lambda j: (0, 0)),
            pl.BlockSpec(w_vt.shape, lambda j: (0, 0)),
            pl.BlockSpec((blk, LANE), lambda j: (tok(j), 0)),
            pl.BlockSpec((blk, LANE), lambda j: (tok(j), 0)),
        ],
        out_specs=[
            pl.BlockSpec((blk, n_heads * 2 * LANE), lambda j: (j, 0)),
            pl.BlockSpec((n_heads * V_DIM, blk), lambda j: (0, j)),
            pl.BlockSpec((blk, R), lambda j: (j, 0)),
        ],
        out_shape=[
            jax.ShapeDtypeStruct((Mk, n_heads * 2 * LANE), BF16),
            jax.ShapeDtypeStruct((n_heads * V_DIM, Mk), BF16),
            jax.ShapeDtypeStruct((Mk, R), F32),
        ],
        compiler_params=_params("parallel"),
        name="mla_kv",
    )(zc, zk, cache_ckv, cache_kpe2, kv_ng.reshape(1, R), w_k, w_vt, cos_t, sin_t)


ATTN_TQ = 256
ATTN_TK = 2048


def _attn_kernel(tq, chunks, q_ref, k_ref, vt_ref, o_ref, st0_ref, st1_ref):
    n_heads = q_ref.shape[1] // (2 * LANE)
    units = [(h, u) for h in range(n_heads) for u in range(q_ref.shape[0] // tq)]
    st_refs = (st0_ref, st1_ref)

    def scores(n, c0, c1):
        h, u = units[n]
        hd = slice(h * 2 * LANE, (h + 1) * 2 * LANE)
        st = _dot_nt(k_ref[c0:c1, hd], q_ref[u * tq:(u + 1) * tq, hd])
        st_refs[n % 2][c0:c1, :] = st
        return jnp.max(st, axis=0, keepdims=True)

    def weights(n, c0, c1, m):
        h, _ = units[n]
        p = jnp.exp2(st_refs[n % 2][c0:c1, :] - m)
        return jnp.sum(p, axis=0, keepdims=True), _dot(vt_ref[h * V_DIM:(h + 1) * V_DIM, c0:c1], p.astype(BF16))

    m_prev = None
    for n in range(len(units) + 1):
        m_new = acc = l = None
        for c0, c1 in chunks:
            if n < len(units):
                mc = scores(n, c0, c1)
                m_new = mc if m_new is None else jnp.maximum(m_new, mc)
            if n > 0:
                lc, pv = weights(n - 1, c0, c1, m_prev)
                l = lc if l is None else l + lc
                acc = pv if acc is None else acc + pv
        if n > 0:
            h, u = units[n - 1]
            o_ref[u * tq:(u + 1) * tq, h * V_DIM:(h + 1) * V_DIM] = (acc / l).T.astype(o_ref.dtype)
        m_prev = m_new


def _attn_kernel_aliased(tq, chunks, q_ref, k_ref, vt_ref, o_prev_ref, o_ref, st0_ref, st1_ref):
    del o_prev_ref
    _attn_kernel(tq, chunks, q_ref, k_ref, vt_ref, o_ref, st0_ref, st1_ref)


def _attention(qcat, kcat, vt, o_prev, n_seq, n_heads, hps, tq, q_blocks_per_seq, q_block0, tk, k_block0):
    sub_q = ATTN_TQ if tq % ATTN_TQ == 0 else LANE
    assert tq % sub_q == 0 and n_heads % hps == 0
    q_spec = pl.BlockSpec((tq, hps * 2 * LANE), lambda s, h, i: (q_block0 + s * q_blocks_per_seq + i, h))
    in_specs = [
        q_spec,
        pl.BlockSpec((tk, hps * 2 * LANE), lambda s, h, i: (k_block0 + s, h)),
        pl.BlockSpec((hps * V_DIM, tk), lambda s, h, i: (h, k_block0 + s)),
    ]
    args = [qcat, kcat, vt]
    chunks = tuple((c0, min(c0 + ATTN_TK, tk)) for c0 in range(0, tk, ATTN_TK))
    kern = functools.partial(_attn_kernel, sub_q, chunks)
    aliases = {}
    if o_prev is not None:
        in_specs.append(pl.BlockSpec(memory_space=pl.ANY))
        args.append(o_prev)
        aliases = {3: 0}
        kern = functools.partial(_attn_kernel_aliased, sub_q, chunks)
    return pl.pallas_call(
        kern,
        grid=(n_seq, n_heads // hps, q_blocks_per_seq),
        in_specs=in_specs,
        out_specs=pl.BlockSpec((tq, hps * V_DIM), lambda s, h, i: (q_block0 + s * q_blocks_per_seq + i, h)),
        out_shape=jax.ShapeDtypeStruct((qcat.shape[0], n_heads * V_DIM), BF16),
        input_output_aliases=aliases,
        scratch_shapes=[pltpu.VMEM((tk, sub_q), F32), pltpu.VMEM((tk, sub_q), F32)],
        compiler_params=_params("parallel", "parallel", "arbitrary"),
        name="mla_attn",
    )(*args)


def _rope_partner(w):
    q = ROPE_DIM // 4
    return jnp.flip(w.reshape(w.shape[:-1] + (2, 2, q)), axis=-2).reshape(w.shape)


def _rope_tables(rows):
    half, quarter = ROPE_DIM // 2, ROPE_DIM // 4
    inv = ROPE_BASE ** (-jnp.arange(quarter, dtype=F32) * 2.0 / half)
    t = jnp.arange(rows.lat_len)
    r = (t // GRID_W).astype(F32)
    c = (t % GRID_W).astype(F32)
    ang_r = r[:, None] * inv[None, :]
    ang_c = c[:, None] * inv[None, :]
    cos = jnp.concatenate([jnp.cos(ang_r), jnp.cos(ang_r), jnp.cos(ang_c), jnp.cos(ang_c)], axis=1)
    sin = jnp.concatenate([-jnp.sin(ang_r), jnp.sin(ang_r), -jnp.sin(ang_c), jnp.sin(ang_c)], axis=1)
    cos = jnp.tile(jnp.concatenate([cos, cos], axis=1), (rows.n_lat, 1))
    sin = jnp.tile(jnp.concatenate([sin, sin], axis=1), (rows.n_lat, 1))
    cos = jnp.concatenate([jnp.ones((rows.mc, LANE), F32), cos], axis=0)
    sin = jnp.concatenate([jnp.zeros((rows.mc, LANE), F32), sin], axis=0)
    return cos, sin


def kernel(x_prompt, x_sample, state_gla, state_ret, cache_ckv, cache_kpe, c, c_ctx, mod_w, mod_b, norm1_g, norm2_g, ab_w_in, gla_gate_w2, gla_gate_b, ret_decay, gla_norm_g, ret_norm_g, ab_w_out, mla_w_in, mla_q_norm_g, mla_w_uq, mla_kv_norm_g, mla_w_ukv, mla_w_out, ffn_w_in, ffn_conv, ffn_w_out, final_norm_g):
    B, S, D = x_prompt.shape
    NB, T, _ = x_sample.shape
    depth = mod_w.shape[0]
    _, _, _, HA, DK, DV = state_gla.shape
    HB = state_ret.shape[3]
    assert HA == HB and state_ret.shape[4:] == (DK, DV)
    GR = gla_gate_w2.shape[2]
    past = cache_ckv.shape[2]
    q_rank = mla_q_norm_g.shape[1]
    kv_rank = mla_kv_norm_g.shape[1]
    HC = mla_w_uq.shape[2] // (NOPE_DIM + ROPE_DIM)
    F = ffn_w_out.shape[1]
    rows = _Rows(B * S, S, T, NB)
    M = rows.m
    tm = min(512, S * B, T)
    assert rows.mc % tm == 0 and T % tm == 0
    assert S & (S - 1) == 0 and T & (T - 1) == 0

    tm_proj = min(1024, S * B, T)
    assert rows.mc % tm_proj == 0 and T % tm_proj == 0
    cond = jnp.concatenate([c_ctx[None, :], c, jnp.zeros((MOD_GROUPS - 1 - NB, D), F32)], axis=0)
    mod3 = _modulation(cond, mod_w, mod_b)

    x = (x_prompt.reshape(B * S, D), x_sample.reshape(NB * T, D))
    h = _normmod(rows, x, norm1_g[0], mod3[0], tm)
    ffn_w = (ffn_w_in.astype(BF16), ffn_conv, ffn_w_out.astype(BF16))
    new_gla, new_ret, new_ckv, new_kpe = [], [], [], []
    for l in range(depth):
        if l % 2 == 0:
            e = l // 2
            names = ("gq", "gk", "gv", "gg", "glr", "rq", "rk", "rv", "rg")
            sizes = (HA * DK, HA * DK, HA * DV, HA * DV, 2 * GR, HB * DK, HB * DK, HB * DV, HB * DV)
            src, o = {}, 0
            for n, sz in zip(names, sizes):
                src[n] = (o, sz)
                o += sz
            order = ("gq", "gk", "gv", "gg", "rq", "rk", "rv", "rg")
            col, o = {}, 0
            for n in order:
                col[n] = (o, src[n][1])
                o += src[n][1]
            so, sz = src["glr"]
            assert so % LANE == 0 and so + LANE <= ab_w_in.shape[2]
            w_all = lax.optimization_barrier(ab_w_in[e].astype(BF16))
            w_in = jnp.concatenate([w_all[:, :so], w_all[:, so + sz:]], axis=1)
            w_glr = w_all[:, so:so + LANE]
            z, glr = _inproj(h, w_in, w_glr, BF16, tm_proj)
            tb = min(256, S)
            extra = None
            for dirn in (0, 1):
                cfg = _ScanCfg(rows, B, tb, HA, DK, DV, backward=bool(dirn))
                gate_w = jnp.zeros((LANE, HA * DK), F32).at[dirn * GR:(dirn + 1) * GR].set(gla_gate_w2[e, dirn])
                rd = jnp.broadcast_to(ret_decay[e, dirn][:, None, None], (HB, 8, LANE))
                y, sg, sr = _scan(cfg, z, glr, col, gate_w.astype(BF16), gla_gate_b[e, dirn].reshape(1, -1), rd,
                                  state_gla, state_ret, e, extra)
                extra = (y, sg, sr, gla_norm_g[e].reshape(1, -1), ret_norm_g[e].reshape(1, -1))
            new_gla.append(sg)
            new_ret.append(sr)
            x, h = _outproj(rows, y, ab_w_out[e].astype(BF16), x, norm2_g[l], mod3[l], tm)
        else:
            i = l // 2
            w = mla_w_in[i]
            w_kpe = w[:, q_rank + kv_rank:]
            w_in = w[:, :q_rank + kv_rank].astype(BF16)
            w_kpe_p = _rope_partner(w_kpe)
            w_side = jnp.concatenate([w_kpe, w_kpe, w_kpe_p, w_kpe_p], axis=1).astype(BF16)
            col = {"cq": (0, q_rank), "ckv": (q_rank, kv_rank)}
            zc, zk = _inproj(h, w_in, w_side, F32, tm_proj)
            cos_t, sin_t = _rope_tables(rows)
            wq = mla_w_uq[i].reshape(q_rank, HC, NOPE_DIM + ROPE_DIM)
            wq_rope = wq[:, :, NOPE_DIM:]
            w_uq = jnp.concatenate([wq[:, :, :NOPE_DIM].reshape(q_rank, -1), wq_rope.reshape(q_rank, -1),
                                    _rope_partner(wq_rope).reshape(q_rank, -1)], axis=1).astype(BF16)
            scale = (NOPE_DIM + ROPE_DIM) ** -0.5 * LOG2_E
            assert col["cq"][0] == 0
            qcat = _mla_q(zc, mla_q_norm_g[i], w_uq, cos_t, sin_t, HC, scale, tm)
            wkv = mla_w_ukv[i].reshape(kv_rank, HC, NOPE_DIM + V_DIM)
            w_k = wkv[:, :, :NOPE_DIM].reshape(kv_rank, -1).astype(BF16)
            w_vt = wkv[:, :, NOPE_DIM:].reshape(kv_rank, -1).T.astype(BF16)
            assert S % past == 0 and T % past == 0
            keys = _KeyRows(rows, past)
            ckpe = cache_kpe[:, i].reshape(NB * past, ROPE_DIM)
            kcat, vt, ckvn = _mla_kv(keys, zc, zk, col, cache_ckv[:, i].reshape(NB * past, kv_rank),
                                     jnp.concatenate([ckpe, ckpe], axis=1), mla_kv_norm_g[i], w_k, w_vt,
                                     cos_t, sin_t, HC)
            tk_lat = past + T
            assert (keys.lat_blocks * past) % S == 0
            tq = min(2048, T)
            assert rows.mc % tq == 0
            o = _attention(qcat, kcat, vt, None, B, HC, HC, S, 1, 0, S, keys.lat_blocks * past // S)
            o = _attention(qcat, kcat, vt, o, NB, HC, 1, tq, T // tq, rows.mc // tq, tk_lat, 0)
            new_ckv.append(ckvn[keys.lat_blocks * past:].reshape(B, S, kv_rank))
            new_kpe.append(zk[:rows.mc, :ROPE_DIM].reshape(B, S, ROPE_DIM))
            x, h = _outproj(rows, o, mla_w_out[i].astype(BF16), x, norm2_g[l], mod3[l], tm)
        tf = _largest_tile(F, 512)
        if l == depth - 1:
            y_ctx, y_lat = _ffn(rows, h, x, mod3[l], l, *ffn_w, tm, tf, final_g=final_norm_g)
        else:
            x, h = _ffn(rows, h, x, mod3[l], l, *ffn_w, tm, tf, next_norm=(norm1_g[l + 1], mod3[l + 1]))

    y_prompt = y_ctx.reshape(B, S, D)
    y_sample = y_lat.reshape(NB, T, D)
    return (y_prompt, y_sample, jnp.stack(new_gla, axis=1), jnp.stack(new_ret, axis=1),
            jnp.stack(new_ckv, axis=1), jnp.stack(new_kpe, axis=1))
```

```python
import functools

import jax
import jax.numpy as jnp
from jax import lax
from jax.experimental import pallas as pl
from jax.experimental.pallas import tpu as pltpu

F32 = jnp.float32
BF16 = jnp.bfloat16

NORM_EPS = 1e-6
GATE_TEMP = 16.0
CHUNK = 64
GRID_W = 64
ROPE_BASE = 10000.0
ROPE_DIM = 64
NOPE_DIM = 128
V_DIM = 128
CONV_W = 3
LOG2_E = 1.4426950408889634

LANE = 128
BF16_SUBLANE = 16
MOD_GROUPS = 8
VMEM_LIMIT = 56 * 1024 * 1024

NT_DIMS = (((1,), (1,)), ((), ()))
TN_DIMS = (((0,), (0,)), ((), ()))


def _dot(a, b):
    return jnp.dot(a, b, preferred_element_type=F32)


def _dot_nt(a, b):
    return lax.dot_general(a, b, NT_DIMS, preferred_element_type=F32)


def _dot_tn(a, b):
    return lax.dot_general(a, b, TN_DIMS, preferred_element_type=F32)


def _params(*sem):
    return pltpu.CompilerParams(dimension_semantics=sem, vmem_limit_bytes=VMEM_LIMIT)


def _rms(x, g):
    ms = jnp.mean(x * x, axis=-1, keepdims=True)
    return (x * lax.rsqrt(ms + NORM_EPS)) * g


def _silu(x):
    return x * jax.nn.sigmoid(x)


def _log_sigmoid(x):
    return jnp.minimum(x, 0.0) - jnp.log1p(jnp.exp(-jnp.abs(x)))


def _largest_tile(n, cap):
    best = None
    for t in range(LANE, min(n, cap) + 1, LANE):
        if n % t == 0:
            best = t
    assert best is not None, (n, cap)
    return best


def _mod_kernel(c_ref, w_ref, b_ref, o_ref):
    s = _silu(c_ref[...]).astype(BF16)
    m = _dot(s, w_ref[0].astype(BF16)) + b_ref[0]
    for g in range(MOD_GROUPS):
        o_ref[0, g, 0] = m[g:g + 1]


def _modulation(cond, mod_w, mod_b):
    L, D, N = mod_w.shape
    n_vec = N // D
    out = pl.pallas_call(
        _mod_kernel,
        grid=(L, n_vec),
        in_specs=[
            pl.BlockSpec((MOD_GROUPS, D), lambda l, k: (0, 0)),
            pl.BlockSpec((1, D, D), lambda l, k: (l, 0, k)),
            pl.BlockSpec((1, 1, D), lambda l, k: (l, 0, k)),
        ],
        out_specs=pl.BlockSpec((1, MOD_GROUPS, 1, 1, D), lambda l, k: (l, 0, k, 0, 0)),
        out_shape=jax.ShapeDtypeStruct((L, MOD_GROUPS, n_vec, 1, D), F32),
        compiler_params=_params("parallel", "parallel"),
        name="modulation",
    )(cond, mod_w, mod_b.reshape(L, 1, N))
    return out.reshape(L, MOD_GROUPS * n_vec, 1, D)


class _Rows:
    def __init__(self, mc, seq, lat_len, n_lat):
        self.mc, self.seq, self.lat_len, self.n_lat = mc, seq, lat_len, n_lat
        self.m = mc + lat_len * n_lat

    def group(self, i, tm):
        r = i * tm
        return jnp.where(r < self.mc, 0, 1 + (r - self.mc) // self.lat_len)

    def mod_spec(self, which, tm, d):
        return pl.BlockSpec((1, 1, d), lambda i, *_: (self.group(i, tm) * 6 + which, 0, 0))

    def split_specs(self, tm, d):
        n_ctx = self.mc // tm
        return [pl.BlockSpec((tm, d), lambda i, *_: (jnp.minimum(i, n_ctx - 1), 0)),
                pl.BlockSpec((tm, d), lambda i, *_: (jnp.maximum(i - n_ctx, 0), 0))]


ROW_CHUNK = 16
ROW_PARTS = 2


def _for_row_chunks(n_rows, body):
    def it(c, carry):
        body(pl.ds(pl.multiple_of(c * ROW_CHUNK, ROW_CHUNK), ROW_CHUNK))
        return carry
    lax.fori_loop(0, n_rows // ROW_CHUNK, it, 0, unroll=4)


def _fold_gain(gs_ref, g_ref, sc_ref):
    gs_ref[...] = g_ref[...] * (1.0 + sc_ref[0])


def _norm_mod_rows(x, gs_ref, sh_ref):
    ms = jnp.mean(x * x, axis=-1, keepdims=True)
    return (x * lax.rsqrt(ms + NORM_EPS)) * gs_ref[...] + sh_ref[0]


def _normmod_kernel(rows, tm, xa_ref, xb_ref, g_ref, sh_ref, sc_ref, h_ref, gs_ref):
    _fold_gain(gs_ref, g_ref, sc_ref)

    def run(x_ref):
        def body(r):
            h_ref[r, :] = _norm_mod_rows(x_ref[r, :], gs_ref, sh_ref).astype(h_ref.dtype)
        _for_row_chunks(tm, body)

    is_ctx = pl.program_id(0) * tm < rows.mc
    pl.when(is_ctx)(lambda: run(xa_ref))
    pl.when(jnp.logical_not(is_ctx))(lambda: run(xb_ref))


def _normmod(rows, x_pair, g, mod3, tm):
    D = x_pair[0].shape[1]
    return pl.pallas_call(
        functools.partial(_normmod_kernel, rows, tm),
        grid=(rows.m // tm,),
        in_specs=rows.split_specs(tm, D) + [
            pl.BlockSpec((1, D), lambda i: (0, 0)),
            rows.mod_spec(0, tm, D),
            rows.mod_spec(1, tm, D),
        ],
        out_specs=pl.BlockSpec((tm, D), lambda i: (i, 0)),
        out_shape=jax.ShapeDtypeStruct((rows.m, D), BF16),
        scratch_shapes=[pltpu.VMEM((1, D), F32)],
        compiler_params=_params("parallel"),
        name="normmod",
    )(*x_pair, g.reshape(1, D), mod3, mod3)


def _inproj_kernel(h_ref, w_ref, ws_ref, o_ref, os_ref):
    @pl.when(pl.program_id(1) == 0)
    def _():
        os_ref[...] = _dot(h_ref[...], ws_ref[...])

    o_ref[...] = _dot(h_ref[...], w_ref[...]).astype(o_ref.dtype)


def _inproj(h, w, w_side, out_dtype, tm, tn_cap=2048):
    M, D = h.shape
    N = w.shape[1]
    NS = w_side.shape[1]
    tn = _largest_tile(N, tn_cap)
    return pl.pallas_call(
        _inproj_kernel,
        grid=(M // tm, N // tn),
        in_specs=[
            pl.BlockSpec((tm, D), lambda i, j: (i, 0)),
            pl.BlockSpec((D, tn), lambda i, j: (0, j)),
            pl.BlockSpec((D, NS), lambda i, j: (0, 0)),
        ],
        out_specs=[pl.BlockSpec((tm, tn), lambda i, j: (i, j)), pl.BlockSpec((tm, NS), lambda i, j: (i, 0))],
        out_shape=[jax.ShapeDtypeStruct((M, N), out_dtype), jax.ShapeDtypeStruct((M, NS), F32)],
        compiler_params=_params("parallel", "arbitrary"),
        name="inproj",
    )(h, w, w_side)


def _outproj_kernel(rows, tm, split, y_ref, w_ref, *rest):
    if split:
        xa_ref, xb_ref, gate_ref, g_ref, sh_ref, sc_ref, o_ref, h_ref, acc_ref, gs_ref = rest
    else:
        xa_ref, gate_ref, g_ref, sh_ref, sc_ref, o_ref, h_ref, acc_ref, gs_ref = rest
        xb_ref = xa_ref
    _fold_gain(gs_ref, g_ref, sc_ref)
    is_ctx = pl.program_id(0) * tm < rows.mc
    part = tm // ROW_PARTS

    def finish(r):
        x = jnp.where(is_ctx, xa_ref[r, :], xb_ref[r, :]) if split else xa_ref[r, :]
        x1 = x + gate_ref[0] * acc_ref[r, :]
        o_ref[r, :] = x1
        h_ref[r, :] = _norm_mod_rows(x1, gs_ref, sh_ref).astype(h_ref.dtype)

    for p in range(ROW_PARTS + 1):
        if p < ROW_PARTS:
            rp = slice(p * part, (p + 1) * part)
            acc_ref[rp, :] = _dot(y_ref[rp, :], w_ref[...])
        if p > 0:
            for c in range(part // ROW_CHUNK):
                start = (p - 1) * part + c * ROW_CHUNK
                finish(slice(start, start + ROW_CHUNK))


def _outproj(rows, y, w, x, g, mod3, tm):
    M, K = y.shape
    N = w.shape[1]
    split = isinstance(x, (tuple, list))
    x_specs = rows.split_specs(tm, N) if split else [pl.BlockSpec((tm, N), lambda i: (i, 0))]
    x_args = list(x) if split else [x]
    return pl.pallas_call(
        functools.partial(_outproj_kernel, rows, tm, split),
        grid=(M // tm,),
        in_specs=[
            pl.BlockSpec((tm, K), lambda i: (i, 0)),
            pl.BlockSpec((K, N), lambda i: (0, 0)),
        ] + x_specs + [
            rows.mod_spec(2, tm, N),
            pl.BlockSpec((1, N), lambda i: (0, 0)),
            rows.mod_spec(3, tm, N),
            rows.mod_spec(4, tm, N),
        ],
        out_specs=[pl.BlockSpec((tm, N), lambda i: (i, 0)), pl.BlockSpec((tm, N), lambda i: (i, 0))],
        out_shape=[jax.ShapeDtypeStruct((M, N), F32), jax.ShapeDtypeStruct((M, N), BF16)],
        scratch_shapes=[pltpu.VMEM((tm, N), F32), pltpu.VMEM((1, N), F32)],
        compiler_params=_params("parallel"),
        name="outproj",
    )(y, w, *x_args, mod3, g.reshape(1, N), mod3, mod3)


HALO = BF16_SUBLANE


def _ffn_kernel(rows, tm, final_norm, hp_ref, h_ref, hn_ref, x_ref, gate_ref,
                wa_ref, wb_ref, cw_ref, wo_ref, *rest):
    if final_norm:
        fg_ref, o_ctx_ref, o_lat_ref, hs_ref, a_ref, acc_ref = rest
    else:
        g_ref, sh_ref, sc_ref, o_ref, hnext_ref, hs_ref, a_ref, acc_ref, gs_ref = rest
    i = pl.program_id(0)
    f = pl.program_id(1)

    @pl.when(f == 0)
    def _():
        hs_ref[0:tm, :] = h_ref[...]
        last_row = lax.broadcasted_iota(jnp.int32, (HALO, 1), 0) == HALO - 1
        hs_ref[tm:, :] = jnp.where(last_row, hp_ref[...].astype(F32), hn_ref[...].astype(F32)).astype(BF16)
        acc_ref[...] = jnp.zeros_like(acc_ref)

    def hidden_tile():
        a = _dot(hs_ref[...], wa_ref[0])
        a_ref[HALO:, :] = a
        a_ref[HALO - 8:HALO, :] = a[tm + HALO - 8:, :]
        b = _dot(h_ref[...], wb_ref[0])
        row = i * tm + lax.broadcasted_iota(jnp.int32, (tm, 1), 0)
        pos = jnp.where(row < rows.mc, row & (rows.seq - 1), (row - rows.mc) & (rows.lat_len - 1))
        seq_len = jnp.where(row < rows.mc, rows.seq, rows.lat_len)
        a_prev = jnp.where(pos == 0, 0.0, a_ref[pl.ds(HALO - 1, tm), :])
        a_next = jnp.where(pos == seq_len - 1, 0.0, a_ref[pl.ds(HALO + 1, tm), :])
        a_mid = a_ref[pl.ds(HALO, tm), :]
        cw = cw_ref[0]
        a = cw[0:1] * a_prev + cw[1:2] * a_mid + cw[2:3] * a_next
        return (_silu(a) * b).astype(BF16)

    def residual(r):
        return x_ref[r, :] + gate_ref[0] * acc_ref[r, :]

    def last_step(finish):
        act = hidden_tile()
        part = tm // ROW_PARTS
        for p in range(ROW_PARTS + 1):
            if p < ROW_PARTS:
                rp = slice(p * part, (p + 1) * part)
                acc_ref[rp, :] += _dot(act[rp, :], wo_ref[0])
            if p > 0:
                for c in range(part // ROW_CHUNK):
                    start = (p - 1) * part + c * ROW_CHUNK
                    finish(slice(start, start + ROW_CHUNK))

    last = pl.num_programs(1) - 1

    @pl.when(f < last)
    def _():
        acc_ref[...] += _dot(hidden_tile(), wo_ref[0])

    if final_norm:
        def finish_to(out_ref):
            def finish(r):
                out_ref[r, :] = _rms(residual(r), fg_ref[...])
            return finish

        is_ctx = i * tm < rows.mc
        pl.when((f == last) & is_ctx)(lambda: last_step(finish_to(o_ctx_ref)))
        pl.when((f == last) & jnp.logical_not(is_ctx))(lambda: last_step(finish_to(o_lat_ref)))
    else:
        def finish(r):
            x2 = residual(r)
            o_ref[r, :] = x2
            hnext_ref[r, :] = _norm_mod_rows(x2, gs_ref, sh_ref).astype(hnext_ref.dtype)

        @pl.when(f == last)
        def _():
            _fold_gain(gs_ref, g_ref, sc_ref)
            last_step(finish)


def _ffn(rows, h, x, mod3, layer, w_in, conv_w, w_out, tm, tf, next_norm=None, final_g=None):
    M, D = x.shape
    F = w_out.shape[1]
    nf = F // tf
    nhalo = M // HALO
    final_norm = final_g is not None
    kern = functools.partial(_ffn_kernel, rows, tm, final_norm)
    in_specs = [
        pl.BlockSpec((HALO, D), lambda i, f: (jnp.maximum(i * (tm // HALO) - 1, 0), 0)),
        pl.BlockSpec((tm, D), lambda i, f: (i, 0)),
        pl.BlockSpec((HALO, D), lambda i, f: (jnp.minimum((i + 1) * (tm // HALO), nhalo - 1), 0)),
        pl.BlockSpec((tm, D), lambda i, f: (i, 0)),
        rows.mod_spec(5, tm, D),
        pl.BlockSpec((1, D, tf), lambda i, f: (layer, 0, f)),
        pl.BlockSpec((1, D, tf), lambda i, f: (layer, 0, nf + f)),
        pl.BlockSpec((1, CONV_W, tf), lambda i, f: (layer, 0, f)),
        pl.BlockSpec((1, tf, D), lambda i, f: (layer, f, 0)),
    ]
    args = [h, h, h, x, mod3, w_in, w_in, conv_w, w_out]
    vec = pl.BlockSpec((1, D), lambda i, f: (0, 0))
    if final_norm:
        n_ctx = rows.mc // tm
        in_specs += [vec]
        args += [final_g.reshape(1, D)]
        out_specs = [pl.BlockSpec((tm, D), lambda i, f: (jnp.minimum(i, n_ctx - 1), 0)),
                     pl.BlockSpec((tm, D), lambda i, f: (jnp.maximum(i - n_ctx, 0), 0))]
        out_shape = [jax.ShapeDtypeStruct((rows.mc, D), F32), jax.ShapeDtypeStruct((M - rows.mc, D), F32)]
        row_sem = "arbitrary"
    else:
        g_next, mod3_next = next_norm
        in_specs += [vec, rows.mod_spec(0, tm, D), rows.mod_spec(1, tm, D)]
        args += [g_next.reshape(1, D), mod3_next, mod3_next]
        out_specs = [pl.BlockSpec((tm, D), lambda i, f: (i, 0)), pl.BlockSpec((tm, D), lambda i, f: (i, 0))]
        out_shape = [jax.ShapeDtypeStruct((M, D), F32), jax.ShapeDtypeStruct((M, D), BF16)]
        row_sem = "parallel"
    return pl.pallas_call(
        kern,
        grid=(M // tm, nf),
        in_specs=in_specs,
        out_specs=out_specs,
        out_shape=out_shape,
        scratch_shapes=[
            pltpu.VMEM((tm + HALO, D), BF16),
            pltpu.VMEM((tm + 2 * HALO, tf), F32),
            pltpu.VMEM((tm, D), F32),
        ] + ([] if final_norm else [pltpu.VMEM((1, D), F32)]),
        compiler_params=_params(row_sem, "arbitrary"),
        name="convffn",
    )(*args)


class _ScanCfg:
    def __init__(self, rows, batch, tb, h, dk, dv, backward):
        self.rows, self.batch, self.tb, self.h, self.dk, self.dv = rows, batch, tb, h, dk, dv
        self.backward = backward
        self.cps = rows.seq // tb
        self.lps = rows.lat_len // tb
        self.ctx_blocks = batch * self.cps
        self.nblk = rows.m // tb

    def block(self, i):
        return self.nblk - 1 - i if self.backward else i

    def is_ctx(self, r):
        return r < self.ctx_blocks

    def seq_pos(self, r):
        ctx = self.is_ctx(r)
        return (jnp.where(ctx, r % self.cps, (r - self.ctx_blocks) % self.lps),
                jnp.where(ctx, self.cps, self.lps))

    def lat_seq(self, r):
        return jnp.clip((r - self.ctx_blocks) // self.lps, 0, self.rows.n_lat - 1)

    def ctx_seq(self, r):
        return jnp.clip(r // self.cps, 0, self.batch - 1)


def _scan_kernel(cfg, gq_ref, gk_ref, gv_ref, rq_ref, rk_ref, rv_ref, glr_ref, gw_ref, gb_ref, rd_ref,
                 s0g_ref, s0r_ref, *rest):
    scratch = rest[-8:]
    sg_ref, sr_ref, dend_ref, qd_ref, kinv_ref, kend_ref, rks_ref, rkd_ref = scratch
    if cfg.backward:
        of_ref, gg_ref, rg_ref, gng_ref, rng_ref, _, _, y_ref, sgo_ref, sro_ref = rest[:-8]
    else:
        o_ref, sgo_ref, sro_ref = rest[:-8]
    H, DK, DV, C = cfg.h, cfg.dk, cfg.dv, CHUNK
    r = cfg.block(pl.program_id(0))
    blk, nblk_seq = cfg.seq_pos(r)
    first = blk == (nblk_seq - 1 if cfg.backward else 0)
    last = blk == (0 if cfg.backward else nblk_seq - 1)
    is_ctx = cfg.is_ctx(r)

    @pl.when(first & is_ctx)
    def _():
        sg_ref[...] = jnp.zeros_like(sg_ref)
        sr_ref[...] = jnp.zeros_like(sr_ref)

    @pl.when(first & jnp.logical_not(is_ctx))
    def _():
        sg_ref[...] = s0g_ref[0, 0, 0]
        sr_ref[...] = s0r_ref[0, 0, 0]

    ti = lax.broadcasted_iota(jnp.int32, (C, C), 0)
    tj = lax.broadcasted_iota(jnp.int32, (C, C), 1)
    sees = (tj >= ti) if cfg.backward else (tj <= ti)
    dist = jnp.abs(ti - tj).astype(F32)
    rowi = lax.broadcasted_iota(jnp.int32, (C, LANE), 0)
    to_end = (rowi if cfg.backward else C - 1 - rowi).astype(F32)
    from_start = (C - rowi if cfg.backward else rowi + 1).astype(F32)

    n_chunks = cfg.tb // C
    HDK = H * DK
    scale = DK ** -0.5

    bi = lax.broadcasted_iota(jnp.int32, (cfg.tb, cfg.tb), 0)
    bj = lax.broadcasted_iota(jnp.int32, (cfg.tb, cfg.tb), 1)
    same_chunk = (bi & -C) == (bj & -C)
    blk_tri = jnp.where(same_chunk & ((bj >= bi) if cfg.backward else (bj <= bi)), 1.0, 0.0).astype(BF16)
    pre = _dot(glr_ref[...].astype(BF16), gw_ref[...]) + gb_ref[...]
    la = _log_sigmoid(pre) / GATE_TEMP
    la_hi = la.astype(BF16)
    la_lo = (la - la_hi.astype(F32)).astype(BF16)
    b = _dot(blk_tri, la_hi) + _dot(blk_tri, la_lo)
    b3 = b.reshape(n_chunks, C, HDK)
    b_end = b3[:, 0:1, :] if cfg.backward else b3[:, C - 1:C, :]
    eb = jnp.exp(b)
    chunk_of_row = jnp.right_shift(lax.broadcasted_iota(jnp.int32, (cfg.tb, LANE), 0), C.bit_length() - 1)
    lane = lax.broadcasted_iota(jnp.int32, (cfg.tb, LANE), 1)
    in_chunk = jnp.where(chunk_of_row == lane, 1.0, 0.0).astype(BF16)
    dend_ref[...] = jnp.exp(_dot_tn(la_hi, in_chunk) + _dot_tn(la_lo, in_chunk))
    gk = gk_ref[...].astype(F32)
    qd_ref[...] = (gq_ref[...].astype(F32) * scale * eb).astype(BF16)
    kinv_ref[...] = (gk * jnp.exp(-b)).astype(BF16)
    kend_ref[...] = (gk * jnp.exp(b_end - b3).reshape(cfg.tb, HDK)).astype(BF16)

    decays, q_decs, c_decs, k_dec_cols = [], [], [], []
    for h in range(H):
        lg = _log_sigmoid(rd_ref[h])
        lg_c = jnp.broadcast_to(lg[0:1, 0:C], (C, C))
        lg_l = jnp.broadcast_to(lg[0:1, :], (C, LANE))
        decays.append(jnp.where(sees, jnp.exp(lg_c * dist), 0.0))
        q_decs.append(jnp.concatenate([jnp.exp(lg_l * from_start)] * (DV // LANE), axis=1))
        c_decs.append(jnp.concatenate([jnp.exp(lg[0:1, :] * float(C))] * (DV // LANE), axis=1))
        k_dec_cols.append(jnp.concatenate([jnp.exp(lg_l * to_end)] * n_chunks, axis=0))
    rk = rk_ref[...].astype(F32) * scale
    rks_ref[...] = rk.astype(BF16)
    rkd_ref[...] = (rk * jnp.concatenate(k_dec_cols, axis=1)).astype(BF16)

    for ci in range(n_chunks):
        c = n_chunks - 1 - ci if cfg.backward else ci
        rs = slice(c * C, (c + 1) * C)
        heads = range(H)
        ksl = [slice(h * DK, (h + 1) * DK) for h in heads]
        vsl = [slice(h * DV, (h + 1) * DV) for h in heads]
        att_raw = [_dot_nt(qd_ref[rs, ksl[h]], kinv_ref[rs, ksl[h]]) for h in heads]
        ratt_raw = [_dot_nt(rq_ref[rs, ksl[h]], rks_ref[rs, ksl[h]]) for h in heads]
        qs_g = [_dot(qd_ref[rs, ksl[h]], sg_ref[h].astype(BF16)) for h in heads]
        qs_r = [_dot(rq_ref[rs, ksl[h]], sr_ref[h].astype(BF16)) for h in heads]
        for h in heads:
            att = jnp.where(sees, att_raw[h], 0.0).astype(BF16)
            o_g = _dot(att, gv_ref[rs, vsl[h]]) + qs_g[h]
            ratt = (ratt_raw[h] * decays[h]).astype(BF16)
            o_r = _dot(ratt, rv_ref[rs, vsl[h]]) + qs_r[h] * q_decs[h]
            rs_cols = slice(H * DV + h * DV, H * DV + (h + 1) * DV)
            if cfg.backward:
                for o, cols, gate_ref, ng_ref in ((o_g, vsl[h], gg_ref, gng_ref), (o_r, rs_cols, rg_ref, rng_ref)):
                    tot = o + of_ref[rs, cols]
                    mu = jnp.mean(tot, axis=-1, keepdims=True)
                    d = tot - mu
                    var = jnp.mean(d * d, axis=-1, keepdims=True)
                    yn = d * lax.rsqrt(var + NORM_EPS) * ng_ref[:, vsl[h]]
                    y_ref[rs, cols] = (_silu(gate_ref[rs, vsl[h]].astype(F32)) * yn).astype(y_ref.dtype)
            else:
                o_ref[rs, vsl[h]] = o_g
                o_ref[rs, rs_cols] = o_r
        for h in heads:
            dend = dend_ref[ksl[h], c:c + 1]
            sg_ref[h] = sg_ref[h] * dend + _dot_tn(kend_ref[rs, ksl[h]], gv_ref[rs, vsl[h]])
            sr_ref[h] = sr_ref[h] * c_decs[h] + _dot_tn(rkd_ref[rs, ksl[h]], rv_ref[rs, vsl[h]])

    @pl.when(last & is_ctx)
    def _():
        sgo_ref[0, 0] = sg_ref[...]
        sro_ref[0, 0] = sr_ref[...]


def _scan(cfg, z, glr, col, gate_w, gate_b, ret_decay_t, s0g, s0r, layer, extra):
    H, DK, DV, tb = cfg.h, cfg.dk, cfg.dv, cfg.tb
    M = z.shape[0]
    dirn = int(cfg.backward)

    def zspec(name):
        off, width = col[name]
        assert off % width == 0, (name, off, width)
        return pl.BlockSpec((tb, width), lambda i: (cfg.block(i), off // width))

    def full(shape):
        return pl.BlockSpec(shape, lambda i: (0,) * len(shape))

    state_in = pl.BlockSpec((1, 1, 1, H, DK, DV),
                            lambda i: (cfg.lat_seq(cfg.block(i)), layer, dirn, 0, 0, 0))
    state_out = pl.BlockSpec((1, 1, H, DK, DV), lambda i: (cfg.ctx_seq(cfg.block(i)), dirn, 0, 0, 0))
    row_blk = pl.BlockSpec((tb, 2 * H * DV), lambda i: (cfg.block(i), 0))
    glr_spec = pl.BlockSpec((tb, glr.shape[1]), lambda i: (cfg.block(i), 0))
    in_specs = [zspec("gq"), zspec("gk"), zspec("gv"), zspec("rq"), zspec("rk"), zspec("rv"), glr_spec,
                full(gate_w.shape), full(gate_b.shape), full(ret_decay_t.shape), state_in, state_in]
    args = [z, z, z, z, z, z, glr, gate_w, gate_b, ret_decay_t, s0g, s0r]
    state_shape = jax.ShapeDtypeStruct((cfg.batch, 2, H, DK, DV), F32)
    aliases = {}
    if cfg.backward:
        o_fwd, sg_fwd, sr_fwd, gla_ng, ret_ng = extra
        in_specs += [row_blk, zspec("gg"), zspec("rg"), full(gla_ng.shape), full(ret_ng.shape),
                     pl.BlockSpec(memory_space=pl.ANY), pl.BlockSpec(memory_space=pl.ANY)]
        args += [o_fwd, z, z, gla_ng, ret_ng, sg_fwd, sr_fwd]
        aliases = {len(args) - 2: 1, len(args) - 1: 2}
        out0 = jax.ShapeDtypeStruct((M, 2 * H * DV), BF16)
    else:
        out0 = jax.ShapeDtypeStruct((M, 2 * H * DV), F32)
    return pl.pallas_call(
        functools.partial(_scan_kernel, cfg),
        grid=(cfg.nblk,),
        in_specs=in_specs,
        out_specs=[row_blk, state_out, state_out],
        out_shape=[out0, state_shape, state_shape],
        input_output_aliases=aliases,
        scratch_shapes=[pltpu.VMEM((H, DK, DV), F32), pltpu.VMEM((H, DK, DV), F32),
                        pltpu.VMEM((H * DK, LANE), F32)] + [pltpu.VMEM((tb, H * DK), BF16)] * 5,
        compiler_params=_params("arbitrary"),
        name="scan_bwd" if cfg.backward else "scan_fwd",
    )(*args)


def _rope_keep(h, shape):
    lane = lax.broadcasted_iota(jnp.int32, shape, 1)
    return (lane < ROPE_DIM) if h % 2 == 0 else (lane >= ROPE_DIM)


def _mla_q_kernel(n_heads, scale, cq_ref, qng_ref, w_ref, cos_ref, sin_ref, o_ref):
    cqn = _rms(cq_ref[...], qng_ref[...]).astype(BF16)
    q = _dot(cqn, w_ref[...])
    nope_w = n_heads * NOPE_DIM
    rope_w = n_heads * ROPE_DIM
    cos, sin = cos_ref[...], sin_ref[...]
    for h in range(n_heads):
        p = h // 2
        tile = q[:, nope_w + p * LANE:nope_w + (p + 1) * LANE]
        partner = q[:, nope_w + rope_w + p * LANE:nope_w + rope_w + (p + 1) * LANE]
        rot = (tile * cos + partner * sin) * scale
        rot = jnp.where(_rope_keep(h, rot.shape), rot, 0.0)
        base = h * 2 * LANE
        o_ref[:, base:base + LANE] = (q[:, h * NOPE_DIM:(h + 1) * NOPE_DIM] * scale).astype(BF16)
        o_ref[:, base + LANE:base + 2 * LANE] = rot.astype(BF16)


def _mla_q(zc, q_ng, w_uq, cos_t, sin_t, n_heads, scale, tm):
    M = zc.shape[0]
    R = q_ng.shape[0]
    N = w_uq.shape[1]
    return pl.pallas_call(
        functools.partial(_mla_q_kernel, n_heads, scale),
        grid=(M // tm,),
        in_specs=[
            pl.BlockSpec((tm, R), lambda i: (i, 0)),
            pl.BlockSpec((1, R), lambda i: (0, 0)),
            pl.BlockSpec((R, N), lambda i: (0, 0)),
            pl.BlockSpec((tm, LANE), lambda i: (i, 0)),
            pl.BlockSpec((tm, LANE), lambda i: (i, 0)),
        ],
        out_specs=pl.BlockSpec((tm, n_heads * 2 * LANE), lambda i: (i, 0)),
        out_shape=jax.ShapeDtypeStruct((M, n_heads * 2 * LANE), BF16),
        compiler_params=_params("parallel"),
        name="mla_q",
    )(zc, q_ng.reshape(1, R), w_uq, cos_t, sin_t)


class _KeyRows:
    def __init__(self, rows, blk):
        self.rows, self.blk = rows, blk
        self.per_lat = 1 + rows.lat_len // blk
        self.lat_blocks = rows.n_lat * self.per_lat
        self.ctx_blocks = rows.mc // blk
        self.n = self.lat_blocks + self.ctx_blocks

    def is_cache(self, j):
        return (j < self.lat_blocks) & (j % self.per_lat == 0)

    def cache_idx(self, j):
        return jnp.clip(j // self.per_lat, 0, self.rows.n_lat - 1)

    def token_block(self, j):
        b = j // self.per_lat
        t = jnp.maximum(j % self.per_lat - 1, 0)
        lat = self.ctx_blocks + b * (self.per_lat - 1) + t
        return jnp.where(j < self.lat_blocks, lat, j - self.lat_blocks)


def _mla_kv_kernel(keys, n_heads, ckv_ref, kpe_ref, cckv_ref, ckpe_ref, kvng_ref, wk_ref, wvt_ref, cos_ref, sin_ref,
                   kcat_ref, vt_ref, ckvn_ref):
    cached = keys.is_cache(pl.program_id(0))
    kpe = kpe_ref[...]
    rot = kpe[:, :LANE] * cos_ref[...] + kpe[:, LANE:] * sin_ref[...]
    ckvn = jnp.where(cached, cckv_ref[...], _rms(ckv_ref[...], kvng_ref[...]))
    kr = jnp.where(cached, ckpe_ref[...], rot)
    ckvn_ref[...] = ckvn
    ckvn_b = ckvn.astype(BF16)
    kn = _dot(ckvn_b, wk_ref[...])
    vt_ref[...] = _dot_nt(wvt_ref[...], ckvn_b).astype(BF16)
    for h in range(n_heads):
        base = h * 2 * LANE
        kcat_ref[:, base:base + LANE] = kn[:, h * NOPE_DIM:(h + 1) * NOPE_DIM].astype(BF16)
        kcat_ref[:, base + LANE:base + 2 * LANE] = jnp.where(_rope_keep(h, kr.shape), kr, 0.0).astype(BF16)


def _mla_kv(keys, zc, zk, col, cache_ckv, cache_kpe2, kv_ng, w_k, w_vt, cos_t, sin_t, n_heads):
    blk = keys.blk
    R = kv_ng.shape[0]
    Mk = keys.n * blk
    ckv_off, ckv_w = col["ckv"]
    assert ckv_off % ckv_w == 0
    tok = keys.token_block
    return pl.pallas_call(
        functools.partial(_mla_kv_kernel, keys, n_heads),
        grid=(keys.n,),
        in_specs=[
            pl.BlockSpec((blk, ckv_w), lambda j: (tok(j), ckv_off // ckv_w)),
            pl.BlockSpec((blk, 2 * LANE), lambda j: (tok(j), 0)),
            pl.BlockSpec((blk, R), lambda j: (keys.cache_idx(j), 0)),
            pl.BlockSpec((blk, LANE), lambda j: (keys.cache_idx(j), 0)),
            pl.BlockSpec((1, R), lambda j: (0, 0)),
            pl.BlockSpec(w_k.shape, lambda j: (0, 0)),
            pl.BlockSpec(w_vt.shape, lambda j: (0, 0)),
            pl.BlockSpec((blk, LANE), lambda j: (tok(j), 0)),
            pl.BlockSpec((blk, LANE), lambda j: (tok(j), 0)),
        ],
        out_specs=[
            pl.BlockSpec((blk, n_heads * 2 * LANE), lambda j: (j, 0)),
            pl.BlockSpec((n_heads * V_DIM, blk), lambda j: (0, j)),
            pl.BlockSpec((blk, R), lambda j: (j, 0)),
        ],
        out_shape=[
            jax.ShapeDtypeStruct((Mk, n_heads * 2 * LANE), BF16),
            jax.ShapeDtypeStruct((n_heads * V_DIM, Mk), BF16),
            jax.ShapeDtypeStruct((Mk, R), F32),
        ],
        compiler_params=_params("parallel"),
        name="mla_kv",
    )(zc, zk, cache_ckv, cache_kpe2, kv_ng.reshape(1, R), w_k, w_vt, cos_t, sin_t)


ATTN_TQ = 256
ATTN_TK = 2048


def _attn_kernel(tq, chunks, q_ref, k_ref, vt_ref, o_ref, st0_ref, st1_ref):
    n_heads = q_ref.shape[1] // (2 * LANE)
    units = [(h, u) for h in range(n_heads) for u in range(q_ref.shape[0] // tq)]
    st_refs = (st0_ref, st1_ref)

    def scores(n, c0, c1):
        h, u = units[n]
        hd = slice(h * 2 * LANE, (h + 1) * 2 * LANE)
        st = _dot_nt(k_ref[c0:c1, hd], q_ref[u * tq:(u + 1) * tq, hd])
        st_refs[n % 2][c0:c1, :] = st
        return jnp.max(st, axis=0, keepdims=True)

    def weights(n, c0, c1, m):
        h, _ = units[n]
        p = jnp.exp2(st_refs[n % 2][c0:c1, :] - m)
        return jnp.sum(p, axis=0, keepdims=True), _dot(vt_ref[h * V_DIM:(h + 1) * V_DIM, c0:c1], p.astype(BF16))

    m_prev = None
    for n in range(len(units) + 1):
        m_new = acc = l = None
        for c0, c1 in chunks:
            if n < len(units):
                mc = scores(n, c0, c1)
                m_new = mc if m_new is None else jnp.maximum(m_new, mc)
            if n > 0:
                lc, pv = weights(n - 1, c0, c1, m_prev)
                l = lc if l is None else l + lc
                acc = pv if acc is None else acc + pv
        if n > 0:
            h, u = units[n - 1]
            o_ref[u * tq:(u + 1) * tq, h * V_DIM:(h + 1) * V_DIM] = (acc / l).T.astype(o_ref.dtype)
        m_prev = m_new


def _attn_kernel_aliased(tq, chunks, q_ref, k_ref, vt_ref, o_prev_ref, o_ref, st0_ref, st1_ref):
    del o_prev_ref
    _attn_kernel(tq, chunks, q_ref, k_ref, vt_ref, o_ref, st0_ref, st1_ref)


def _attention(qcat, kcat, vt, o_prev, n_seq, n_heads, hps, tq, q_blocks_per_seq, q_block0, tk, k_block0):
    sub_q = ATTN_TQ if tq % ATTN_TQ == 0 else LANE
    assert tq % sub_q == 0 and n_heads % hps == 0
    q_spec = pl.BlockSpec((tq, hps * 2 * LANE), lambda s, h, i: (q_block0 + s * q_blocks_per_seq + i, h))
    in_specs = [
        q_spec,
        pl.BlockSpec((tk, hps * 2 * LANE), lambda s, h, i: (k_block0 + s, h)),
        pl.BlockSpec((hps * V_DIM, tk), lambda s, h, i: (h, k_block0 + s)),
    ]
    args = [qcat, kcat, vt]
    chunks = tuple((c0, min(c0 + ATTN_TK, tk)) for c0 in range(0, tk, ATTN_TK))
    kern = functools.partial(_attn_kernel, sub_q, chunks)
    aliases = {}
    if o_prev is not None:
        in_specs.append(pl.BlockSpec(memory_space=pl.ANY))
        args.append(o_prev)
        aliases = {3: 0}
        kern = functools.partial(_attn_kernel_aliased, sub_q, chunks)
    return pl.pallas_call(
        kern,
        grid=(n_seq, n_heads // hps, q_blocks_per_seq),
        in_specs=in_specs,
        out_specs=pl.BlockSpec((tq, hps * V_DIM), lambda s, h, i: (q_block0 + s * q_blocks_per_seq + i, h)),
        out_shape=jax.ShapeDtypeStruct((qcat.shape[0], n_heads * V_DIM), BF16),
        input_output_aliases=aliases,
        scratch_shapes=[pltpu.VMEM((tk, sub_q), F32), pltpu.VMEM((tk, sub_q), F32)],
        compiler_params=_params("parallel", "parallel", "arbitrary"),
        name="mla_attn",
    )(*args)


def _rope_partner(w):
    q = ROPE_DIM // 4
    return jnp.flip(w.reshape(w.shape[:-1] + (2, 2, q)), axis=-2).reshape(w.shape)


def _rope_tables(rows):
    half, quarter = ROPE_DIM // 2, ROPE_DIM // 4
    inv = ROPE_BASE ** (-jnp.arange(quarter, dtype=F32) * 2.0 / half)
    t = jnp.arange(rows.lat_len)
    r = (t // GRID_W).astype(F32)
    c = (t % GRID_W).astype(F32)
    ang_r = r[:, None] * inv[None, :]
    ang_c = c[:, None] * inv[None, :]
    cos = jnp.concatenate([jnp.cos(ang_r), jnp.cos(ang_r), jnp.cos(ang_c), jnp.cos(ang_c)], axis=1)
    sin = jnp.concatenate([-jnp.sin(ang_r), jnp.sin(ang_r), -jnp.sin(ang_c), jnp.sin(ang_c)], axis=1)
    cos = jnp.tile(jnp.concatenate([cos, cos], axis=1), (rows.n_lat, 1))
    sin = jnp.tile(jnp.concatenate([sin, sin], axis=1), (rows.n_lat, 1))
    cos = jnp.concatenate([jnp.ones((rows.mc, LANE), F32), cos], axis=0)
    sin = jnp.concatenate([jnp.zeros((rows.mc, LANE), F32), sin], axis=0)
    return cos, sin


def kernel(x_prompt, x_sample, state_gla, state_ret, cache_ckv, cache_kpe, c, c_ctx, mod_w, mod_b, norm1_g, norm2_g, ab_w_in, gla_gate_w2, gla_gate_b, ret_decay, gla_norm_g, ret_norm_g, ab_w_out, mla_w_in, mla_q_norm_g, mla_w_uq, mla_kv_norm_g, mla_w_ukv, mla_w_out, ffn_w_in, ffn_conv, ffn_w_out, final_norm_g):
    B, S, D = x_prompt.shape
    NB, T, _ = x_sample.shape
    depth = mod_w.shape[0]
    _, _, _, HA, DK, DV = state_gla.shape
    HB = state_ret.shape[3]
    assert HA == HB and state_ret.shape[4:] == (DK, DV)
    GR = gla_gate_w2.shape[2]
    past = cache_ckv.shape[2]
    q_rank = mla_q_norm_g.shape[1]
    kv_rank = mla_kv_norm_g.shape[1]
    HC = mla_w_uq.shape[2] // (NOPE_DIM + ROPE_DIM)
    F = ffn_w_out.shape[1]
    rows = _Rows(B * S, S, T, NB)
    M = rows.m
    tm = min(512, S * B, T)
    assert rows.mc % tm == 0 and T % tm == 0
    assert S & (S - 1) == 0 and T & (T - 1) == 0

    tm_proj = min(1024, S * B, T)
    assert rows.mc % tm_proj == 0 and T % tm_proj == 0
    cond = jnp.concatenate([c_ctx[None, :], c, jnp.zeros((MOD_GROUPS - 1 - NB, D), F32)], axis=0)
    mod3 = _modulation(cond, mod_w, mod_b)

    x = (x_prompt.reshape(B * S, D), x_sample.reshape(NB * T, D))
    h = _normmod(rows, x, norm1_g[0], mod3[0], tm)
    ffn_w = (ffn_w_in.astype(BF16), ffn_conv, ffn_w_out.astype(BF16))
    new_gla, new_ret, new_ckv, new_kpe = [], [], [], []
    for l in range(depth):
        if l % 2 == 0:
            e = l // 2
            names = ("gq", "gk", "gv", "gg", "glr", "rq", "rk", "rv", "rg")
            sizes = (HA * DK, HA * DK, HA * DV, HA * DV, 2 * GR, HB * DK, HB * DK, HB * DV, HB * DV)
            src, o = {}, 0
            for n, sz in zip(names, sizes):
                src[n] = (o, sz)
                o += sz
            order = ("gq", "gk", "gv", "gg", "rq", "rk", "rv", "rg")
            col, o = {}, 0
            for n in order:
                col[n] = (o, src[n][1])
                o += src[n][1]
            so, sz = src["glr"]
            assert so % LANE == 0 and so + LANE <= ab_w_in.shape[2]
            w_all = lax.optimization_barrier(ab_w_in[e].astype(BF16))
            w_in = jnp.concatenate([w_all[:, :so], w_all[:, so + sz:]], axis=1)
            w_glr = w_all[:, so:so + LANE]
            z, glr = _inproj(h, w_in, w_glr, BF16, tm_proj)
            tb = min(256, S)
            extra = None
            for dirn in (0, 1):
                cfg = _ScanCfg(rows, B, tb, HA, DK, DV, backward=bool(dirn))
                gate_w = jnp.zeros((LANE, HA * DK), F32).at[dirn * GR:(dirn + 1) * GR].set(gla_gate_w2[e, dirn])
                rd = jnp.broadcast_to(ret_decay[e, dirn][:, None, None], (HB, 8, LANE))
                y, sg, sr = _scan(cfg, z, glr, col, gate_w.astype(BF16), gla_gate_b[e, dirn].reshape(1, -1), rd,
                                  state_gla, state_ret, e, extra)
                extra = (y, sg, sr, gla_norm_g[e].reshape(1, -1), ret_norm_g[e].reshape(1, -1))
            new_gla.append(sg)
            new_ret.append(sr)
            x, h = _outproj(rows, y, ab_w_out[e].astype(BF16), x, norm2_g[l], mod3[l], tm)
        else:
            i = l // 2
            w = mla_w_in[i]
            w_kpe = w[:, q_rank + kv_rank:]
            w_in = w[:, :q_rank + kv_rank].astype(BF16)
            w_kpe_p = _rope_partner(w_kpe)
            w_side = jnp.concatenate([w_kpe, w_kpe, w_kpe_p, w_kpe_p], axis=1).astype(BF16)
            col = {"cq": (0, q_rank), "ckv": (q_rank, kv_rank)}
            zc, zk = _inproj(h, w_in, w_side, F32, tm_proj)
            cos_t, sin_t = _rope_tables(rows)
            wq = mla_w_uq[i].reshape(q_rank, HC, NOPE_DIM + ROPE_DIM)
            wq_rope = wq[:, :, NOPE_DIM:]
            w_uq = jnp.concatenate([wq[:, :, :NOPE_DIM].reshape(q_rank, -1), wq_rope.reshape(q_rank, -1),
                                    _rope_partner(wq_rope).reshape(q_rank, -1)], axis=1).astype(BF16)
            scale = (NOPE_DIM + ROPE_DIM) ** -0.5 * LOG2_E
            assert col["cq"][0] == 0
            qcat = _mla_q(zc, mla_q_norm_g[i], w_uq, cos_t, sin_t, HC, scale, tm)
            wkv = mla_w_ukv[i].reshape(kv_rank, HC, NOPE_DIM + V_DIM)
            w_k = wkv[:, :, :NOPE_DIM].reshape(kv_rank, -1).astype(BF16)
            w_vt = wkv[:, :, NOPE_DIM:].reshape(kv_rank, -1).T.astype(BF16)
            assert S % past == 0 and T % past == 0
            keys = _KeyRows(rows, past)
            ckpe = cache_kpe[:, i].reshape(NB * past, ROPE_DIM)
            kcat, vt, ckvn = _mla_kv(keys, zc, zk, col, cache_ckv[:, i].reshape(NB * past, kv_rank),
                                     jnp.concatenate([ckpe, ckpe], axis=1), mla_kv_norm_g[i], w_k, w_vt,
                                     cos_t, sin_t, HC)
            tk_lat = past + T
            assert (keys.lat_blocks * past) % S == 0
            tq = min(2048, T)
            assert rows.mc % tq == 0
            o = _attention(qcat, kcat, vt, None, B, HC, HC, S, 1, 0, S, keys.lat_blocks * past // S)
            o = _attention(qcat, kcat, vt, o, NB, HC, 1, tq, T // tq, rows.mc // tq, tk_lat, 0)
            new_ckv.append(ckvn[keys.lat_blocks * past:].reshape(B, S, kv_rank))
            new_kpe.append(zk[:rows.mc, :ROPE_DIM].reshape(B, S, ROPE_DIM))
            x, h = _outproj(rows, o, mla_w_out[i].astype(BF16), x, norm2_g[l], mod3[l], tm)
        tf = _largest_tile(F, 512)
        if l == depth - 1:
            y_ctx, y_lat = _ffn(rows, h, x, mod3[l], l, *ffn_w, tm, tf, final_g=final_norm_g)
        else:
            x, h = _ffn(rows, h, x, mod3[l], l, *ffn_w, tm, tf, next_norm=(norm1_g[l + 1], mod3[l + 1]))

    y_prompt = y_ctx.reshape(B, S, D)
    y_sample = y_lat.reshape(NB, T, D)
    return (y_prompt, y_sample, jnp.stack(new_gla, axis=1), jnp.stack(new_ret, axis=1),
            jnp.stack(new_ckv, axis=1), jnp.stack(new_kpe, axis=1))
```

```python
import functools

import jax
import jax.numpy as jnp
from jax import lax
from jax.experimental import pallas as pl
from jax.experimental.pallas import tpu as pltpu

F32 = jnp.float32
BF16 = jnp.bfloat16

NORM_EPS = 1e-6
GATE_TEMP = 16.0
CHUNK = 64
GRID_W = 64
ROPE_BASE = 10000.0
ROPE_DIM = 64
NOPE_DIM = 128
V_DIM = 128
CONV_W = 3
LOG2_E = 1.4426950408889634

LANE = 128
BF16_SUBLANE = 16
MOD_GROUPS = 8
VMEM_LIMIT = 56 * 1024 * 1024

NT_DIMS = (((1,), (1,)), ((), ()))
TN_DIMS = (((0,), (0,)), ((), ()))


def _dot(a, b):
    return jnp.dot(a, b, preferred_element_type=F32)


def _dot_nt(a, b):
    return lax.dot_general(a, b, NT_DIMS, preferred_element_type=F32)


def _dot_tn(a, b):
    return lax.dot_general(a, b, TN_DIMS, preferred_element_type=F32)


def _params(*sem):
    return pltpu.CompilerParams(dimension_semantics=sem, vmem_limit_bytes=VMEM_LIMIT)


def _rms(x, g):
    ms = jnp.mean(x * x, axis=-1, keepdims=True)
    return (x * lax.rsqrt(ms + NORM_EPS)) * g


def _silu(x):
    return x * jax.nn.sigmoid(x)


def _log_sigmoid(x):
    return jnp.minimum(x, 0.0) - jnp.log(1.0 + jnp.exp(-jnp.abs(x)))


def _largest_tile(n, cap):
    best = None
    for t in range(LANE, min(n, cap) + 1, LANE):
        if n % t == 0:
            best = t
    assert best is not None, (n, cap)
    return best


def _mod_kernel(c_ref, w_ref, b_ref, o_ref):
    s = _silu(c_ref[...]).astype(BF16)
    m = _dot(s, w_ref[0].astype(BF16)) + b_ref[0]
    for g in range(MOD_GROUPS):
        o_ref[0, g, 0] = m[g:g + 1]


def _modulation(cond, mod_w, mod_b):
    L, D, N = mod_w.shape
    n_vec = N // D
    out = pl.pallas_call(
        _mod_kernel,
        grid=(L, n_vec),
        in_specs=[
            pl.BlockSpec((MOD_GROUPS, D), lambda l, k: (0, 0)),
            pl.BlockSpec((1, D, D), lambda l, k: (l, 0, k)),
            pl.BlockSpec((1, 1, D), lambda l, k: (l, 0, k)),
        ],
        out_specs=pl.BlockSpec((1, MOD_GROUPS, 1, 1, D), lambda l, k: (l, 0, k, 0, 0)),
        out_shape=jax.ShapeDtypeStruct((L, MOD_GROUPS, n_vec, 1, D), F32),
        compiler_params=_params("parallel", "parallel"),
        name="modulation",
    )(cond, mod_w, mod_b.reshape(L, 1, N))
    return out.reshape(L, MOD_GROUPS * n_vec, 1, D)


class _Rows:
    def __init__(self, mc, seq, lat_len, n_lat):
        self.mc, self.seq, self.lat_len, self.n_lat = mc, seq, lat_len, n_lat
        self.m = mc + lat_len * n_lat

    def group(self, i, tm):
        r = i * tm
        return jnp.where(r < self.mc, 0, 1 + (r - self.mc) // self.lat_len)

    def mod_spec(self, which, tm, d):
        return pl.BlockSpec((1, 1, d), lambda i, *_: (self.group(i, tm) * 6 + which, 0, 0))

    def split_specs(self, tm, d):
        n_ctx = self.mc // tm
        return [pl.BlockSpec((tm, d), lambda i, *_: (jnp.minimum(i, n_ctx - 1), 0)),
                pl.BlockSpec((tm, d), lambda i, *_: (jnp.maximum(i - n_ctx, 0), 0))]


ROW_CHUNK = 16
ROW_PARTS = 2


def _for_row_chunks(n_rows, body):
    def it(c, carry):
        body(pl.ds(pl.multiple_of(c * ROW_CHUNK, ROW_CHUNK), ROW_CHUNK))
        return carry
    lax.fori_loop(0, n_rows // ROW_CHUNK, it, 0, unroll=4)


def _fold_gain(gs_ref, g_ref, sc_ref):
    gs_ref[...] = g_ref[...] * (1.0 + sc_ref[0])


def _norm_mod_rows(x, gs_ref, sh_ref):
    ms = jnp.mean(x * x, axis=-1, keepdims=True)
    return (x * lax.rsqrt(ms + NORM_EPS)) * gs_ref[...] + sh_ref[0]


def _normmod_kernel(rows, tm, xa_ref, xb_ref, g_ref, sh_ref, sc_ref, h_ref, gs_ref):
    _fold_gain(gs_ref, g_ref, sc_ref)

    def run(x_ref):
        def body(r):
            h_ref[r, :] = _norm_mod_rows(x_ref[r, :], gs_ref, sh_ref).astype(h_ref.dtype)
        _for_row_chunks(tm, body)

    is_ctx = pl.program_id(0) * tm < rows.mc
    pl.when(is_ctx)(lambda: run(xa_ref))
    pl.when(jnp.logical_not(is_ctx))(lambda: run(xb_ref))


def _normmod(rows, x_pair, g, mod3, tm):
    D = x_pair[0].shape[1]
    return pl.pallas_call(
        functools.partial(_normmod_kernel, rows, tm),
        grid=(rows.m // tm,),
        in_specs=rows.split_specs(tm, D) + [
            pl.BlockSpec((1, D), lambda i: (0, 0)),
            rows.mod_spec(0, tm, D),
            rows.mod_spec(1, tm, D),
        ],
        out_specs=pl.BlockSpec((tm, D), lambda i: (i, 0)),
        out_shape=jax.ShapeDtypeStruct((rows.m, D), BF16),
        scratch_shapes=[pltpu.VMEM((1, D), F32)],
        compiler_params=_params("parallel"),
        name="normmod",
    )(*x_pair, g.reshape(1, D), mod3, mod3)


def _inproj_kernel(h_ref, w_ref, ws_ref, o_ref, os_ref):
    @pl.when(pl.program_id(1) == 0)
    def _():
        os_ref[...] = _dot(h_ref[...], ws_ref[...])

    o_ref[...] = _dot(h_ref[...], w_ref[...]).astype(o_ref.dtype)


def _inproj(h, w, w_side, out_dtype, tm, tn_cap=2048):
    M, D = h.shape
    N = w.shape[1]
    NS = w_side.shape[1]
    tn = _largest_tile(N, tn_cap)
    return pl.pallas_call(
        _inproj_kernel,
        grid=(M // tm, N // tn),
        in_specs=[
            pl.BlockSpec((tm, D), lambda i, j: (i, 0)),
            pl.BlockSpec((D, tn), lambda i, j: (0, j)),
            pl.BlockSpec((D, NS), lambda i, j: (0, 0)),
        ],
        out_specs=[pl.BlockSpec((tm, tn), lambda i, j: (i, j)), pl.BlockSpec((tm, NS), lambda i, j: (i, 0))],
        out_shape=[jax.ShapeDtypeStruct((M, N), out_dtype), jax.ShapeDtypeStruct((M, NS), F32)],
        compiler_params=_params("parallel", "arbitrary"),
        name="inproj",
    )(h, w, w_side)


def _outproj_kernel(rows, tm, split, y_ref, w_ref, *rest):
    if split:
        xa_ref, xb_ref, gate_ref, g_ref, sh_ref, sc_ref, o_ref, h_ref, acc_ref, gs_ref = rest
    else:
        xa_ref, gate_ref, g_ref, sh_ref, sc_ref, o_ref, h_ref, acc_ref, gs_ref = rest
        xb_ref = xa_ref
    _fold_gain(gs_ref, g_ref, sc_ref)
    is_ctx = pl.program_id(0) * tm < rows.mc
    part = tm // ROW_PARTS

    def finish(r):
        x = jnp.where(is_ctx, xa_ref[r, :], xb_ref[r, :]) if split else xa_ref[r, :]
        x1 = x + gate_ref[0] * acc_ref[r, :]
        o_ref[r, :] = x1
        h_ref[r, :] = _norm_mod_rows(x1, gs_ref, sh_ref).astype(h_ref.dtype)

    for p in range(ROW_PARTS + 1):
        if p < ROW_PARTS:
            rp = slice(p * part, (p + 1) * part)
            acc_ref[rp, :] = _dot(y_ref[rp, :], w_ref[...])
        if p > 0:
            for c in range(part // ROW_CHUNK):
                start = (p - 1) * part + c * ROW_CHUNK
                finish(slice(start, start + ROW_CHUNK))


def _outproj(rows, y, w, x, g, mod3, tm):
    M, K = y.shape
    N = w.shape[1]
    split = isinstance(x, (tuple, list))
    x_specs = rows.split_specs(tm, N) if split else [pl.BlockSpec((tm, N), lambda i: (i, 0))]
    x_args = list(x) if split else [x]
    return pl.pallas_call(
        functools.partial(_outproj_kernel, rows, tm, split),
        grid=(M // tm,),
        in_specs=[
            pl.BlockSpec((tm, K), lambda i: (i, 0)),
            pl.BlockSpec((K, N), lambda i: (0, 0)),
        ] + x_specs + [
            rows.mod_spec(2, tm, N),
            pl.BlockSpec((1, N), lambda i: (0, 0)),
            rows.mod_spec(3, tm, N),
            rows.mod_spec(4, tm, N),
        ],
        out_specs=[pl.BlockSpec((tm, N), lambda i: (i, 0)), pl.BlockSpec((tm, N), lambda i: (i, 0))],
        out_shape=[jax.ShapeDtypeStruct((M, N), F32), jax.ShapeDtypeStruct((M, N), BF16)],
        scratch_shapes=[pltpu.VMEM((tm, N), F32), pltpu.VMEM((1, N), F32)],
        compiler_params=_params("parallel"),
        name="outproj",
    )(y, w, *x_args, mod3, g.reshape(1, N), mod3, mod3)


HALO = BF16_SUBLANE


def _ffn_kernel(rows, tm, final_norm, hp_ref, h_ref, hn_ref, x_ref, gate_ref,
                wa_ref, wb_ref, cw_ref, wo_ref, *rest):
    if final_norm:
        fg_ref, o_ctx_ref, o_lat_ref, hs_ref, a_ref, acc_ref = rest
    else:
        g_ref, sh_ref, sc_ref, o_ref, hnext_ref, hs_ref, a_ref, acc_ref, gs_ref = rest
    i = pl.program_id(0)
    f = pl.program_id(1)

    @pl.when(f == 0)
    def _():
        hs_ref[0:tm, :] = h_ref[...]
        last_row = lax.broadcasted_iota(jnp.int32, (HALO, 1), 0) == HALO - 1
        hs_ref[tm:, :] = jnp.where(last_row, hp_ref[...].astype(F32), hn_ref[...].astype(F32)).astype(BF16)
        acc_ref[...] = jnp.zeros_like(acc_ref)

    def hidden_tile():
        a = _dot(hs_ref[...], wa_ref[0])
        a_ref[HALO:, :] = a
        a_ref[HALO - 8:HALO, :] = a[tm + HALO - 8:, :]
        b = _dot(h_ref[...], wb_ref[0])
        row = i * tm + lax.broadcasted_iota(jnp.int32, (tm, 1), 0)
        pos = jnp.where(row < rows.mc, row & (rows.seq - 1), (row - rows.mc) & (rows.lat_len - 1))
        seq_len = jnp.where(row < rows.mc, rows.seq, rows.lat_len)
        a_prev = jnp.where(pos == 0, 0.0, a_ref[pl.ds(HALO - 1, tm), :])
        a_next = jnp.where(pos == seq_len - 1, 0.0, a_ref[pl.ds(HALO + 1, tm), :])
        a_mid = a_ref[pl.ds(HALO, tm), :]
        cw = cw_ref[0]
        a = cw[0:1] * a_prev + cw[1:2] * a_mid + cw[2:3] * a_next
        return (_silu(a) * b).astype(BF16)

    def residual(r):
        return x_ref[r, :] + gate_ref[0] * acc_ref[r, :]

    def last_step(finish):
        act = hidden_tile()
        part = tm // ROW_PARTS
        for p in range(ROW_PARTS + 1):
            if p < ROW_PARTS:
                rp = slice(p * part, (p + 1) * part)
                acc_ref[rp, :] += _dot(act[rp, :], wo_ref[0])
            if p > 0:
                for c in range(part // ROW_CHUNK):
                    start = (p - 1) * part + c * ROW_CHUNK
                    finish(slice(start, start + ROW_CHUNK))

    last = pl.num_programs(1) - 1

    @pl.when(f < last)
    def _():
        acc_ref[...] += _dot(hidden_tile(), wo_ref[0])

    if final_norm:
        def finish_to(out_ref):
            def finish(r):
                out_ref[r, :] = _rms(residual(r), fg_ref[...])
            return finish

        is_ctx = i * tm < rows.mc
        pl.when((f == last) & is_ctx)(lambda: last_step(finish_to(o_ctx_ref)))
        pl.when((f == last) & jnp.logical_not(is_ctx))(lambda: last_step(finish_to(o_lat_ref)))
    else:
        def finish(r):
            x2 = residual(r)
            o_ref[r, :] = x2
            hnext_ref[r, :] = _norm_mod_rows(x2, gs_ref, sh_ref).astype(hnext_ref.dtype)

        @pl.when(f == last)
        def _():
            _fold_gain(gs_ref, g_ref, sc_ref)
            last_step(finish)


def _ffn(rows, h, x, mod3, layer, w_in, conv_w, w_out, tm, tf, next_norm=None, final_g=None):
    M, D = x.shape
    F = w_out.shape[1]
    nf = F // tf
    nhalo = M // HALO
    final_norm = final_g is not None
    kern = functools.partial(_ffn_kernel, rows, tm, final_norm)
    in_specs = [
        pl.BlockSpec((HALO, D), lambda i, f: (jnp.maximum(i * (tm // HALO) - 1, 0), 0)),
        pl.BlockSpec((tm, D), lambda i, f: (i, 0)),
        pl.BlockSpec((HALO, D), lambda i, f: (jnp.minimum((i + 1) * (tm // HALO), nhalo - 1), 0)),
        pl.BlockSpec((tm, D), lambda i, f: (i, 0)),
        rows.mod_spec(5, tm, D),
        pl.BlockSpec((1, D, tf), lambda i, f: (layer, 0, f)),
        pl.BlockSpec((1, D, tf), lambda i, f: (layer, 0, nf + f)),
        pl.BlockSpec((1, CONV_W, tf), lambda i, f: (layer, 0, f)),
        pl.BlockSpec((1, tf, D), lambda i, f: (layer, f, 0)),
    ]
    args = [h, h, h, x, mod3, w_in, w_in, conv_w, w_out]
    vec = pl.BlockSpec((1, D), lambda i, f: (0, 0))
    if final_norm:
        n_ctx = rows.mc // tm
        in_specs += [vec]
        args += [final_g.reshape(1, D)]
        out_specs = [pl.BlockSpec((tm, D), lambda i, f: (jnp.minimum(i, n_ctx - 1), 0)),
                     pl.BlockSpec((tm, D), lambda i, f: (jnp.maximum(i - n_ctx, 0), 0))]
        out_shape = [jax.ShapeDtypeStruct((rows.mc, D), F32), jax.ShapeDtypeStruct((M - rows.mc, D), F32)]
        row_sem = "arbitrary"
    else:
        g_next, mod3_next = next_norm
        in_specs += [vec, rows.mod_spec(0, tm, D), rows.mod_spec(1, tm, D)]
        args += [g_next.reshape(1, D), mod3_next, mod3_next]
        out_specs = [pl.BlockSpec((tm, D), lambda i, f: (i, 0)), pl.BlockSpec((tm, D), lambda i, f: (i, 0))]
        out_shape = [jax.ShapeDtypeStruct((M, D), F32), jax.ShapeDtypeStruct((M, D), BF16)]
        row_sem = "parallel"
    return pl.pallas_call(
        kern,
        grid=(M // tm, nf),
        in_specs=in_specs,
        out_specs=out_specs,
        out_shape=out_shape,
        scratch_shapes=[
            pltpu.VMEM((tm + HALO, D), BF16),
            pltpu.VMEM((tm + 2 * HALO, tf), F32),
            pltpu.VMEM((tm, D), F32),
        ] + ([] if final_norm else [pltpu.VMEM((1, D), F32)]),
        compiler_params=_params(row_sem, "arbitrary"),
        name="convffn",
    )(*args)


class _ScanCfg:
    def __init__(self, rows, batch, tb, h, dk, dv, backward):
        self.rows, self.batch, self.tb, self.h, self.dk, self.dv = rows, batch, tb, h, dk, dv
        self.backward = backward
        self.cps = rows.seq // tb
        self.lps = rows.lat_len // tb
        self.ctx_blocks = batch * self.cps
        self.nblk = rows.m // tb

    def block(self, i):
        return self.nblk - 1 - i if self.backward else i

    def is_ctx(self, r):
        return r < self.ctx_blocks

    def seq_pos(self, r):
        ctx = self.is_ctx(r)
        return (jnp.where(ctx, r % self.cps, (r - self.ctx_blocks) % self.lps),
                jnp.where(ctx, self.cps, self.lps))

    def lat_seq(self, r):
        return jnp.clip((r - self.ctx_blocks) // self.lps, 0, self.rows.n_lat - 1)

    def ctx_seq(self, r):
        return jnp.clip(r // self.cps, 0, self.batch - 1)


def _scan_kernel(cfg, gq_ref, gk_ref, gv_ref, rq_ref, rk_ref, rv_ref, glr_ref, gw_ref, gb_ref, rd_ref,
                 s0g_ref, s0r_ref, *rest):
    scratch = rest[-8:]
    sg_ref, sr_ref, dend_ref, qd_ref, kinv_ref, kend_ref, rks_ref, rkd_ref = scratch
    if cfg.backward:
        of_ref, gg_ref, rg_ref, gng_ref, rng_ref, _, _, y_ref, sgo_ref, sro_ref = rest[:-8]
    else:
        o_ref, sgo_ref, sro_ref = rest[:-8]
    H, DK, DV, C = cfg.h, cfg.dk, cfg.dv, CHUNK
    r = cfg.block(pl.program_id(0))
    blk, nblk_seq = cfg.seq_pos(r)
    first = blk == (nblk_seq - 1 if cfg.backward else 0)
    last = blk == (0 if cfg.backward else nblk_seq - 1)
    is_ctx = cfg.is_ctx(r)

    @pl.when(first & is_ctx)
    def _():
        sg_ref[...] = jnp.zeros_like(sg_ref)
        sr_ref[...] = jnp.zeros_like(sr_ref)

    @pl.when(first & jnp.logical_not(is_ctx))
    def _():
        sg_ref[...] = s0g_ref[0, 0, 0]
        sr_ref[...] = s0r_ref[0, 0, 0]

    ti = lax.broadcasted_iota(jnp.int32, (C, C), 0)
    tj = lax.broadcasted_iota(jnp.int32, (C, C), 1)
    sees = (tj >= ti) if cfg.backward else (tj <= ti)
    dist = jnp.abs(ti - tj).astype(F32)
    rowi = lax.broadcasted_iota(jnp.int32, (C, LANE), 0)
    to_end = (rowi if cfg.backward else C - 1 - rowi).astype(F32)
    from_start = (C - rowi if cfg.backward else rowi + 1).astype(F32)

    n_chunks = cfg.tb // C
    HDK = H * DK
    scale = DK ** -0.5

    bi = lax.broadcasted_iota(jnp.int32, (cfg.tb, cfg.tb), 0)
    bj = lax.broadcasted_iota(jnp.int32, (cfg.tb, cfg.tb), 1)
    same_chunk = (bi & -C) == (bj & -C)
    blk_tri = jnp.where(same_chunk & ((bj >= bi) if cfg.backward else (bj <= bi)), 1.0, 0.0).astype(BF16)
    pre = _dot(glr_ref[...].astype(BF16), gw_ref[...]) + gb_ref[...]
    la = _log_sigmoid(pre) / GATE_TEMP
    la_hi = la.astype(BF16)
    la_lo = (la - la_hi.astype(F32)).astype(BF16)
    b = _dot(blk_tri, la_hi) + _dot(blk_tri, la_lo)
    b3 = b.reshape(n_chunks, C, HDK)
    b_end = b3[:, 0:1, :] if cfg.backward else b3[:, C - 1:C, :]
    eb = jnp.exp(b)
    chunk_of_row = jnp.right_shift(lax.broadcasted_iota(jnp.int32, (cfg.tb, LANE), 0), C.bit_length() - 1)
    lane = lax.broadcasted_iota(jnp.int32, (cfg.tb, LANE), 1)
    in_chunk = jnp.where(chunk_of_row == lane, 1.0, 0.0).astype(BF16)
    dend = jnp.exp(_dot_tn(la_hi, in_chunk) + _dot_tn(la_lo, in_chunk))
    for c in range(n_chunks):
        dend_ref[:, c * LANE:(c + 1) * LANE] = jnp.broadcast_to(dend[:, c:c + 1], (HDK, LANE))
    gk = gk_ref[...].astype(F32)
    qd_ref[...] = (gq_ref[...].astype(F32) * scale * eb).astype(BF16)
    kinv_ref[...] = (gk * jnp.exp(-b)).astype(BF16)
    kend_ref[...] = (gk * jnp.exp(b_end - b3).reshape(cfg.tb, HDK)).astype(BF16)

    decays, q_decs, c_decs, k_dec_cols = [], [], [], []
    for h in range(H):
        lg = _log_sigmoid(rd_ref[h])
        lg_c = jnp.broadcast_to(lg[0:1, 0:C], (C, C))
        lg_l = jnp.broadcast_to(lg[0:1, :], (C, LANE))
        decays.append(jnp.where(sees, jnp.exp(lg_c * dist), 0.0))
        q_decs.append(jnp.concatenate([jnp.exp(lg_l * from_start)] * (DV // LANE), axis=1))
        c_decs.append(jnp.concatenate([jnp.exp(lg[0:1, :] * float(C))] * (DV // LANE), axis=1))
        k_dec_cols.append(jnp.concatenate([jnp.exp(lg_l * to_end)] * n_chunks, axis=0))
    rk = rk_ref[...].astype(F32) * scale
    rks_ref[...] = rk.astype(BF16)
    rkd_ref[...] = (rk * jnp.concatenate(k_dec_cols, axis=1)).astype(BF16)

    for ci in range(n_chunks):
        c = n_chunks - 1 - ci if cfg.backward else ci
        rs = slice(c * C, (c + 1) * C)
        heads = range(H)
        ksl = [slice(h * DK, (h + 1) * DK) for h in heads]
        vsl = [slice(h * DV, (h + 1) * DV) for h in heads]
        att_raw = [_dot_nt(qd_ref[rs, ksl[h]], kinv_ref[rs, ksl[h]]) for h in heads]
        ratt_raw = [_dot_nt(rq_ref[rs, ksl[h]], rks_ref[rs, ksl[h]]) for h in heads]
        qs_g = [_dot(qd_ref[rs, ksl[h]], sg_ref[h].astype(BF16)) for h in heads]
        qs_r = [_dot(rq_ref[rs, ksl[h]], sr_ref[h].astype(BF16)) for h in heads]
        for h in heads:
            att = jnp.where(sees, att_raw[h], 0.0).astype(BF16)
            o_g = _dot(att, gv_ref[rs, vsl[h]]) + qs_g[h]
            ratt = (ratt_raw[h] * decays[h]).astype(BF16)
            o_r = _dot(ratt, rv_ref[rs, vsl[h]]) + qs_r[h] * q_decs[h]
            rs_cols = slice(H * DV + h * DV, H * DV + (h + 1) * DV)
            if cfg.backward:
                for o, cols, gate_ref, ng_ref in ((o_g, vsl[h], gg_ref, gng_ref), (o_r, rs_cols, rg_ref, rng_ref)):
                    tot = o + of_ref[rs, cols]
                    mu = jnp.mean(tot, axis=-1, keepdims=True)
                    d = tot - mu
                    var = jnp.mean(d * d, axis=-1, keepdims=True)
                    yn = d * lax.rsqrt(var + NORM_EPS) * ng_ref[:, vsl[h]]
                    y_ref[rs, cols] = (_silu(gate_ref[rs, vsl[h]].astype(F32)) * yn).astype(y_ref.dtype)
            else:
                o_ref[rs, vsl[h]] = o_g
                o_ref[rs, rs_cols] = o_r
        for h in heads:
            dend = dend_ref[ksl[h], c * LANE:(c + 1) * LANE]
            sg_ref[h] = (sg_ref[h] * jnp.concatenate([dend] * (DV // LANE), axis=1)
                         + _dot_tn(kend_ref[rs, ksl[h]], gv_ref[rs, vsl[h]]))
            sr_ref[h] = sr_ref[h] * c_decs[h] + _dot_tn(rkd_ref[rs, ksl[h]], rv_ref[rs, vsl[h]])

    @pl.when(last & is_ctx)
    def _():
        sgo_ref[0, 0] = sg_ref[...]
        sro_ref[0, 0] = sr_ref[...]


def _scan(cfg, z, glr, col, gate_w, gate_b, ret_decay_t, s0g, s0r, layer, extra):
    H, DK, DV, tb = cfg.h, cfg.dk, cfg.dv, cfg.tb
    M = z.shape[0]
    dirn = int(cfg.backward)

    def zspec(name):
        off, width = col[name]
        assert off % width == 0, (name, off, width)
        return pl.BlockSpec((tb, width), lambda i: (cfg.block(i), off // width))

    def full(shape):
        return pl.BlockSpec(shape, lambda i: (0,) * len(shape))

    state_in = pl.BlockSpec((1, 1, 1, H, DK, DV),
                            lambda i: (cfg.lat_seq(cfg.block(i)), layer, dirn, 0, 0, 0))
    state_out = pl.BlockSpec((1, 1, H, DK, DV), lambda i: (cfg.ctx_seq(cfg.block(i)), dirn, 0, 0, 0))
    row_blk = pl.BlockSpec((tb, 2 * H * DV), lambda i: (cfg.block(i), 0))
    glr_spec = pl.BlockSpec((tb, glr.shape[1]), lambda i: (cfg.block(i), 0))
    in_specs = [zspec("gq"), zspec("gk"), zspec("gv"), zspec("rq"), zspec("rk"), zspec("rv"), glr_spec,
                full(gate_w.shape), full(gate_b.shape), full(ret_decay_t.shape), state_in, state_in]
    args = [z, z, z, z, z, z, glr, gate_w, gate_b, ret_decay_t, s0g, s0r]
    state_shape = jax.ShapeDtypeStruct((cfg.batch, 2, H, DK, DV), F32)
    aliases = {}
    if cfg.backward:
        o_fwd, sg_fwd, sr_fwd, gla_ng, ret_ng = extra
        in_specs += [row_blk, zspec("gg"), zspec("rg"), full(gla_ng.shape), full(ret_ng.shape),
                     pl.BlockSpec(memory_space=pl.ANY), pl.BlockSpec(memory_space=pl.ANY)]
        args += [o_fwd, z, z, gla_ng, ret_ng, sg_fwd, sr_fwd]
        aliases = {len(args) - 2: 1, len(args) - 1: 2}
        out0 = jax.ShapeDtypeStruct((M, 2 * H * DV), BF16)
    else:
        out0 = jax.ShapeDtypeStruct((M, 2 * H * DV), F32)
    return pl.pallas_call(
        functools.partial(_scan_kernel, cfg),
        grid=(cfg.nblk,),
        in_specs=in_specs,
        out_specs=[row_blk, state_out, state_out],
        out_shape=[out0, state_shape, state_shape],
        input_output_aliases=aliases,
        scratch_shapes=[pltpu.VMEM((H, DK, DV), F32), pltpu.VMEM((H, DK, DV), F32),
                        pltpu.VMEM((H * DK, (tb // CHUNK) * LANE), F32)] + [pltpu.VMEM((tb, H * DK), BF16)] * 5,
        compiler_params=_params("arbitrary"),
        name="scan_bwd" if cfg.backward else "scan_fwd",
    )(*args)


def _rope_keep(h, shape):
    lane = lax.broadcasted_iota(jnp.int32, shape, 1)
    return (lane < ROPE_DIM) if h % 2 == 0 else (lane >= ROPE_DIM)


def _mla_q_kernel(n_heads, scale, cq_ref, qng_ref, w_ref, cos_ref, sin_ref, o_ref):
    cqn = _rms(cq_ref[...], qng_ref[...]).astype(BF16)
    q = _dot(cqn, w_ref[...])
    nope_w = n_heads * NOPE_DIM
    rope_w = n_heads * ROPE_DIM
    cos, sin = cos_ref[...], sin_ref[...]
    for h in range(n_heads):
        p = h // 2
        tile = q[:, nope_w + p * LANE:nope_w + (p + 1) * LANE]
        partner = q[:, nope_w + rope_w + p * LANE:nope_w + rope_w + (p + 1) * LANE]
        rot = (tile * cos + partner * sin) * scale
        rot = jnp.where(_rope_keep(h, rot.shape), rot, 0.0)
        base = h * 2 * LANE
        o_ref[:, base:base + LANE] = (q[:, h * NOPE_DIM:(h + 1) * NOPE_DIM] * scale).astype(BF16)
        o_ref[:, base + LANE:base + 2 * LANE] = rot.astype(BF16)


def _mla_q(zc, q_ng, w_uq, cos_t, sin_t, n_heads, scale, tm):
    M = zc.shape[0]
    R = q_ng.shape[0]
    N = w_uq.shape[1]
    return pl.pallas_call(
        functools.partial(_mla_q_kernel, n_heads, scale),
        grid=(M // tm,),
        in_specs=[
            pl.BlockSpec((tm, R), lambda i: (i, 0)),
            pl.BlockSpec((1, R), lambda i: (0, 0)),
            pl.BlockSpec((R, N), lambda i: (0, 0)),
            pl.BlockSpec((tm, LANE), lambda i: (i, 0)),
            pl.BlockSpec((tm, LANE), lambda i: (i, 0)),
        ],
        out_specs=pl.BlockSpec((tm, n_heads * 2 * LANE), lambda i: (i, 0)),
        out_shape=jax.ShapeDtypeStruct((M, n_heads * 2 * LANE), BF16),
        compiler_params=_params("parallel"),
        name="mla_q",
    )(zc, q_ng.reshape(1, R), w_uq, cos_t, sin_t)


class _KeyRows:
    def __init__(self, rows, blk):
        self.rows, self.blk = rows, blk
        self.per_lat = 1 + rows.lat_len // blk
        self.lat_blocks = rows.n_lat * self.per_lat
        self.ctx_blocks = rows.mc // blk
        self.n = self.lat_blocks + self.ctx_blocks

    def is_cache(self, j):
        return (j < self.lat_blocks) & (j % self.per_lat == 0)

    def cache_idx(self, j):
        return jnp.clip(j // self.per_lat, 0, self.rows.n_lat - 1)

    def token_block(self, j):
        b = j // self.per_lat
        t = jnp.maximum(j % self.per_lat - 1, 0)
        lat = self.ctx_blocks + b * (self.per_lat - 1) + t
        return jnp.where(j < self.lat_blocks, lat, j - self.lat_blocks)


def _mla_kv_kernel(keys, n_heads, ckv_ref, kpe_ref, cckv_ref, ckpe_ref, kvng_ref, wk_ref, wvt_ref, cos_ref, sin_ref,
                   kcat_ref, vt_ref, ckvn_ref):
    cached = keys.is_cache(pl.program_id(0))
    kpe = kpe_ref[...]
    rot = kpe[:, :LANE] * cos_ref[...] + kpe[:, LANE:] * sin_ref[...]
    ckvn = jnp.where(cached, cckv_ref[...], _rms(ckv_ref[...], kvng_ref[...]))
    kr = jnp.where(cached, ckpe_ref[...], rot)
    ckvn_ref[...] = ckvn
    ckvn_b = ckvn.astype(BF16)
    kn = _dot(ckvn_b, wk_ref[...])
    vt_ref[...] = _dot_nt(wvt_ref[...], ckvn_b).astype(BF16)
    for h in range(n_heads):
        base = h * 2 * LANE
        kcat_ref[:, base:base + LANE] = kn[:, h * NOPE_DIM:(h + 1) * NOPE_DIM].astype(BF16)
        kcat_ref[:, base + LANE:base + 2 * LANE] = jnp.where(_rope_keep(h, kr.shape), kr, 0.0).astype(BF16)


def _mla_kv(keys, zc, zk, col, cache_ckv, cache_kpe2, kv_ng, w_k, w_vt, cos_t, sin_t, n_heads):
    blk = keys.blk
    R = kv_ng.shape[0]
    Mk = keys.n * blk
    ckv_off, ckv_w = col["ckv"]
    assert ckv_off % ckv_w == 0
    tok = keys.token_block
    return pl.pallas_call(
        functools.partial(_mla_kv_kernel, keys, n_heads),
        grid=(keys.n,),
        in_specs=[
            pl.BlockSpec((blk, ckv_w), lambda j: (tok(j), ckv_off // ckv_w)),
            pl.BlockSpec((blk, 2 * LANE), lambda j: (tok(j), 0)),
            pl.BlockSpec((blk, R), lambda j: (keys.cache_idx(j), 0)),
            pl.BlockSpec((blk, LANE), lambda j: (keys.cache_idx(j), 0)),
            pl.BlockSpec((1, R), lambda j: (0, 0)),
            pl.BlockSpec(w_k.shape, lambda j: (0, 0)),
            pl.BlockSpec(w_vt.shape, lambda j: (0, 0)),
            pl.BlockSpec((blk, LANE), lambda j: (tok(j), 0)),
            pl.BlockSpec((blk, LANE), lambda j: (tok(j), 0)),
        ],
        out_specs=[
            pl.BlockSpec((blk, n_heads * 2 * LANE), lambda j: (j, 0)),
            pl.BlockSpec((n_heads * V_DIM, blk), lambda j: (0, j)),
            pl.BlockSpec((blk, R), lambda j: (j, 0)),
        ],
        out_shape=[
            jax.ShapeDtypeStruct((Mk, n_heads * 2 * LANE), BF16),
            jax.ShapeDtypeStruct((n_heads * V_DIM, Mk), BF16),
            jax.ShapeDtypeStruct((Mk, R), F32),
        ],
        compiler_params=_params("parallel"),
        name="mla_kv",
    )(zc, zk, cache_ckv, cache_kpe2, kv_ng.reshape(1, R), w_k, w_vt, cos_t, sin_t)


ATTN_TQ = 256
ATTN_TK = 2048


def _attn_kernel(tq, chunks, q_ref, k_ref, vt_ref, o_ref, st0_ref, st1_ref):
    n_heads = q_ref.shape[1] // (2 * LANE)
    units = [(h, u) for h in range(n_heads) for u in range(q_ref.shape[0] // tq)]
    st_refs = (st0_ref, st1_ref)

    def scores(n, c0, c1):
        h, u = units[n]
        hd = slice(h * 2 * LANE, (h + 1) * 2 * LANE)
        st = _dot_nt(k_ref[c0:c1, hd], q_ref[u * tq:(u + 1) * tq, hd])
        st_refs[n % 2][c0:c1, :] = st
        return jnp.max(st, axis=0, keepdims=True)

    def weights(n, c0, c1, m):
        h, _ = units[n]
        p = jnp.exp2(st_refs[n % 2][c0:c1, :] - m)
        return jnp.sum(p, axis=0, keepdims=True), _dot(vt_ref[h * V_DIM:(h + 1) * V_DIM, c0:c1], p.astype(BF16))

    m_prev = None
    for n in range(len(units) + 1):
        m_new = acc = l = None
        for c0, c1 in chunks:
            if n < len(units):
                mc = scores(n, c0, c1)
                m_new = mc if m_new is None else jnp.maximum(m_new, mc)
            if n > 0:
                lc, pv = weights(n - 1, c0, c1, m_prev)
                l = lc if l is None else l + lc
                acc = pv if acc is None else acc + pv
        if n > 0:
            h, u = units[n - 1]
            o_ref[u * tq:(u + 1) * tq, h * V_DIM:(h + 1) * V_DIM] = (acc / l).T.astype(o_ref.dtype)
        m_prev = m_new


def _attn_kernel_aliased(tq, chunks, q_ref, k_ref, vt_ref, o_prev_ref, o_ref, st0_ref, st1_ref):
    del o_prev_ref
    _attn_kernel(tq, chunks, q_ref, k_ref, vt_ref, o_ref, st0_ref, st1_ref)


def _attention(qcat, kcat, vt, o_prev, n_seq, n_heads, hps, tq, q_blocks_per_seq, q_block0, tk, k_block0):
    sub_q = ATTN_TQ if tq % ATTN_TQ == 0 else LANE
    assert tq % sub_q == 0 and n_heads % hps == 0
    q_spec = pl.BlockSpec((tq, hps * 2 * LANE), lambda s, h, i: (q_block0 + s * q_blocks_per_seq + i, h))
    in_specs = [
        q_spec,
        pl.BlockSpec((tk, hps * 2 * LANE), lambda s, h, i: (k_block0 + s, h)),
        pl.BlockSpec((hps * V_DIM, tk), lambda s, h, i: (h, k_block0 + s)),
    ]
    args = [qcat, kcat, vt]
    chunks = tuple((c0, min(c0 + ATTN_TK, tk)) for c0 in range(0, tk, ATTN_TK))
    kern = functools.partial(_attn_kernel, sub_q, chunks)
    aliases = {}
    if o_prev is not None:
        in_specs.append(pl.BlockSpec(memory_space=pl.ANY))
        args.append(o_prev)
        aliases = {3: 0}
        kern = functools.partial(_attn_kernel_aliased, sub_q, chunks)
    return pl.pallas_call(
        kern,
        grid=(n_seq, n_heads // hps, q_blocks_per_seq),
        in_specs=in_specs,
        out_specs=pl.BlockSpec((tq, hps * V_DIM), lambda s, h, i: (q_block0 + s * q_blocks_per_seq + i, h)),
        out_shape=jax.ShapeDtypeStruct((qcat.shape[0], n_heads * V_DIM), BF16),
        input_output_aliases=aliases,
        scratch_shapes=[pltpu.VMEM((tk, sub_q), F32), pltpu.VMEM((tk, sub_q), F32)],
        compiler_params=_params("parallel", "parallel", "arbitrary"),
        name="mla_attn",
    )(*args)


def _rope_partner(w):
    q = ROPE_DIM // 4
    return jnp.flip(w.reshape(w.shape[:-1] + (2, 2, q)), axis=-2).reshape(w.shape)


def _rope_tables(rows):
    half, quarter = ROPE_DIM // 2, ROPE_DIM // 4
    inv = ROPE_BASE ** (-jnp.arange(quarter, dtype=F32) * 2.0 / half)
    t = jnp.arange(rows.lat_len)
    r = (t // GRID_W).astype(F32)
    c = (t % GRID_W).astype(F32)
    ang_r = r[:, None] * inv[None, :]
    ang_c = c[:, None] * inv[None, :]
    cos = jnp.concatenate([jnp.cos(ang_r), jnp.cos(ang_r), jnp.cos(ang_c), jnp.cos(ang_c)], axis=1)
    sin = jnp.concatenate([-jnp.sin(ang_r), jnp.sin(ang_r), -jnp.sin(ang_c), jnp.sin(ang_c)], axis=1)
    cos = jnp.tile(jnp.concatenate([cos, cos], axis=1), (rows.n_lat, 1))
    sin = jnp.tile(jnp.concatenate([sin, sin], axis=1), (rows.n_lat, 1))
    cos = jnp.concatenate([jnp.ones((rows.mc, LANE), F32), cos], axis=0)
    sin = jnp.concatenate([jnp.zeros((rows.mc, LANE), F32), sin], axis=0)
    return cos, sin


def kernel(x_prompt, x_sample, state_gla, state_ret, cache_ckv, cache_kpe, c, c_ctx, mod_w, mod_b, norm1_g, norm2_g, ab_w_in, gla_gate_w2, gla_gate_b, ret_decay, gla_norm_g, ret_norm_g, ab_w_out, mla_w_in, mla_q_norm_g, mla_w_uq, mla_kv_norm_g, mla_w_ukv, mla_w_out, ffn_w_in, ffn_conv, ffn_w_out, final_norm_g):
    B, S, D = x_prompt.shape
    NB, T, _ = x_sample.shape
    depth = mod_w.shape[0]
    _, _, _, HA, DK, DV = state_gla.shape
    HB = state_ret.shape[3]
    assert HA == HB and state_ret.shape[4:] == (DK, DV)
    GR = gla_gate_w2.shape[2]
    past = cache_ckv.shape[2]
    q_rank = mla_q_norm_g.shape[1]
    kv_rank = mla_kv_norm_g.shape[1]
    HC = mla_w_uq.shape[2] // (NOPE_DIM + ROPE_DIM)
    F = ffn_w_out.shape[1]
    rows = _Rows(B * S, S, T, NB)
    M = rows.m
    tm = min(512, S * B, T)
    assert rows.mc % tm == 0 and T % tm == 0
    assert S & (S - 1) == 0 and T & (T - 1) == 0

    tm_proj = min(1024, S * B, T)
    assert rows.mc % tm_proj == 0 and T % tm_proj == 0
    cond = jnp.concatenate([c_ctx[None, :], c, jnp.zeros((MOD_GROUPS - 1 - NB, D), F32)], axis=0)
    mod3 = _modulation(cond, mod_w, mod_b)

    x = (x_prompt.reshape(B * S, D), x_sample.reshape(NB * T, D))
    h = _normmod(rows, x, norm1_g[0], mod3[0], tm)
    ffn_w = (ffn_w_in.astype(BF16), ffn_conv, ffn_w_out.astype(BF16))
    new_gla, new_ret, new_ckv, new_kpe = [], [], [], []
    for l in range(depth):
        if l % 2 == 0:
            e = l // 2
            names = ("gq", "gk", "gv", "gg", "glr", "rq", "rk", "rv", "rg")
            sizes = (HA * DK, HA * DK, HA * DV, HA * DV, 2 * GR, HB * DK, HB * DK, HB * DV, HB * DV)
            src, o = {}, 0
            for n, sz in zip(names, sizes):
                src[n] = (o, sz)
                o += sz
            order = ("gq", "gk", "gv", "gg", "rq", "rk", "rv", "rg")
            col, o = {}, 0
            for n in order:
                col[n] = (o, src[n][1])
                o += src[n][1]
            so, sz = src["glr"]
            assert so % LANE == 0 and so + LANE <= ab_w_in.shape[2]
            w_all = lax.optimization_barrier(ab_w_in[e].astype(BF16))
            w_in = jnp.concatenate([w_all[:, :so], w_all[:, so + sz:]], axis=1)
            w_glr = w_all[:, so:so + LANE]
            z, glr = _inproj(h, w_in, w_glr, BF16, tm_proj)
            tb = min(256, S)
            extra = None
            for dirn in (0, 1):
                cfg = _ScanCfg(rows, B, tb, HA, DK, DV, backward=bool(dirn))
                gate_w = jnp.zeros((LANE, HA * DK), F32).at[dirn * GR:(dirn + 1) * GR].set(gla_gate_w2[e, dirn])
                rd = jnp.broadcast_to(ret_decay[e, dirn][:, None, None], (HB, 8, LANE))
                y, sg, sr = _scan(cfg, z, glr, col, gate_w.astype(BF16), gla_gate_b[e, dirn].reshape(1, -1), rd,
                                  state_gla, state_ret, e, extra)
                extra = (y, sg, sr, gla_norm_g[e].reshape(1, -1), ret_norm_g[e].reshape(1, -1))
            new_gla.append(sg)
            new_ret.append(sr)
            x, h = _outproj(rows, y, ab_w_out[e].astype(BF16), x, norm2_g[l], mod3[l], tm)
        else:
            i = l // 2
            w = mla_w_in[i]
            w_kpe = w[:, q_rank + kv_rank:]
            w_in = w[:, :q_rank + kv_rank].astype(BF16)
            w_kpe_p = _rope_partner(w_kpe)
            w_side = jnp.concatenate([w_kpe, w_kpe, w_kpe_p, w_kpe_p], axis=1).astype(BF16)
            col = {"cq": (0, q_rank), "ckv": (q_rank, kv_rank)}
            zc, zk = _inproj(h, w_in, w_side, F32, tm_proj)
            cos_t, sin_t = _rope_tables(rows)
            wq = mla_w_uq[i].reshape(q_rank, HC, NOPE_DIM + ROPE_DIM)
            wq_rope = wq[:, :, NOPE_DIM:]
            w_uq = jnp.concatenate([wq[:, :, :NOPE_DIM].reshape(q_rank, -1), wq_rope.reshape(q_rank, -1),
                                    _rope_partner(wq_rope).reshape(q_rank, -1)], axis=1).astype(BF16)
            scale = (NOPE_DIM + ROPE_DIM) ** -0.5 * LOG2_E
            assert col["cq"][0] == 0
            qcat = _mla_q(zc, mla_q_norm_g[i], w_uq, cos_t, sin_t, HC, scale, tm)
            wkv = mla_w_ukv[i].reshape(kv_rank, HC, NOPE_DIM + V_DIM)
            w_k = wkv[:, :, :NOPE_DIM].reshape(kv_rank, -1).astype(BF16)
            w_vt = wkv[:, :, NOPE_DIM:].reshape(kv_rank, -1).T.astype(BF16)
            assert S % past == 0 and T % past == 0
            keys = _KeyRows(rows, past)
            ckpe = cache_kpe[:, i].reshape(NB * past, ROPE_DIM)
            kcat, vt, ckvn = _mla_kv(keys, zc, zk, col, cache_ckv[:, i].reshape(NB * past, kv_rank),
                                     jnp.concatenate([ckpe, ckpe], axis=1), mla_kv_norm_g[i], w_k, w_vt,
                                     cos_t, sin_t, HC)
            tk_lat = past + T
            assert (keys.lat_blocks * past) % S == 0
            tq = min(4096, T)
            assert rows.mc % tq == 0
            o = _attention(qcat, kcat, vt, None, B, HC, HC, S, 1, 0, S, keys.lat_blocks * past // S)
            o = _attention(qcat, kcat, vt, o, NB, HC, 1, tq, T // tq, rows.mc // tq, tk_lat, 0)
            new_ckv.append(ckvn[keys.lat_blocks * past:].reshape(B, S, kv_rank))
            new_kpe.append(zk[:rows.mc, :ROPE_DIM].reshape(B, S, ROPE_DIM))
            x, h = _outproj(rows, o, mla_w_out[i].astype(BF16), x, norm2_g[l], mod3[l], tm)
        tf = _largest_tile(F, 512)
        if l == depth - 1:
            y_ctx, y_lat = _ffn(rows, h, x, mod3[l], l, *ffn_w, tm, tf, final_g=final_norm_g)
        else:
            x, h = _ffn(rows, h, x, mod3[l], l, *ffn_w, tm, tf, next_norm=(norm1_g[l + 1], mod3[l + 1]))

    y_prompt = y_ctx.reshape(B, S, D)
    y_sample = y_lat.reshape(NB, T, D)
    return (y_prompt, y_sample, jnp.stack(new_gla, axis=1), jnp.stack(new_ret, axis=1),
            jnp.stack(new_ckv, axis=1), jnp.stack(new_kpe, axis=1))
```

```python
import functools

import jax
import jax.numpy as jnp
from jax import lax
from jax.experimental import pallas as pl
from jax.experimental.pallas import tpu as pltpu

F32 = jnp.float32
BF16 = jnp.bfloat16

NORM_EPS = 1e-6
GATE_TEMP = 16.0
CHUNK = 64
GRID_W = 64
ROPE_BASE = 10000.0
ROPE_DIM = 64
NOPE_DIM = 128
V_DIM = 128
CONV_W = 3
LOG2_E = 1.4426950408889634

LANE = 128
BF16_SUBLANE = 16
MOD_GROUPS = 8
VMEM_LIMIT = 56 * 1024 * 1024

NT_DIMS = (((1,), (1,)), ((), ()))
TN_DIMS = (((0,), (0,)), ((), ()))


def _dot(a, b):
    return jnp.dot(a, b, preferred_element_type=F32)


def _dot_nt(a, b):
    return lax.dot_general(a, b, NT_DIMS, preferred_element_type=F32)


def _dot_tn(a, b):
    return lax.dot_general(a, b, TN_DIMS, preferred_element_type=F32)


def _params(*sem):
    return pltpu.CompilerParams(dimension_semantics=sem, vmem_limit_bytes=VMEM_LIMIT)


def _rms(x, g):
    ms = jnp.mean(x * x, axis=-1, keepdims=True)
    return (x * lax.rsqrt(ms + NORM_EPS)) * g


def _silu(x):
    return x * jax.nn.sigmoid(x)


def _log_sigmoid(x):
    return jnp.minimum(x, 0.0) - jnp.log(1.0 + jnp.exp(-jnp.abs(x)))


def _largest_tile(n, cap):
    best = None
    for t in range(LANE, min(n, cap) + 1, LANE):
        if n % t == 0:
            best = t
    assert best is not None, (n, cap)
    return best


def _mod_kernel(c_ref, w_ref, b_ref, o_ref):
    s = _silu(c_ref[...]).astype(BF16)
    m = _dot(s, w_ref[0].astype(BF16)) + b_ref[0]
    for g in range(MOD_GROUPS):
        o_ref[0, g, 0] = m[g:g + 1]


def _modulation(cond, mod_w, mod_b):
    L, D, N = mod_w.shape
    n_vec = N // D
    out = pl.pallas_call(
        _mod_kernel,
        grid=(L, n_vec),
        in_specs=[
            pl.BlockSpec((MOD_GROUPS, D), lambda l, k: (0, 0)),
            pl.BlockSpec((1, D, D), lambda l, k: (l, 0, k)),
            pl.BlockSpec((1, 1, D), lambda l, k: (l, 0, k)),
        ],
        out_specs=pl.BlockSpec((1, MOD_GROUPS, 1, 1, D), lambda l, k: (l, 0, k, 0, 0)),
        out_shape=jax.ShapeDtypeStruct((L, MOD_GROUPS, n_vec, 1, D), F32),
        compiler_params=_params("parallel", "parallel"),
        name="modulation",
    )(cond, mod_w, mod_b.reshape(L, 1, N))
    return out.reshape(L, MOD_GROUPS * n_vec, 1, D)


class _Rows:
    def __init__(self, mc, seq, lat_len, n_lat):
        self.mc, self.seq, self.lat_len, self.n_lat = mc, seq, lat_len, n_lat
        self.m = mc + lat_len * n_lat

    def group(self, i, tm):
        r = i * tm
        return jnp.where(r < self.mc, 0, 1 + (r - self.mc) // self.lat_len)

    def mod_spec(self, which, tm, d):
        return pl.BlockSpec((1, 1, d), lambda i, *_: (self.group(i, tm) * 6 + which, 0, 0))

    def split_specs(self, tm, d):
        n_ctx = self.mc // tm
        return [pl.BlockSpec((tm, d), lambda i, *_: (jnp.minimum(i, n_ctx - 1), 0)),
                pl.BlockSpec((tm, d), lambda i, *_: (jnp.maximum(i - n_ctx, 0), 0))]


ROW_CHUNK = 16
ROW_PARTS = 2


def _for_row_chunks(n_rows, body):
    def it(c, carry):
        body(pl.ds(pl.multiple_of(c * ROW_CHUNK, ROW_CHUNK), ROW_CHUNK))
        return carry
    lax.fori_loop(0, n_rows // ROW_CHUNK, it, 0, unroll=4)


def _fold_gain(gs_ref, g_ref, sc_ref):
    gs_ref[...] = g_ref[...] * (1.0 + sc_ref[0])


def _norm_mod_rows(x, gs_ref, sh_ref):
    ms = jnp.mean(x * x, axis=-1, keepdims=True)
    return (x * lax.rsqrt(ms + NORM_EPS)) * gs_ref[...] + sh_ref[0]


def _normmod_kernel(rows, tm, xa_ref, xb_ref, g_ref, sh_ref, sc_ref, h_ref, gs_ref):
    _fold_gain(gs_ref, g_ref, sc_ref)

    def run(x_ref):
        def body(r):
            h_ref[r, :] = _norm_mod_rows(x_ref[r, :], gs_ref, sh_ref).astype(h_ref.dtype)
        _for_row_chunks(tm, body)

    is_ctx = pl.program_id(0) * tm < rows.mc
    pl.when(is_ctx)(lambda: run(xa_ref))
    pl.when(jnp.logical_not(is_ctx))(lambda: run(xb_ref))


def _normmod(rows, x_pair, g, mod3, tm):
    D = x_pair[0].shape[1]
    return pl.pallas_call(
        functools.partial(_normmod_kernel, rows, tm),
        grid=(rows.m // tm,),
        in_specs=rows.split_specs(tm, D) + [
            pl.BlockSpec((1, D), lambda i: (0, 0)),
            rows.mod_spec(0, tm, D),
            rows.mod_spec(1, tm, D),
        ],
        out_specs=pl.BlockSpec((tm, D), lambda i: (i, 0)),
        out_shape=jax.ShapeDtypeStruct((rows.m, D), BF16),
        scratch_shapes=[pltpu.VMEM((1, D), F32)],
        compiler_params=_params("parallel"),
        name="normmod",
    )(*x_pair, g.reshape(1, D), mod3, mod3)


def _inproj_kernel(h_ref, w_ref, ws_ref, o_ref, os_ref):
    @pl.when(pl.program_id(1) == 0)
    def _():
        os_ref[...] = _dot(h_ref[...], ws_ref[...])

    o_ref[...] = _dot(h_ref[...], w_ref[...]).astype(o_ref.dtype)


def _inproj(h, w, w_side, out_dtype, tm, tn_cap=2048):
    M, D = h.shape
    N = w.shape[1]
    NS = w_side.shape[1]
    tn = _largest_tile(N, tn_cap)
    return pl.pallas_call(
        _inproj_kernel,
        grid=(M // tm, N // tn),
        in_specs=[
            pl.BlockSpec((tm, D), lambda i, j: (i, 0)),
            pl.BlockSpec((D, tn), lambda i, j: (0, j)),
            pl.BlockSpec((D, NS), lambda i, j: (0, 0)),
        ],
        out_specs=[pl.BlockSpec((tm, tn), lambda i, j: (i, j)), pl.BlockSpec((tm, NS), lambda i, j: (i, 0))],
        out_shape=[jax.ShapeDtypeStruct((M, N), out_dtype), jax.ShapeDtypeStruct((M, NS), F32)],
        compiler_params=_params("parallel", "arbitrary"),
        name="inproj",
    )(h, w, w_side)


def _outproj_kernel(rows, tm, y_split, x_split, *refs):
    refs = list(refs)
    y_refs = [refs.pop(0) for _ in range(2 if y_split else 1)]
    w_ref = refs.pop(0)
    x_refs = [refs.pop(0) for _ in range(2 if x_split else 1)]
    gate_ref, g_ref, sh_ref, sc_ref, o_ref, h_ref, acc_ref, gs_ref = refs
    _fold_gain(gs_ref, g_ref, sc_ref)
    is_ctx = pl.program_id(0) * tm < rows.mc
    part = tm // ROW_PARTS

    def rows_of(pair, r):
        return jnp.where(is_ctx, pair[0][r, :], pair[1][r, :]) if len(pair) == 2 else pair[0][r, :]

    def finish(r):
        x1 = rows_of(x_refs, r) + gate_ref[0] * acc_ref[r, :]
        o_ref[r, :] = x1
        h_ref[r, :] = _norm_mod_rows(x1, gs_ref, sh_ref).astype(h_ref.dtype)

    for p in range(ROW_PARTS + 1):
        if p < ROW_PARTS:
            rp = slice(p * part, (p + 1) * part)
            acc_ref[rp, :] = _dot(rows_of(y_refs, rp), w_ref[...])
        if p > 0:
            for c in range(part // ROW_CHUNK):
                start = (p - 1) * part + c * ROW_CHUNK
                finish(slice(start, start + ROW_CHUNK))


def _outproj(rows, y, w, x, g, mod3, tm):
    K, N = w.shape
    M = rows.m

    def operand(a, width):
        if isinstance(a, (tuple, list)):
            return True, rows.split_specs(tm, width), list(a)
        return False, [pl.BlockSpec((tm, width), lambda i: (i, 0))], [a]

    y_split, y_specs, y_args = operand(y, K)
    x_split, x_specs, x_args = operand(x, N)
    return pl.pallas_call(
        functools.partial(_outproj_kernel, rows, tm, y_split, x_split),
        grid=(M // tm,),
        in_specs=y_specs + [pl.BlockSpec((K, N), lambda i: (0, 0))] + x_specs + [
            rows.mod_spec(2, tm, N),
            pl.BlockSpec((1, N), lambda i: (0, 0)),
            rows.mod_spec(3, tm, N),
            rows.mod_spec(4, tm, N),
        ],
        out_specs=[pl.BlockSpec((tm, N), lambda i: (i, 0)), pl.BlockSpec((tm, N), lambda i: (i, 0))],
        out_shape=[jax.ShapeDtypeStruct((M, N), F32), jax.ShapeDtypeStruct((M, N), BF16)],
        scratch_shapes=[pltpu.VMEM((tm, N), F32), pltpu.VMEM((1, N), F32)],
        compiler_params=_params("parallel"),
        name="outproj",
    )(*y_args, w, *x_args, mod3, g.reshape(1, N), mod3, mod3)


HALO = BF16_SUBLANE


def _ffn_kernel(rows, tm, final_norm, hp_ref, h_ref, hn_ref, x_ref, gate_ref,
                wa_ref, wb_ref, cw_ref, wo_ref, *rest):
    if final_norm:
        fg_ref, o_ctx_ref, o_lat_ref, hs_ref, a_ref, acc_ref = rest
    else:
        g_ref, sh_ref, sc_ref, o_ref, hnext_ref, hs_ref, a_ref, acc_ref, gs_ref = rest
    i = pl.program_id(0)
    f = pl.program_id(1)

    @pl.when(f == 0)
    def _():
        hs_ref[0:tm, :] = h_ref[...]
        last_row = lax.broadcasted_iota(jnp.int32, (HALO, 1), 0) == HALO - 1
        hs_ref[tm:, :] = jnp.where(last_row, hp_ref[...].astype(F32), hn_ref[...].astype(F32)).astype(BF16)
        acc_ref[...] = jnp.zeros_like(acc_ref)

    def hidden_tile():
        a = _dot(hs_ref[...], wa_ref[0])
        a_ref[HALO:, :] = a
        a_ref[HALO - 8:HALO, :] = a[tm + HALO - 8:, :]
        b = _dot(h_ref[...], wb_ref[0])
        row = i * tm + lax.broadcasted_iota(jnp.int32, (tm, 1), 0)
        pos = jnp.where(row < rows.mc, row & (rows.seq - 1), (row - rows.mc) & (rows.lat_len - 1))
        seq_len = jnp.where(row < rows.mc, rows.seq, rows.lat_len)
        a_prev = jnp.where(pos == 0, 0.0, a_ref[pl.ds(HALO - 1, tm), :])
        a_next = jnp.where(pos == seq_len - 1, 0.0, a_ref[pl.ds(HALO + 1, tm), :])
        a_mid = a_ref[pl.ds(HALO, tm), :]
        cw = cw_ref[0]
        a = cw[0:1] * a_prev + cw[1:2] * a_mid + cw[2:3] * a_next
        return (_silu(a) * b).astype(BF16)

    def residual(r):
        return x_ref[r, :] + gate_ref[0] * acc_ref[r, :]

    def last_step(finish):
        act = hidden_tile()
        part = tm // ROW_PARTS
        for p in range(ROW_PARTS + 1):
            if p < ROW_PARTS:
                rp = slice(p * part, (p + 1) * part)
                acc_ref[rp, :] += _dot(act[rp, :], wo_ref[0])
            if p > 0:
                for c in range(part // ROW_CHUNK):
                    start = (p - 1) * part + c * ROW_CHUNK
                    finish(slice(start, start + ROW_CHUNK))

    last = pl.num_programs(1) - 1

    @pl.when(f < last)
    def _():
        acc_ref[...] += _dot(hidden_tile(), wo_ref[0])

    if final_norm:
        def finish_to(out_ref):
            def finish(r):
                out_ref[r, :] = _rms(residual(r), fg_ref[...])
            return finish

        is_ctx = i * tm < rows.mc
        pl.when((f == last) & is_ctx)(lambda: last_step(finish_to(o_ctx_ref)))
        pl.when((f == last) & jnp.logical_not(is_ctx))(lambda: last_step(finish_to(o_lat_ref)))
    else:
        def finish(r):
            x2 = residual(r)
            o_ref[r, :] = x2
            hnext_ref[r, :] = _norm_mod_rows(x2, gs_ref, sh_ref).astype(hnext_ref.dtype)

        @pl.when(f == last)
        def _():
            _fold_gain(gs_ref, g_ref, sc_ref)
            last_step(finish)


def _ffn(rows, h, x, mod3, layer, w_in, conv_w, w_out, tm, tf, next_norm=None, final_g=None):
    M, D = x.shape
    F = w_out.shape[1]
    nf = F // tf
    nhalo = M // HALO
    final_norm = final_g is not None
    kern = functools.partial(_ffn_kernel, rows, tm, final_norm)
    in_specs = [
        pl.BlockSpec((HALO, D), lambda i, f: (jnp.maximum(i * (tm // HALO) - 1, 0), 0)),
        pl.BlockSpec((tm, D), lambda i, f: (i, 0)),
        pl.BlockSpec((HALO, D), lambda i, f: (jnp.minimum((i + 1) * (tm // HALO), nhalo - 1), 0)),
        pl.BlockSpec((tm, D), lambda i, f: (i, 0)),
        rows.mod_spec(5, tm, D),
        pl.BlockSpec((1, D, tf), lambda i, f: (layer, 0, f)),
        pl.BlockSpec((1, D, tf), lambda i, f: (layer, 0, nf + f)),
        pl.BlockSpec((1, CONV_W, tf), lambda i, f: (layer, 0, f)),
        pl.BlockSpec((1, tf, D), lambda i, f: (layer, f, 0)),
    ]
    args = [h, h, h, x, mod3, w_in, w_in, conv_w, w_out]
    vec = pl.BlockSpec((1, D), lambda i, f: (0, 0))
    if final_norm:
        n_ctx = rows.mc // tm
        in_specs += [vec]
        args += [final_g.reshape(1, D)]
        out_specs = [pl.BlockSpec((tm, D), lambda i, f: (jnp.minimum(i, n_ctx - 1), 0)),
                     pl.BlockSpec((tm, D), lambda i, f: (jnp.maximum(i - n_ctx, 0), 0))]
        out_shape = [jax.ShapeDtypeStruct((rows.mc, D), F32), jax.ShapeDtypeStruct((M - rows.mc, D), F32)]
        row_sem = "arbitrary"
    else:
        g_next, mod3_next = next_norm
        in_specs += [vec, rows.mod_spec(0, tm, D), rows.mod_spec(1, tm, D)]
        args += [g_next.reshape(1, D), mod3_next, mod3_next]
        out_specs = [pl.BlockSpec((tm, D), lambda i, f: (i, 0)), pl.BlockSpec((tm, D), lambda i, f: (i, 0))]
        out_shape = [jax.ShapeDtypeStruct((M, D), F32), jax.ShapeDtypeStruct((M, D), BF16)]
        row_sem = "parallel"
    return pl.pallas_call(
        kern,
        grid=(M // tm, nf),
        in_specs=in_specs,
        out_specs=out_specs,
        out_shape=out_shape,
        scratch_shapes=[
            pltpu.VMEM((tm + HALO, D), BF16),
            pltpu.VMEM((tm + 2 * HALO, tf), F32),
            pltpu.VMEM((tm, D), F32),
        ] + ([] if final_norm else [pltpu.VMEM((1, D), F32)]),
        compiler_params=_params(row_sem, "arbitrary"),
        name="convffn",
    )(*args)


class _ScanCfg:
    def __init__(self, rows, batch, tb, h, dk, dv, backward):
        self.rows, self.batch, self.tb, self.h, self.dk, self.dv = rows, batch, tb, h, dk, dv
        self.backward = backward
        self.cps = rows.seq // tb
        self.lps = rows.lat_len // tb
        self.ctx_blocks = batch * self.cps
        self.nblk = rows.m // tb

    def block(self, i):
        return self.nblk - 1 - i if self.backward else i

    def is_ctx(self, r):
        return r < self.ctx_blocks

    def seq_pos(self, r):
        ctx = self.is_ctx(r)
        return (jnp.where(ctx, r % self.cps, (r - self.ctx_blocks) % self.lps),
                jnp.where(ctx, self.cps, self.lps))

    def lat_seq(self, r):
        return jnp.clip((r - self.ctx_blocks) // self.lps, 0, self.rows.n_lat - 1)

    def ctx_seq(self, r):
        return jnp.clip(r // self.cps, 0, self.batch - 1)


def _scan_kernel(cfg, gq_ref, gk_ref, gv_ref, rq_ref, rk_ref, rv_ref, glr_ref, gw_ref, gb_ref, rd_ref,
                 s0g_ref, s0r_ref, *rest):
    scratch = rest[-8:]
    sg_ref, sr_ref, dend_ref, qd_ref, kinv_ref, kend_ref, rks_ref, rkd_ref = scratch
    if cfg.backward:
        of_ref, gg_ref, rg_ref, gng_ref, rng_ref, sgf_ref, srf_ref, y_ref, sgo_ref, sro_ref = rest[:-8]
    else:
        o_ref, sgo_ref, sro_ref = rest[:-8]
    H, DK, DV, C = cfg.h, cfg.dk, cfg.dv, CHUNK
    r = cfg.block(pl.program_id(0))
    blk, nblk_seq = cfg.seq_pos(r)
    first = blk == (nblk_seq - 1 if cfg.backward else 0)
    last = blk == (0 if cfg.backward else nblk_seq - 1)
    is_ctx = cfg.is_ctx(r)

    @pl.when(first & is_ctx)
    def _():
        sg_ref[...] = jnp.zeros_like(sg_ref)
        sr_ref[...] = jnp.zeros_like(sr_ref)

    @pl.when(first & jnp.logical_not(is_ctx))
    def _():
        sg_ref[...] = s0g_ref[0, 0, 0]
        sr_ref[...] = s0r_ref[0, 0, 0]

    ti = lax.broadcasted_iota(jnp.int32, (C, C), 0)
    tj = lax.broadcasted_iota(jnp.int32, (C, C), 1)
    sees = (tj >= ti) if cfg.backward else (tj <= ti)
    dist = jnp.abs(ti - tj).astype(F32)
    rowi = lax.broadcasted_iota(jnp.int32, (C, LANE), 0)
    to_end = (rowi if cfg.backward else C - 1 - rowi).astype(F32)
    from_start = (C - rowi if cfg.backward else rowi + 1).astype(F32)

    n_chunks = cfg.tb // C
    HDK = H * DK
    scale = DK ** -0.5

    bi = lax.broadcasted_iota(jnp.int32, (cfg.tb, cfg.tb), 0)
    bj = lax.broadcasted_iota(jnp.int32, (cfg.tb, cfg.tb), 1)
    same_chunk = (bi & -C) == (bj & -C)
    blk_tri = jnp.where(same_chunk & ((bj >= bi) if cfg.backward else (bj <= bi)), 1.0, 0.0).astype(BF16)
    pre = _dot(glr_ref[...].astype(BF16), gw_ref[...]) + gb_ref[...]
    la = _log_sigmoid(pre) / GATE_TEMP
    la_hi = la.astype(BF16)
    la_lo = (la - la_hi.astype(F32)).astype(BF16)
    b = _dot(blk_tri, la_hi) + _dot(blk_tri, la_lo)
    b3 = b.reshape(n_chunks, C, HDK)
    b_end = b3[:, 0:1, :] if cfg.backward else b3[:, C - 1:C, :]
    eb = jnp.exp(b)
    chunk_of_row = jnp.right_shift(lax.broadcasted_iota(jnp.int32, (cfg.tb, LANE), 0), C.bit_length() - 1)
    lane = lax.broadcasted_iota(jnp.int32, (cfg.tb, LANE), 1)
    in_chunk = jnp.where(chunk_of_row == lane, 1.0, 0.0).astype(BF16)
    dend = jnp.exp(_dot_tn(la_hi, in_chunk) + _dot_tn(la_lo, in_chunk))
    for c in range(n_chunks):
        dend_ref[:, c * LANE:(c + 1) * LANE] = jnp.broadcast_to(dend[:, c:c + 1], (HDK, LANE))
    gk = gk_ref[...].astype(F32)
    qd_ref[...] = (gq_ref[...].astype(F32) * scale * eb).astype(BF16)
    kinv_ref[...] = (gk * jnp.exp(-b)).astype(BF16)
    kend_ref[...] = (gk * jnp.exp(b_end - b3).reshape(cfg.tb, HDK)).astype(BF16)

    decays, q_decs, c_decs, k_dec_cols = [], [], [], []
    for h in range(H):
        lg = _log_sigmoid(rd_ref[h])
        lg_c = jnp.broadcast_to(lg[0:1, 0:C], (C, C))
        lg_l = jnp.broadcast_to(lg[0:1, :], (C, LANE))
        decays.append(jnp.where(sees, jnp.exp(lg_c * dist), 0.0))
        q_decs.append(jnp.concatenate([jnp.exp(lg_l * from_start)] * (DV // LANE), axis=1))
        c_decs.append(jnp.concatenate([jnp.exp(lg[0:1, :] * float(C))] * (DV // LANE), axis=1))
        k_dec_cols.append(jnp.concatenate([jnp.exp(lg_l * to_end)] * n_chunks, axis=0))
    rk = rk_ref[...].astype(F32) * scale
    rks_ref[...] = rk.astype(BF16)
    rkd_ref[...] = (rk * jnp.concatenate(k_dec_cols, axis=1)).astype(BF16)

    for ci in range(n_chunks):
        c = n_chunks - 1 - ci if cfg.backward else ci
        rs = slice(c * C, (c + 1) * C)
        heads = range(H)
        ksl = [slice(h * DK, (h + 1) * DK) for h in heads]
        vsl = [slice(h * DV, (h + 1) * DV) for h in heads]
        att_raw = [_dot_nt(qd_ref[rs, ksl[h]], kinv_ref[rs, ksl[h]]) for h in heads]
        ratt_raw = [_dot_nt(rq_ref[rs, ksl[h]], rks_ref[rs, ksl[h]]) for h in heads]
        qs_g = [_dot(qd_ref[rs, ksl[h]], sg_ref[h].astype(BF16)) for h in heads]
        qs_r = [_dot(rq_ref[rs, ksl[h]], sr_ref[h].astype(BF16)) for h in heads]
        for h in heads:
            att = jnp.where(sees, att_raw[h], 0.0).astype(BF16)
            o_g = _dot(att, gv_ref[rs, vsl[h]]) + qs_g[h]
            ratt = (ratt_raw[h] * decays[h]).astype(BF16)
            o_r = _dot(ratt, rv_ref[rs, vsl[h]]) + qs_r[h] * q_decs[h]
            rs_cols = slice(H * DV + h * DV, H * DV + (h + 1) * DV)
            if cfg.backward:
                for o, cols, gate_ref, ng_ref in ((o_g, vsl[h], gg_ref, gng_ref), (o_r, rs_cols, rg_ref, rng_ref)):
                    tot = o + of_ref[rs, cols]
                    mu = jnp.mean(tot, axis=-1, keepdims=True)
                    d = tot - mu
                    var = jnp.mean(d * d, axis=-1, keepdims=True)
                    yn = d * lax.rsqrt(var + NORM_EPS) * ng_ref[:, vsl[h]]
                    y_ref[rs, cols] = (_silu(gate_ref[rs, vsl[h]].astype(F32)) * yn).astype(y_ref.dtype)
            else:
                o_ref[rs, vsl[h]] = o_g
                o_ref[rs, rs_cols] = o_r
        for h in heads:
            dend = dend_ref[ksl[h], c * LANE:(c + 1) * LANE]
            sg_ref[h] = (sg_ref[h] * jnp.concatenate([dend] * (DV // LANE), axis=1)
                         + _dot_tn(kend_ref[rs, ksl[h]], gv_ref[rs, vsl[h]]))
            sr_ref[h] = sr_ref[h] * c_decs[h] + _dot_tn(rkd_ref[rs, ksl[h]], rv_ref[rs, vsl[h]])

    @pl.when(last & is_ctx)
    def _():
        if cfg.backward:
            sgo_ref[0, 0] = sgf_ref[0]
            sro_ref[0, 0] = srf_ref[0]
            sgo_ref[0, 1] = sg_ref[...]
            sro_ref[0, 1] = sr_ref[...]
        else:
            sgo_ref[0] = sg_ref[...]
            sro_ref[0] = sr_ref[...]


def _scan(cfg, z, glr, col, gate_w, gate_b, ret_decay_t, s0g, s0r, layer, extra):
    H, DK, DV, tb = cfg.h, cfg.dk, cfg.dv, cfg.tb
    M = z.shape[0]
    dirn = int(cfg.backward)

    def zspec(name):
        off, width = col[name]
        assert off % width == 0, (name, off, width)
        return pl.BlockSpec((tb, width), lambda i: (cfg.block(i), off // width))

    def full(shape):
        return pl.BlockSpec(shape, lambda i: (0,) * len(shape))

    state_in = pl.BlockSpec((1, 1, 1, H, DK, DV),
                            lambda i: (cfg.lat_seq(cfg.block(i)), layer, dirn, 0, 0, 0))
    fwd_state = pl.BlockSpec((1, H, DK, DV), lambda i: (cfg.ctx_seq(cfg.block(i)), 0, 0, 0))
    row_blk = pl.BlockSpec((tb, 2 * H * DV), lambda i: (cfg.block(i), 0))
    glr_spec = pl.BlockSpec((tb, glr.shape[1]), lambda i: (cfg.block(i), 0))
    in_specs = [zspec("gq"), zspec("gk"), zspec("gv"), zspec("rq"), zspec("rk"), zspec("rv"), glr_spec,
                full(gate_w.shape), full(gate_b.shape), full(ret_decay_t.shape), state_in, state_in]
    args = [z, z, z, z, z, z, glr, gate_w, gate_b, ret_decay_t, s0g, s0r]
    if cfg.backward:
        o_fwd, sg_fwd, sr_fwd, gla_ng, ret_ng = extra
        in_specs += [row_blk, zspec("gg"), zspec("rg"), full(gla_ng.shape), full(ret_ng.shape), fwd_state, fwd_state]
        args += [o_fwd, z, z, gla_ng, ret_ng, sg_fwd, sr_fwd]
        out0 = jax.ShapeDtypeStruct((M, 2 * H * DV), BF16)
        state_out = pl.BlockSpec((1, 2, H, DK, DV), lambda i: (cfg.ctx_seq(cfg.block(i)), 0, 0, 0, 0))
        state_shape = jax.ShapeDtypeStruct((cfg.batch, 2, H, DK, DV), F32)
    else:
        out0 = jax.ShapeDtypeStruct((M, 2 * H * DV), F32)
        state_out = fwd_state
        state_shape = jax.ShapeDtypeStruct((cfg.batch, H, DK, DV), F32)
    return pl.pallas_call(
        functools.partial(_scan_kernel, cfg),
        grid=(cfg.nblk,),
        in_specs=in_specs,
        out_specs=[row_blk, state_out, state_out],
        out_shape=[out0, state_shape, state_shape],
        scratch_shapes=[pltpu.VMEM((H, DK, DV), F32), pltpu.VMEM((H, DK, DV), F32),
                        pltpu.VMEM((H * DK, (tb // CHUNK) * LANE), F32)] + [pltpu.VMEM((tb, H * DK), BF16)] * 5,
        compiler_params=_params("arbitrary"),
        name="scan_bwd" if cfg.backward else "scan_fwd",
    )(*args)


def _rope_keep(h, shape):
    lane = lax.broadcasted_iota(jnp.int32, shape, 1)
    return (lane < ROPE_DIM) if h % 2 == 0 else (lane >= ROPE_DIM)


def _mla_q_kernel(n_heads, scale, cq_ref, qng_ref, w_ref, cos_ref, sin_ref, o_ref):
    cqn = _rms(cq_ref[...], qng_ref[...]).astype(BF16)
    q = _dot(cqn, w_ref[...])
    nope_w = n_heads * NOPE_DIM
    rope_w = n_heads * ROPE_DIM
    cos, sin = cos_ref[...], sin_ref[...]
    for h in range(n_heads):
        p = h // 2
        tile = q[:, nope_w + p * LANE:nope_w + (p + 1) * LANE]
        partner = q[:, nope_w + rope_w + p * LANE:nope_w + rope_w + (p + 1) * LANE]
        rot = (tile * cos + partner * sin) * scale
        rot = jnp.where(_rope_keep(h, rot.shape), rot, 0.0)
        base = h * 2 * LANE
        o_ref[:, base:base + LANE] = (q[:, h * NOPE_DIM:(h + 1) * NOPE_DIM] * scale).astype(BF16)
        o_ref[:, base + LANE:base + 2 * LANE] = rot.astype(BF16)


def _mla_q(zc, q_ng, w_uq, cos_t, sin_t, n_heads, scale, tm):
    M = zc.shape[0]
    R = q_ng.shape[0]
    N = w_uq.shape[1]
    return pl.pallas_call(
        functools.partial(_mla_q_kernel, n_heads, scale),
        grid=(M // tm,),
        in_specs=[
            pl.BlockSpec((tm, R), lambda i: (i, 0)),
            pl.BlockSpec((1, R), lambda i: (0, 0)),
            pl.BlockSpec((R, N), lambda i: (0, 0)),
            pl.BlockSpec((tm, LANE), lambda i: (i, 0)),
            pl.BlockSpec((tm, LANE), lambda i: (i, 0)),
        ],
        out_specs=pl.BlockSpec((tm, n_heads * 2 * LANE), lambda i: (i, 0)),
        out_shape=jax.ShapeDtypeStruct((M, n_heads * 2 * LANE), BF16),
        compiler_params=_params("parallel"),
        name="mla_q",
    )(zc, q_ng.reshape(1, R), w_uq, cos_t, sin_t)


class _KeyRows:
    def __init__(self, rows, blk):
        self.rows, self.blk = rows, blk
        self.per_lat = 1 + rows.lat_len // blk
        self.lat_blocks = rows.n_lat * self.per_lat
        self.ctx_blocks = rows.mc // blk
        self.n = self.lat_blocks + self.ctx_blocks

    def is_cache(self, j):
        return (j < self.lat_blocks) & (j % self.per_lat == 0)

    def cache_idx(self, j):
        return jnp.clip(j // self.per_lat, 0, self.rows.n_lat - 1)

    def token_block(self, j):
        b = j // self.per_lat
        t = jnp.maximum(j % self.per_lat - 1, 0)
        lat = self.ctx_blocks + b * (self.per_lat - 1) + t
        return jnp.where(j < self.lat_blocks, lat, j - self.lat_blocks)


def _mla_kv_kernel(keys, n_heads, ckv_ref, kpe_ref, cckv_ref, ckpe_ref, kvng_ref, wk_ref, wvt_ref, cos_ref, sin_ref,
                   kcat_ref, vt_ref, ckvn_ref):
    cached = keys.is_cache(pl.program_id(0))
    kpe = kpe_ref[...]
    rot = kpe[:, :LANE] * cos_ref[...] + kpe[:, LANE:] * sin_ref[...]
    ckvn = jnp.where(cached, cckv_ref[...], _rms(ckv_ref[...], kvng_ref[...]))
    kr = jnp.where(cached, ckpe_ref[...], rot)
    ckvn_ref[...] = ckvn
    ckvn_b = ckvn.astype(BF16)
    kn = _dot(ckvn_b, wk_ref[...])
    vt_ref[...] = _dot_nt(wvt_ref[...], ckvn_b).astype(BF16)
    for h in range(n_heads):
        base = h * 2 * LANE
        kcat_ref[:, base:base + LANE] = kn[:, h * NOPE_DIM:(h + 1) * NOPE_DIM].astype(BF16)
        kcat_ref[:, base + LANE:base + 2 * LANE] = jnp.where(_rope_keep(h, kr.shape), kr, 0.0).astype(BF16)


def _mla_kv(keys, zc, zk, col, cache_ckv, cache_kpe2, kv_ng, w_k, w_vt, cos_t, sin_t, n_heads):
    blk = keys.blk
    R = kv_ng.shape[0]
    Mk = keys.n * blk
    ckv_off, ckv_w = col["ckv"]
    assert ckv_off % ckv_w == 0
    tok = keys.token_block
    return pl.pallas_call(
        functools.partial(_mla_kv_kernel, keys, n_heads),
        grid=(keys.n,),
        in_specs=[
            pl.BlockSpec((blk, ckv_w), lambda j: (tok(j), ckv_off // ckv_w)),
            pl.BlockSpec((blk, 2 * LANE), lambda j: (tok(j), 0)),
            pl.BlockSpec((blk, R), lambda j: (keys.cache_idx(j), 0)),
            pl.BlockSpec((blk, LANE), lambda j: (keys.cache_idx(j), 0)),
            pl.BlockSpec((1, R), lambda j: (0, 0)),
            pl.BlockSpec(w_k.shape, lambda j: (0, 0)),
            pl.BlockSpec(w_vt.shape, lambda j: (0, 0)),
            pl.BlockSpec((blk, LANE), lambda j: (tok(j), 0)),
            pl.BlockSpec((blk, LANE), lambda j: (tok(j), 0)),
        ],
        out_specs=[
            pl.BlockSpec((blk, n_heads * 2 * LANE), lambda j: (j, 0)),
            pl.BlockSpec((n_heads * V_DIM, blk), lambda j: (0, j)),
            pl.BlockSpec((blk, R), lambda j: (j, 0)),
        ],
        out_shape=[
            jax.ShapeDtypeStruct((Mk, n_heads * 2 * LANE), BF16),
            jax.ShapeDtypeStruct((n_heads * V_DIM, Mk), BF16),
            jax.ShapeDtypeStruct((Mk, R), F32),
        ],
        compiler_params=_params("parallel"),
        name="mla_kv",
    )(zc, zk, cache_ckv, cache_kpe2, kv_ng.reshape(1, R), w_k, w_vt, cos_t, sin_t)


ATTN_TQ = 256
ATTN_TK = 2048


def _attn_kernel(tq, chunks, q_ref, k_ref, vt_ref, o_ref, st0_ref, st1_ref):
    n_heads = q_ref.shape[1] // (2 * LANE)
    units = [(h, u) for h in range(n_heads) for u in range(q_ref.shape[0] // tq)]
    st_refs = (st0_ref, st1_ref)

    def scores(n, c0, c1):
        h, u = units[n]
        hd = slice(h * 2 * LANE, (h + 1) * 2 * LANE)
        st = _dot_nt(k_ref[c0:c1, hd], q_ref[u * tq:(u + 1) * tq, hd])
        st_refs[n % 2][c0:c1, :] = st
        return jnp.max(st, axis=0, keepdims=True)

    def weights(n, c0, c1, m):
        h, _ = units[n]
        p = jnp.exp2(st_refs[n % 2][c0:c1, :] - m)
        return jnp.sum(p, axis=0, keepdims=True), _dot(vt_ref[h * V_DIM:(h + 1) * V_DIM, c0:c1], p.astype(BF16))

    m_prev = None
    for n in range(len(units) + 1):
        m_new = acc = l = None
        for c0, c1 in chunks:
            if n < len(units):
                mc = scores(n, c0, c1)
                m_new = mc if m_new is None else jnp.maximum(m_new, mc)
            if n > 0:
                lc, pv = weights(n - 1, c0, c1, m_prev)
                l = lc if l is None else l + lc
                acc = pv if acc is None else acc + pv
        if n > 0:
            h, u = units[n - 1]
            o_ref[u * tq:(u + 1) * tq, h * V_DIM:(h + 1) * V_DIM] = (acc / l).T.astype(o_ref.dtype)
        m_prev = m_new


def _attention(qcat, kcat, vt, n_seq, n_heads, hps, tq, q_blocks_per_seq, q_block0, tk, k_block0):
    sub_q = ATTN_TQ if tq % ATTN_TQ == 0 else LANE
    assert tq % sub_q == 0 and n_heads % hps == 0
    chunks = tuple((c0, min(c0 + ATTN_TK, tk)) for c0 in range(0, tk, ATTN_TK))
    return pl.pallas_call(
        functools.partial(_attn_kernel, sub_q, chunks),
        grid=(n_seq, n_heads // hps, q_blocks_per_seq),
        in_specs=[
            pl.BlockSpec((tq, hps * 2 * LANE), lambda s, h, i: (q_block0 + s * q_blocks_per_seq + i, h)),
            pl.BlockSpec((tk, hps * 2 * LANE), lambda s, h, i: (k_block0 + s, h)),
            pl.BlockSpec((hps * V_DIM, tk), lambda s, h, i: (h, k_block0 + s)),
        ],
        out_specs=pl.BlockSpec((tq, hps * V_DIM), lambda s, h, i: (s * q_blocks_per_seq + i, h)),
        out_shape=jax.ShapeDtypeStruct((n_seq * q_blocks_per_seq * tq, n_heads * V_DIM), BF16),
        scratch_shapes=[pltpu.VMEM((tk, sub_q), F32), pltpu.VMEM((tk, sub_q), F32)],
        compiler_params=_params("parallel", "parallel", "arbitrary"),
        name="mla_attn",
    )(qcat, kcat, vt)


def _rope_partner(w):
    q = ROPE_DIM // 4
    return jnp.flip(w.reshape(w.shape[:-1] + (2, 2, q)), axis=-2).reshape(w.shape)


def _rope_tables(rows):
    half, quarter = ROPE_DIM // 2, ROPE_DIM // 4
    inv = ROPE_BASE ** (-jnp.arange(quarter, dtype=F32) * 2.0 / half)
    t = jnp.arange(rows.lat_len)
    r = (t // GRID_W).astype(F32)
    c = (t % GRID_W).astype(F32)
    ang_r = r[:, None] * inv[None, :]
    ang_c = c[:, None] * inv[None, :]
    cos = jnp.concatenate([jnp.cos(ang_r), jnp.cos(ang_r), jnp.cos(ang_c), jnp.cos(ang_c)], axis=1)
    sin = jnp.concatenate([-jnp.sin(ang_r), jnp.sin(ang_r), -jnp.sin(ang_c), jnp.sin(ang_c)], axis=1)
    cos = jnp.tile(jnp.concatenate([cos, cos], axis=1), (rows.n_lat, 1))
    sin = jnp.tile(jnp.concatenate([sin, sin], axis=1), (rows.n_lat, 1))
    cos = jnp.concatenate([jnp.ones((rows.mc, LANE), F32), cos], axis=0)
    sin = jnp.concatenate([jnp.zeros((rows.mc, LANE), F32), sin], axis=0)
    return cos, sin


def kernel(x_prompt, x_sample, state_gla, state_ret, cache_ckv, cache_kpe, c, c_ctx, mod_w, mod_b, norm1_g, norm2_g, ab_w_in, gla_gate_w2, gla_gate_b, ret_decay, gla_norm_g, ret_norm_g, ab_w_out, mla_w_in, mla_q_norm_g, mla_w_uq, mla_kv_norm_g, mla_w_ukv, mla_w_out, ffn_w_in, ffn_conv, ffn_w_out, final_norm_g):
    B, S, D = x_prompt.shape
    NB, T, _ = x_sample.shape
    depth = mod_w.shape[0]
    _, _, _, HA, DK, DV = state_gla.shape
    HB = state_ret.shape[3]
    assert HA == HB and state_ret.shape[4:] == (DK, DV)
    GR = gla_gate_w2.shape[2]
    past = cache_ckv.shape[2]
    q_rank = mla_q_norm_g.shape[1]
    kv_rank = mla_kv_norm_g.shape[1]
    HC = mla_w_uq.shape[2] // (NOPE_DIM + ROPE_DIM)
    F = ffn_w_out.shape[1]
    rows = _Rows(B * S, S, T, NB)
    M = rows.m
    tm = min(512, S * B, T)
    assert rows.mc % tm == 0 and T % tm == 0
    assert S & (S - 1) == 0 and T & (T - 1) == 0

    tm_proj = min(1024, S * B, T)
    assert rows.mc % tm_proj == 0 and T % tm_proj == 0
    cond = jnp.concatenate([c_ctx[None, :], c, jnp.zeros((MOD_GROUPS - 1 - NB, D), F32)], axis=0)
    mod3 = _modulation(cond, mod_w, mod_b)

    x = (x_prompt.reshape(B * S, D), x_sample.reshape(NB * T, D))
    h = _normmod(rows, x, norm1_g[0], mod3[0], tm)
    ffn_w = (ffn_w_in.astype(BF16), ffn_conv, ffn_w_out.astype(BF16))
    new_gla, new_ret, new_ckv, new_kpe = [], [], [], []
    for l in range(depth):
        if l % 2 == 0:
            e = l // 2
            names = ("gq", "gk", "gv", "gg", "glr", "rq", "rk", "rv", "rg")
            sizes = (HA * DK, HA * DK, HA * DV, HA * DV, 2 * GR, HB * DK, HB * DK, HB * DV, HB * DV)
            src, o = {}, 0
            for n, sz in zip(names, sizes):
                src[n] = (o, sz)
                o += sz
            order = ("gq", "gk", "gv", "gg", "rq", "rk", "rv", "rg")
            col, o = {}, 0
            for n in order:
                col[n] = (o, src[n][1])
                o += src[n][1]
            so, sz = src["glr"]
            assert so % LANE == 0 and so + LANE <= ab_w_in.shape[2]
            w_all = lax.optimization_barrier(ab_w_in[e].astype(BF16))
            w_in = jnp.concatenate([w_all[:, :so], w_all[:, so + sz:]], axis=1)
            w_glr = w_all[:, so:so + LANE]
            z, glr = _inproj(h, w_in, w_glr, BF16, tm_proj)
            tb = min(256, S)
            extra = None
            for dirn in (0, 1):
                cfg = _ScanCfg(rows, B, tb, HA, DK, DV, backward=bool(dirn))
                gate_w = jnp.zeros((LANE, HA * DK), F32).at[dirn * GR:(dirn + 1) * GR].set(gla_gate_w2[e, dirn])
                rd = jnp.broadcast_to(ret_decay[e, dirn][:, None, None], (HB, 8, LANE))
                y, sg, sr = _scan(cfg, z, glr, col, gate_w.astype(BF16), gla_gate_b[e, dirn].reshape(1, -1), rd,
                                  state_gla, state_ret, e, extra)
                extra = (y, sg, sr, gla_norm_g[e].reshape(1, -1), ret_norm_g[e].reshape(1, -1))
            new_gla.append(sg)
            new_ret.append(sr)
            x, h = _outproj(rows, y, ab_w_out[e].astype(BF16), x, norm2_g[l], mod3[l], tm)
        else:
            i = l // 2
            w = mla_w_in[i]
            w_kpe = w[:, q_rank + kv_rank:]
            w_in = w[:, :q_rank + kv_rank].astype(BF16)
            w_kpe_p = _rope_partner(w_kpe)
            w_side = jnp.concatenate([w_kpe, w_kpe, w_kpe_p, w_kpe_p], axis=1).astype(BF16)
            col = {"cq": (0, q_rank), "ckv": (q_rank, kv_rank)}
            zc, zk = _inproj(h, w_in, w_side, F32, tm_proj)
            cos_t, sin_t = _rope_tables(rows)
            wq = mla_w_uq[i].reshape(q_rank, HC, NOPE_DIM + ROPE_DIM)
            wq_rope = wq[:, :, NOPE_DIM:]
            w_uq = jnp.concatenate([wq[:, :, :NOPE_DIM].reshape(q_rank, -1), wq_rope.reshape(q_rank, -1),
                                    _rope_partner(wq_rope).reshape(q_rank, -1)], axis=1).astype(BF16)
            scale = (NOPE_DIM + ROPE_DIM) ** -0.5 * LOG2_E
            assert col["cq"][0] == 0
            qcat = _mla_q(zc, mla_q_norm_g[i], w_uq, cos_t, sin_t, HC, scale, tm)
            wkv = mla_w_ukv[i].reshape(kv_rank, HC, NOPE_DIM + V_DIM)
            w_k = wkv[:, :, :NOPE_DIM].reshape(kv_rank, -1).astype(BF16)
            w_vt = wkv[:, :, NOPE_DIM:].reshape(kv_rank, -1).T.astype(BF16)
            assert S % past == 0 and T % past == 0
            keys = _KeyRows(rows, past)
            ckpe = cache_kpe[:, i].reshape(NB * past, ROPE_DIM)
            kcat, vt, ckvn = _mla_kv(keys, zc, zk, col, cache_ckv[:, i].reshape(NB * past, kv_rank),
                                     jnp.concatenate([ckpe, ckpe], axis=1), mla_kv_norm_g[i], w_k, w_vt,
                                     cos_t, sin_t, HC)
            tk_lat = past + T
            assert (keys.lat_blocks * past) % S == 0
            tq = min(4096, T)
            assert rows.mc % tq == 0
            o = (_attention(qcat, kcat, vt, B, HC, HC, S, 1, 0, S, keys.lat_blocks * past // S),
                 _attention(qcat, kcat, vt, NB, HC, 1, tq, T // tq, rows.mc // tq, tk_lat, 0))
            new_ckv.append(ckvn[keys.lat_blocks * past:].reshape(B, S, kv_rank))
            new_kpe.append(zk[:rows.mc, :ROPE_DIM].reshape(B, S, ROPE_DIM))
            x, h = _outproj(rows, o, mla_w_out[i].astype(BF16), x, norm2_g[l], mod3[l], tm)
        tf = _largest_tile(F, 512)
        if l == depth - 1:
            y_ctx, y_lat = _ffn(rows, h, x, mod3[l], l, *ffn_w, tm, tf, final_g=final_norm_g)
        else:
            x, h = _ffn(rows, h, x, mod3[l], l, *ffn_w, tm, tf, next_norm=(norm1_g[l + 1], mod3[l + 1]))

    y_prompt = y_ctx.reshape(B, S, D)
    y_sample = y_lat.reshape(NB, T, D)
    return (y_prompt, y_sample, jnp.stack(new_gla, axis=1), jnp.stack(new_ret, axis=1),
            jnp.stack(new_ckv, axis=1), jnp.stack(new_kpe, axis=1))
```

```python
import functools

import jax
import jax.numpy as jnp
from jax import lax
from jax.experimental import pallas as pl
from jax.experimental.pallas import tpu as pltpu

F32 = jnp.float32
BF16 = jnp.bfloat16

NORM_EPS = 1e-6
GATE_TEMP = 16.0
CHUNK = 64
GRID_W = 64
ROPE_BASE = 10000.0
ROPE_DIM = 64
NOPE_DIM = 128
V_DIM = 128
CONV_W = 3
LOG2_E = 1.4426950408889634

LANE = 128
BF16_SUBLANE = 16
MOD_GROUPS = 8
VMEM_LIMIT = 56 * 1024 * 1024

NT_DIMS = (((1,), (1,)), ((), ()))
TN_DIMS = (((0,), (0,)), ((), ()))


def _dot(a, b):
    return jnp.dot(a, b, preferred_element_type=F32)


def _dot_nt(a, b):
    return lax.dot_general(a, b, NT_DIMS, preferred_element_type=F32)


def _dot_tn(a, b):
    return lax.dot_general(a, b, TN_DIMS, preferred_element_type=F32)


def _params(*sem):
    return pltpu.CompilerParams(dimension_semantics=sem, vmem_limit_bytes=VMEM_LIMIT)


def _rms(x, g):
    ms = jnp.mean(x * x, axis=-1, keepdims=True)
    return (x * lax.rsqrt(ms + NORM_EPS)) * g


def _silu(x):
    return x * jax.nn.sigmoid(x)


def _log_sigmoid(x):
    return jnp.minimum(x, 0.0) - jnp.log(1.0 + jnp.exp(-jnp.abs(x)))


def _largest_tile(n, cap):
    best = None
    for t in range(LANE, min(n, cap) + 1, LANE):
        if n % t == 0:
            best = t
    assert best is not None, (n, cap)
    return best


def _mod_kernel(c_ref, w_ref, b_ref, o_ref):
    s = _silu(c_ref[...]).astype(BF16)
    m = _dot(s, w_ref[0].astype(BF16)) + b_ref[0]
    for g in range(MOD_GROUPS):
        o_ref[0, g, 0] = m[g:g + 1]


def _modulation(cond, mod_w, mod_b):
    L, D, N = mod_w.shape
    n_vec = N // D
    out = pl.pallas_call(
        _mod_kernel,
        grid=(L, n_vec),
        in_specs=[
            pl.BlockSpec((MOD_GROUPS, D), lambda l, k: (0, 0)),
            pl.BlockSpec((1, D, D), lambda l, k: (l, 0, k)),
            pl.BlockSpec((1, 1, D), lambda l, k: (l, 0, k)),
        ],
        out_specs=pl.BlockSpec((1, MOD_GROUPS, 1, 1, D), lambda l, k: (l, 0, k, 0, 0)),
        out_shape=jax.ShapeDtypeStruct((L, MOD_GROUPS, n_vec, 1, D), F32),
        compiler_params=_params("parallel", "parallel"),
        name="modulation",
    )(cond, mod_w, mod_b.reshape(L, 1, N))
    return out.reshape(L, MOD_GROUPS * n_vec, 1, D)


class _Rows:
    def __init__(self, mc, seq, lat_len, n_lat):
        self.mc, self.seq, self.lat_len, self.n_lat = mc, seq, lat_len, n_lat
        self.m = mc + lat_len * n_lat

    def group(self, i, tm):
        r = i * tm
        return jnp.where(r < self.mc, 0, 1 + (r - self.mc) // self.lat_len)

    def mod_spec(self, which, tm, d):
        return pl.BlockSpec((1, 1, d), lambda i, *_: (self.group(i, tm) * 6 + which, 0, 0))

    def split_specs(self, tm, d):
        n_ctx = self.mc // tm
        return [pl.BlockSpec((tm, d), lambda i, *_: (jnp.minimum(i, n_ctx - 1), 0)),
                pl.BlockSpec((tm, d), lambda i, *_: (jnp.maximum(i - n_ctx, 0), 0))]


ROW_CHUNK = 16
ROW_PARTS = 2


def _for_row_chunks(n_rows, body):
    def it(c, carry):
        body(pl.ds(pl.multiple_of(c * ROW_CHUNK, ROW_CHUNK), ROW_CHUNK))
        return carry
    lax.fori_loop(0, n_rows // ROW_CHUNK, it, 0, unroll=4)


def _fold_gain(gs_ref, g_ref, sc_ref):
    gs_ref[...] = g_ref[...] * (1.0 + sc_ref[0])


def _norm_mod_rows(x, gs_ref, sh_ref):
    ms = jnp.mean(x * x, axis=-1, keepdims=True)
    return (x * lax.rsqrt(ms + NORM_EPS)) * gs_ref[...] + sh_ref[0]


def _normmod_kernel(rows, tm, xa_ref, xb_ref, g_ref, sh_ref, sc_ref, h_ref, gs_ref):
    _fold_gain(gs_ref, g_ref, sc_ref)

    def run(x_ref):
        def body(r):
            h_ref[r, :] = _norm_mod_rows(x_ref[r, :], gs_ref, sh_ref).astype(h_ref.dtype)
        _for_row_chunks(tm, body)

    is_ctx = pl.program_id(0) * tm < rows.mc
    pl.when(is_ctx)(lambda: run(xa_ref))
    pl.when(jnp.logical_not(is_ctx))(lambda: run(xb_ref))


def _normmod(rows, x_pair, g, mod3, tm):
    D = x_pair[0].shape[1]
    return pl.pallas_call(
        functools.partial(_normmod_kernel, rows, tm),
        grid=(rows.m // tm,),
        in_specs=rows.split_specs(tm, D) + [
            pl.BlockSpec((1, D), lambda i: (0, 0)),
            rows.mod_spec(0, tm, D),
            rows.mod_spec(1, tm, D),
        ],
        out_specs=pl.BlockSpec((tm, D), lambda i: (i, 0)),
        out_shape=jax.ShapeDtypeStruct((rows.m, D), BF16),
        scratch_shapes=[pltpu.VMEM((1, D), F32)],
        compiler_params=_params("parallel"),
        name="normmod",
    )(*x_pair, g.reshape(1, D), mod3, mod3)


def _inproj_kernel(h_ref, w_ref, ws_ref, o_ref, os_ref):
    @pl.when(pl.program_id(1) == 0)
    def _():
        os_ref[...] = _dot(h_ref[...], ws_ref[...])

    o_ref[...] = _dot(h_ref[...], w_ref[...]).astype(o_ref.dtype)


def _inproj(h, w, w_side, out_dtype, tm, tn_cap=2048):
    M, D = h.shape
    N = w.shape[1]
    NS = w_side.shape[1]
    tn = _largest_tile(N, tn_cap)
    return pl.pallas_call(
        _inproj_kernel,
        grid=(M // tm, N // tn),
        in_specs=[
            pl.BlockSpec((tm, D), lambda i, j: (i, 0)),
            pl.BlockSpec((D, tn), lambda i, j: (0, j)),
            pl.BlockSpec((D, NS), lambda i, j: (0, 0)),
        ],
        out_specs=[pl.BlockSpec((tm, tn), lambda i, j: (i, j)), pl.BlockSpec((tm, NS), lambda i, j: (i, 0))],
        out_shape=[jax.ShapeDtypeStruct((M, N), out_dtype), jax.ShapeDtypeStruct((M, NS), F32)],
        compiler_params=_params("parallel", "arbitrary"),
        name="inproj",
    )(h, w, w_side)


def _outproj_kernel(rows, tm, y_split, x_split, *refs):
    refs = list(refs)
    y_refs = [refs.pop(0) for _ in range(2 if y_split else 1)]
    w_ref = refs.pop(0)
    x_refs = [refs.pop(0) for _ in range(2 if x_split else 1)]
    gate_ref, g_ref, sh_ref, sc_ref, o_ref, h_ref, acc_ref, gs_ref = refs
    _fold_gain(gs_ref, g_ref, sc_ref)
    is_ctx = pl.program_id(0) * tm < rows.mc
    part = tm // ROW_PARTS

    def rows_of(pair, r):
        return jnp.where(is_ctx, pair[0][r, :], pair[1][r, :]) if len(pair) == 2 else pair[0][r, :]

    def finish(r):
        x1 = rows_of(x_refs, r) + gate_ref[0] * acc_ref[r, :]
        o_ref[r, :] = x1
        h_ref[r, :] = _norm_mod_rows(x1, gs_ref, sh_ref).astype(h_ref.dtype)

    for p in range(ROW_PARTS + 1):
        if p < ROW_PARTS:
            rp = slice(p * part, (p + 1) * part)
            acc_ref[rp, :] = _dot(rows_of(y_refs, rp), w_ref[...])
        if p > 0:
            for c in range(part // ROW_CHUNK):
                start = (p - 1) * part + c * ROW_CHUNK
                finish(slice(start, start + ROW_CHUNK))


def _outproj(rows, y, w, x, g, mod3, tm):
    K, N = w.shape
    M = rows.m

    def operand(a, width):
        if isinstance(a, (tuple, list)):
            return True, rows.split_specs(tm, width), list(a)
        return False, [pl.BlockSpec((tm, width), lambda i: (i, 0))], [a]

    y_split, y_specs, y_args = operand(y, K)
    x_split, x_specs, x_args = operand(x, N)
    return pl.pallas_call(
        functools.partial(_outproj_kernel, rows, tm, y_split, x_split),
        grid=(M // tm,),
        in_specs=y_specs + [pl.BlockSpec((K, N), lambda i: (0, 0))] + x_specs + [
            rows.mod_spec(2, tm, N),
            pl.BlockSpec((1, N), lambda i: (0, 0)),
            rows.mod_spec(3, tm, N),
            rows.mod_spec(4, tm, N),
        ],
        out_specs=[pl.BlockSpec((tm, N), lambda i: (i, 0)), pl.BlockSpec((tm, N), lambda i: (i, 0))],
        out_shape=[jax.ShapeDtypeStruct((M, N), F32), jax.ShapeDtypeStruct((M, N), BF16)],
        scratch_shapes=[pltpu.VMEM((tm, N), F32), pltpu.VMEM((1, N), F32)],
        compiler_params=_params("parallel"),
        name="outproj",
    )(*y_args, w, *x_args, mod3, g.reshape(1, N), mod3, mod3)


HALO = BF16_SUBLANE


def _ffn_kernel(rows, tm, final_norm, hp_ref, h_ref, hn_ref, x_ref, gate_ref,
                wa_ref, wb_ref, cw_ref, wo_ref, *rest):
    if final_norm:
        fg_ref, o_ctx_ref, o_lat_ref, hs_ref, a_ref, acc_ref = rest
    else:
        g_ref, sh_ref, sc_ref, o_ref, hnext_ref, hs_ref, a_ref, acc_ref, gs_ref = rest
    i = pl.program_id(0)
    f = pl.program_id(1)

    @pl.when(f == 0)
    def _():
        hs_ref[0:tm, :] = h_ref[...]
        last_row = lax.broadcasted_iota(jnp.int32, (HALO, 1), 0) == HALO - 1
        hs_ref[tm:, :] = jnp.where(last_row, hp_ref[...].astype(F32), hn_ref[...].astype(F32)).astype(BF16)
        acc_ref[...] = jnp.zeros_like(acc_ref)

    def hidden_tile():
        a = _dot(hs_ref[...], wa_ref[0])
        a_ref[HALO:, :] = a
        a_ref[HALO - 8:HALO, :] = a[tm + HALO - 8:, :]
        b = _dot(h_ref[...], wb_ref[0])
        row = i * tm + lax.broadcasted_iota(jnp.int32, (tm, 1), 0)
        pos = jnp.where(row < rows.mc, row & (rows.seq - 1), (row - rows.mc) & (rows.lat_len - 1))
        seq_len = jnp.where(row < rows.mc, rows.seq, rows.lat_len)
        a_prev = jnp.where(pos == 0, 0.0, a_ref[pl.ds(HALO - 1, tm), :])
        a_next = jnp.where(pos == seq_len - 1, 0.0, a_ref[pl.ds(HALO + 1, tm), :])
        a_mid = a_ref[pl.ds(HALO, tm), :]
        cw = cw_ref[0]
        a = cw[0:1] * a_prev + cw[1:2] * a_mid + cw[2:3] * a_next
        return (_silu(a) * b).astype(BF16)

    def residual(r):
        return x_ref[r, :] + gate_ref[0] * acc_ref[r, :]

    def last_step(finish):
        act = hidden_tile()
        part = tm // ROW_PARTS
        for p in range(ROW_PARTS + 1):
            if p < ROW_PARTS:
                rp = slice(p * part, (p + 1) * part)
                acc_ref[rp, :] += _dot(act[rp, :], wo_ref[0])
            if p > 0:
                for c in range(part // ROW_CHUNK):
                    start = (p - 1) * part + c * ROW_CHUNK
                    finish(slice(start, start + ROW_CHUNK))

    last = pl.num_programs(1) - 1

    @pl.when(f < last)
    def _():
        acc_ref[...] += _dot(hidden_tile(), wo_ref[0])

    if final_norm:
        def finish_to(out_ref):
            def finish(r):
                out_ref[r, :] = _rms(residual(r), fg_ref[...])
            return finish

        is_ctx = i * tm < rows.mc
        pl.when((f == last) & is_ctx)(lambda: last_step(finish_to(o_ctx_ref)))
        pl.when((f == last) & jnp.logical_not(is_ctx))(lambda: last_step(finish_to(o_lat_ref)))
    else:
        def finish(r):
            x2 = residual(r)
            o_ref[r, :] = x2
            hnext_ref[r, :] = _norm_mod_rows(x2, gs_ref, sh_ref).astype(hnext_ref.dtype)

        @pl.when(f == last)
        def _():
            _fold_gain(gs_ref, g_ref, sc_ref)
            last_step(finish)


def _ffn(rows, h, x, mod3, layer, w_in, conv_w, w_out, tm, tf, next_norm=None, final_g=None):
    M, D = x.shape
    F = w_out.shape[1]
    nf = F // tf
    nhalo = M // HALO
    final_norm = final_g is not None
    kern = functools.partial(_ffn_kernel, rows, tm, final_norm)
    in_specs = [
        pl.BlockSpec((HALO, D), lambda i, f: (jnp.maximum(i * (tm // HALO) - 1, 0), 0)),
        pl.BlockSpec((tm, D), lambda i, f: (i, 0)),
        pl.BlockSpec((HALO, D), lambda i, f: (jnp.minimum((i + 1) * (tm // HALO), nhalo - 1), 0)),
        pl.BlockSpec((tm, D), lambda i, f: (i, 0)),
        rows.mod_spec(5, tm, D),
        pl.BlockSpec((1, D, tf), lambda i, f: (layer, 0, f)),
        pl.BlockSpec((1, D, tf), lambda i, f: (layer, 0, nf + f)),
        pl.BlockSpec((1, CONV_W, tf), lambda i, f: (layer, 0, f)),
        pl.BlockSpec((1, tf, D), lambda i, f: (layer, f, 0)),
    ]
    args = [h, h, h, x, mod3, w_in, w_in, conv_w, w_out]
    vec = pl.BlockSpec((1, D), lambda i, f: (0, 0))
    if final_norm:
        n_ctx = rows.mc // tm
        in_specs += [vec]
        args += [final_g.reshape(1, D)]
        out_specs = [pl.BlockSpec((tm, D), lambda i, f: (jnp.minimum(i, n_ctx - 1), 0)),
                     pl.BlockSpec((tm, D), lambda i, f: (jnp.maximum(i - n_ctx, 0), 0))]
        out_shape = [jax.ShapeDtypeStruct((rows.mc, D), F32), jax.ShapeDtypeStruct((M - rows.mc, D), F32)]
        row_sem = "arbitrary"
    else:
        g_next, mod3_next = next_norm
        in_specs += [vec, rows.mod_spec(0, tm, D), rows.mod_spec(1, tm, D)]
        args += [g_next.reshape(1, D), mod3_next, mod3_next]
        out_specs = [pl.BlockSpec((tm, D), lambda i, f: (i, 0)), pl.BlockSpec((tm, D), lambda i, f: (i, 0))]
        out_shape = [jax.ShapeDtypeStruct((M, D), F32), jax.ShapeDtypeStruct((M, D), BF16)]
        row_sem = "parallel"
    return pl.pallas_call(
        kern,
        grid=(M // tm, nf),
        in_specs=in_specs,
        out_specs=out_specs,
        out_shape=out_shape,
        scratch_shapes=[
            pltpu.VMEM((tm + HALO, D), BF16),
            pltpu.VMEM((tm + 2 * HALO, tf), F32),
            pltpu.VMEM((tm, D), F32),
        ] + ([] if final_norm else [pltpu.VMEM((1, D), F32)]),
        compiler_params=_params(row_sem, "arbitrary"),
        name="convffn",
    )(*args)


class _ScanCfg:
    def __init__(self, rows, batch, tb, h, dk, dv, backward):
        self.rows, self.batch, self.tb, self.h, self.dk, self.dv = rows, batch, tb, h, dk, dv
        self.backward = backward
        self.cps = rows.seq // tb
        self.lps = rows.lat_len // tb
        self.ctx_blocks = batch * self.cps
        self.nblk = rows.m // tb

    def block(self, i):
        return self.nblk - 1 - i if self.backward else i

    def is_ctx(self, r):
        return r < self.ctx_blocks

    def seq_pos(self, r):
        ctx = self.is_ctx(r)
        return (jnp.where(ctx, r % self.cps, (r - self.ctx_blocks) % self.lps),
                jnp.where(ctx, self.cps, self.lps))

    def lat_seq(self, r):
        return jnp.clip((r - self.ctx_blocks) // self.lps, 0, self.rows.n_lat - 1)

    def ctx_seq(self, r):
        return jnp.clip(r // self.cps, 0, self.batch - 1)


def _scan_kernel(cfg, gq_ref, gk_ref, gv_ref, rq_ref, rk_ref, rv_ref, glr_ref, gw_ref, gb_ref, rd_ref,
                 s0g_ref, s0r_ref, *rest):
    scratch = rest[-8:]
    sg_ref, sr_ref, dend_ref, qd_ref, kinv_ref, kend_ref, rks_ref, rkd_ref = scratch
    if cfg.backward:
        of_ref, gg_ref, rg_ref, gng_ref, rng_ref, sgf_ref, srf_ref, y_ref, sgo_ref, sro_ref = rest[:-8]
    else:
        o_ref, sgo_ref, sro_ref = rest[:-8]
    H, DK, DV, C = cfg.h, cfg.dk, cfg.dv, CHUNK
    r = cfg.block(pl.program_id(0))
    blk, nblk_seq = cfg.seq_pos(r)
    first = blk == (nblk_seq - 1 if cfg.backward else 0)
    last = blk == (0 if cfg.backward else nblk_seq - 1)
    is_ctx = cfg.is_ctx(r)

    @pl.when(first & is_ctx)
    def _():
        sg_ref[...] = jnp.zeros_like(sg_ref)
        sr_ref[...] = jnp.zeros_like(sr_ref)

    @pl.when(first & jnp.logical_not(is_ctx))
    def _():
        sg_ref[...] = s0g_ref[0, 0, 0]
        sr_ref[...] = s0r_ref[0, 0, 0]

    ti = lax.broadcasted_iota(jnp.int32, (C, C), 0)
    tj = lax.broadcasted_iota(jnp.int32, (C, C), 1)
    sees = (tj >= ti) if cfg.backward else (tj <= ti)
    dist = jnp.abs(ti - tj).astype(F32)
    rowi = lax.broadcasted_iota(jnp.int32, (C, LANE), 0)
    to_end = (rowi if cfg.backward else C - 1 - rowi).astype(F32)
    from_start = (C - rowi if cfg.backward else rowi + 1).astype(F32)

    n_chunks = cfg.tb // C
    HDK = H * DK
    scale = DK ** -0.5

    bi = lax.broadcasted_iota(jnp.int32, (cfg.tb, cfg.tb), 0)
    bj = lax.broadcasted_iota(jnp.int32, (cfg.tb, cfg.tb), 1)
    same_chunk = (bi & -C) == (bj & -C)
    blk_tri = jnp.where(same_chunk & ((bj >= bi) if cfg.backward else (bj <= bi)), 1.0, 0.0).astype(BF16)
    pre = _dot(glr_ref[...].astype(BF16), gw_ref[...]) + gb_ref[...]
    la = _log_sigmoid(pre) / GATE_TEMP
    la_hi = la.astype(BF16)
    la_lo = (la - la_hi.astype(F32)).astype(BF16)
    b = _dot(blk_tri, la_hi) + _dot(blk_tri, la_lo)
    b3 = b.reshape(n_chunks, C, HDK)
    b_end = b3[:, 0:1, :] if cfg.backward else b3[:, C - 1:C, :]
    eb = jnp.exp(b)
    chunk_of_row = jnp.right_shift(lax.broadcasted_iota(jnp.int32, (cfg.tb, LANE), 0), C.bit_length() - 1)
    lane = lax.broadcasted_iota(jnp.int32, (cfg.tb, LANE), 1)
    in_chunk = jnp.where(chunk_of_row == lane, 1.0, 0.0).astype(BF16)
    dend = jnp.exp(_dot_tn(la_hi, in_chunk) + _dot_tn(la_lo, in_chunk))
    for c in range(n_chunks):
        dend_ref[:, c * LANE:(c + 1) * LANE] = jnp.broadcast_to(dend[:, c:c + 1], (HDK, LANE))
    gk = gk_ref[...].astype(F32)
    qd_ref[...] = (gq_ref[...].astype(F32) * scale * eb).astype(BF16)
    kinv_ref[...] = (gk * jnp.exp(-b)).astype(BF16)
    kend_ref[...] = (gk * jnp.exp(b_end - b3).reshape(cfg.tb, HDK)).astype(BF16)

    decays, q_decs, c_decs, k_dec_cols = [], [], [], []
    for h in range(H):
        lg = _log_sigmoid(rd_ref[h])
        lg_c = jnp.broadcast_to(lg[0:1, 0:C], (C, C))
        lg_l = jnp.broadcast_to(lg[0:1, :], (C, LANE))
        decays.append(jnp.where(sees, jnp.exp(lg_c * dist), 0.0))
        q_decs.append(jnp.concatenate([jnp.exp(lg_l * from_start)] * (DV // LANE), axis=1))
        c_decs.append(jnp.concatenate([jnp.exp(lg[0:1, :] * float(C))] * (DV // LANE), axis=1))
        k_dec_cols.append(jnp.concatenate([jnp.exp(lg_l * to_end)] * n_chunks, axis=0))
    rk = rk_ref[...].astype(F32) * scale
    rks_ref[...] = rk.astype(BF16)
    rkd_ref[...] = (rk * jnp.concatenate(k_dec_cols, axis=1)).astype(BF16)

    for ci in range(n_chunks):
        c = n_chunks - 1 - ci if cfg.backward else ci
        rs = slice(c * C, (c + 1) * C)
        heads = range(H)
        ksl = [slice(h * DK, (h + 1) * DK) for h in heads]
        vsl = [slice(h * DV, (h + 1) * DV) for h in heads]
        att_raw = [_dot_nt(qd_ref[rs, ksl[h]], kinv_ref[rs, ksl[h]]) for h in heads]
        ratt_raw = [_dot_nt(rq_ref[rs, ksl[h]], rks_ref[rs, ksl[h]]) for h in heads]
        qs_g = [_dot(qd_ref[rs, ksl[h]], sg_ref[h].astype(BF16)) for h in heads]
        qs_r = [_dot(rq_ref[rs, ksl[h]], sr_ref[h].astype(BF16)) for h in heads]
        for h in heads:
            att = jnp.where(sees, att_raw[h], 0.0).astype(BF16)
            o_g = _dot(att, gv_ref[rs, vsl[h]]) + qs_g[h]
            ratt = (ratt_raw[h] * decays[h]).astype(BF16)
            o_r = _dot(ratt, rv_ref[rs, vsl[h]]) + qs_r[h] * q_decs[h]
            rs_cols = slice(H * DV + h * DV, H * DV + (h + 1) * DV)
            if cfg.backward:
                for o, cols, gate_ref, ng_ref in ((o_g, vsl[h], gg_ref, gng_ref), (o_r, rs_cols, rg_ref, rng_ref)):
                    tot = o + of_ref[rs, cols]
                    mu = jnp.mean(tot, axis=-1, keepdims=True)
                    d = tot - mu
                    var = jnp.mean(d * d, axis=-1, keepdims=True)
                    yn = d * lax.rsqrt(var + NORM_EPS) * ng_ref[:, vsl[h]]
                    y_ref[rs, cols] = (_silu(gate_ref[rs, vsl[h]].astype(F32)) * yn).astype(y_ref.dtype)
            else:
                o_ref[rs, vsl[h]] = o_g
                o_ref[rs, rs_cols] = o_r
        for h in heads:
            dend = dend_ref[ksl[h], c * LANE:(c + 1) * LANE]
            sg_ref[h] = (sg_ref[h] * jnp.concatenate([dend] * (DV // LANE), axis=1)
                         + _dot_tn(kend_ref[rs, ksl[h]], gv_ref[rs, vsl[h]]))
            sr_ref[h] = sr_ref[h] * c_decs[h] + _dot_tn(rkd_ref[rs, ksl[h]], rv_ref[rs, vsl[h]])

    @pl.when(last & is_ctx)
    def _():
        if cfg.backward:
            sgo_ref[0, 0] = sgf_ref[0]
            sro_ref[0, 0] = srf_ref[0]
            sgo_ref[0, 1] = sg_ref[...]
            sro_ref[0, 1] = sr_ref[...]
        else:
            sgo_ref[0] = sg_ref[...]
            sro_ref[0] = sr_ref[...]


def _scan(cfg, z, glr, col, gate_w, gate_b, ret_decay_t, s0g, s0r, layer, extra):
    H, DK, DV, tb = cfg.h, cfg.dk, cfg.dv, cfg.tb
    M = z.shape[0]
    dirn = int(cfg.backward)

    def zspec(name):
        off, width = col[name]
        assert off % width == 0, (name, off, width)
        return pl.BlockSpec((tb, width), lambda i: (cfg.block(i), off // width))

    def full(shape):
        return pl.BlockSpec(shape, lambda i: (0,) * len(shape))

    state_in = pl.BlockSpec((1, 1, 1, H, DK, DV),
                            lambda i: (cfg.lat_seq(cfg.block(i)), layer, dirn, 0, 0, 0))
    fwd_state = pl.BlockSpec((1, H, DK, DV), lambda i: (cfg.ctx_seq(cfg.block(i)), 0, 0, 0))
    row_blk = pl.BlockSpec((tb, 2 * H * DV), lambda i: (cfg.block(i), 0))
    glr_spec = pl.BlockSpec((tb, glr.shape[1]), lambda i: (cfg.block(i), 0))
    in_specs = [zspec("gq"), zspec("gk"), zspec("gv"), zspec("rq"), zspec("rk"), zspec("rv"), glr_spec,
                full(gate_w.shape), full(gate_b.shape), full(ret_decay_t.shape), state_in, state_in]
    args = [z, z, z, z, z, z, glr, gate_w, gate_b, ret_decay_t, s0g, s0r]
    if cfg.backward:
        o_fwd, sg_fwd, sr_fwd, gla_ng, ret_ng = extra
        in_specs += [row_blk, zspec("gg"), zspec("rg"), full(gla_ng.shape), full(ret_ng.shape), fwd_state, fwd_state]
        args += [o_fwd, z, z, gla_ng, ret_ng, sg_fwd, sr_fwd]
        out0 = jax.ShapeDtypeStruct((M, 2 * H * DV), BF16)
        state_out = pl.BlockSpec((1, 2, H, DK, DV), lambda i: (cfg.ctx_seq(cfg.block(i)), 0, 0, 0, 0))
        state_shape = jax.ShapeDtypeStruct((cfg.batch, 2, H, DK, DV), F32)
    else:
        out0 = jax.ShapeDtypeStruct((M, 2 * H * DV), F32)
        state_out = fwd_state
        state_shape = jax.ShapeDtypeStruct((cfg.batch, H, DK, DV), F32)
    return pl.pallas_call(
        functools.partial(_scan_kernel, cfg),
        grid=(cfg.nblk,),
        in_specs=in_specs,
        out_specs=[row_blk, state_out, state_out],
        out_shape=[out0, state_shape, state_shape],
        scratch_shapes=[pltpu.VMEM((H, DK, DV), F32), pltpu.VMEM((H, DK, DV), F32),
                        pltpu.VMEM((H * DK, (tb // CHUNK) * LANE), F32)] + [pltpu.VMEM((tb, H * DK), BF16)] * 5,
        compiler_params=_params("arbitrary"),
        name="scan_bwd" if cfg.backward else "scan_fwd",
    )(*args)


def _rope_keep(h, shape):
    lane = lax.broadcasted_iota(jnp.int32, shape, 1)
    return (lane < ROPE_DIM) if h % 2 == 0 else (lane >= ROPE_DIM)


def _mla_q_kernel(n_heads, scale, cq_ref, qng_ref, w_ref, cos_ref, sin_ref, o_ref):
    cqn = _rms(cq_ref[...], qng_ref[...]).astype(BF16)
    q = _dot(cqn, w_ref[...])
    nope_w = n_heads * NOPE_DIM
    rope_w = n_heads * ROPE_DIM
    cos, sin = cos_ref[...], sin_ref[...]
    for h in range(n_heads):
        p = h // 2
        tile = q[:, nope_w + p * LANE:nope_w + (p + 1) * LANE]
        partner = q[:, nope_w + rope_w + p * LANE:nope_w + rope_w + (p + 1) * LANE]
        rot = (tile * cos + partner * sin) * scale
        rot = jnp.where(_rope_keep(h, rot.shape), rot, 0.0)
        base = h * 2 * LANE
        o_ref[:, base:base + LANE] = (q[:, h * NOPE_DIM:(h + 1) * NOPE_DIM] * scale).astype(BF16)
        o_ref[:, base + LANE:base + 2 * LANE] = rot.astype(BF16)


def _mla_q(zc, q_ng, w_uq, cos_t, sin_t, n_heads, scale, tm):
    M = zc.shape[0]
    R = q_ng.shape[0]
    N = w_uq.shape[1]
    return pl.pallas_call(
        functools.partial(_mla_q_kernel, n_heads, scale),
        grid=(M // tm,),
        in_specs=[
            pl.BlockSpec((tm, R), lambda i: (i, 0)),
            pl.BlockSpec((1, R), lambda i: (0, 0)),
            pl.BlockSpec((R, N), lambda i: (0, 0)),
            pl.BlockSpec((tm, LANE), lambda i: (i, 0)),
            pl.BlockSpec((tm, LANE), lambda i: (i, 0)),
        ],
        out_specs=pl.BlockSpec((tm, n_heads * 2 * LANE), lambda i: (i, 0)),
        out_shape=jax.ShapeDtypeStruct((M, n_heads * 2 * LANE), BF16),
        compiler_params=_params("parallel"),
        name="mla_q",
    )(zc, q_ng.reshape(1, R), w_uq, cos_t, sin_t)


class _KeyRows:
    def __init__(self, rows, blk):
        self.rows, self.blk = rows, blk
        self.per_lat = 1 + rows.lat_len // blk
        self.lat_blocks = rows.n_lat * self.per_lat
        self.ctx_blocks = rows.mc // blk
        self.n = self.lat_blocks + self.ctx_blocks

    def is_cache(self, j):
        return (j < self.lat_blocks) & (j % self.per_lat == 0)

    def cache_idx(self, j):
        return jnp.clip(j // self.per_lat, 0, self.rows.n_lat - 1)

    def token_block(self, j):
        b = j // self.per_lat
        t = jnp.maximum(j % self.per_lat - 1, 0)
        lat = self.ctx_blocks + b * (self.per_lat - 1) + t
        return jnp.where(j < self.lat_blocks, lat, j - self.lat_blocks)


def _mla_kv_kernel(keys, n_heads, ckv_ref, kpe_ref, cckv_ref, ckpe_ref, kvng_ref, wk_ref, wvt_ref, cos_ref, sin_ref,
                   kcat_ref, vt_ref, ckvn_ref):
    cached = keys.is_cache(pl.program_id(0))
    kpe = kpe_ref[...]
    rot = kpe[:, :LANE] * cos_ref[...] + kpe[:, LANE:] * sin_ref[...]
    ckvn = jnp.where(cached, cckv_ref[...], _rms(ckv_ref[...], kvng_ref[...]))
    kr = jnp.where(cached, ckpe_ref[...], rot)
    ckvn_ref[...] = ckvn
    ckvn_b = ckvn.astype(BF16)
    kn = _dot(ckvn_b, wk_ref[...])
    vt_ref[...] = _dot_nt(wvt_ref[...], ckvn_b).astype(BF16)
    for h in range(n_heads):
        base = h * 2 * LANE
        kcat_ref[:, base:base + LANE] = kn[:, h * NOPE_DIM:(h + 1) * NOPE_DIM].astype(BF16)
        kcat_ref[:, base + LANE:base + 2 * LANE] = jnp.where(_rope_keep(h, kr.shape), kr, 0.0).astype(BF16)


def _mla_kv(keys, zc, zk, col, cache_ckv, cache_kpe2, kv_ng, w_k, w_vt, cos_t, sin_t, n_heads):
    blk = keys.blk
    R = kv_ng.shape[0]
    Mk = keys.n * blk
    ckv_off, ckv_w = col["ckv"]
    assert ckv_off % ckv_w == 0
    tok = keys.token_block
    return pl.pallas_call(
        functools.partial(_mla_kv_kernel, keys, n_heads),
        grid=(keys.n,),
        in_specs=[
            pl.BlockSpec((blk, ckv_w), lambda j: (tok(j), ckv_off // ckv_w)),
            pl.BlockSpec((blk, 2 * LANE), lambda j: (tok(j), 0)),
            pl.BlockSpec((blk, R), lambda j: (keys.cache_idx(j), 0)),
            pl.BlockSpec((blk, LANE), lambda j: (keys.cache_idx(j), 0)),
            pl.BlockSpec((1, R), lambda j: (0, 0)),
            pl.BlockSpec(w_k.shape, lambda j: (0, 0)),
            pl.BlockSpec(w_vt.shape, lambda j: (0, 0)),
            pl.BlockSpec((blk, LANE), lambda j: (tok(j), 0)),
            pl.BlockSpec((blk, LANE), lambda j: (tok(j), 0)),
        ],
        out_specs=[
            pl.BlockSpec((blk, n_heads * 2 * LANE), lambda j: (j, 0)),
            pl.BlockSpec((n_heads * V_DIM, blk), lambda j: (0, j)),
            pl.BlockSpec((blk, R), lambda j: (j, 0)),
        ],
        out_shape=[
            jax.ShapeDtypeStruct((Mk, n_heads * 2 * LANE), BF16),
            jax.ShapeDtypeStruct((n_heads * V_DIM, Mk), BF16),
            jax.ShapeDtypeStruct((Mk, R), F32),
        ],
        compiler_params=_params("parallel"),
        name="mla_kv",
    )(zc, zk, cache_ckv, cache_kpe2, kv_ng.reshape(1, R), w_k, w_vt, cos_t, sin_t)


ATTN_TQ = 256
ATTN_TK = 2048
ATTN_QK_KEYS = 1024


def _attn_kernel(tq, chunks, q_ref, k_ref, vt_ref, o_ref, st0_ref, st1_ref):
    n_heads = q_ref.shape[1] // (2 * LANE)
    units = [(h, u) for h in range(n_heads) for u in range(q_ref.shape[0] // tq)]
    st_refs = (st0_ref, st1_ref)

    def scores(n, c0, c1):
        h, u = units[n]
        hd = slice(h * 2 * LANE, (h + 1) * 2 * LANE)
        m = None
        for k0 in range(c0, c1, ATTN_QK_KEYS):
            k1 = min(k0 + ATTN_QK_KEYS, c1)
            st = _dot_nt(k_ref[k0:k1, hd], q_ref[u * tq:(u + 1) * tq, hd])
            st_refs[n % 2][k0:k1, :] = st
            mk = jnp.max(st, axis=0, keepdims=True)
            m = mk if m is None else jnp.maximum(m, mk)
        return m

    def weights(n, c0, c1, m):
        h, _ = units[n]
        p = jnp.exp2(st_refs[n % 2][c0:c1, :] - m)
        return jnp.sum(p, axis=0, keepdims=True), _dot(vt_ref[h * V_DIM:(h + 1) * V_DIM, c0:c1], p.astype(BF16))

    m_prev = None
    for n in range(len(units) + 1):
        m_new = acc = l = None
        for c0, c1 in chunks:
            if n < len(units):
                mc = scores(n, c0, c1)
                m_new = mc if m_new is None else jnp.maximum(m_new, mc)
            if n > 0:
                lc, pv = weights(n - 1, c0, c1, m_prev)
                l = lc if l is None else l + lc
                acc = pv if acc is None else acc + pv
        if n > 0:
            h, u = units[n - 1]
            o_ref[u * tq:(u + 1) * tq, h * V_DIM:(h + 1) * V_DIM] = (acc / l).T.astype(o_ref.dtype)
        m_prev = m_new


def _attention(qcat, kcat, vt, n_seq, n_heads, hps, tq, q_blocks_per_seq, q_block0, tk, k_block0):
    sub_q = ATTN_TQ if tq % ATTN_TQ == 0 else LANE
    assert tq % sub_q == 0 and n_heads % hps == 0
    chunks = tuple((c0, min(c0 + ATTN_TK, tk)) for c0 in range(0, tk, ATTN_TK))
    return pl.pallas_call(
        functools.partial(_attn_kernel, sub_q, chunks),
        grid=(n_seq, n_heads // hps, q_blocks_per_seq),
        in_specs=[
            pl.BlockSpec((tq, hps * 2 * LANE), lambda s, h, i: (q_block0 + s * q_blocks_per_seq + i, h)),
            pl.BlockSpec((tk, hps * 2 * LANE), lambda s, h, i: (k_block0 + s, h)),
            pl.BlockSpec((hps * V_DIM, tk), lambda s, h, i: (h, k_block0 + s)),
        ],
        out_specs=pl.BlockSpec((tq, hps * V_DIM), lambda s, h, i: (s * q_blocks_per_seq + i, h)),
        out_shape=jax.ShapeDtypeStruct((n_seq * q_blocks_per_seq * tq, n_heads * V_DIM), BF16),
        scratch_shapes=[pltpu.VMEM((tk, sub_q), F32), pltpu.VMEM((tk, sub_q), F32)],
        compiler_params=_params("parallel", "parallel", "arbitrary"),
        name="mla_attn",
    )(qcat, kcat, vt)


def _rope_partner(w):
    q = ROPE_DIM // 4
    return jnp.flip(w.reshape(w.shape[:-1] + (2, 2, q)), axis=-2).reshape(w.shape)


def _rope_tables(rows):
    half, quarter = ROPE_DIM // 2, ROPE_DIM // 4
    inv = ROPE_BASE ** (-jnp.arange(quarter, dtype=F32) * 2.0 / half)
    t = jnp.arange(rows.lat_len)
    r = (t // GRID_W).astype(F32)
    c = (t % GRID_W).astype(F32)
    ang_r = r[:, None] * inv[None, :]
    ang_c = c[:, None] * inv[None, :]
    cos = jnp.concatenate([jnp.cos(ang_r), jnp.cos(ang_r), jnp.cos(ang_c), jnp.cos(ang_c)], axis=1)
    sin = jnp.concatenate([-jnp.sin(ang_r), jnp.sin(ang_r), -jnp.sin(ang_c), jnp.sin(ang_c)], axis=1)
    cos = jnp.tile(jnp.concatenate([cos, cos], axis=1), (rows.n_lat, 1))
    sin = jnp.tile(jnp.concatenate([sin, sin], axis=1), (rows.n_lat, 1))
    cos = jnp.concatenate([jnp.ones((rows.mc, LANE), F32), cos], axis=0)
    sin = jnp.concatenate([jnp.zeros((rows.mc, LANE), F32), sin], axis=0)
    return cos, sin


def kernel(x_prompt, x_sample, state_gla, state_ret, cache_ckv, cache_kpe, c, c_ctx, mod_w, mod_b, norm1_g, norm2_g, ab_w_in, gla_gate_w2, gla_gate_b, ret_decay, gla_norm_g, ret_norm_g, ab_w_out, mla_w_in, mla_q_norm_g, mla_w_uq, mla_kv_norm_g, mla_w_ukv, mla_w_out, ffn_w_in, ffn_conv, ffn_w_out, final_norm_g):
    B, S, D = x_prompt.shape
    NB, T, _ = x_sample.shape
    depth = mod_w.shape[0]
    _, _, _, HA, DK, DV = state_gla.shape
    HB = state_ret.shape[3]
    assert HA == HB and state_ret.shape[4:] == (DK, DV)
    GR = gla_gate_w2.shape[2]
    past = cache_ckv.shape[2]
    q_rank = mla_q_norm_g.shape[1]
    kv_rank = mla_kv_norm_g.shape[1]
    HC = mla_w_uq.shape[2] // (NOPE_DIM + ROPE_DIM)
    F = ffn_w_out.shape[1]
    rows = _Rows(B * S, S, T, NB)
    M = rows.m
    tm = min(512, S * B, T)
    assert rows.mc % tm == 0 and T % tm == 0
    assert S & (S - 1) == 0 and T & (T - 1) == 0

    tm_proj = min(1024, S * B, T)
    assert rows.mc % tm_proj == 0 and T % tm_proj == 0
    cond = jnp.concatenate([c_ctx[None, :], c, jnp.zeros((MOD_GROUPS - 1 - NB, D), F32)], axis=0)
    mod3 = _modulation(cond, mod_w, mod_b)

    x = (x_prompt.reshape(B * S, D), x_sample.reshape(NB * T, D))
    h = _normmod(rows, x, norm1_g[0], mod3[0], tm)
    ffn_w = (ffn_w_in.astype(BF16), ffn_conv, ffn_w_out.astype(BF16))
    new_gla, new_ret, new_ckv, new_kpe = [], [], [], []
    for l in range(depth):
        if l % 2 == 0:
            e = l // 2
            names = ("gq", "gk", "gv", "gg", "glr", "rq", "rk", "rv", "rg")
            sizes = (HA * DK, HA * DK, HA * DV, HA * DV, 2 * GR, HB * DK, HB * DK, HB * DV, HB * DV)
            src, o = {}, 0
            for n, sz in zip(names, sizes):
                src[n] = (o, sz)
                o += sz
            order = ("gq", "gk", "gv", "gg", "rq", "rk", "rv", "rg")
            col, o = {}, 0
            for n in order:
                col[n] = (o, src[n][1])
                o += src[n][1]
            so, sz = src["glr"]
            assert so % LANE == 0 and so + LANE <= ab_w_in.shape[2]
            w_all = lax.optimization_barrier(ab_w_in[e].astype(BF16))
            w_in = jnp.concatenate([w_all[:, :so], w_all[:, so + sz:]], axis=1)
            w_glr = w_all[:, so:so + LANE]
            z, glr = _inproj(h, w_in, w_glr, BF16, tm_proj)
            tb = min(256, S)
            extra = None
            for dirn in (0, 1):
                cfg = _ScanCfg(rows, B, tb, HA, DK, DV, backward=bool(dirn))
                gate_w = jnp.zeros((LANE, HA * DK), F32).at[dirn * GR:(dirn + 1) * GR].set(gla_gate_w2[e, dirn])
                rd = jnp.broadcast_to(ret_decay[e, dirn][:, None, None], (HB, 8, LANE))
                y, sg, sr = _scan(cfg, z, glr, col, gate_w.astype(BF16), gla_gate_b[e, dirn].reshape(1, -1), rd,
                                  state_gla, state_ret, e, extra)
                extra = (y, sg, sr, gla_norm_g[e].reshape(1, -1), ret_norm_g[e].reshape(1, -1))
            new_gla.append(sg)
            new_ret.append(sr)
            x, h = _outproj(rows, y, ab_w_out[e].astype(BF16), x, norm2_g[l], mod3[l], tm)
        else:
            i = l // 2
            w = mla_w_in[i]
            w_kpe = w[:, q_rank + kv_rank:]
            w_in = w[:, :q_rank + kv_rank].astype(BF16)
            w_kpe_p = _rope_partner(w_kpe)
            w_side = jnp.concatenate([w_kpe, w_kpe, w_kpe_p, w_kpe_p], axis=1).astype(BF16)
            col = {"cq": (0, q_rank), "ckv": (q_rank, kv_rank)}
            zc, zk = _inproj(h, w_in, w_side, F32, tm_proj)
            cos_t, sin_t = _rope_tables(rows)
            wq = mla_w_uq[i].reshape(q_rank, HC, NOPE_DIM + ROPE_DIM)
            wq_rope = wq[:, :, NOPE_DIM:]
            w_uq = jnp.concatenate([wq[:, :, :NOPE_DIM].reshape(q_rank, -1), wq_rope.reshape(q_rank, -1),
                                    _rope_partner(wq_rope).reshape(q_rank, -1)], axis=1).astype(BF16)
            scale = (NOPE_DIM + ROPE_DIM) ** -0.5 * LOG2_E
            assert col["cq"][0] == 0
            qcat = _mla_q(zc, mla_q_norm_g[i], w_uq, cos_t, sin_t, HC, scale, tm)
            wkv = mla_w_ukv[i].reshape(kv_rank, HC, NOPE_DIM + V_DIM)
            w_k = wkv[:, :, :NOPE_DIM].reshape(kv_rank, -1).astype(BF16)
            w_vt = wkv[:, :, NOPE_DIM:].reshape(kv_rank, -1).T.astype(BF16)
            assert S % past == 0 and T % past == 0
            keys = _KeyRows(rows, past)
            ckpe = cache_kpe[:, i].reshape(NB * past, ROPE_DIM)
            kcat, vt, ckvn = _mla_kv(keys, zc, zk, col, cache_ckv[:, i].reshape(NB * past, kv_rank),
                                     jnp.concatenate([ckpe, ckpe], axis=1), mla_kv_norm_g[i], w_k, w_vt,
                                     cos_t, sin_t, HC)
            tk_lat = past + T
            assert (keys.lat_blocks * past) % S == 0
            tq = min(4096, T)
            assert rows.mc % tq == 0
            o = (_attention(qcat, kcat, vt, B, HC, HC, S, 1, 0, S, keys.lat_blocks * past // S),
                 _attention(qcat, kcat, vt, NB, HC, 1, tq, T // tq, rows.mc // tq, tk_lat, 0))
            new_ckv.append(ckvn[keys.lat_blocks * past:].reshape(B, S, kv_rank))
            new_kpe.append(zk[:rows.mc, :ROPE_DIM].reshape(B, S, ROPE_DIM))
            x, h = _outproj(rows, o, mla_w_out[i].astype(BF16), x, norm2_g[l], mod3[l], tm)
        tf = _largest_tile(F, 512)
        if l == depth - 1:
            y_ctx, y_lat = _ffn(rows, h, x, mod3[l], l, *ffn_w, tm, tf, final_g=final_norm_g)
        else:
            x, h = _ffn(rows, h, x, mod3[l], l, *ffn_w, tm, tf, next_norm=(norm1_g[l + 1], mod3[l + 1]))

    y_prompt = y_ctx.reshape(B, S, D)
    y_sample = y_lat.reshape(NB, T, D)
    return (y_prompt, y_sample, jnp.stack(new_gla, axis=1), jnp.stack(new_ret, axis=1),
            jnp.stack(new_ckv, axis=1), jnp.stack(new_kpe, axis=1))
```

```python
import functools

import jax
import jax.numpy as jnp
from jax import lax
from jax.experimental import pallas as pl
from jax.experimental.pallas import tpu as pltpu

F32 = jnp.float32
BF16 = jnp.bfloat16

NORM_EPS = 1e-6
GATE_TEMP = 16.0
CHUNK = 64
GRID_W = 64
ROPE_BASE = 10000.0
ROPE_DIM = 64
NOPE_DIM = 128
V_DIM = 128
CONV_W = 3
LOG2_E = 1.4426950408889634

LANE = 128
BF16_SUBLANE = 16
MOD_GROUPS = 8
VMEM_LIMIT = 56 * 1024 * 1024

NT_DIMS = (((1,), (1,)), ((), ()))
TN_DIMS = (((0,), (0,)), ((), ()))


def _dot(a, b):
    return jnp.dot(a, b, preferred_element_type=F32)


def _dot_nt(a, b):
    return lax.dot_general(a, b, NT_DIMS, preferred_element_type=F32)


def _dot_tn(a, b):
    return lax.dot_general(a, b, TN_DIMS, preferred_element_type=F32)


def _params(*sem):
    return pltpu.CompilerParams(dimension_semantics=sem, vmem_limit_bytes=VMEM_LIMIT)


def _rms(x, g):
    ms = jnp.mean(x * x, axis=-1, keepdims=True)
    return (x * lax.rsqrt(ms + NORM_EPS)) * g


def _silu(x):
    return x * jax.nn.sigmoid(x)


def _log_sigmoid(x):
    return jnp.minimum(x, 0.0) - jnp.log(1.0 + jnp.exp(-jnp.abs(x)))


def _largest_tile(n, cap):
    best = None
    for t in range(LANE, min(n, cap) + 1, LANE):
        if n % t == 0:
            best = t
    assert best is not None, (n, cap)
    return best


def _mod_kernel(c_ref, w_ref, b_ref, o_ref):
    s = _silu(c_ref[...]).astype(BF16)
    m = _dot(s, w_ref[0].astype(BF16)) + b_ref[0]
    for g in range(MOD_GROUPS):
        o_ref[0, g, 0] = m[g:g + 1]


def _modulation(cond, mod_w, mod_b):
    L, D, N = mod_w.shape
    n_vec = N // D
    out = pl.pallas_call(
        _mod_kernel,
        grid=(L, n_vec),
        in_specs=[
            pl.BlockSpec((MOD_GROUPS, D), lambda l, k: (0, 0)),
            pl.BlockSpec((1, D, D), lambda l, k: (l, 0, k)),
            pl.BlockSpec((1, 1, D), lambda l, k: (l, 0, k)),
        ],
        out_specs=pl.BlockSpec((1, MOD_GROUPS, 1, 1, D), lambda l, k: (l, 0, k, 0, 0)),
        out_shape=jax.ShapeDtypeStruct((L, MOD_GROUPS, n_vec, 1, D), F32),
        compiler_params=_params("parallel", "parallel"),
        name="modulation",
    )(cond, mod_w, mod_b.reshape(L, 1, N))
    return out.reshape(L, MOD_GROUPS * n_vec, 1, D)


class _Rows:
    def __init__(self, mc, seq, lat_len, n_lat):
        self.mc, self.seq, self.lat_len, self.n_lat = mc, seq, lat_len, n_lat
        self.m = mc + lat_len * n_lat

    def group(self, i, tm):
        r = i * tm
        return jnp.where(r < self.mc, 0, 1 + (r - self.mc) // self.lat_len)

    def mod_spec(self, which, tm, d):
        return pl.BlockSpec((1, 1, d), lambda i, *_: (self.group(i, tm) * 6 + which, 0, 0))

    def split_specs(self, tm, d):
        n_ctx = self.mc // tm
        return [pl.BlockSpec((tm, d), lambda i, *_: (jnp.minimum(i, n_ctx - 1), 0)),
                pl.BlockSpec((tm, d), lambda i, *_: (jnp.maximum(i - n_ctx, 0), 0))]


ROW_CHUNK = 16
ROW_PARTS = 2


def _for_row_chunks(n_rows, body):
    def it(c, carry):
        body(pl.ds(pl.multiple_of(c * ROW_CHUNK, ROW_CHUNK), ROW_CHUNK))
        return carry
    lax.fori_loop(0, n_rows // ROW_CHUNK, it, 0, unroll=4)


def _fold_gain(gs_ref, g_ref, sc_ref):
    gs_ref[...] = g_ref[...] * (1.0 + sc_ref[0])


def _norm_mod_rows(x, gs_ref, sh_ref):
    ms = jnp.mean(x * x, axis=-1, keepdims=True)
    return (x * lax.rsqrt(ms + NORM_EPS)) * gs_ref[...] + sh_ref[0]


def _normmod_kernel(rows, tm, xa_ref, xb_ref, g_ref, sh_ref, sc_ref, h_ref, gs_ref):
    _fold_gain(gs_ref, g_ref, sc_ref)

    def run(x_ref):
        def body(r):
            h_ref[r, :] = _norm_mod_rows(x_ref[r, :], gs_ref, sh_ref).astype(h_ref.dtype)
        _for_row_chunks(tm, body)

    is_ctx = pl.program_id(0) * tm < rows.mc
    pl.when(is_ctx)(lambda: run(xa_ref))
    pl.when(jnp.logical_not(is_ctx))(lambda: run(xb_ref))


def _normmod(rows, x_pair, g, mod3, tm):
    D = x_pair[0].shape[1]
    return pl.pallas_call(
        functools.partial(_normmod_kernel, rows, tm),
        grid=(rows.m // tm,),
        in_specs=rows.split_specs(tm, D) + [
            pl.BlockSpec((1, D), lambda i: (0, 0)),
            rows.mod_spec(0, tm, D),
            rows.mod_spec(1, tm, D),
        ],
        out_specs=pl.BlockSpec((tm, D), lambda i: (i, 0)),
        out_shape=jax.ShapeDtypeStruct((rows.m, D), BF16),
        scratch_shapes=[pltpu.VMEM((1, D), F32)],
        compiler_params=_params("parallel"),
        name="normmod",
    )(*x_pair, g.reshape(1, D), mod3, mod3)


def _inproj_kernel(h_ref, w_ref, ws_ref, o_ref, os_ref):
    @pl.when(pl.program_id(1) == 0)
    def _():
        os_ref[...] = _dot(h_ref[...], ws_ref[...])

    o_ref[...] = _dot(h_ref[...], w_ref[...]).astype(o_ref.dtype)


def _inproj(h, w, w_side, out_dtype, tm, tn_cap=2048):
    M, D = h.shape
    N = w.shape[1]
    NS = w_side.shape[1]
    tn = _largest_tile(N, tn_cap)
    return pl.pallas_call(
        _inproj_kernel,
        grid=(M // tm, N // tn),
        in_specs=[
            pl.BlockSpec((tm, D), lambda i, j: (i, 0)),
            pl.BlockSpec((D, tn), lambda i, j: (0, j)),
            pl.BlockSpec((D, NS), lambda i, j: (0, 0)),
        ],
        out_specs=[pl.BlockSpec((tm, tn), lambda i, j: (i, j)), pl.BlockSpec((tm, NS), lambda i, j: (i, 0))],
        out_shape=[jax.ShapeDtypeStruct((M, N), out_dtype), jax.ShapeDtypeStruct((M, NS), F32)],
        compiler_params=_params("parallel", "arbitrary"),
        name="inproj",
    )(h, w, w_side)


def _outproj_kernel(rows, tm, y_split, x_split, *refs):
    refs = list(refs)
    y_refs = [refs.pop(0) for _ in range(2 if y_split else 1)]
    w_ref = refs.pop(0)
    x_refs = [refs.pop(0) for _ in range(2 if x_split else 1)]
    gate_ref, g_ref, sh_ref, sc_ref, o_ref, h_ref, acc_ref, gs_ref = refs
    _fold_gain(gs_ref, g_ref, sc_ref)
    is_ctx = pl.program_id(0) * tm < rows.mc
    part = tm // ROW_PARTS

    def rows_of(pair, r):
        return jnp.where(is_ctx, pair[0][r, :], pair[1][r, :]) if len(pair) == 2 else pair[0][r, :]

    def finish(r):
        x1 = rows_of(x_refs, r) + gate_ref[0] * acc_ref[r, :]
        o_ref[r, :] = x1
        h_ref[r, :] = _norm_mod_rows(x1, gs_ref, sh_ref).astype(h_ref.dtype)

    for p in range(ROW_PARTS + 1):
        if p < ROW_PARTS:
            rp = slice(p * part, (p + 1) * part)
            acc_ref[rp, :] = _dot(rows_of(y_refs, rp), w_ref[...])
        if p > 0:
            for c in range(part // ROW_CHUNK):
                start = (p - 1) * part + c * ROW_CHUNK
                finish(slice(start, start + ROW_CHUNK))


def _outproj(rows, y, w, x, g, mod3, tm):
    K, N = w.shape
    M = rows.m

    def operand(a, width):
        if isinstance(a, (tuple, list)):
            return True, rows.split_specs(tm, width), list(a)
        return False, [pl.BlockSpec((tm, width), lambda i: (i, 0))], [a]

    y_split, y_specs, y_args = operand(y, K)
    x_split, x_specs, x_args = operand(x, N)
    return pl.pallas_call(
        functools.partial(_outproj_kernel, rows, tm, y_split, x_split),
        grid=(M // tm,),
        in_specs=y_specs + [pl.BlockSpec((K, N), lambda i: (0, 0))] + x_specs + [
            rows.mod_spec(2, tm, N),
            pl.BlockSpec((1, N), lambda i: (0, 0)),
            rows.mod_spec(3, tm, N),
            rows.mod_spec(4, tm, N),
        ],
        out_specs=[pl.BlockSpec((tm, N), lambda i: (i, 0)), pl.BlockSpec((tm, N), lambda i: (i, 0))],
        out_shape=[jax.ShapeDtypeStruct((M, N), F32), jax.ShapeDtypeStruct((M, N), BF16)],
        scratch_shapes=[pltpu.VMEM((tm, N), F32), pltpu.VMEM((1, N), F32)],
        compiler_params=_params("parallel"),
        name="outproj",
    )(*y_args, w, *x_args, mod3, g.reshape(1, N), mod3, mod3)


HALO = BF16_SUBLANE


def _ffn_kernel(rows, tm, final_norm, hp_ref, h_ref, hn_ref, x_ref, gate_ref,
                wa_ref, wb_ref, cw_ref, wo_ref, *rest):
    if final_norm:
        fg_ref, o_ctx_ref, o_lat_ref, hs_ref, a_ref, acc_ref = rest
    else:
        g_ref, sh_ref, sc_ref, o_ref, hnext_ref, hs_ref, a_ref, acc_ref, gs_ref = rest
    i = pl.program_id(0)
    f = pl.program_id(1)

    @pl.when(f == 0)
    def _():
        hs_ref[0:tm, :] = h_ref[...]
        last_row = lax.broadcasted_iota(jnp.int32, (HALO, 1), 0) == HALO - 1
        hs_ref[tm:, :] = jnp.where(last_row, hp_ref[...].astype(F32), hn_ref[...].astype(F32)).astype(BF16)
        acc_ref[...] = jnp.zeros_like(acc_ref)

    def hidden_tile():
        a = _dot(hs_ref[...], wa_ref[0])
        a_ref[HALO:, :] = a
        a_ref[HALO - 8:HALO, :] = a[tm + HALO - 8:, :]
        b = _dot(h_ref[...], wb_ref[0])
        row = i * tm + lax.broadcasted_iota(jnp.int32, (tm, 1), 0)
        pos = jnp.where(row < rows.mc, row & (rows.seq - 1), (row - rows.mc) & (rows.lat_len - 1))
        seq_len = jnp.where(row < rows.mc, rows.seq, rows.lat_len)
        a_prev = jnp.where(pos == 0, 0.0, a_ref[pl.ds(HALO - 1, tm), :])
        a_next = jnp.where(pos == seq_len - 1, 0.0, a_ref[pl.ds(HALO + 1, tm), :])
        a_mid = a_ref[pl.ds(HALO, tm), :]
        cw = cw_ref[0]
        a = cw[0:1] * a_prev + cw[1:2] * a_mid + cw[2:3] * a_next
        return (_silu(a) * b).astype(BF16)

    def residual(r):
        return x_ref[r, :] + gate_ref[0] * acc_ref[r, :]

    def last_step(finish):
        act = hidden_tile()
        part = tm // ROW_PARTS
        for p in range(ROW_PARTS + 1):
            if p < ROW_PARTS:
                rp = slice(p * part, (p + 1) * part)
                acc_ref[rp, :] += _dot(act[rp, :], wo_ref[0])
            if p > 0:
                for c in range(part // ROW_CHUNK):
                    start = (p - 1) * part + c * ROW_CHUNK
                    finish(slice(start, start + ROW_CHUNK))

    last = pl.num_programs(1) - 1

    @pl.when(f < last)
    def _():
        acc_ref[...] += _dot(hidden_tile(), wo_ref[0])

    if final_norm:
        def finish_to(out_ref):
            def finish(r):
                out_ref[r, :] = _rms(residual(r), fg_ref[...])
            return finish

        is_ctx = i * tm < rows.mc
        pl.when((f == last) & is_ctx)(lambda: last_step(finish_to(o_ctx_ref)))
        pl.when((f == last) & jnp.logical_not(is_ctx))(lambda: last_step(finish_to(o_lat_ref)))
    else:
        def finish(r):
            x2 = residual(r)
            o_ref[r, :] = x2
            hnext_ref[r, :] = _norm_mod_rows(x2, gs_ref, sh_ref).astype(hnext_ref.dtype)

        @pl.when(f == last)
        def _():
            _fold_gain(gs_ref, g_ref, sc_ref)
            last_step(finish)


def _ffn(rows, h, x, mod3, layer, w_in, conv_w, w_out, tm, tf, next_norm=None, final_g=None):
    M, D = x.shape
    F = w_out.shape[1]
    nf = F // tf
    nhalo = M // HALO
    final_norm = final_g is not None
    kern = functools.partial(_ffn_kernel, rows, tm, final_norm)
    in_specs = [
        pl.BlockSpec((HALO, D), lambda i, f: (jnp.maximum(i * (tm // HALO) - 1, 0), 0)),
        pl.BlockSpec((tm, D), lambda i, f: (i, 0)),
        pl.BlockSpec((HALO, D), lambda i, f: (jnp.minimum((i + 1) * (tm // HALO), nhalo - 1), 0)),
        pl.BlockSpec((tm, D), lambda i, f: (i, 0)),
        rows.mod_spec(5, tm, D),
        pl.BlockSpec((1, D, tf), lambda i, f: (layer, 0, f)),
        pl.BlockSpec((1, D, tf), lambda i, f: (layer, 0, nf + f)),
        pl.BlockSpec((1, CONV_W, tf), lambda i, f: (layer, 0, f)),
        pl.BlockSpec((1, tf, D), lambda i, f: (layer, f, 0)),
    ]
    args = [h, h, h, x, mod3, w_in, w_in, conv_w, w_out]
    vec = pl.BlockSpec((1, D), lambda i, f: (0, 0))
    if final_norm:
        n_ctx = rows.mc // tm
        in_specs += [vec]
        args += [final_g.reshape(1, D)]
        out_specs = [pl.BlockSpec((tm, D), lambda i, f: (jnp.minimum(i, n_ctx - 1), 0)),
                     pl.BlockSpec((tm, D), lambda i, f: (jnp.maximum(i - n_ctx, 0), 0))]
        out_shape = [jax.ShapeDtypeStruct((rows.mc, D), F32), jax.ShapeDtypeStruct((M - rows.mc, D), F32)]
        row_sem = "arbitrary"
    else:
        g_next, mod3_next = next_norm
        in_specs += [vec, rows.mod_spec(0, tm, D), rows.mod_spec(1, tm, D)]
        args += [g_next.reshape(1, D), mod3_next, mod3_next]
        out_specs = [pl.BlockSpec((tm, D), lambda i, f: (i, 0)), pl.BlockSpec((tm, D), lambda i, f: (i, 0))]
        out_shape = [jax.ShapeDtypeStruct((M, D), F32), jax.ShapeDtypeStruct((M, D), BF16)]
        row_sem = "parallel"
    return pl.pallas_call(
        kern,
        grid=(M // tm, nf),
        in_specs=in_specs,
        out_specs=out_specs,
        out_shape=out_shape,
        scratch_shapes=[
            pltpu.VMEM((tm + HALO, D), BF16),
            pltpu.VMEM((tm + 2 * HALO, tf), F32),
            pltpu.VMEM((tm, D), F32),
        ] + ([] if final_norm else [pltpu.VMEM((1, D), F32)]),
        compiler_params=_params(row_sem, "arbitrary"),
        name="convffn",
    )(*args)


class _ScanCfg:
    def __init__(self, rows, batch, tb, h, dk, dv, backward):
        self.rows, self.batch, self.tb, self.h, self.dk, self.dv = rows, batch, tb, h, dk, dv
        self.backward = backward
        self.cps = rows.seq // tb
        self.lps = rows.lat_len // tb
        self.ctx_blocks = batch * self.cps
        self.nblk = rows.m // tb

    def block(self, i):
        return self.nblk - 1 - i if self.backward else i

    def is_ctx(self, r):
        return r < self.ctx_blocks

    def seq_pos(self, r):
        ctx = self.is_ctx(r)
        return (jnp.where(ctx, r % self.cps, (r - self.ctx_blocks) % self.lps),
                jnp.where(ctx, self.cps, self.lps))

    def lat_seq(self, r):
        return jnp.clip((r - self.ctx_blocks) // self.lps, 0, self.rows.n_lat - 1)

    def ctx_seq(self, r):
        return jnp.clip(r // self.cps, 0, self.batch - 1)


def _scan_kernel(cfg, gq_ref, gk_ref, gv_ref, rq_ref, rk_ref, rv_ref, glr_ref, gw_ref, gb_ref, rd_ref,
                 s0g_ref, s0r_ref, *rest):
    scratch = rest[-8:]
    sg_ref, sr_ref, dend_ref, qd_ref, kinv_ref, kend_ref, rks_ref, rkd_ref = scratch
    if cfg.backward:
        of_ref, gg_ref, rg_ref, gng_ref, rng_ref, sgf_ref, srf_ref, y_ref, sgo_ref, sro_ref = rest[:-8]
    else:
        o_ref, sgo_ref, sro_ref = rest[:-8]
    H, DK, DV, C = cfg.h, cfg.dk, cfg.dv, CHUNK
    r = cfg.block(pl.program_id(0))
    blk, nblk_seq = cfg.seq_pos(r)
    first = blk == (nblk_seq - 1 if cfg.backward else 0)
    last = blk == (0 if cfg.backward else nblk_seq - 1)
    is_ctx = cfg.is_ctx(r)

    @pl.when(first & is_ctx)
    def _():
        sg_ref[...] = jnp.zeros_like(sg_ref)
        sr_ref[...] = jnp.zeros_like(sr_ref)

    @pl.when(first & jnp.logical_not(is_ctx))
    def _():
        sg_ref[...] = s0g_ref[0, 0, 0]
        sr_ref[...] = s0r_ref[0, 0, 0]

    ti = lax.broadcasted_iota(jnp.int32, (C, C), 0)
    tj = lax.broadcasted_iota(jnp.int32, (C, C), 1)
    sees = (tj >= ti) if cfg.backward else (tj <= ti)
    dist = jnp.abs(ti - tj).astype(F32)
    rowi = lax.broadcasted_iota(jnp.int32, (C, LANE), 0)
    to_end = (rowi if cfg.backward else C - 1 - rowi).astype(F32)
    from_start = (C - rowi if cfg.backward else rowi + 1).astype(F32)

    n_chunks = cfg.tb // C
    HDK = H * DK
    scale = DK ** -0.5

    bi = lax.broadcasted_iota(jnp.int32, (cfg.tb, cfg.tb), 0)
    bj = lax.broadcasted_iota(jnp.int32, (cfg.tb, cfg.tb), 1)
    same_chunk = (bi & -C) == (bj & -C)
    blk_tri = jnp.where(same_chunk & ((bj >= bi) if cfg.backward else (bj <= bi)), 1.0, 0.0).astype(BF16)
    pre = _dot(glr_ref[...].astype(BF16), gw_ref[...]) + gb_ref[...]
    la = _log_sigmoid(pre) / GATE_TEMP
    la_hi = la.astype(BF16)
    la_lo = (la - la_hi.astype(F32)).astype(BF16)
    b = _dot(blk_tri, la_hi) + _dot(blk_tri, la_lo)
    b3 = b.reshape(n_chunks, C, HDK)
    b_end = b3[:, 0:1, :] if cfg.backward else b3[:, C - 1:C, :]
    eb = jnp.exp(b)
    chunk_of_row = jnp.right_shift(lax.broadcasted_iota(jnp.int32, (cfg.tb, LANE), 0), C.bit_length() - 1)
    lane = lax.broadcasted_iota(jnp.int32, (cfg.tb, LANE), 1)
    in_chunk = jnp.where(chunk_of_row == lane, 1.0, 0.0).astype(BF16)
    dend = jnp.exp(_dot_tn(la_hi, in_chunk) + _dot_tn(la_lo, in_chunk))
    for c in range(n_chunks):
        dend_ref[:, c * LANE:(c + 1) * LANE] = jnp.broadcast_to(dend[:, c:c + 1], (HDK, LANE))
    gk = gk_ref[...].astype(F32)
    qd_ref[...] = (gq_ref[...].astype(F32) * scale * eb).astype(BF16)
    kinv_ref[...] = (gk * jnp.exp(-b)).astype(BF16)
    kend_ref[...] = (gk * jnp.exp(b_end - b3).reshape(cfg.tb, HDK)).astype(BF16)

    decays, q_decs, c_decs, k_dec_cols = [], [], [], []
    for h in range(H):
        lg = _log_sigmoid(rd_ref[h])
        lg_c = jnp.broadcast_to(lg[0:1, 0:C], (C, C))
        lg_l = jnp.broadcast_to(lg[0:1, :], (C, LANE))
        decays.append(jnp.where(sees, jnp.exp(lg_c * dist), 0.0))
        q_decs.append(jnp.concatenate([jnp.exp(lg_l * from_start)] * (DV // LANE), axis=1))
        c_decs.append(jnp.concatenate([jnp.exp(lg[0:1, :] * float(C))] * (DV // LANE), axis=1))
        k_dec_cols.append(jnp.concatenate([jnp.exp(lg_l * to_end)] * n_chunks, axis=0))
    rk = rk_ref[...].astype(F32) * scale
    rks_ref[...] = rk.astype(BF16)
    rkd_ref[...] = (rk * jnp.concatenate(k_dec_cols, axis=1)).astype(BF16)

    for ci in range(n_chunks):
        c = n_chunks - 1 - ci if cfg.backward else ci
        rs = slice(c * C, (c + 1) * C)
        heads = range(H)
        ksl = [slice(h * DK, (h + 1) * DK) for h in heads]
        vsl = [slice(h * DV, (h + 1) * DV) for h in heads]
        att_raw = [_dot_nt(qd_ref[rs, ksl[h]], kinv_ref[rs, ksl[h]]) for h in heads]
        ratt_raw = [_dot_nt(rq_ref[rs, ksl[h]], rks_ref[rs, ksl[h]]) for h in heads]
        qs_g = [_dot(qd_ref[rs, ksl[h]], sg_ref[h].astype(BF16)) for h in heads]
        qs_r = [_dot(rq_ref[rs, ksl[h]], sr_ref[h].astype(BF16)) for h in heads]
        for h in heads:
            att = jnp.where(sees, att_raw[h], 0.0).astype(BF16)
            o_g = _dot(att, gv_ref[rs, vsl[h]]) + qs_g[h]
            ratt = (ratt_raw[h] * decays[h]).astype(BF16)
            o_r = _dot(ratt, rv_ref[rs, vsl[h]]) + qs_r[h] * q_decs[h]
            rs_cols = slice(H * DV + h * DV, H * DV + (h + 1) * DV)
            if cfg.backward:
                for o, cols, gate_ref, ng_ref in ((o_g, vsl[h], gg_ref, gng_ref), (o_r, rs_cols, rg_ref, rng_ref)):
                    tot = o + of_ref[rs, cols]
                    mu = jnp.mean(tot, axis=-1, keepdims=True)
                    d = tot - mu
                    var = jnp.mean(d * d, axis=-1, keepdims=True)
                    yn = d * lax.rsqrt(var + NORM_EPS) * ng_ref[:, vsl[h]]
                    y_ref[rs, cols] = (_silu(gate_ref[rs, vsl[h]].astype(F32)) * yn).astype(y_ref.dtype)
            else:
                o_ref[rs, vsl[h]] = o_g
                o_ref[rs, rs_cols] = o_r
        for h in heads:
            dend = dend_ref[ksl[h], c * LANE:(c + 1) * LANE]
            sg_ref[h] = (sg_ref[h] * jnp.concatenate([dend] * (DV // LANE), axis=1)
                         + _dot_tn(kend_ref[rs, ksl[h]], gv_ref[rs, vsl[h]]))
            sr_ref[h] = sr_ref[h] * c_decs[h] + _dot_tn(rkd_ref[rs, ksl[h]], rv_ref[rs, vsl[h]])

    @pl.when(last & is_ctx)
    def _():
        if cfg.backward:
            sgo_ref[0, 0] = sgf_ref[0]
            sro_ref[0, 0] = srf_ref[0]
            sgo_ref[0, 1] = sg_ref[...]
            sro_ref[0, 1] = sr_ref[...]
        else:
            sgo_ref[0] = sg_ref[...]
            sro_ref[0] = sr_ref[...]


def _scan(cfg, z, glr, col, gate_w, gate_b, ret_decay_t, s0g, s0r, layer, extra):
    H, DK, DV, tb = cfg.h, cfg.dk, cfg.dv, cfg.tb
    M = z.shape[0]
    dirn = int(cfg.backward)

    def zspec(name):
        off, width = col[name]
        assert off % width == 0, (name, off, width)
        return pl.BlockSpec((tb, width), lambda i: (cfg.block(i), off // width))

    def full(shape):
        return pl.BlockSpec(shape, lambda i: (0,) * len(shape))

    state_in = pl.BlockSpec((1, 1, 1, H, DK, DV),
                            lambda i: (cfg.lat_seq(cfg.block(i)), layer, dirn, 0, 0, 0))
    fwd_state = pl.BlockSpec((1, H, DK, DV), lambda i: (cfg.ctx_seq(cfg.block(i)), 0, 0, 0))
    row_blk = pl.BlockSpec((tb, 2 * H * DV), lambda i: (cfg.block(i), 0))
    glr_spec = pl.BlockSpec((tb, glr.shape[1]), lambda i: (cfg.block(i), 0))
    in_specs = [zspec("gq"), zspec("gk"), zspec("gv"), zspec("rq"), zspec("rk"), zspec("rv"), glr_spec,
                full(gate_w.shape), full(gate_b.shape), full(ret_decay_t.shape), state_in, state_in]
    args = [z, z, z, z, z, z, glr, gate_w, gate_b, ret_decay_t, s0g, s0r]
    if cfg.backward:
        o_fwd, sg_fwd, sr_fwd, gla_ng, ret_ng = extra
        in_specs += [row_blk, zspec("gg"), zspec("rg"), full(gla_ng.shape), full(ret_ng.shape), fwd_state, fwd_state]
        args += [o_fwd, z, z, gla_ng, ret_ng, sg_fwd, sr_fwd]
        out0 = jax.ShapeDtypeStruct((M, 2 * H * DV), BF16)
        state_out = pl.BlockSpec((1, 2, H, DK, DV), lambda i: (cfg.ctx_seq(cfg.block(i)), 0, 0, 0, 0))
        state_shape = jax.ShapeDtypeStruct((cfg.batch, 2, H, DK, DV), F32)
    else:
        out0 = jax.ShapeDtypeStruct((M, 2 * H * DV), F32)
        state_out = fwd_state
        state_shape = jax.ShapeDtypeStruct((cfg.batch, H, DK, DV), F32)
    return pl.pallas_call(
        functools.partial(_scan_kernel, cfg),
        grid=(cfg.nblk,),
        in_specs=in_specs,
        out_specs=[row_blk, state_out, state_out],
        out_shape=[out0, state_shape, state_shape],
        scratch_shapes=[pltpu.VMEM((H, DK, DV), F32), pltpu.VMEM((H, DK, DV), F32),
                        pltpu.VMEM((H * DK, (tb // CHUNK) * LANE), F32)] + [pltpu.VMEM((tb, H * DK), BF16)] * 5,
        compiler_params=_params("arbitrary"),
        name="scan_bwd" if cfg.backward else "scan_fwd",
    )(*args)


def _rope_keep(h, shape):
    lane = lax.broadcasted_iota(jnp.int32, shape, 1)
    return (lane < ROPE_DIM) if h % 2 == 0 else (lane >= ROPE_DIM)


def _mla_q_kernel(n_heads, scale, cq_ref, qng_ref, w_ref, cos_ref, sin_ref, o_ref):
    cqn = _rms(cq_ref[...], qng_ref[...]).astype(BF16)
    q = _dot(cqn, w_ref[...])
    nope_w = n_heads * NOPE_DIM
    rope_w = n_heads * ROPE_DIM
    cos, sin = cos_ref[...], sin_ref[...]
    for h in range(n_heads):
        p = h // 2
        tile = q[:, nope_w + p * LANE:nope_w + (p + 1) * LANE]
        partner = q[:, nope_w + rope_w + p * LANE:nope_w + rope_w + (p + 1) * LANE]
        rot = (tile * cos + partner * sin) * scale
        rot = jnp.where(_rope_keep(h, rot.shape), rot, 0.0)
        base = h * 2 * LANE
        o_ref[:, base:base + LANE] = (q[:, h * NOPE_DIM:(h + 1) * NOPE_DIM] * scale).astype(BF16)
        o_ref[:, base + LANE:base + 2 * LANE] = rot.astype(BF16)


def _mla_q(zc, q_ng, w_uq, cos_t, sin_t, n_heads, scale, tm):
    M = zc.shape[0]
    R = q_ng.shape[0]
    N = w_uq.shape[1]
    return pl.pallas_call(
        functools.partial(_mla_q_kernel, n_heads, scale),
        grid=(M // tm,),
        in_specs=[
            pl.BlockSpec((tm, R), lambda i: (i, 0)),
            pl.BlockSpec((1, R), lambda i: (0, 0)),
            pl.BlockSpec((R, N), lambda i: (0, 0)),
            pl.BlockSpec((tm, LANE), lambda i: (i, 0)),
            pl.BlockSpec((tm, LANE), lambda i: (i, 0)),
        ],
        out_specs=pl.BlockSpec((tm, n_heads * 2 * LANE), lambda i: (i, 0)),
        out_shape=jax.ShapeDtypeStruct((M, n_heads * 2 * LANE), BF16),
        compiler_params=_params("parallel"),
        name="mla_q",
    )(zc, q_ng.reshape(1, R), w_uq, cos_t, sin_t)


class _KeyRows:
    def __init__(self, rows, blk):
        self.rows, self.blk = rows, blk
        self.per_lat = 1 + rows.lat_len // blk
        self.lat_blocks = rows.n_lat * self.per_lat
        self.ctx_blocks = rows.mc // blk
        self.n = self.lat_blocks + self.ctx_blocks

    def is_cache(self, j):
        return (j < self.lat_blocks) & (j % self.per_lat == 0)

    def cache_idx(self, j):
        return jnp.clip(j // self.per_lat, 0, self.rows.n_lat - 1)

    def token_block(self, j):
        b = j // self.per_lat
        t = jnp.maximum(j % self.per_lat - 1, 0)
        lat = self.ctx_blocks + b * (self.per_lat - 1) + t
        return jnp.where(j < self.lat_blocks, lat, j - self.lat_blocks)


def _mla_kv_kernel(keys, n_heads, ckv_ref, kpe_ref, cckv_ref, ckpe_ref, kvng_ref, wk_ref, wvt_ref, cos_ref, sin_ref,
                   kcat_ref, vt_ref, ckvn_ref):
    cached = keys.is_cache(pl.program_id(0))
    kpe = kpe_ref[...]
    rot = kpe[:, :LANE] * cos_ref[...] + kpe[:, LANE:] * sin_ref[...]
    ckvn = jnp.where(cached, cckv_ref[...], _rms(ckv_ref[...], kvng_ref[...]))
    kr = jnp.where(cached, ckpe_ref[...], rot)
    ckvn_ref[...] = ckvn
    ckvn_b = ckvn.astype(BF16)
    kn = _dot(ckvn_b, wk_ref[...])
    vt_ref[...] = _dot_nt(wvt_ref[...], ckvn_b).astype(BF16)
    for h in range(n_heads):
        base = h * 2 * LANE
        kcat_ref[:, base:base + LANE] = kn[:, h * NOPE_DIM:(h + 1) * NOPE_DIM].astype(BF16)
        kcat_ref[:, base + LANE:base + 2 * LANE] = jnp.where(_rope_keep(h, kr.shape), kr, 0.0).astype(BF16)


def _mla_kv(keys, zc, zk, col, cache_ckv, cache_kpe2, kv_ng, w_k, w_vt, cos_t, sin_t, n_heads):
    blk = keys.blk
    R = kv_ng.shape[0]
    Mk = keys.n * blk
    ckv_off, ckv_w = col["ckv"]
    assert ckv_off % ckv_w == 0
    tok = keys.token_block
    return pl.pallas_call(
        functools.partial(_mla_kv_kernel, keys, n_heads),
        grid=(keys.n,),
        in_specs=[
            pl.BlockSpec((blk, ckv_w), lambda j: (tok(j), ckv_off // ckv_w)),
            pl.BlockSpec((blk, 2 * LANE), lambda j: (tok(j), 0)),
            pl.BlockSpec((blk, R), lambda j: (keys.cache_idx(j), 0)),
            pl.BlockSpec((blk, LANE), lambda j: (keys.cache_idx(j), 0)),
            pl.BlockSpec((1, R), lambda j: (0, 0)),
            pl.BlockSpec(w_k.shape, lambda j: (0, 0)),
            pl.BlockSpec(w_vt.shape, lambda j: (0, 0)),
            pl.BlockSpec((blk, LANE), lambda j: (tok(j), 0)),
            pl.BlockSpec((blk, LANE), lambda j: (tok(j), 0)),
        ],
        out_specs=[
            pl.BlockSpec((blk, n_heads * 2 * LANE), lambda j: (j, 0)),
            pl.BlockSpec((n_heads * V_DIM, blk), lambda j: (0, j)),
            pl.BlockSpec((blk, R), lambda j: (j, 0)),
        ],
        out_shape=[
            jax.ShapeDtypeStruct((Mk, n_heads * 2 * LANE), BF16),
            jax.ShapeDtypeStruct((n_heads * V_DIM, Mk), BF16),
            jax.ShapeDtypeStruct((Mk, R), F32),
        ],
        compiler_params=_params("parallel"),
        name="mla_kv",
    )(zc, zk, cache_ckv, cache_kpe2, kv_ng.reshape(1, R), w_k, w_vt, cos_t, sin_t)


ATTN_TQ = 256
ATTN_TK = 1024


def _attn_kernel(tq, chunks, q_ref, k_ref, vt_ref, o_ref, st0_ref, st1_ref):
    n_heads = q_ref.shape[1] // (2 * LANE)
    units = [(h, u) for h in range(n_heads) for u in range(q_ref.shape[0] // tq)]
    st_refs = (st0_ref, st1_ref)

    def scores(n, c0, c1):
        h, u = units[n]
        hd = slice(h * 2 * LANE, (h + 1) * 2 * LANE)
        st = _dot_nt(k_ref[c0:c1, hd], q_ref[u * tq:(u + 1) * tq, hd])
        st_refs[n % 2][c0:c1, :] = st
        return jnp.max(st, axis=0, keepdims=True)

    def weights(n, c0, c1, m):
        h, _ = units[n]
        p = jnp.exp2(st_refs[n % 2][c0:c1, :] - m)
        return jnp.sum(p, axis=0, keepdims=True), _dot(vt_ref[h * V_DIM:(h + 1) * V_DIM, c0:c1], p.astype(BF16))

    m_prev = None
    for n in range(len(units) + 1):
        m_new = acc = l = None
        for c0, c1 in chunks:
            if n < len(units):
                mc = scores(n, c0, c1)
                m_new = mc if m_new is None else jnp.maximum(m_new, mc)
            if n > 0:
                lc, pv = weights(n - 1, c0, c1, m_prev)
                l = lc if l is None else l + lc
                acc = pv if acc is None else acc + pv
        if n > 0:
            h, u = units[n - 1]
            o_ref[u * tq:(u + 1) * tq, h * V_DIM:(h + 1) * V_DIM] = (acc / l).T.astype(o_ref.dtype)
        m_prev = m_new


def _attention(qcat, kcat, vt, n_seq, n_heads, hps, tq, q_blocks_per_seq, q_block0, tk, k_block0):
    sub_q = ATTN_TQ if tq % ATTN_TQ == 0 else LANE
    assert tq % sub_q == 0 and n_heads % hps == 0
    chunks = tuple((c0, min(c0 + ATTN_TK, tk)) for c0 in range(0, tk, ATTN_TK))
    return pl.pallas_call(
        functools.partial(_attn_kernel, sub_q, chunks),
        grid=(n_seq, n_heads // hps, q_blocks_per_seq),
        in_specs=[
            pl.BlockSpec((tq, hps * 2 * LANE), lambda s, h, i: (q_block0 + s * q_blocks_per_seq + i, h)),
            pl.BlockSpec((tk, hps * 2 * LANE), lambda s, h, i: (k_block0 + s, h)),
            pl.BlockSpec((hps * V_DIM, tk), lambda s, h, i: (h, k_block0 + s)),
        ],
        out_specs=pl.BlockSpec((tq, hps * V_DIM), lambda s, h, i: (s * q_blocks_per_seq + i, h)),
        out_shape=jax.ShapeDtypeStruct((n_seq * q_blocks_per_seq * tq, n_heads * V_DIM), BF16),
        scratch_shapes=[pltpu.VMEM((tk, sub_q), F32), pltpu.VMEM((tk, sub_q), F32)],
        compiler_params=_params("parallel", "parallel", "arbitrary"),
        name="mla_attn",
    )(qcat, kcat, vt)


def _rope_partner(w):
    q = ROPE_DIM // 4
    return jnp.flip(w.reshape(w.shape[:-1] + (2, 2, q)), axis=-2).reshape(w.shape)


def _rope_tables(rows):
    half, quarter = ROPE_DIM // 2, ROPE_DIM // 4
    inv = ROPE_BASE ** (-jnp.arange(quarter, dtype=F32) * 2.0 / half)
    t = jnp.arange(rows.lat_len)
    r = (t // GRID_W).astype(F32)
    c = (t % GRID_W).astype(F32)
    ang_r = r[:, None] * inv[None, :]
    ang_c = c[:, None] * inv[None, :]
    cos = jnp.concatenate([jnp.cos(ang_r), jnp.cos(ang_r), jnp.cos(ang_c), jnp.cos(ang_c)], axis=1)
    sin = jnp.concatenate([-jnp.sin(ang_r), jnp.sin(ang_r), -jnp.sin(ang_c), jnp.sin(ang_c)], axis=1)
    cos = jnp.tile(jnp.concatenate([cos, cos], axis=1), (rows.n_lat, 1))
    sin = jnp.tile(jnp.concatenate([sin, sin], axis=1), (rows.n_lat, 1))
    cos = jnp.concatenate([jnp.ones((rows.mc, LANE), F32), cos], axis=0)
    sin = jnp.concatenate([jnp.zeros((rows.mc, LANE), F32), sin], axis=0)
    return cos, sin


def kernel(x_prompt, x_sample, state_gla, state_ret, cache_ckv, cache_kpe, c, c_ctx, mod_w, mod_b, norm1_g, norm2_g, ab_w_in, gla_gate_w2, gla_gate_b, ret_decay, gla_norm_g, ret_norm_g, ab_w_out, mla_w_in, mla_q_norm_g, mla_w_uq, mla_kv_norm_g, mla_w_ukv, mla_w_out, ffn_w_in, ffn_conv, ffn_w_out, final_norm_g):
    B, S, D = x_prompt.shape
    NB, T, _ = x_sample.shape
    depth = mod_w.shape[0]
    _, _, _, HA, DK, DV = state_gla.shape
    HB = state_ret.shape[3]
    assert HA == HB and state_ret.shape[4:] == (DK, DV)
    GR = gla_gate_w2.shape[2]
    past = cache_ckv.shape[2]
    q_rank = mla_q_norm_g.shape[1]
    kv_rank = mla_kv_norm_g.shape[1]
    HC = mla_w_uq.shape[2] // (NOPE_DIM + ROPE_DIM)
    F = ffn_w_out.shape[1]
    rows = _Rows(B * S, S, T, NB)
    M = rows.m
    tm = min(512, S * B, T)
    assert rows.mc % tm == 0 and T % tm == 0
    assert S & (S - 1) == 0 and T & (T - 1) == 0

    tm_proj = min(1024, S * B, T)
    assert rows.mc % tm_proj == 0 and T % tm_proj == 0
    cond = jnp.concatenate([c_ctx[None, :], c, jnp.zeros((MOD_GROUPS - 1 - NB, D), F32)], axis=0)
    mod3 = _modulation(cond, mod_w, mod_b)

    x = (x_prompt.reshape(B * S, D), x_sample.reshape(NB * T, D))
    h = _normmod(rows, x, norm1_g[0], mod3[0], tm)
    ffn_w = (ffn_w_in.astype(BF16), ffn_conv, ffn_w_out.astype(BF16))
    new_gla, new_ret, new_ckv, new_kpe = [], [], [], []
    for l in range(depth):
        if l % 2 == 0:
            e = l // 2
            names = ("gq", "gk", "gv", "gg", "glr", "rq", "rk", "rv", "rg")
            sizes = (HA * DK, HA * DK, HA * DV, HA * DV, 2 * GR, HB * DK, HB * DK, HB * DV, HB * DV)
            src, o = {}, 0
            for n, sz in zip(names, sizes):
                src[n] = (o, sz)
                o += sz
            order = ("gq", "gk", "gv", "gg", "rq", "rk", "rv", "rg")
            col, o = {}, 0
            for n in order:
                col[n] = (o, src[n][1])
                o += src[n][1]
            so, sz = src["glr"]
            assert so % LANE == 0 and so + LANE <= ab_w_in.shape[2]
            w_all = lax.optimization_barrier(ab_w_in[e].astype(BF16))
            w_in = jnp.concatenate([w_all[:, :so], w_all[:, so + sz:]], axis=1)
            w_glr = w_all[:, so:so + LANE]
            z, glr = _inproj(h, w_in, w_glr, BF16, tm_proj)
            tb = min(256, S)
            extra = None
            for dirn in (0, 1):
                cfg = _ScanCfg(rows, B, tb, HA, DK, DV, backward=bool(dirn))
                gate_w = jnp.zeros((LANE, HA * DK), F32).at[dirn * GR:(dirn + 1) * GR].set(gla_gate_w2[e, dirn])
                rd = jnp.broadcast_to(ret_decay[e, dirn][:, None, None], (HB, 8, LANE))
                y, sg, sr = _scan(cfg, z, glr, col, gate_w.astype(BF16), gla_gate_b[e, dirn].reshape(1, -1), rd,
                                  state_gla, state_ret, e, extra)
                extra = (y, sg, sr, gla_norm_g[e].reshape(1, -1), ret_norm_g[e].reshape(1, -1))
            new_gla.append(sg)
            new_ret.append(sr)
            x, h = _outproj(rows, y, ab_w_out[e].astype(BF16), x, norm2_g[l], mod3[l], tm)
        else:
            i = l // 2
            w = mla_w_in[i]
            w_kpe = w[:, q_rank + kv_rank:]
            w_in = w[:, :q_rank + kv_rank].astype(BF16)
            w_kpe_p = _rope_partner(w_kpe)
            w_side = jnp.concatenate([w_kpe, w_kpe, w_kpe_p, w_kpe_p], axis=1).astype(BF16)
            col = {"cq": (0, q_rank), "ckv": (q_rank, kv_rank)}
            zc, zk = _inproj(h, w_in, w_side, F32, tm_proj)
            cos_t, sin_t = _rope_tables(rows)
            wq = mla_w_uq[i].reshape(q_rank, HC, NOPE_DIM + ROPE_DIM)
            wq_rope = wq[:, :, NOPE_DIM:]
            w_uq = jnp.concatenate([wq[:, :, :NOPE_DIM].reshape(q_rank, -1), wq_rope.reshape(q_rank, -1),
                                    _rope_partner(wq_rope).reshape(q_rank, -1)], axis=1).astype(BF16)
            scale = (NOPE_DIM + ROPE_DIM) ** -0.5 * LOG2_E
            assert col["cq"][0] == 0
            qcat = _mla_q(zc, mla_q_norm_g[i], w_uq, cos_t, sin_t, HC, scale, tm)
            wkv = mla_w_ukv[i].reshape(kv_rank, HC, NOPE_DIM + V_DIM)
            w_k = wkv[:, :, :NOPE_DIM].reshape(kv_rank, -1).astype(BF16)
            w_vt = wkv[:, :, NOPE_DIM:].reshape(kv_rank, -1).T.astype(BF16)
            assert S % past == 0 and T % past == 0
            keys = _KeyRows(rows, past)
            ckpe = cache_kpe[:, i].reshape(NB * past, ROPE_DIM)
            kcat, vt, ckvn = _mla_kv(keys, zc, zk, col, cache_ckv[:, i].reshape(NB * past, kv_rank),
                                     jnp.concatenate([ckpe, ckpe], axis=1), mla_kv_norm_g[i], w_k, w_vt,
                                     cos_t, sin_t, HC)
            tk_lat = past + T
            assert (keys.lat_blocks * past) % S == 0
            tq = min(4096, T)
            assert rows.mc % tq == 0
            o = (_attention(qcat, kcat, vt, B, HC, HC, S, 1, 0, S, keys.lat_blocks * past // S),
                 _attention(qcat, kcat, vt, NB, HC, 1, tq, T // tq, rows.mc // tq, tk_lat, 0))
            new_ckv.append(ckvn[keys.lat_blocks * past:].reshape(B, S, kv_rank))
            new_kpe.append(zk[:rows.mc, :ROPE_DIM].reshape(B, S, ROPE_DIM))
            x, h = _outproj(rows, o, mla_w_out[i].astype(BF16), x, norm2_g[l], mod3[l], tm)
        tf = _largest_tile(F, 512)
        if l == depth - 1:
            y_ctx, y_lat = _ffn(rows, h, x, mod3[l], l, *ffn_w, tm, tf, final_g=final_norm_g)
        else:
            x, h = _ffn(rows, h, x, mod3[l], l, *ffn_w, tm, tf, next_norm=(norm1_g[l + 1], mod3[l + 1]))

    y_prompt = y_ctx.reshape(B, S, D)
    y_sample = y_lat.reshape(NB, T, D)
    return (y_prompt, y_sample, jnp.stack(new_gla, axis=1), jnp.stack(new_ret, axis=1),
            jnp.stack(new_ckv, axis=1), jnp.stack(new_kpe, axis=1))
```
